```python
import math
import jax, jax.numpy as jnp
from jax import lax
import numpy as np

D_MODEL = 1024
BATCH = 8
SEQ = 4096
DEPTH = 4

N_POOL_LAYERS = DEPTH // 2
N_ATTN_LAYERS = DEPTH - N_POOL_LAYERS
POOL_WINDOWS = (2, 4, 8, 16)
N_POOL_GROUPS = len(POOL_WINDOWS)
POOL_GROUP_DIM = D_MODEL // N_POOL_GROUPS
BRANCHES = ((128, 1), (512, 4), (2048, 16))
N_BRANCHES = len(BRANCHES)
HEAD_DIM = 64
N_HEADS = D_MODEL // HEAD_DIM
D_ATTN = N_HEADS * HEAD_DIM
ATTN_BLOCK = 128
D_FF = 2816
CONV_WIDTH = 3
EPS = 1e-6
ADA_SCALE = 0.5

kernel_name = "yoco_pool_dilated_alibi_hybrid"


def _rmsnorm(x, g):
    x32 = x.astype(jnp.float32)
    y = x32 * lax.rsqrt(jnp.mean(x32 * x32, axis=-1, keepdims=True) + EPS)
    return (y * g.astype(jnp.float32)).astype(x.dtype)


def _modulate(h, shift, scale):
    return h * (1 + scale[:, None, :]) + shift[:, None, :]


def _alibi_slopes(n):
    def pow2(m):
        start = 2.0 ** (-(2.0 ** -(math.log2(m) - 3)))
        return [start ** (i + 1) for i in range(m)]
    if math.log2(n).is_integer():
        s = pow2(n)
    else:
        c = 2 ** math.floor(math.log2(n))
        s = pow2(c) + pow2(2 * c)[0::2][: n - c]
    s = np.asarray(s, dtype=np.float32)
    return -np.sort(-s)


def _pool_mixer(h, w_in, w_grp, scale, w_out):
    b, s, _ = h.shape
    u = (h @ w_in).reshape(b, s, N_POOL_GROUPS, POOL_GROUP_DIM)
    u32 = u.astype(jnp.float32)
    csum = jnp.cumsum(u32, axis=1)
    t = jnp.arange(s)
    outs = []
    for g, w in enumerate(POOL_WINDOWS):
        cs = csum[:, :, g]
        lag = jnp.pad(cs, ((0, 0), (w, 0), (0, 0)))[:, :s]
        count = jnp.minimum(t + 1, w).astype(jnp.float32)[None, :, None]
        pooled = (cs - lag) / count - u32[:, :, g]
        outs.append(jnp.einsum('bsc,cd->bsd', pooled.astype(h.dtype), w_grp[g]))
    y = jnp.concatenate(outs, axis=-1) * scale
    return y @ w_out


def _causal_dwconv(a, w, bias):
    s = a.shape[1]
    ap = jnp.pad(a, ((0, 0), (CONV_WIDTH - 1, 0), (0, 0)))
    y = bias
    for k in range(CONV_WIDTH):
        y = y + ap[:, k:k + s] * w[k]
    return y


def _conv_ffn(h, w_up, conv_w, conv_b, w_down):
    a, v = jnp.split(h @ w_up, 2, axis=-1)
    a = _causal_dwconv(a, conv_w, conv_b)
    return (jax.nn.silu(a) * v) @ w_down


def _dilated_branch(q, k, v, window, dilation, slopes):
    b, s, h, dh = q.shape
    n_steps = window // dilation
    blk = max(ATTN_BLOCK, n_steps)
    sub_len = s // dilation
    nb = -(-sub_len // blk)
    sub_pad = nb * blk

    def to_sub(t):
        t = t.reshape(b, sub_len, dilation, h, dh).transpose(0, 2, 1, 3, 4).reshape(b * dilation, sub_len, h, dh)
        return jnp.pad(t, ((0, 0), (0, sub_pad - sub_len), (0, 0), (0, 0)))

    def with_prev(t):
        tb = t.reshape(-1, nb, blk, h, dh)
        prev = jnp.pad(tb, ((0, 0), (1, 0), (0, 0), (0, 0), (0, 0)))[:, :nb]
        return jnp.concatenate([prev, tb], axis=2)

    qb = to_sub(q).reshape(-1, nb, blk, h, dh)
    kb = with_prev(to_sub(k))
    vb = with_prev(to_sub(v))

    scores = jnp.einsum('bnqhd,bnkhd->bnhqk', qb, kb).astype(jnp.float32) * (dh ** -0.5)
    qi = jnp.arange(blk)[:, None] + blk
    ki = jnp.arange(2 * blk)[None, :]
    delta = qi - ki
    key_idx = jnp.arange(nb)[:, None] * blk + jnp.arange(2 * blk)[None, :] - blk
    valid = ((delta >= 0) & (delta <= n_steps))[None] & (key_idx >= 0)[:, None, :]
    bias = -slopes[:, None, None] * (delta * dilation).astype(jnp.float32)[None]
    scores = jnp.where(valid[None, :, None], scores + bias[None, None], -jnp.inf)
    lse = jax.nn.logsumexp(scores, axis=-1)
    p = jnp.exp(scores - lse[..., None])
    out = jnp.einsum('bnhqk,bnkhd->bnqhd', p.astype(v.dtype), vb)

    out = out.reshape(b, dilation, sub_pad, h, dh)[:, :, :sub_len]
    out = out.transpose(0, 2, 1, 3, 4).reshape(b, s, h, dh)
    lse = lse.transpose(0, 1, 3, 2).reshape(b, dilation, sub_pad, h)[:, :, :sub_len]
    lse = lse.transpose(0, 2, 1, 3).reshape(b, s, h)
    return out, lse


def _dilated_attention(h, kv, w_q, w_o, slopes):
    b, s, _ = h.shape
    q = (h @ w_q).reshape(b, s, N_BRANCHES, N_HEADS, HEAD_DIM)
    outs, lses = [], []
    for g, (window, dil) in enumerate(BRANCHES):
        o, l = _dilated_branch(q[:, :, g], kv[:, :, 0, g], kv[:, :, 1, g], window, dil, slopes[g])
        outs.append(o)
        lses.append(l)
    wts = jax.nn.softmax(jnp.stack(lses, axis=0), axis=0)
    o = jnp.sum(wts[..., None] * jnp.stack(outs, axis=0).astype(jnp.float32), axis=0)
    return o.reshape(b, s, D_ATTN).astype(h.dtype) @ w_o


def _fwd_setup_inputs(seed: int = 0) -> dict:
    key = jax.random.key(seed)
    ks = jax.random.split(key, 24)

    def nrm(k, shape, scale):
        return jax.random.normal(k, shape, jnp.float32) * scale

    D, F, G = D_MODEL, D_FF, N_BRANCHES
    return {
        "x": nrm(ks[0], (BATCH, SEQ, D), 1.0),
        "c": nrm(ks[1], (BATCH, D), 1.0),
        "ada_w": nrm(ks[2], (DEPTH, D, 6 * D), ADA_SCALE * D ** -0.5),
        "ada_b": nrm(ks[3], (DEPTH, 6 * D), 0.02),
        "norm1_g": 1.0 + nrm(ks[4], (DEPTH, D), 0.05),
        "norm2_g": 1.0 + nrm(ks[5], (DEPTH, D), 0.05),
        "pool_w_in": nrm(ks[6], (N_POOL_LAYERS, D, D), D ** -0.5),
        "pool_w_grp": nrm(ks[7], (N_POOL_LAYERS, N_POOL_GROUPS, POOL_GROUP_DIM, POOL_GROUP_DIM), POOL_GROUP_DIM ** -0.5),
        "pool_scale": 1.0 + nrm(ks[8], (N_POOL_LAYERS, D), 0.1),
        "pool_w_out": nrm(ks[9], (N_POOL_LAYERS, D, D), D ** -0.5),
        "kv_norm_g": 1.0 + nrm(ks[10], (D,), 0.05),
        "kv_ada_w": nrm(ks[11], (D, 2 * D), ADA_SCALE * D ** -0.5),
        "kv_ada_b": nrm(ks[12], (2 * D,), 0.02),
        "w_kv": nrm(ks[13], (D, 2 * G * D_ATTN), D ** -0.5),
        "attn_w_q": nrm(ks[14], (N_ATTN_LAYERS, D, G * D_ATTN), D ** -0.5),
        "attn_w_o": nrm(ks[15], (N_ATTN_LAYERS, D_ATTN, D), D_ATTN ** -0.5),
        "ffn_w_up": nrm(ks[16], (DEPTH, D, 2 * F), D ** -0.5),
        "ffn_conv_w": nrm(ks[17], (DEPTH, CONV_WIDTH, F), CONV_WIDTH ** -0.5),
        "ffn_conv_b": nrm(ks[18], (DEPTH, F), 0.02),
        "ffn_w_down": nrm(ks[19], (DEPTH, F, D), F ** -0.5),
        "final_g": 1.0 + nrm(ks[20], (D,), 0.05),
    }


def _fwd_reference(x, c, ada_w, ada_b, norm1_g, norm2_g, pool_w_in, pool_w_grp, pool_scale, pool_w_out,
              kv_norm_g, kv_ada_w, kv_ada_b, w_kv, attn_w_q, attn_w_o,
              ffn_w_up, ffn_conv_w, ffn_conv_b, ffn_w_down, final_g):
    b, s, _ = x.shape
    cond = jax.nn.silu(c)
    slopes = jnp.asarray(_alibi_slopes(N_BRANCHES * N_HEADS)).reshape(N_BRANCHES, N_HEADS)
    kv = None
    for layer in range(DEPTH):
        mod = cond @ ada_w[layer] + ada_b[layer]
        sh1, sc1, g1, sh2, sc2, g2 = jnp.split(mod, 6, axis=-1)
        h = _modulate(_rmsnorm(x, norm1_g[layer]), sh1, sc1)
        if layer < N_POOL_LAYERS:
            y = _pool_mixer(h, pool_w_in[layer], pool_w_grp[layer], pool_scale[layer], pool_w_out[layer])
        else:
            if layer == N_POOL_LAYERS:
                kv_shift, kv_scale = jnp.split(cond @ kv_ada_w + kv_ada_b, 2, axis=-1)
                hkv = _modulate(_rmsnorm(x, kv_norm_g), kv_shift, kv_scale)
                kv = (hkv @ w_kv).reshape(b, s, 2, N_BRANCHES, N_HEADS, HEAD_DIM)
            j = layer - N_POOL_LAYERS
            y = _dilated_attention(h, kv, attn_w_q[j], attn_w_o[j], slopes)
        x = x + g1[:, None, :] * y
        h = _modulate(_rmsnorm(x, norm2_g[layer]), sh2, sc2)
        x = x + g2[:, None, :] * _conv_ffn(h, ffn_w_up[layer], ffn_conv_w[layer], ffn_conv_b[layer], ffn_w_down[layer])
    return _rmsnorm(x, final_g)


import jax as _jax
import jax.numpy as _jnp

TWIN_FORMAT = 'train_step'
FWD_PARAMS = ['x', 'c', 'ada_w', 'ada_b', 'norm1_g', 'norm2_g', 'pool_w_in', 'pool_w_grp', 'pool_scale', 'pool_w_out', 'kv_norm_g', 'kv_ada_w', 'kv_ada_b', 'w_kv', 'attn_w_q', 'attn_w_o', 'ffn_w_up', 'ffn_conv_w', 'ffn_conv_b', 'ffn_w_down', 'final_g']
TWIN_WEIGHTS = ['ada_w', 'ada_b', 'norm1_g', 'norm2_g', 'pool_w_in', 'pool_w_grp', 'pool_scale', 'pool_w_out', 'kv_norm_g', 'kv_ada_w', 'kv_ada_b', 'w_kv', 'attn_w_q', 'attn_w_o', 'ffn_w_up', 'ffn_conv_w', 'ffn_conv_b', 'ffn_w_down', 'final_g']
TWIN_DIFF_INPUT = 'x'
TWIN_INPUTS = ['x', 'c', 'ada_w', 'ada_b', 'norm1_g', 'norm2_g', 'pool_w_in', 'pool_w_grp', 'pool_scale', 'pool_w_out', 'kv_norm_g', 'kv_ada_w', 'kv_ada_b', 'w_kv', 'attn_w_q', 'attn_w_o', 'ffn_w_up', 'ffn_conv_w', 'ffn_conv_b', 'ffn_w_down', 'final_g', 'loss_target', 'm_ada_w', 'm_ada_b', 'm_norm1_g', 'm_norm2_g', 'm_pool_w_in', 'm_pool_w_grp', 'm_pool_scale', 'm_pool_w_out', 'm_kv_norm_g', 'm_kv_ada_w', 'm_kv_ada_b', 'm_w_kv', 'm_attn_w_q', 'm_attn_w_o', 'm_ffn_w_up', 'm_ffn_conv_w', 'm_ffn_conv_b', 'm_ffn_w_down', 'm_final_g', 'v_ada_w', 'v_ada_b', 'v_norm1_g', 'v_norm2_g', 'v_pool_w_in', 'v_pool_w_grp', 'v_pool_scale', 'v_pool_w_out', 'v_kv_norm_g', 'v_kv_ada_w', 'v_kv_ada_b', 'v_w_kv', 'v_attn_w_q', 'v_attn_w_o', 'v_ffn_w_up', 'v_ffn_conv_w', 'v_ffn_conv_b', 'v_ffn_w_down', 'v_final_g']
TWIN_OUTPUTS = ['loss', 'grad_x', 'grad_ada_w', 'grad_ada_b', 'grad_norm1_g', 'grad_norm2_g', 'grad_pool_w_in', 'grad_pool_w_grp', 'grad_pool_scale', 'grad_pool_w_out', 'grad_kv_norm_g', 'grad_kv_ada_w', 'grad_kv_ada_b', 'grad_w_kv', 'grad_attn_w_q', 'grad_attn_w_o', 'grad_ffn_w_up', 'grad_ffn_conv_w', 'grad_ffn_conv_b', 'grad_ffn_w_down', 'grad_final_g', 'delta_ada_w', 'delta_ada_b', 'delta_norm1_g', 'delta_norm2_g', 'delta_pool_w_in', 'delta_pool_w_grp', 'delta_pool_scale', 'delta_pool_w_out', 'delta_kv_norm_g', 'delta_kv_ada_w', 'delta_kv_ada_b', 'delta_w_kv', 'delta_attn_w_q', 'delta_attn_w_o', 'delta_ffn_w_up', 'delta_ffn_conv_w', 'delta_ffn_conv_b', 'delta_ffn_w_down', 'delta_final_g', 'new_m_ada_w', 'new_m_ada_b', 'new_m_norm1_g', 'new_m_norm2_g', 'new_m_pool_w_in', 'new_m_pool_w_grp', 'new_m_pool_scale', 'new_m_pool_w_out', 'new_m_kv_norm_g', 'new_m_kv_ada_w', 'new_m_kv_ada_b', 'new_m_w_kv', 'new_m_attn_w_q', 'new_m_attn_w_o', 'new_m_ffn_w_up', 'new_m_ffn_conv_w', 'new_m_ffn_conv_b', 'new_m_ffn_w_down', 'new_m_final_g', 'new_v_ada_w', 'new_v_ada_b', 'new_v_norm1_g', 'new_v_norm2_g', 'new_v_pool_w_in', 'new_v_pool_w_grp', 'new_v_pool_scale', 'new_v_pool_w_out', 'new_v_kv_norm_g', 'new_v_kv_ada_w', 'new_v_kv_ada_b', 'new_v_w_kv', 'new_v_attn_w_q', 'new_v_attn_w_o', 'new_v_ffn_w_up', 'new_v_ffn_conv_w', 'new_v_ffn_conv_b', 'new_v_ffn_w_down', 'new_v_final_g']
TWIN_LEAF_KINDS = {'loss': 'loss', 'grad_x': 'grad_x', 'grad_ada_w': 'grad_w', 'grad_ada_b': 'grad_w', 'grad_norm1_g': 'grad_w', 'grad_norm2_g': 'grad_w', 'grad_pool_w_in': 'grad_w', 'grad_pool_w_grp': 'grad_w', 'grad_pool_scale': 'grad_w', 'grad_pool_w_out': 'grad_w', 'grad_kv_norm_g': 'grad_w', 'grad_kv_ada_w': 'grad_w', 'grad_kv_ada_b': 'grad_w', 'grad_w_kv': 'grad_w', 'grad_attn_w_q': 'grad_w', 'grad_attn_w_o': 'grad_w', 'grad_ffn_w_up': 'grad_w', 'grad_ffn_conv_w': 'grad_w', 'grad_ffn_conv_b': 'grad_w', 'grad_ffn_w_down': 'grad_w', 'grad_final_g': 'grad_w', 'delta_ada_w': 'delta_w', 'delta_ada_b': 'delta_w', 'delta_norm1_g': 'delta_w', 'delta_norm2_g': 'delta_w', 'delta_pool_w_in': 'delta_w', 'delta_pool_w_grp': 'delta_w', 'delta_pool_scale': 'delta_w', 'delta_pool_w_out': 'delta_w', 'delta_kv_norm_g': 'delta_w', 'delta_kv_ada_w': 'delta_w', 'delta_kv_ada_b': 'delta_w', 'delta_w_kv': 'delta_w', 'delta_attn_w_q': 'delta_w', 'delta_attn_w_o': 'delta_w', 'delta_ffn_w_up': 'delta_w', 'delta_ffn_conv_w': 'delta_w', 'delta_ffn_conv_b': 'delta_w', 'delta_ffn_w_down': 'delta_w', 'delta_final_g': 'delta_w', 'new_m_ada_w': 'new_m', 'new_m_ada_b': 'new_m', 'new_m_norm1_g': 'new_m', 'new_m_norm2_g': 'new_m', 'new_m_pool_w_in': 'new_m', 'new_m_pool_w_grp': 'new_m', 'new_m_pool_scale': 'new_m', 'new_m_pool_w_out': 'new_m', 'new_m_kv_norm_g': 'new_m', 'new_m_kv_ada_w': 'new_m', 'new_m_kv_ada_b': 'new_m', 'new_m_w_kv': 'new_m', 'new_m_attn_w_q': 'new_m', 'new_m_attn_w_o': 'new_m', 'new_m_ffn_w_up': 'new_m', 'new_m_ffn_conv_w': 'new_m', 'new_m_ffn_conv_b': 'new_m', 'new_m_ffn_w_down': 'new_m', 'new_m_final_g': 'new_m', 'new_v_ada_w': 'new_v', 'new_v_ada_b': 'new_v', 'new_v_norm1_g': 'new_v', 'new_v_norm2_g': 'new_v', 'new_v_pool_w_in': 'new_v', 'new_v_pool_w_grp': 'new_v', 'new_v_pool_scale': 'new_v', 'new_v_pool_w_out': 'new_v', 'new_v_kv_norm_g': 'new_v', 'new_v_kv_ada_w': 'new_v', 'new_v_kv_ada_b': 'new_v', 'new_v_w_kv': 'new_v', 'new_v_attn_w_q': 'new_v', 'new_v_attn_w_o': 'new_v', 'new_v_ffn_w_up': 'new_v', 'new_v_ffn_conv_w': 'new_v', 'new_v_ffn_conv_b': 'new_v', 'new_v_ffn_w_down': 'new_v', 'new_v_final_g': 'new_v'}


def _forward(args):
    return _fwd_reference(*[args[k] for k in FWD_PARAMS])


def _output_shape():
    out = _jax.eval_shape(lambda: _forward(_fwd_setup_inputs(0)))
    return out.shape, out.dtype

N_MICROBATCH = 1
ADAM_LR = 0.001
ADAM_B1 = 0.9
ADAM_B2 = 0.999
ADAM_EPS = 1e-08
ADAM_WD = 0.01
ADAM_STEP = 10
PER_EXAMPLE_BATCH_AXIS = {'x': 0, 'c': 0, 'loss_target': 0}
SHARED_INPUTS = []
_WEIGHT_DTYPES = {'ada_w': _jnp.float32, 'ada_b': _jnp.float32, 'norm1_g': _jnp.float32, 'norm2_g': _jnp.float32, 'pool_w_in': _jnp.float32, 'pool_w_grp': _jnp.float32, 'pool_scale': _jnp.float32, 'pool_w_out': _jnp.float32, 'kv_norm_g': _jnp.float32, 'kv_ada_w': _jnp.float32, 'kv_ada_b': _jnp.float32, 'w_kv': _jnp.float32, 'attn_w_q': _jnp.float32, 'attn_w_o': _jnp.float32, 'ffn_w_up': _jnp.float32, 'ffn_conv_w': _jnp.float32, 'ffn_conv_b': _jnp.float32, 'ffn_w_down': _jnp.float32, 'final_g': _jnp.float32}
MOMENT_SCALE = {'ada_w': 6.233333e-02, 'ada_b': 1.087399e-01, 'norm1_g': 3.991682e-02, 'norm2_g': 5.122819e-02, 'pool_w_in': 5.241963e-02, 'pool_w_grp': 5.250662e-02, 'pool_scale': 5.356019e-02, 'pool_w_out': 5.268079e-02, 'kv_norm_g': 4.101507e-02, 'kv_ada_w': 2.831513e-02, 'kv_ada_b': 4.665953e-02, 'w_kv': 1.799461e-02, 'attn_w_q': 1.144547e-02, 'attn_w_o': 2.400254e-02, 'ffn_w_up': 2.303318e-02, 'ffn_conv_w': 2.351266e-02, 'ffn_conv_b': 2.036020e-02, 'ffn_w_down': 3.774025e-02, 'final_g': 3.213557e+01}


def _to_microbatches(a, axis):
    t = _jnp.moveaxis(a, axis, 0)
    t = t.reshape((N_MICROBATCH, t.shape[0] // N_MICROBATCH) + t.shape[1:])
    return _jnp.moveaxis(t, 1, axis + 1)


def setup_inputs(seed: int = 0) -> dict:
    inp = _fwd_setup_inputs(seed)
    key = _jax.random.fold_in(_jax.random.key(seed), 7919)
    shape, _ = _output_shape()
    out = dict(inp)
    out["loss_target"] = _jax.random.normal(_jax.random.fold_in(key, 0), shape, _jnp.float32)
    for i, name in enumerate(TWIN_WEIGHTS):
        w = inp[name].astype(_jnp.float32)
        if MOMENT_SCALE is None:
            s = _jnp.sqrt(_jnp.mean(_jnp.square(w)) + 1e-30)
        else:
            s = MOMENT_SCALE[name]
        km, kv = _jax.random.split(_jax.random.fold_in(key, i + 1))
        out[name] = w
        out["m_" + name] = s * _jax.random.normal(km, w.shape, _jnp.float32)
        out["v_" + name] = (s * s) * _jax.random.uniform(kv, w.shape, _jnp.float32, 0.5, 1.5)
    if N_MICROBATCH > 1:
        for name, axis in PER_EXAMPLE_BATCH_AXIS.items():
            out[name] = _to_microbatches(out[name], axis)
    return {'x': out['x'], 'c': out['c'], 'ada_w': out['ada_w'], 'ada_b': out['ada_b'], 'norm1_g': out['norm1_g'], 'norm2_g': out['norm2_g'], 'pool_w_in': out['pool_w_in'], 'pool_w_grp': out['pool_w_grp'], 'pool_scale': out['pool_scale'], 'pool_w_out': out['pool_w_out'], 'kv_norm_g': out['kv_norm_g'], 'kv_ada_w': out['kv_ada_w'], 'kv_ada_b': out['kv_ada_b'], 'w_kv': out['w_kv'], 'attn_w_q': out['attn_w_q'], 'attn_w_o': out['attn_w_o'], 'ffn_w_up': out['ffn_w_up'], 'ffn_conv_w': out['ffn_conv_w'], 'ffn_conv_b': out['ffn_conv_b'], 'ffn_w_down': out['ffn_w_down'], 'final_g': out['final_g'], 'loss_target': out['loss_target'], 'm_ada_w': out['m_ada_w'], 'm_ada_b': out['m_ada_b'], 'm_norm1_g': out['m_norm1_g'], 'm_norm2_g': out['m_norm2_g'], 'm_pool_w_in': out['m_pool_w_in'], 'm_pool_w_grp': out['m_pool_w_grp'], 'm_pool_scale': out['m_pool_scale'], 'm_pool_w_out': out['m_pool_w_out'], 'm_kv_norm_g': out['m_kv_norm_g'], 'm_kv_ada_w': out['m_kv_ada_w'], 'm_kv_ada_b': out['m_kv_ada_b'], 'm_w_kv': out['m_w_kv'], 'm_attn_w_q': out['m_attn_w_q'], 'm_attn_w_o': out['m_attn_w_o'], 'm_ffn_w_up': out['m_ffn_w_up'], 'm_ffn_conv_w': out['m_ffn_conv_w'], 'm_ffn_conv_b': out['m_ffn_conv_b'], 'm_ffn_w_down': out['m_ffn_w_down'], 'm_final_g': out['m_final_g'], 'v_ada_w': out['v_ada_w'], 'v_ada_b': out['v_ada_b'], 'v_norm1_g': out['v_norm1_g'], 'v_norm2_g': out['v_norm2_g'], 'v_pool_w_in': out['v_pool_w_in'], 'v_pool_w_grp': out['v_pool_w_grp'], 'v_pool_scale': out['v_pool_scale'], 'v_pool_w_out': out['v_pool_w_out'], 'v_kv_norm_g': out['v_kv_norm_g'], 'v_kv_ada_w': out['v_kv_ada_w'], 'v_kv_ada_b': out['v_kv_ada_b'], 'v_w_kv': out['v_w_kv'], 'v_attn_w_q': out['v_attn_w_q'], 'v_attn_w_o': out['v_attn_w_o'], 'v_ffn_w_up': out['v_ffn_w_up'], 'v_ffn_conv_w': out['v_ffn_conv_w'], 'v_ffn_conv_b': out['v_ffn_conv_b'], 'v_ffn_w_down': out['v_ffn_w_down'], 'v_final_g': out['v_final_g']}


def _loss(weights, diff, rest, loss_target):
    with _jax.named_scope("forward"):
        args = {**rest, TWIN_DIFF_INPUT: diff, **{k: w.astype(_WEIGHT_DTYPES[k]) for k, w in weights.items()}}
        y = _forward(args)
    with _jax.named_scope("loss_head"):
        err = _jnp.square(y.astype(_jnp.float32) - loss_target)
        return 0.5 * _jnp.sum(_jnp.mean(err, axis=-1)) if err.ndim else 0.5 * err


def _adamw(w, g, m, v):
    m = ADAM_B1 * m + (1.0 - ADAM_B1) * g
    v = ADAM_B2 * v + (1.0 - ADAM_B2) * _jnp.square(g)
    m_hat = m / (1.0 - ADAM_B1 ** ADAM_STEP)
    v_hat = v / (1.0 - ADAM_B2 ** ADAM_STEP)
    delta = -ADAM_LR * (m_hat / (_jnp.sqrt(v_hat) + ADAM_EPS) + ADAM_WD * w)
    return delta, m, v


def reference(x, c, ada_w, ada_b, norm1_g, norm2_g, pool_w_in, pool_w_grp, pool_scale, pool_w_out, kv_norm_g, kv_ada_w, kv_ada_b, w_kv, attn_w_q, attn_w_o, ffn_w_up, ffn_conv_w, ffn_conv_b, ffn_w_down, final_g, loss_target, m_ada_w, m_ada_b, m_norm1_g, m_norm2_g, m_pool_w_in, m_pool_w_grp, m_pool_scale, m_pool_w_out, m_kv_norm_g, m_kv_ada_w, m_kv_ada_b, m_w_kv, m_attn_w_q, m_attn_w_o, m_ffn_w_up, m_ffn_conv_w, m_ffn_conv_b, m_ffn_w_down, m_final_g, v_ada_w, v_ada_b, v_norm1_g, v_norm2_g, v_pool_w_in, v_pool_w_grp, v_pool_scale, v_pool_w_out, v_kv_norm_g, v_kv_ada_w, v_kv_ada_b, v_w_kv, v_attn_w_q, v_attn_w_o, v_ffn_w_up, v_ffn_conv_w, v_ffn_conv_b, v_ffn_w_down, v_final_g):
    given = dict(x=x, c=c, ada_w=ada_w, ada_b=ada_b, norm1_g=norm1_g, norm2_g=norm2_g, pool_w_in=pool_w_in, pool_w_grp=pool_w_grp, pool_scale=pool_scale, pool_w_out=pool_w_out, kv_norm_g=kv_norm_g, kv_ada_w=kv_ada_w, kv_ada_b=kv_ada_b, w_kv=w_kv, attn_w_q=attn_w_q, attn_w_o=attn_w_o, ffn_w_up=ffn_w_up, ffn_conv_w=ffn_conv_w, ffn_conv_b=ffn_conv_b, ffn_w_down=ffn_w_down, final_g=final_g, loss_target=loss_target, m_ada_w=m_ada_w, m_ada_b=m_ada_b, m_norm1_g=m_norm1_g, m_norm2_g=m_norm2_g, m_pool_w_in=m_pool_w_in, m_pool_w_grp=m_pool_w_grp, m_pool_scale=m_pool_scale, m_pool_w_out=m_pool_w_out, m_kv_norm_g=m_kv_norm_g, m_kv_ada_w=m_kv_ada_w, m_kv_ada_b=m_kv_ada_b, m_w_kv=m_w_kv, m_attn_w_q=m_attn_w_q, m_attn_w_o=m_attn_w_o, m_ffn_w_up=m_ffn_w_up, m_ffn_conv_w=m_ffn_conv_w, m_ffn_conv_b=m_ffn_conv_b, m_ffn_w_down=m_ffn_w_down, m_final_g=m_final_g, v_ada_w=v_ada_w, v_ada_b=v_ada_b, v_norm1_g=v_norm1_g, v_norm2_g=v_norm2_g, v_pool_w_in=v_pool_w_in, v_pool_w_grp=v_pool_w_grp, v_pool_scale=v_pool_scale, v_pool_w_out=v_pool_w_out, v_kv_norm_g=v_kv_norm_g, v_kv_ada_w=v_kv_ada_w, v_kv_ada_b=v_kv_ada_b, v_w_kv=v_w_kv, v_attn_w_q=v_attn_w_q, v_attn_w_o=v_attn_w_o, v_ffn_w_up=v_ffn_w_up, v_ffn_conv_w=v_ffn_conv_w, v_ffn_conv_b=v_ffn_conv_b, v_ffn_w_down=v_ffn_w_down, v_final_g=v_final_g)
    weights = {n: given[n] for n in TWIN_WEIGHTS}
    shared = {n: given[n] for n in SHARED_INPUTS}
    per_example = {n: given[n] for n in ['x', 'c']}
    grad_fn = _jax.value_and_grad(_loss, argnums=(0, 1))

    def one_microbatch(ex, loss_target):
        ex = dict(ex)
        diff = ex.pop(TWIN_DIFF_INPUT)
        return grad_fn(weights, diff, {**shared, **ex}, loss_target)

    if N_MICROBATCH == 1:
        loss, (grad_w, grad_x) = one_microbatch(per_example, given["loss_target"])
    else:
        def body(carry, xs):
            loss_sum, grad_sum = carry
            l_k, (gw_k, gx_k) = one_microbatch(xs[0], xs[1])
            with _jax.named_scope("update"):
                return (loss_sum + l_k, _jax.tree.map(_jnp.add, grad_sum, gw_k)), gx_k

        init = (_jnp.zeros((), _jnp.float32), _jax.tree.map(_jnp.zeros_like, weights))
        (loss, grad_w), grad_x = _jax.lax.scan(body, init, (per_example, given["loss_target"]))
    with _jax.named_scope("update"):
        delta_w, new_m, new_v = {}, {}, {}
        for n in TWIN_WEIGHTS:
            delta_w[n], new_m[n], new_v[n] = _adamw(weights[n], grad_w[n], given["m_" + n], given["v_" + n])
    return (loss, grad_x, *[grad_w[n] for n in TWIN_WEIGHTS], *[delta_w[n] for n in TWIN_WEIGHTS],
            *[new_m[n] for n in TWIN_WEIGHTS], *[new_v[n] for n in TWIN_WEIGHTS])
```

```python
import math

import numpy as np
import jax
import jax.numpy as jnp
from jax import lax
from jax.experimental import pallas as pl
from jax.experimental.pallas import tpu as pltpu

F32 = jnp.float32
BF = jnp.bfloat16

POOL_WINDOWS = (2, 4, 8, 16)
BRANCHES = ((128, 1), (512, 4), (2048, 16))
HEAD_DIM = 64
ATTN_BLOCK = 128
CONV_WIDTH = 3
EPS = 1e-6
ADAM_LR = 0.001
ADAM_B1 = 0.9
ADAM_B2 = 0.999
ADAM_EPS = 1e-08
ADAM_WD = 0.01
ADAM_STEP = 10

N_DEV = 8
LANES = 128
POOL_HALO = 16
CONV_HALO = 8
VMEM_LIMIT = 48 * 1024 * 1024
NEG = -1e30

MESH = pl.DeviceIdType.MESH
ANY = pl.BlockSpec(memory_space=pl.ANY)


def _params(sem=None):
    if sem is None:
        return pltpu.CompilerParams(vmem_limit_bytes=VMEM_LIMIT)
    return pltpu.CompilerParams(dimension_semantics=sem, vmem_limit_bytes=VMEM_LIMIT)


def _pick(dim, pref, mult=LANES):
    if dim <= pref:
        return dim
    t = (pref // mult) * mult
    while t >= mult:
        if dim % t == 0:
            return t
        t -= mult
    return dim


def _alibi_slopes(n):
    def pow2(m):
        start = 2.0 ** (-(2.0 ** -(math.log2(m) - 3)))
        return [start ** (i + 1) for i in range(m)]
    if math.log2(n).is_integer():
        s = pow2(n)
    else:
        c = 2 ** math.floor(math.log2(n))
        s = pow2(c) + pow2(2 * c)[0::2][: n - c]
    s = np.asarray(s, dtype=np.float32)
    return -np.sort(-s)


def _my_place():
    return lax.axis_index("x"), lax.axis_index("y"), lax.axis_index("c")


def _all_gather(x, name):
    r, c = x.shape

    def body(x_ref, out_ref, send_sems, recv_sems, local_sem):
        xi, yi, ci = _my_place()
        me, sibling = (xi, yi, ci), (xi, yi, 1 - ci)
        chips = [(1 - xi, yi), (xi, 1 - yi), (1 - xi, 1 - yi)]

        def slot(px, py, pc):
            return out_ref.at[4 * px + 2 * py + pc]

        def copy(k, block, to, src=None):
            return pltpu.make_async_remote_copy(
                src_ref=slot(*block) if src is None else src, dst_ref=slot(*block),
                send_sem=send_sems.at[k], recv_sem=recv_sems.at[k], device_id=to, device_id_type=MESH)

        mine = pltpu.make_async_copy(x_ref, slot(*me), local_sem)
        mine.start()
        first = [copy(0, me, sibling, src=x_ref)]
        first += [copy(1 + j, me, (*chip, ci), src=x_ref) for j, chip in enumerate(chips)]
        for cp in first:
            cp.start()
        passed = [copy(4 + j, (*chip, ci), sibling) for j, chip in enumerate(chips)]
        for j, chip in enumerate(chips):
            copy(1 + j, (*chip, ci), me).wait_recv()
            passed[j].start()
        copy(0, sibling, me).wait_recv()
        for j, chip in enumerate(chips):
            copy(4 + j, (*chip, 1 - ci), me).wait_recv()
        for cp in first + passed:
            cp.wait_send()
        mine.wait()

    return pl.pallas_call(
        body, name=name,
        out_shape=jax.ShapeDtypeStruct((N_DEV, r, c), x.dtype),
        in_specs=[ANY], out_specs=ANY,
        scratch_shapes=[pltpu.SemaphoreType.DMA((7,)), pltpu.SemaphoreType.DMA((7,)), pltpu.SemaphoreType.DMA],
    )(x)


def _all_to_all(x, name):
    _, r, c = x.shape

    def body(x_ref, out_ref, send_sems, recv_sems, local_sem):
        xi, yi, ci = _my_place()
        me = 4 * xi + 2 * yi + ci
        mine = pltpu.make_async_copy(x_ref.at[me], out_ref.at[me], local_sem)
        mine.start()
        copies = []
        for k in range(1, N_DEV):
            px = 1 - xi if k & 4 else xi
            py = 1 - yi if k & 2 else yi
            pc = 1 - ci if k & 1 else ci
            peer = 4 * px + 2 * py + pc
            cp = pltpu.make_async_remote_copy(
                src_ref=x_ref.at[peer], dst_ref=out_ref.at[me],
                send_sem=send_sems.at[k - 1], recv_sem=recv_sems.at[k - 1],
                device_id=(px, py, pc), device_id_type=MESH)
            cp.start()
            copies.append(cp)
        for cp in copies:
            cp.wait()
        mine.wait()

    return pl.pallas_call(
        body, name=name,
        out_shape=jax.ShapeDtypeStruct(x.shape, x.dtype),
        in_specs=[ANY], out_specs=ANY,
        scratch_shapes=[pltpu.SemaphoreType.DMA((7,)), pltpu.SemaphoreType.DMA((7,)), pltpu.SemaphoreType.DMA],
    )(x)


def _to_rows(vec):
    n = vec.shape[0]
    unit = 8 * LANES
    pad = (-n) % unit
    if pad:
        vec = jnp.concatenate([vec, jnp.zeros((pad,), vec.dtype)])
    return vec.reshape(-1, LANES)


def _mm(a, b, name, *, ta=False, tb=False, out_dtype=F32, tm=1024, tn=512, tk=1024,
        gate=None, resid=None, pre_dtype=None):
    if ta:
        kk, m = a.shape
    else:
        m, kk = a.shape
    if tb:
        n, k2 = b.shape
    else:
        k2, n = b.shape
    assert kk == k2, (a.shape, b.shape, ta, tb)
    tm, tn, tk = _pick(m, tm), _pick(n, tn), _pick(kk, tk)
    nk = kk // tk
    a_spec = pl.BlockSpec((tk, tm), lambda i, j, k: (k, i)) if ta else pl.BlockSpec((tm, tk), lambda i, j, k: (i, k))
    b_spec = pl.BlockSpec((tn, tk), lambda i, j, k: (j, k)) if tb else pl.BlockSpec((tk, tn), lambda i, j, k: (k, j))
    dims = (((0 if ta else 1,), (1 if tb else 0,)), ((), ()))
    in_specs, args = [a_spec, b_spec], [a, b]
    if gate is not None:
        in_specs.append(pl.BlockSpec((1, tn), lambda i, j, k: (0, j)))
        args.append(gate)
    if resid is not None:
        in_specs.append(pl.BlockSpec((tm, tn), lambda i, j, k: (i, j)))
        args.append(resid)
    o_spec = pl.BlockSpec((tm, tn), lambda i, j, k: (i, j))
    out_shape, out_specs = [jax.ShapeDtypeStruct((m, n), out_dtype)], [o_spec]
    if pre_dtype is not None:
        out_shape.insert(0, jax.ShapeDtypeStruct((m, n), pre_dtype))
        out_specs.insert(0, o_spec)
    n_in = len(args)
    n_out = len(out_shape)

    def body(*refs):
        a_ref, b_ref = refs[0], refs[1]
        extra = list(refs[2:n_in])
        outs = refs[n_in:n_in + n_out]
        gate_ref = extra.pop(0) if gate is not None else None
        resid_ref = extra.pop(0) if resid is not None else None

        def product():
            return lax.dot_general(a_ref[...].astype(BF), b_ref[...].astype(BF), dims, preferred_element_type=F32)

        def finish(r):
            if pre_dtype is not None:
                outs[0][...] = r.astype(pre_dtype)
            if gate_ref is not None:
                r = r * gate_ref[...]
            if resid_ref is not None:
                r = resid_ref[...] + r
            outs[-1][...] = r.astype(out_dtype)

        if nk == 1:
            finish(product())
        else:
            acc = refs[n_in + n_out]
            k = pl.program_id(2)

            @pl.when(k == 0)
            def _():
                acc[...] = jnp.zeros_like(acc)

            acc[...] += product()

            @pl.when(k == nk - 1)
            def _():
                finish(acc[...])

    res = pl.pallas_call(
        body, name=name, grid=(m // tm, n // tn, nk),
        in_specs=in_specs, out_specs=out_specs, out_shape=out_shape,
        scratch_shapes=[pltpu.VMEM((tm, tn), F32)] if nk > 1 else [],
        compiler_params=_params(("parallel", "parallel", "arbitrary")),
    )(*args)
    return res if pre_dtype is not None else res[0]


def _ada_fwd(cond16, w, name):
    nl, d, n = w.shape

    def body(c_ref, w_ref, o_ref):
        o_ref[...] = jnp.dot(c_ref[...].astype(BF), w_ref[...].astype(BF), preferred_element_type=F32)

    return pl.pallas_call(
        body, name=name, grid=(nl,),
        in_specs=[pl.BlockSpec((16, d), lambda l: (0, 0)), pl.BlockSpec((None, d, n), lambda l: (l, 0, 0))],
        out_specs=pl.BlockSpec((None, 16, n), lambda l: (l, 0, 0)),
        out_shape=jax.ShapeDtypeStruct((nl, 16, n), F32),
        compiler_params=_params(("parallel",)),
    )(cond16, w)


def _ada_bwd(cond16, dmod, name):
    nl, _, n = dmod.shape
    d = cond16.shape[1]

    def body(c_ref, g_ref, o_ref):
        o_ref[...] = lax.dot_general(c_ref[...].astype(BF), g_ref[...].astype(BF), (((0,), (0,)), ((), ())),
                                     preferred_element_type=F32)

    return pl.pallas_call(
        body, name=name, grid=(nl,),
        in_specs=[pl.BlockSpec((16, d), lambda l: (0, 0)), pl.BlockSpec((None, 16, n), lambda l: (l, 0, 0))],
        out_specs=pl.BlockSpec((None, d, n), lambda l: (l, 0, 0)),
        out_shape=jax.ShapeDtypeStruct((nl, d, n), F32),
        compiler_params=_params(("parallel",)),
    )(cond16, dmod)


def _row_spec(tm, d):
    return pl.BlockSpec((tm, d), lambda i: (i, 0))


def _vec_spec(d):
    return pl.BlockSpec((1, d), lambda i: (0, 0))


def _norm_mod(x, g, sc, sh, name):
    s, d = x.shape
    tm = _pick(s, 512, 8)

    def body(x_ref, g_ref, sc_ref, sh_ref, o_ref):
        xv = x_ref[...]
        r = lax.rsqrt(jnp.mean(xv * xv, axis=-1, keepdims=True) + EPS)
        y = (xv * r) * g_ref[...]
        o_ref[...] = (y * (1.0 + sc_ref[...]) + sh_ref[...]).astype(BF)

    return pl.pallas_call(
        body, name=name, grid=(s // tm,),
        in_specs=[_row_spec(tm, d), _vec_spec(d), _vec_spec(d), _vec_spec(d)],
        out_specs=_row_spec(tm, d), out_shape=jax.ShapeDtypeStruct((s, d), BF),
        compiler_params=_params(("parallel",)),
    )(x, g, sc, sh)


def _norm_mod_bwd(x, dh, g, sc, dx_in, name):
    s, d = x.shape
    tm = _pick(s, 512, 8)

    def body(x_ref, dh_ref, g_ref, sc_ref, dxin_ref, dx_ref, dsh_ref, dw_ref):
        i = pl.program_id(0)
        xv = x_ref[...]
        dhv = dh_ref[...].astype(F32)
        r = lax.rsqrt(jnp.mean(xv * xv, axis=-1, keepdims=True) + EPS)
        xn = xv * r
        dxn = dhv * (g_ref[...] * (1.0 + sc_ref[...]))
        dx_ref[...] = dxin_ref[...] + r * (dxn - xn * jnp.mean(dxn * xn, axis=-1, keepdims=True))

        @pl.when(i == 0)
        def _():
            dsh_ref[...] = jnp.zeros_like(dsh_ref)
            dw_ref[...] = jnp.zeros_like(dw_ref)

        dsh_ref[...] += jnp.sum(dhv, axis=0, keepdims=True)
        dw_ref[...] += jnp.sum(dhv * xn, axis=0, keepdims=True)

    return pl.pallas_call(
        body, name=name, grid=(s // tm,),
        in_specs=[_row_spec(tm, d), _row_spec(tm, d), _vec_spec(d), _vec_spec(d), _row_spec(tm, d)],
        out_specs=[_row_spec(tm, d), _vec_spec(d), _vec_spec(d)],
        out_shape=[jax.ShapeDtypeStruct((s, d), F32), jax.ShapeDtypeStruct((1, d), F32),
                   jax.ShapeDtypeStruct((1, d), F32)],
        compiler_params=_params(("arbitrary",)),
    )(x, dh, g, sc, dx_in)


def _gate_bwd(dx, y, gate, name):
    s, d = dx.shape
    tm = _pick(s, 512, 8)

    def body(dx_ref, y_ref, g_ref, dy_ref, dg_ref):
        i = pl.program_id(0)
        dxv = dx_ref[...]
        dy_ref[...] = (dxv * g_ref[...]).astype(BF)

        @pl.when(i == 0)
        def _():
            dg_ref[...] = jnp.zeros_like(dg_ref)

        dg_ref[...] += jnp.sum(dxv * y_ref[...].astype(F32), axis=0, keepdims=True)

    return pl.pallas_call(
        body, name=name, grid=(s // tm,),
        in_specs=[_row_spec(tm, d), _row_spec(tm, d), _vec_spec(d)],
        out_specs=[_row_spec(tm, d), _vec_spec(d)],
        out_shape=[jax.ShapeDtypeStruct((s, d), BF), jax.ShapeDtypeStruct((1, d), F32)],
        compiler_params=_params(("arbitrary",)),
    )(dx, y, gate)


def _final_loss(x, tgt, g, name):
    s, d = x.shape
    tm = _pick(s, 512, 8)

    def body(x_ref, t_ref, g_ref, dx_ref, loss_ref, dg_ref):
        i = pl.program_id(0)
        xv = x_ref[...]
        gv = g_ref[...]
        r = lax.rsqrt(jnp.mean(xv * xv, axis=-1, keepdims=True) + EPS)
        xn = xv * r
        err = xn * gv - t_ref[...]
        dy = err * (1.0 / d)
        dxn = dy * gv
        dx_ref[...] = r * (dxn - xn * jnp.mean(dxn * xn, axis=-1, keepdims=True))

        @pl.when(i == 0)
        def _():
            loss_ref[...] = jnp.zeros_like(loss_ref)
            dg_ref[...] = jnp.zeros_like(dg_ref)

        part = 0.5 * jnp.sum(jnp.sum(err * err, axis=-1, keepdims=True) * (1.0 / d), axis=0, keepdims=True)
        loss_ref[...] += jnp.broadcast_to(part, loss_ref.shape)
        dg_ref[...] += jnp.sum(dy * xn, axis=0, keepdims=True)

    return pl.pallas_call(
        body, name=name, grid=(s // tm,),
        in_specs=[_row_spec(tm, d), _row_spec(tm, d), _vec_spec(d)],
        out_specs=[_row_spec(tm, d), pl.BlockSpec((8, LANES), lambda i: (0, 0)), _vec_spec(d)],
        out_shape=[jax.ShapeDtypeStruct((s, d), F32), jax.ShapeDtypeStruct((8, LANES), F32),
                   jax.ShapeDtypeStruct((1, d), F32)],
        compiler_params=_params(("arbitrary",)),
    )(x, tgt, g)


def _pool_counts(tm, gd, row0, w):
    t = lax.broadcasted_iota(jnp.int32, (tm, gd), 0) + row0
    return jnp.minimum(t + 1, w).astype(F32)


def _pool_fwd(u, name):
    s, d = u.shape
    tm = _pick(s, 256, POOL_HALO)
    gd = d // len(POOL_WINDOWS)
    per = tm // POOL_HALO

    def body(prev_ref, cur_ref, o_ref, ext):
        i = pl.program_id(0)
        ext[0:POOL_HALO, :] = jnp.where(i > 0, prev_ref[...], 0.0)
        ext[POOL_HALO:, :] = cur_ref[...]
        for g, w in enumerate(POOL_WINDOWS):
            cols = slice(g * gd, (g + 1) * gd)
            acc = ext[POOL_HALO:POOL_HALO + tm, cols]
            own = acc
            for k in range(1, w):
                acc = acc + ext[POOL_HALO - k:POOL_HALO - k + tm, cols]
            o_ref[:, cols] = (acc / _pool_counts(tm, gd, i * tm, w) - own).astype(BF)

    return pl.pallas_call(
        body, name=name, grid=(s // tm,),
        in_specs=[pl.BlockSpec((POOL_HALO, d), lambda i: (jnp.maximum(i * per - 1, 0), 0)), _row_spec(tm, d)],
        out_specs=_row_spec(tm, d), out_shape=jax.ShapeDtypeStruct((s, d), BF),
        scratch_shapes=[pltpu.VMEM((tm + POOL_HALO, d), F32)],
        compiler_params=_params(("parallel",)),
    )(u, u)


def _pool_bwd(dp, name):
    s, d = dp.shape
    tm = _pick(s, 256, POOL_HALO)
    gd = d // len(POOL_WINDOWS)
    per = tm // POOL_HALO
    nt = s // tm
    last_halo = s // POOL_HALO - 1

    def body(cur_ref, nxt_ref, o_ref, ext):
        i = pl.program_id(0)
        for g, w in enumerate(POOL_WINDOWS):
            cols = slice(g * gd, (g + 1) * gd)
            ext[0:tm, cols] = cur_ref[:, cols].astype(F32) / _pool_counts(tm, gd, i * tm, w)
            nxt = nxt_ref[:, cols].astype(F32) / _pool_counts(POOL_HALO, gd, (i + 1) * tm, w)
            ext[tm:, cols] = jnp.where(i < nt - 1, nxt, 0.0)
        for g, w in enumerate(POOL_WINDOWS):
            cols = slice(g * gd, (g + 1) * gd)
            acc = ext[0:tm, cols]
            for k in range(1, w):
                acc = acc + ext[k:k + tm, cols]
            o_ref[:, cols] = (acc - cur_ref[:, cols].astype(F32)).astype(BF)

    return pl.pallas_call(
        body, name=name, grid=(nt,),
        in_specs=[_row_spec(tm, d), pl.BlockSpec((POOL_HALO, d), lambda i: (jnp.minimum((i + 1) * per, last_halo), 0))],
        out_specs=_row_spec(tm, d), out_shape=jax.ShapeDtypeStruct((s, d), BF),
        scratch_shapes=[pltpu.VMEM((tm + POOL_HALO, d), F32)],
        compiler_params=_params(("parallel",)),
    )(dp, dp)


def _grp_fwd(p, w, scale, name):
    s, d = p.shape
    ng, gd, _ = w.shape
    tm = _pick(s, 1024, 8)

    def body(p_ref, w_ref, s_ref, z_ref, y_ref):
        z = jnp.dot(p_ref[...], w_ref[...].astype(BF), preferred_element_type=F32)
        z_ref[...] = z.astype(BF)
        y_ref[...] = (z * s_ref[...]).astype(BF)

    blk = pl.BlockSpec((tm, gd), lambda i, g: (i, g))
    return pl.pallas_call(
        body, name=name, grid=(s // tm, ng),
        in_specs=[blk, pl.BlockSpec((None, gd, gd), lambda i, g: (g, 0, 0)), pl.BlockSpec((1, gd), lambda i, g: (0, g))],
        out_specs=[blk, blk],
        out_shape=[jax.ShapeDtypeStruct((s, d), BF), jax.ShapeDtypeStruct((s, d), BF)],
        compiler_params=_params(("parallel", "parallel")),
    )(p, w, scale)


def _grp_bwd(dy, z, w, scale, name):
    s, d = dy.shape
    ng, gd, _ = w.shape
    tm = _pick(s, 1024, 8)

    def body(dy_ref, z_ref, w_ref, s_ref, dz_ref, dp_ref, ds_ref):
        i = pl.program_id(1)
        dyv = dy_ref[...].astype(F32)
        dz = (dyv * s_ref[...]).astype(BF)
        dz_ref[...] = dz
        dp_ref[...] = lax.dot_general(dz, w_ref[...].astype(BF), (((1,), (1,)), ((), ())),
                                      preferred_element_type=F32).astype(BF)

        @pl.when(i == 0)
        def _():
            ds_ref[...] = jnp.zeros_like(ds_ref)

        ds_ref[...] += jnp.sum(dyv * z_ref[...].astype(F32), axis=0, keepdims=True)

    blk = pl.BlockSpec((tm, gd), lambda g, i: (i, g))
    vec = pl.BlockSpec((1, gd), lambda g, i: (0, g))
    return pl.pallas_call(
        body, name=name, grid=(ng, s // tm),
        in_specs=[blk, blk, pl.BlockSpec((None, gd, gd), lambda g, i: (g, 0, 0)), vec],
        out_specs=[blk, blk, vec],
        out_shape=[jax.ShapeDtypeStruct((s, d), BF), jax.ShapeDtypeStruct((s, d), BF),
                   jax.ShapeDtypeStruct((1, d), F32)],
        compiler_params=_params(("parallel", "arbitrary")),
    )(dy, z, w, scale)


def _grp_dw(p, dz, ng, name):
    s, d = p.shape
    gd = d // ng
    tk = _pick(s, 1024, 8)

    def body(p_ref, dz_ref, o_ref):
        k = pl.program_id(1)

        @pl.when(k == 0)
        def _():
            o_ref[...] = jnp.zeros_like(o_ref)

        o_ref[...] += lax.dot_general(p_ref[...], dz_ref[...], (((0,), (0,)), ((), ())), preferred_element_type=F32)

    blk = pl.BlockSpec((tk, gd), lambda g, k: (k, g))
    return pl.pallas_call(
        body, name=name, grid=(ng, s // tk),
        in_specs=[blk, blk], out_specs=pl.BlockSpec((None, gd, gd), lambda g, k: (g, 0, 0)),
        out_shape=jax.ShapeDtypeStruct((ng, gd, gd), F32),
        compiler_params=_params(("parallel", "arbitrary")),
    )(p, dz)


def _sigmoid(a):
    return 1.0 / (1.0 + jnp.exp(-a))


def _ffn_act(up, cw, cb, name):
    s, f2 = up.shape
    f = f2 // 2
    tm = _pick(s, 512, CONV_HALO)
    tn = _pick(f, 256)
    nj = f // tn
    per = tm // CONV_HALO

    def body(prev_ref, a_ref, v_ref, w_ref, b_ref, o_ref, ext):
        i = pl.program_id(0)
        ext[0:CONV_HALO, :] = jnp.where(i > 0, prev_ref[...].astype(F32), 0.0)
        ext[CONV_HALO:, :] = a_ref[...].astype(F32)
        a2 = b_ref[...] + ext[CONV_HALO - 2:CONV_HALO - 2 + tm, :] * w_ref[0:1, :]
        a2 = a2 + ext[CONV_HALO - 1:CONV_HALO - 1 + tm, :] * w_ref[1:2, :]
        a2 = a2 + ext[CONV_HALO:CONV_HALO + tm, :] * w_ref[2:3, :]
        o_ref[...] = (a2 * _sigmoid(a2) * v_ref[...].astype(F32)).astype(BF)

    return pl.pallas_call(
        body, name=name, grid=(s // tm, nj),
        in_specs=[pl.BlockSpec((CONV_HALO, tn), lambda i, j: (jnp.maximum(i * per - 1, 0), j)),
                  pl.BlockSpec((tm, tn), lambda i, j: (i, j)),
                  pl.BlockSpec((tm, tn), lambda i, j: (i, j + nj)),
                  pl.BlockSpec((CONV_WIDTH, tn), lambda i, j: (0, j)),
                  pl.BlockSpec((1, tn), lambda i, j: (0, j))],
        out_specs=pl.BlockSpec((tm, tn), lambda i, j: (i, j)),
        out_shape=jax.ShapeDtypeStruct((s, f), BF),
        scratch_shapes=[pltpu.VMEM((tm + CONV_HALO, tn), F32)],
        compiler_params=_params(("parallel", "parallel")),
    )(up, up, up, cw, cb)


def _ffn_act_bwd(up, dact, cw, cb, name):
    s, f2 = up.shape
    f = f2 // 2
    tm = _pick(s, 512, CONV_HALO)
    tn = _pick(f, 256)
    nj = f // tn
    per = tm // CONV_HALO
    nt = s // tm
    last_halo = s // CONV_HALO - 1
    h = CONV_HALO
    te = tm + h

    def body(ap_ref, a_ref, an_ref, v_ref, vn_ref, d_ref, dn_ref, w_ref, b_ref, da_ref, dv_ref, dc_ref,
             ext_a, dap):
        i = pl.program_id(1)
        ext_a[0:h, :] = jnp.where(i > 0, ap_ref[...].astype(F32), 0.0)
        ext_a[h:h + tm, :] = a_ref[...].astype(F32)
        ext_a[h + tm:, :] = an_ref[...].astype(F32)
        w0, w1, w2 = w_ref[0:1, :], w_ref[1:2, :], w_ref[2:3, :]
        a2 = b_ref[...] + ext_a[h - 2:h - 2 + te, :] * w0 + ext_a[h - 1:h - 1 + te, :] * w1 + ext_a[h:h + te, :] * w2
        sig = _sigmoid(a2)
        dsilu = sig * (1.0 + a2 * (1.0 - sig))
        d_cur = d_ref[...].astype(F32)
        d_nxt = jnp.where(i < nt - 1, dn_ref[...].astype(F32), 0.0)
        dap[0:tm, :] = d_cur * v_ref[...].astype(F32) * dsilu[0:tm, :]
        dap[tm:, :] = d_nxt * vn_ref[...].astype(F32) * dsilu[tm:, :]
        dv_ref[...] = (d_cur * (a2[0:tm, :] * sig[0:tm, :])).astype(BF)
        g0 = dap[0:tm, :]
        da_ref[...] = (dap[2:2 + tm, :] * w0 + dap[1:1 + tm, :] * w1 + g0 * w2).astype(BF)

        @pl.when(i == 0)
        def _():
            dc_ref[...] = jnp.zeros_like(dc_ref)

        dc_ref[0:1, :] += jnp.sum(g0 * ext_a[h - 2:h - 2 + tm, :], axis=0, keepdims=True)
        dc_ref[1:2, :] += jnp.sum(g0 * ext_a[h - 1:h - 1 + tm, :], axis=0, keepdims=True)
        dc_ref[2:3, :] += jnp.sum(g0 * ext_a[h:h + tm, :], axis=0, keepdims=True)
        dc_ref[3:4, :] += jnp.sum(g0, axis=0, keepdims=True)

    def cur(off):
        return pl.BlockSpec((tm, tn), lambda j, i: (i, j + off))

    def nxt(off):
        return pl.BlockSpec((h, tn), lambda j, i: (jnp.minimum((i + 1) * per, last_halo), j + off))

    da, dv, dc = pl.pallas_call(
        body, name=name, grid=(nj, nt),
        in_specs=[pl.BlockSpec((h, tn), lambda j, i: (jnp.maximum(i * per - 1, 0), j)), cur(0), nxt(0),
                  cur(nj), nxt(nj), cur(0), nxt(0),
                  pl.BlockSpec((CONV_WIDTH, tn), lambda j, i: (0, j)), pl.BlockSpec((1, tn), lambda j, i: (0, j))],
        out_specs=[cur(0), cur(0), pl.BlockSpec((8, tn), lambda j, i: (0, j))],
        out_shape=[jax.ShapeDtypeStruct((s, f), BF), jax.ShapeDtypeStruct((s, f), BF),
                   jax.ShapeDtypeStruct((8, f), F32)],
        scratch_shapes=[pltpu.VMEM((tm + 2 * h, tn), F32), pltpu.VMEM((te, tn), F32)],
        compiler_params=_params(("parallel", "arbitrary")),
    )(up, up, up, up, up, dact, dact, cw, cb)
    return da, dv, dc


def _band(blk, n_steps, dil, first):
    qi = lax.broadcasted_iota(jnp.int32, (blk, 2 * blk), 0) + blk
    ki = lax.broadcasted_iota(jnp.int32, (blk, 2 * blk), 1)
    delta = qi - ki
    valid = (delta >= 0) & (delta <= n_steps) & ((ki >= blk) | jnp.logical_not(first))
    return valid, (delta * dil).astype(F32)


def _attn_fwd(q_all, kv, g, slopes, d, name):
    window, dil = BRANCHES[g]
    n_steps = window // dil
    blk = max(ATTN_BLOCK, n_steps)
    s = q_all.shape[0]
    sub = s // dil
    nb = sub // blk
    assert nb * blk == sub
    nh = d // HEAD_DIM
    nbr = len(BRANCHES)
    qv = q_all.reshape(sub, dil * nbr * d)
    kvv = kv.reshape(sub, dil * 2 * nbr * d)
    scale = HEAD_DIM ** -0.5

    def body(q_ref, kp_ref, kc_ref, vp_ref, vc_ref, o_ref, l_ref):
        j = pl.program_id(1)
        valid, dist = _band(blk, n_steps, dil, j == 0)
        qb = q_ref[...]
        kb = jnp.concatenate([kp_ref[...], kc_ref[...]], axis=0)
        vb = jnp.concatenate([vp_ref[...], vc_ref[...]], axis=0)
        for h in range(nh):
            sl = slice(h * HEAD_DIM, (h + 1) * HEAD_DIM)
            sc = lax.dot_general(qb[:, sl], kb[:, sl], (((1,), (1,)), ((), ())), preferred_element_type=F32) * scale
            sc = jnp.where(valid, sc - float(slopes[h]) * dist, NEG)
            m = jnp.max(sc, axis=-1, keepdims=True)
            p = jnp.exp(sc - m)
            den = jnp.sum(p, axis=-1, keepdims=True)
            o = jnp.dot(p.astype(BF), vb[:, sl], preferred_element_type=F32) / den
            o_ref[:, sl] = o.astype(BF)
            l_ref[:, sl] = jnp.broadcast_to(m + jnp.log(den), (blk, HEAD_DIM))

    def spec(col, prev):
        if prev:
            return pl.BlockSpec((blk, d), lambda r, j: (jnp.maximum(j - 1, 0), r * col[0] + col[1]))
        return pl.BlockSpec((blk, d), lambda r, j: (j, r * col[0] + col[1]))

    qcol, kcol, vcol = (nbr, g), (2 * nbr, g), (2 * nbr, nbr + g)
    ospec = pl.BlockSpec((blk, d), lambda r, j: (j, r))
    o, lse = pl.pallas_call(
        body, name=name, grid=(dil, nb),
        in_specs=[spec(qcol, False), spec(kcol, True), spec(kcol, False), spec(vcol, True), spec(vcol, False)],
        out_specs=[ospec, ospec],
        out_shape=[jax.ShapeDtypeStruct((sub, dil * d), BF), jax.ShapeDtypeStruct((sub, dil * d), F32)],
        compiler_params=_params(("parallel", "parallel")),
    )(qv, kvv, kvv, kvv, kvv)
    return o.reshape(s, d), lse.reshape(s, d)


def _attn_combine(os, lses, name):
    s, d = os[0].shape
    tm = _pick(s, 512, 8)
    nbr = len(os)

    def body(*refs):
        o_refs, l_refs = refs[:nbr], refs[nbr:2 * nbr]
        o_ref, lt_ref = refs[2 * nbr], refs[2 * nbr + 1]
        ls = [r[...] for r in l_refs]
        m = ls[0]
        for v in ls[1:]:
            m = jnp.maximum(m, v)
        tot = jnp.exp(ls[0] - m)
        for v in ls[1:]:
            tot = tot + jnp.exp(v - m)
        lt = m + jnp.log(tot)
        acc = jnp.exp(ls[0] - lt) * o_refs[0][...].astype(F32)
        for v, r in zip(ls[1:], o_refs[1:]):
            acc = acc + jnp.exp(v - lt) * r[...].astype(F32)
        o_ref[...] = acc.astype(BF)
        lt_ref[...] = lt

    return pl.pallas_call(
        body, name=name, grid=(s // tm,),
        in_specs=[_row_spec(tm, d)] * (2 * nbr), out_specs=[_row_spec(tm, d), _row_spec(tm, d)],
        out_shape=[jax.ShapeDtypeStruct((s, d), BF), jax.ShapeDtypeStruct((s, d), F32)],
        compiler_params=_params(("parallel",)),
    )(*os, *lses)


def _attn_bwd(q_all, kv, do, o, lt, g, slopes, d, name, dk_in=None, dv_in=None):
    window, dil = BRANCHES[g]
    n_steps = window // dil
    blk = max(ATTN_BLOCK, n_steps)
    s = q_all.shape[0]
    sub = s // dil
    nb = sub // blk
    nh = d // HEAD_DIM
    nbr = len(BRANCHES)
    qv = q_all.reshape(sub, dil * nbr * d)
    kvv = kv.reshape(sub, dil * 2 * nbr * d)
    scale = HEAD_DIM ** -0.5
    acc_in = dk_in is not None

    def body(*refs):
        q_ref, do_ref, o_ref, lt_ref, kp_ref, kc_ref, vp_ref, vc_ref = refs[:8]
        n_in = 10 if acc_in else 8
        dkin_ref, dvin_ref = (refs[8], refs[9]) if acc_in else (None, None)
        dq_ref, dk_ref, dv_ref, keep_k, keep_v, part_k, part_v = refs[n_in:n_in + 7]
        t = pl.program_id(1)

        def emit(prev_k, prev_v):
            if acc_in:
                prev_k = prev_k + dkin_ref[...].astype(F32)
                prev_v = prev_v + dvin_ref[...].astype(F32)
            dk_ref[...] = prev_k.astype(BF)
            dv_ref[...] = prev_v.astype(BF)

        @pl.when(t < nb)
        def _():
            valid, dist = _band(blk, n_steps, dil, t == 0)
            qb = q_ref[...]
            dob = do_ref[...]
            kb = jnp.concatenate([kp_ref[...], kc_ref[...]], axis=0)
            vb = jnp.concatenate([vp_ref[...], vc_ref[...]], axis=0)
            for h in range(nh):
                sl = slice(h * HEAD_DIM, (h + 1) * HEAD_DIM)
                qh, kh, vh, doh = qb[:, sl], kb[:, sl], vb[:, sl], dob[:, sl]
                sc = lax.dot_general(qh, kh, (((1,), (1,)), ((), ())), preferred_element_type=F32) * scale
                sc = sc - float(slopes[h]) * dist
                p = jnp.where(valid, jnp.exp(jnp.minimum(sc - lt_ref[:, sl][:, 0:1], 30.0)), 0.0)
                dlt = jnp.sum(doh.astype(F32) * o_ref[:, sl].astype(F32), axis=-1, keepdims=True)
                dp = lax.dot_general(doh, vh, (((1,), (1,)), ((), ())), preferred_element_type=F32)
                ds = (p * (dp - dlt)).astype(BF)
                dq_ref[:, sl] = (jnp.dot(ds, kh, preferred_element_type=F32) * scale).astype(BF)
                part_k[:, sl] = lax.dot_general(ds, qh, (((0,), (0,)), ((), ())), preferred_element_type=F32) * scale
                part_v[:, sl] = lax.dot_general(p.astype(BF), doh, (((0,), (0,)), ((), ())),
                                                preferred_element_type=F32)

            @pl.when(t > 0)
            def _():
                emit(keep_k[...] + part_k[0:blk, :], keep_v[...] + part_v[0:blk, :])

            keep_k[...] = part_k[blk:, :]
            keep_v[...] = part_v[blk:, :]

        @pl.when(t == nb)
        def _():
            emit(keep_k[...], keep_v[...])

    def qspec(col):
        return pl.BlockSpec((blk, d), lambda r, t: (jnp.minimum(t, nb - 1), r * col[0] + col[1]))

    def kspec(col, prev):
        if prev:
            return pl.BlockSpec((blk, d), lambda r, t: (jnp.maximum(jnp.minimum(t, nb - 1) - 1, 0), r * col[0] + col[1]))
        return qspec(col)

    kout = pl.BlockSpec((blk, d), lambda r, t: (jnp.maximum(t - 1, 0), r))
    qcol, kcol, vcol, one = (nbr, g), (2 * nbr, g), (2 * nbr, nbr + g), (1, 0)
    in_specs = [qspec(qcol), qspec(one), qspec(one), qspec(one),
                kspec(kcol, True), kspec(kcol, False), kspec(vcol, True), kspec(vcol, False)]
    args = [qv, do.reshape(sub, dil * d), o.reshape(sub, dil * d), lt.reshape(sub, dil * d), kvv, kvv, kvv, kvv]
    if acc_in:
        in_specs += [kout, kout]
        args += [dk_in.reshape(sub, dil * d), dv_in.reshape(sub, dil * d)]
    shp = jax.ShapeDtypeStruct((sub, dil * d), BF)
    dq, dk, dv = pl.pallas_call(
        body, name=name, grid=(dil, nb + 1),
        in_specs=in_specs, out_specs=[qspec(one), kout, kout], out_shape=[shp, shp, shp],
        scratch_shapes=[pltpu.VMEM((blk, d), F32), pltpu.VMEM((blk, d), F32),
                        pltpu.VMEM((2 * blk, d), F32), pltpu.VMEM((2 * blk, d), F32)],
        compiler_params=_params(("parallel", "arbitrary")),
    )(*args)
    return dq.reshape(s, d), dk.reshape(s, d), dv.reshape(s, d)


def _adamw(parts, w, m, v, name):
    npart, r, c = parts.shape
    tr = _pick(r, 256, 8)
    c1 = 1.0 / (1.0 - ADAM_B1 ** ADAM_STEP)
    c2 = 1.0 / (1.0 - ADAM_B2 ** ADAM_STEP)

    def body(p_ref, w_ref, m_ref, v_ref, g_ref, d_ref, nm_ref, nv_ref):
        g = p_ref[0].astype(F32)
        for k in range(1, npart):
            g = g + p_ref[k].astype(F32)
        nm = ADAM_B1 * m_ref[...] + (1.0 - ADAM_B1) * g
        nv = ADAM_B2 * v_ref[...] + (1.0 - ADAM_B2) * (g * g)
        g_ref[...] = g
        nm_ref[...] = nm
        nv_ref[...] = nv
        d_ref[...] = -ADAM_LR * ((nm * c1) / (jnp.sqrt(nv * c2) + ADAM_EPS) + ADAM_WD * w_ref[...])

    blk = pl.BlockSpec((tr, c), lambda i: (i, 0))
    shp = jax.ShapeDtypeStruct((r, c), F32)
    return pl.pallas_call(
        body, name=name, grid=(r // tr,),
        in_specs=[pl.BlockSpec((npart, tr, c), lambda i: (0, i, 0)), blk, blk, blk],
        out_specs=[blk, blk, blk, blk], out_shape=[shp, shp, shp, shp],
        compiler_params=_params(("parallel",)),
    )(parts, w, m, v)


_SHARD_AXIS = {"pool_w_in": 1, "pool_w_grp": 2, "pool_w_out": 1, "w_kv": 1, "attn_w_q": 2, "attn_w_o": 1,
               "ffn_w_up": 2, "ffn_w_down": 1}


def _full_from_slots(slots, shard_shape, axis):
    a = slots.reshape((N_DEV,) + tuple(shard_shape))
    a = jnp.moveaxis(a, 0, axis)
    full = list(shard_shape)
    full[axis] *= N_DEV
    return a.reshape(full)


def _slots_from_full(full, axis):
    shp = list(full.shape)
    shp[axis:axis + 1] = [N_DEV, shp[axis] // N_DEV]
    a = jnp.moveaxis(full.reshape(shp), axis, 0)
    return a.reshape(N_DEV, -1)


def kernel(x, c, ada_w, ada_b, norm1_g, norm2_g, pool_w_in, pool_w_grp, pool_scale, pool_w_out, kv_norm_g, kv_ada_w, kv_ada_b, w_kv, attn_w_q, attn_w_o, ffn_w_up, ffn_conv_w, ffn_conv_b, ffn_w_down, final_g, loss_target, m_ada_w, m_ada_b, m_norm1_g, m_norm2_g, m_pool_w_in, m_pool_w_grp, m_pool_scale, m_pool_w_out, m_kv_norm_g, m_kv_ada_w, m_kv_ada_b, m_w_kv, m_attn_w_q, m_attn_w_o, m_ffn_w_up, m_ffn_conv_w, m_ffn_conv_b, m_ffn_w_down, m_final_g, v_ada_w, v_ada_b, v_norm1_g, v_norm2_g, v_pool_w_in, v_pool_w_grp, v_pool_scale, v_pool_w_out, v_kv_norm_g, v_kv_ada_w, v_kv_ada_b, v_w_kv, v_attn_w_q, v_attn_w_o, v_ffn_w_up, v_ffn_conv_w, v_ffn_conv_b, v_ffn_w_down, v_final_g):
    weights = dict(ada_w=ada_w, ada_b=ada_b, norm1_g=norm1_g, norm2_g=norm2_g, pool_w_in=pool_w_in,
                   pool_w_grp=pool_w_grp, pool_scale=pool_scale, pool_w_out=pool_w_out, kv_norm_g=kv_norm_g,
                   kv_ada_w=kv_ada_w, kv_ada_b=kv_ada_b, w_kv=w_kv, attn_w_q=attn_w_q, attn_w_o=attn_w_o,
                   ffn_w_up=ffn_w_up, ffn_conv_w=ffn_conv_w, ffn_conv_b=ffn_conv_b, ffn_w_down=ffn_w_down,
                   final_g=final_g)
    mom1 = dict(ada_w=m_ada_w, ada_b=m_ada_b, norm1_g=m_norm1_g, norm2_g=m_norm2_g, pool_w_in=m_pool_w_in,
                pool_w_grp=m_pool_w_grp, pool_scale=m_pool_scale, pool_w_out=m_pool_w_out, kv_norm_g=m_kv_norm_g,
                kv_ada_w=m_kv_ada_w, kv_ada_b=m_kv_ada_b, w_kv=m_w_kv, attn_w_q=m_attn_w_q, attn_w_o=m_attn_w_o,
                ffn_w_up=m_ffn_w_up, ffn_conv_w=m_ffn_conv_w, ffn_conv_b=m_ffn_conv_b, ffn_w_down=m_ffn_w_down,
                final_g=m_final_g)
    mom2 = dict(ada_w=v_ada_w, ada_b=v_ada_b, norm1_g=v_norm1_g, norm2_g=v_norm2_g, pool_w_in=v_pool_w_in,
                pool_w_grp=v_pool_w_grp, pool_scale=v_pool_scale, pool_w_out=v_pool_w_out, kv_norm_g=v_kv_norm_g,
                kv_ada_w=v_kv_ada_w, kv_ada_b=v_kv_ada_b, w_kv=v_w_kv, attn_w_q=v_attn_w_q, attn_w_o=v_attn_w_o,
                ffn_w_up=v_ffn_w_up, ffn_conv_w=v_ffn_conv_w, ffn_conv_b=v_ffn_conv_b, ffn_w_down=v_ffn_w_down,
                final_g=v_final_g)
    order = list(weights)

    seq, d = x.shape[1], x.shape[2]
    depth = ada_w.shape[0]
    n_pool = pool_w_in.shape[0]
    f = ffn_conv_b.shape[1]
    nbr = len(BRANCHES)
    nh = d // HEAD_DIM
    slopes = _alibi_slopes(nbr * nh).reshape(nbr, nh)
    me = 4 * lax.axis_index("x") + 2 * lax.axis_index("y") + lax.axis_index("c")
    xs = x[0]
    tgt = loss_target[0]

    cond = c * (1.0 / (1.0 + jnp.exp(-c)))
    small_in = jnp.concatenate([cond.reshape(-1), ffn_conv_w.reshape(-1), pool_scale.reshape(-1)])
    n_small_in = small_in.shape[0]
    gath = _all_gather(_to_rows(small_in), "gather_small").reshape(N_DEV, -1)[:, :n_small_in]
    cond_all = gath[:, :d]
    o1 = d + ffn_conv_w.size
    conv_w_full = _full_from_slots(gath[:, d:o1], ffn_conv_w.shape, 2)
    pool_scale_full = _full_from_slots(gath[:, o1:], pool_scale.shape, 1)
    cond16 = jnp.concatenate([cond_all, jnp.zeros_like(cond_all)], axis=0)

    mod_part = _ada_fwd(cond16, ada_w, "ada_fwd")[:, :N_DEV]
    kv_part = _ada_fwd(cond16, kv_ada_w[None], "kv_ada_fwd")[0, :N_DEV]
    n_mod = depth * mod_part.shape[2] + kv_part.shape[1]
    send = jnp.concatenate([jnp.moveaxis(mod_part, 1, 0).reshape(N_DEV, -1), kv_part], axis=1)
    send_rows = jax.vmap(_to_rows)(send)
    got = _all_to_all(send_rows, "exchange_mod").reshape(N_DEV, -1)[:, :n_mod]
    ncol = mod_part.shape[2]
    mods = []
    for l in range(depth):
        row = got[:, l * ncol:(l + 1) * ncol].reshape(1, -1) + ada_b[l][None]
        mods.append([row[:, k * d:(k + 1) * d] for k in range(6)])
    kv_row = got[:, depth * ncol:].reshape(1, -1) + kv_ada_b[None]
    kv_shift, kv_scale = kv_row[:, :d], kv_row[:, d:]

    big = list(_SHARD_AXIS)
    flat = jnp.concatenate([weights[n].astype(BF).reshape(-1) for n in big])
    n_flat = flat.shape[0]
    slots = _all_gather(_to_rows(flat), "gather_weights").reshape(N_DEV, -1)
    full = {}
    off = 0
    for n in big:
        size = weights[n].size
        full[n] = _full_from_slots(slots[:, off:off + size], weights[n].shape, _SHARD_AXIS[n])
        off += size

    def vec(a):
        return a.reshape(1, -1)

    saved = []
    xcur = xs
    kvs = None
    hkv = None
    x_kv = None
    for l in range(depth):
        sh1, sc1, g1, sh2, sc2, g2 = mods[l]
        st = dict(x=xcur)
        h = _norm_mod(xcur, vec(norm1_g[l]), sc1, sh1, f"norm1_{l}")
        st["h"] = h
        if l < n_pool:
            u = _mm(h, full["pool_w_in"][l], f"pool_in_{l}")
            pooled = _pool_fwd(u, f"pool_fwd_{l}")
            z, y = _grp_fwd(pooled, full["pool_w_grp"][l], vec(pool_scale_full[l]), f"grp_fwd_{l}")
            mix, x1 = _mm(y, full["pool_w_out"][l], f"pool_out_{l}", gate=g1, resid=xcur, pre_dtype=BF)
            st.update(pooled=pooled, z=z, y=y)
        else:
            j = l - n_pool
            if j == 0:
                x_kv = xcur
                hkv = _norm_mod(xcur, vec(kv_norm_g), kv_scale, kv_shift, "norm_kv")
                kvs = _mm(hkv, full["w_kv"], "kv_proj", out_dtype=BF)
            q = _mm(h, full["attn_w_q"][j], f"q_proj_{l}", out_dtype=BF)
            outs, lses = [], []
            for g in range(nbr):
                og, lg = _attn_fwd(q, kvs, g, slopes[g], d, f"attn_fwd_{l}_{g}")
                outs.append(og)
                lses.append(lg)
            o, lt = _attn_combine(outs, lses, f"attn_mix_{l}")
            mix, x1 = _mm(o, full["attn_w_o"][j], f"attn_out_{l}", gate=g1, resid=xcur, pre_dtype=BF)
            st.update(q=q, o=o, lt=lt)
        h2 = _norm_mod(x1, vec(norm2_g[l]), sc2, sh2, f"norm2_{l}")
        up = _mm(h2, full["ffn_w_up"][l], f"ffn_up_{l}", out_dtype=BF)
        act = _ffn_act(up, conv_w_full[l], vec(ffn_conv_b[l]), f"ffn_act_{l}")
        ffo, x2 = _mm(act, full["ffn_w_down"][l], f"ffn_down_{l}", gate=g2, resid=x1, pre_dtype=BF)
        st.update(mix=mix, x1=x1, h2=h2, up=up, act=act, ffo=ffo)
        saved.append(st)
        xcur = x2

    dx, loss_blk, d_final_g = _final_loss(xcur, tgt, vec(final_g), "final_loss")
    loss = lax.psum(loss_blk[0, 0], ("x", "y", "c"))

    gfull = {n: [None] * weights[n].shape[0] for n in big if n != "w_kv"}
    d_mod = [None] * depth
    d_n1 = [None] * depth
    d_n2 = [None] * depth
    d_conv = [None] * depth
    d_pscale = [None] * n_pool
    dk_acc = [None] * nbr
    dv_acc = [None] * nbr
    for l in reversed(range(depth)):
        sh1, sc1, g1, sh2, sc2, g2 = mods[l]
        st = saved[l]
        dffo, dg2 = _gate_bwd(dx, st["ffo"], g2, f"gate2_bwd_{l}")
        gfull["ffn_w_down"][l] = _mm(st["act"], dffo, f"ffn_down_dw_{l}", ta=True)
        dact = _mm(dffo, full["ffn_w_down"][l], f"ffn_down_dx_{l}", tb=True, out_dtype=BF)
        da, dv_, dc = _ffn_act_bwd(st["up"], dact, conv_w_full[l], vec(ffn_conv_b[l]), f"ffn_act_bwd_{l}")
        dup = jnp.concatenate([da, dv_], axis=1)
        d_conv[l] = dc
        gfull["ffn_w_up"][l] = _mm(st["h2"], dup, f"ffn_up_dw_{l}", ta=True)
        dh2 = _mm(dup, full["ffn_w_up"][l], f"ffn_up_dx_{l}", tb=True)
        dx1, dsh2, dw2 = _norm_mod_bwd(st["x1"], dh2, vec(norm2_g[l]), sc2, dx, f"norm2_bwd_{l}")
        d_n2[l] = dw2 * (1.0 + sc2)
        dsc2 = dw2 * vec(norm2_g[l])

        dmix, dg1 = _gate_bwd(dx1, st["mix"], g1, f"gate1_bwd_{l}")
        if l < n_pool:
            gfull["pool_w_out"][l] = _mm(st["y"], dmix, f"pool_out_dw_{l}", ta=True)
            dy = _mm(dmix, full["pool_w_out"][l], f"pool_out_dx_{l}", tb=True, out_dtype=BF)
            dz, dpool, dps = _grp_bwd(dy, st["z"], full["pool_w_grp"][l], vec(pool_scale_full[l]), f"grp_bwd_{l}")
            d_pscale[l] = dps
            gfull["pool_w_grp"][l] = _grp_dw(st["pooled"], dz, len(POOL_WINDOWS), f"grp_dw_{l}")
            du = _pool_bwd(dpool, f"pool_bwd_{l}")
            gfull["pool_w_in"][l] = _mm(st["h"], du, f"pool_in_dw_{l}", ta=True)
            dh = _mm(du, full["pool_w_in"][l], f"pool_in_dx_{l}", tb=True)
        else:
            j = l - n_pool
            gfull["attn_w_o"][j] = _mm(st["o"], dmix, f"attn_out_dw_{l}", ta=True)
            do = _mm(dmix, full["attn_w_o"][j], f"attn_out_dx_{l}", tb=True, out_dtype=BF)
            dqs = []
            for g in range(nbr):
                dq_g, dk_g, dv_g = _attn_bwd(st["q"], kvs, do, st["o"], st["lt"], g, slopes[g], d,
                                             f"attn_bwd_{l}_{g}", dk_in=dk_acc[g], dv_in=dv_acc[g])
                dqs.append(dq_g)
                dk_acc[g], dv_acc[g] = dk_g, dv_g
            dq = jnp.concatenate(dqs, axis=1)
            gfull["attn_w_q"][j] = _mm(st["h"], dq, f"q_proj_dw_{l}", ta=True)
            dh = _mm(dq, full["attn_w_q"][j], f"q_proj_dx_{l}", tb=True)
        dx0, dsh1, dw1 = _norm_mod_bwd(st["x"], dh, vec(norm1_g[l]), sc1, dx1, f"norm1_bwd_{l}")
        d_n1[l] = dw1 * (1.0 + sc1)
        dsc1 = dw1 * vec(norm1_g[l])
        d_mod[l] = jnp.concatenate([dsh1, dsc1, dg1, dsh2, dsc2, dg2], axis=1)
        dx = dx0
        if l == n_pool:
            dkv = jnp.concatenate(dk_acc + dv_acc, axis=1)
            g_w_kv = _mm(hkv, dkv, "kv_proj_dw", ta=True)
            dhkv = _mm(dkv, full["w_kv"], "kv_proj_dx", tb=True)
            dx, dsh_kv, dw_kv = _norm_mod_bwd(x_kv, dhkv, vec(kv_norm_g), kv_scale, dx, "norm_kv_bwd")
            d_kv_norm = dw_kv * (1.0 + kv_scale)
            d_kv_mod = jnp.concatenate([dsh_kv, dw_kv * vec(kv_norm_g)], axis=1)
    grad_x = dx[None]

    small = [jnp.concatenate(d_mod, axis=1).reshape(-1), d_kv_mod.reshape(-1),
             jnp.concatenate(d_n1, axis=0).reshape(-1), jnp.concatenate(d_n2, axis=0).reshape(-1),
             d_kv_norm.reshape(-1), d_final_g.reshape(-1),
             jnp.stack([dcl[3] for dcl in d_conv]).reshape(-1),
             jnp.stack([dcl[0:CONV_WIDTH] for dcl in d_conv]).reshape(-1),
             jnp.concatenate(d_pscale, axis=0).reshape(-1)]
    sizes = [a.shape[0] for a in small]
    small_rows = _to_rows(jnp.concatenate(small))
    small_all = _all_gather(small_rows, "gather_small_grads")
    dmod_all = small_all.reshape(N_DEV, -1)[:, :sizes[0] + sizes[1]]

    dmod16 = jnp.concatenate([dmod_all, jnp.zeros_like(dmod_all)], axis=0)
    dm = dmod16[:, :sizes[0]].reshape(16, depth, N_DEV, ncol)
    dm_mine = lax.dynamic_index_in_dim(dm, me, axis=2, keepdims=False)
    g_ada_w = _ada_bwd(cond16, jnp.moveaxis(dm_mine, 0, 1), "ada_bwd")
    nkv = kv_part.shape[1]
    dkm = dmod16[:, sizes[0]:].reshape(16, N_DEV, nkv)
    dkm_mine = lax.dynamic_index_in_dim(dkm, me, axis=1, keepdims=False)
    g_kv_ada_w = _ada_bwd(cond16, dkm_mine[None], "kv_ada_bwd")[0]

    gfull_arr = {n: jnp.stack(v) for n, v in gfull.items()}
    gfull_arr["w_kv"] = g_w_kv
    send_g = jnp.concatenate([_slots_from_full(gfull_arr[n], _SHARD_AXIS[n]).astype(BF) for n in big], axis=1)
    pad = slots.shape[1] - n_flat
    if pad:
        send_g = jnp.concatenate([send_g, jnp.zeros((N_DEV, pad), BF)], axis=1)
    parts = _all_to_all(send_g.reshape(N_DEV, -1, LANES), "exchange_grads").reshape(N_DEV, -1)

    res = {}

    def update(n, parts3, shape2):
        w2, m2, v2 = (a[n].reshape(shape2) for a in (weights, mom1, mom2))
        outs = _adamw(parts3, w2, m2, v2, f"adamw_{n}")
        res[n] = [a.reshape(weights[n].shape) for a in outs]

    off = 0
    for n in big:
        size = weights[n].size
        last = weights[n].shape[-1]
        update(n, parts[:, off:off + size].reshape(N_DEV, -1, last), (-1, last))
        off += size
    update("ada_w", g_ada_w.reshape(1, -1, ncol), (-1, ncol))
    update("kv_ada_w", g_kv_ada_w.reshape(1, -1, nkv), (-1, nkv))

    tot = small_all.reshape(N_DEV, -1)
    offs = np.cumsum([0] + sizes)
    seg = {k: (int(offs[i]), int(offs[i + 1])) for i, k in enumerate(
        ["mod", "kv_mod", "n1", "n2", "kv_norm", "final", "conv_b", "conv_w", "pscale"])}

    def rows_of(a, b):
        return tot[:, a:b]

    nf8 = f // N_DEV
    conv_w_parts = lax.dynamic_slice_in_dim(
        rows_of(*seg["conv_w"]).reshape(N_DEV, depth, CONV_WIDTH, N_DEV, nf8), me, 1, axis=3).reshape(N_DEV, -1)
    nd8 = d // N_DEV
    pscale_parts = lax.dynamic_slice_in_dim(
        rows_of(*seg["pscale"]).reshape(N_DEV, n_pool, N_DEV, nd8), me, 1, axis=2).reshape(N_DEV, -1)
    mod_parts = rows_of(*seg["mod"]).reshape(N_DEV, depth, 6 * d)
    small_names = ["ada_b", "norm1_g", "norm2_g", "pool_scale", "kv_norm_g", "kv_ada_b", "ffn_conv_w",
                   "ffn_conv_b", "final_g"]
    small_parts = [mod_parts.reshape(N_DEV, -1), rows_of(*seg["n1"]), rows_of(*seg["n2"]), pscale_parts,
                   rows_of(*seg["kv_norm"]), rows_of(*seg["kv_mod"]), conv_w_parts, rows_of(*seg["conv_b"]),
                   rows_of(*seg["final"])]
    sp = jnp.concatenate(small_parts, axis=1)
    n_sp = sp.shape[1]
    sp_rows = jax.vmap(_to_rows)(sp)

    def packed(src):
        return _to_rows(jnp.concatenate([src[n].reshape(-1) for n in small_names]))

    outs = _adamw(sp_rows, packed(weights), packed(mom1), packed(mom2), "adamw_small")
    outs = [a.reshape(-1)[:n_sp] for a in outs]
    off = 0
    for n in small_names:
        size = weights[n].size
        res[n] = [a[off:off + size].reshape(weights[n].shape) for a in outs]
        off += size

    grads = [res[n][0] for n in order]
    deltas = [res[n][1] for n in order]
    new_m = [res[n][2] for n in order]
    new_v = [res[n][3] for n in order]
    return (loss, grad_x, *grads, *deltas, *new_m, *new_v)
```

```python
import math

import numpy as np
import jax
import jax.numpy as jnp
from jax import lax
from jax.experimental import pallas as pl
from jax.experimental.pallas import tpu as pltpu

F32 = jnp.float32
BF = jnp.bfloat16

POOL_WINDOWS = (2, 4, 8, 16)
BRANCHES = ((128, 1), (512, 4), (2048, 16))
HEAD_DIM = 64
ATTN_BLOCK = 128
CONV_WIDTH = 3
EPS = 1e-6
ADAM_LR = 0.001
ADAM_B1 = 0.9
ADAM_B2 = 0.999
ADAM_EPS = 1e-08
ADAM_WD = 0.01
ADAM_STEP = 10

N_DEV = 8
LANES = 128
POOL_HALO = 16
CONV_HALO = 8
VMEM_LIMIT = 48 * 1024 * 1024
MM_TILE = 1024
NEG = -1e30

MESH = pl.DeviceIdType.MESH
ANY = pl.BlockSpec(memory_space=pl.ANY)


def _params(sem=None):
    if sem is None:
        return pltpu.CompilerParams(vmem_limit_bytes=VMEM_LIMIT)
    return pltpu.CompilerParams(dimension_semantics=sem, vmem_limit_bytes=VMEM_LIMIT)


def _pick(dim, pref, mult=LANES):
    if dim <= pref:
        return dim
    t = (pref // mult) * mult
    while t >= mult:
        if dim % t == 0:
            return t
        t -= mult
    return dim


def _alibi_slopes(n):
    def pow2(m):
        start = 2.0 ** (-(2.0 ** -(math.log2(m) - 3)))
        return [start ** (i + 1) for i in range(m)]
    if math.log2(n).is_integer():
        s = pow2(n)
    else:
        c = 2 ** math.floor(math.log2(n))
        s = pow2(c) + pow2(2 * c)[0::2][: n - c]
    s = np.asarray(s, dtype=np.float32)
    return -np.sort(-s)


def _my_place():
    return lax.axis_index("x"), lax.axis_index("y"), lax.axis_index("c")


def _all_gather_many(xs, name):
    n = len(xs)

    def body(*refs):
        x_refs, out_refs = refs[:n], refs[n:2 * n]
        send_sems, recv_sems, local_sems = refs[2 * n:]
        xi, yi, ci = _my_place()
        me, sibling = (xi, yi, ci), (xi, yi, 1 - ci)
        chips = [(1 - xi, yi), (xi, 1 - yi), (1 - xi, 1 - yi)]

        def slot(a, px, py, pc):
            return out_refs[a].at[4 * px + 2 * py + pc]

        def copy(a, k, block, to, src=None):
            return pltpu.make_async_remote_copy(
                src_ref=slot(a, *block) if src is None else src, dst_ref=slot(a, *block),
                send_sem=send_sems.at[7 * a + k], recv_sem=recv_sems.at[7 * a + k],
                device_id=to, device_id_type=MESH)

        mine = [pltpu.make_async_copy(x_refs[a], slot(a, *me), local_sems.at[a]) for a in range(n)]
        for cp in mine:
            cp.start()
        sent = []
        for a in range(n):
            first = [copy(a, 0, me, sibling, src=x_refs[a])]
            first += [copy(a, 1 + j, me, (*chip, ci), src=x_refs[a]) for j, chip in enumerate(chips)]
            for cp in first:
                cp.start()
            sent += first
        for a in range(n):
            for j, chip in enumerate(chips):
                copy(a, 1 + j, (*chip, ci), me).wait_recv()
                fwd = copy(a, 4 + j, (*chip, ci), sibling)
                fwd.start()
                sent.append(fwd)
        for a in range(n):
            copy(a, 0, sibling, me).wait_recv()
            for j, chip in enumerate(chips):
                copy(a, 4 + j, (*chip, 1 - ci), me).wait_recv()
        for cp in sent:
            cp.wait_send()
        for cp in mine:
            cp.wait()

    return pl.pallas_call(
        body, name=name,
        out_shape=[jax.ShapeDtypeStruct((N_DEV,) + x.shape, x.dtype) for x in xs],
        in_specs=[ANY] * n, out_specs=[ANY] * n,
        scratch_shapes=[pltpu.SemaphoreType.DMA((7 * n,)), pltpu.SemaphoreType.DMA((7 * n,)),
                        pltpu.SemaphoreType.DMA((n,))],
    )(*xs)


def _all_gather(x, name):
    return _all_gather_many([x], name)[0]


def _all_to_all_many(xs, name):
    n = len(xs)

    def body(*refs):
        x_refs, out_refs = refs[:n], refs[n:2 * n]
        send_sems, recv_sems, local_sems = refs[2 * n:]
        xi, yi, ci = _my_place()
        me = 4 * xi + 2 * yi + ci
        mine = [pltpu.make_async_copy(x_refs[a].at[me], out_refs[a].at[me], local_sems.at[a]) for a in range(n)]
        for cp in mine:
            cp.start()
        copies = []
        for a in range(n):
            for k in range(1, N_DEV):
                px = 1 - xi if k & 4 else xi
                py = 1 - yi if k & 2 else yi
                pc = 1 - ci if k & 1 else ci
                peer = 4 * px + 2 * py + pc
                cp = pltpu.make_async_remote_copy(
                    src_ref=x_refs[a].at[peer], dst_ref=out_refs[a].at[me],
                    send_sem=send_sems.at[7 * a + k - 1], recv_sem=recv_sems.at[7 * a + k - 1],
                    device_id=(px, py, pc), device_id_type=MESH)
                cp.start()
                copies.append(cp)
        for cp in copies:
            cp.wait()
        for cp in mine:
            cp.wait()

    return pl.pallas_call(
        body, name=name,
        out_shape=[jax.ShapeDtypeStruct(x.shape, x.dtype) for x in xs],
        in_specs=[ANY] * n, out_specs=[ANY] * n,
        scratch_shapes=[pltpu.SemaphoreType.DMA((7 * n,)), pltpu.SemaphoreType.DMA((7 * n,)),
                        pltpu.SemaphoreType.DMA((n,))],
    )(*xs)


def _all_to_all(x, name):
    return _all_to_all_many([x], name)[0]


def _to_rows(vec):
    n = vec.shape[0]
    unit = 8 * LANES
    pad = (-n) % unit
    if pad:
        vec = jnp.concatenate([vec, jnp.zeros((pad,), vec.dtype)])
    return vec.reshape(-1, LANES)


def _mm(a, b, name, *, af="mk", bf="kn", of="mn", out_dtype=F32, tm=None, tn=None, tk=None,
        gate=None, resid=None, pre_dtype=None):
    if af == "mk":
        m, kk = a.shape
    elif af == "km":
        kk, m = a.shape
    elif af == "qmk":
        qa, m, tk = a.shape
        kk = qa * tk
    else:
        qa, kk, tm = a.shape
        m = qa * tm
    if bf == "kn":
        k2, n = b.shape
    elif bf == "nk":
        n, k2 = b.shape
    elif bf == "qkn":
        qb, k2, tn = b.shape
        n = qb * tn
    else:
        qb, n, tkb = b.shape
        k2 = qb * tkb
        assert af != "qmk" or tkb == tk
        tk = tkb
    assert kk == k2, (name, a.shape, b.shape, af, bf)
    tm = _pick(m, MM_TILE) if tm is None else tm
    tn = _pick(n, MM_TILE) if tn is None else tn
    tk = _pick(kk, MM_TILE) if tk is None else tk
    assert m % tm == 0 and n % tn == 0 and kk % tk == 0, (name, m, n, kk, tm, tn, tk)
    nk = kk // tk
    a_spec = {"mk": pl.BlockSpec((tm, tk), lambda i, j, k: (i, k)),
              "km": pl.BlockSpec((tk, tm), lambda i, j, k: (k, i)),
              "qmk": pl.BlockSpec((None, tm, tk), lambda i, j, k: (k, i, 0)),
              "qkm": pl.BlockSpec((None, tk, tm), lambda i, j, k: (i, k, 0))}[af]
    b_spec = {"kn": pl.BlockSpec((tk, tn), lambda i, j, k: (k, j)),
              "nk": pl.BlockSpec((tn, tk), lambda i, j, k: (j, k)),
              "qkn": pl.BlockSpec((None, tk, tn), lambda i, j, k: (j, k, 0)),
              "qnk": pl.BlockSpec((None, tn, tk), lambda i, j, k: (k, j, 0))}[bf]
    dims = (((1 if af in ("mk", "qmk") else 0,), (0 if bf in ("kn", "qkn") else 1,)), ((), ()))
    in_specs, args = [a_spec, b_spec], [a, b]
    if gate is not None:
        assert of == "mn"
        in_specs.append(pl.BlockSpec((1, tn), lambda i, j, k: (0, j)))
        args.append(gate)
    if resid is not None:
        assert of == "mn"
        in_specs.append(pl.BlockSpec((tm, tn), lambda i, j, k: (i, j)))
        args.append(resid)
    if of == "mn":
        o_spec, o_shape = pl.BlockSpec((tm, tn), lambda i, j, k: (i, j)), (m, n)
    else:
        o_spec, o_shape = pl.BlockSpec((None, tm, tn), lambda i, j, k: (j, i, 0)), (n // tn, m, tn)
    out_shape, out_specs = [jax.ShapeDtypeStruct(o_shape, out_dtype)], [o_spec]
    if pre_dtype is not None:
        out_shape.insert(0, jax.ShapeDtypeStruct(o_shape, pre_dtype))
        out_specs.insert(0, o_spec)
    n_in = len(args)
    n_out = len(out_shape)

    def body(*refs):
        a_ref, b_ref = refs[0], refs[1]
        extra = list(refs[2:n_in])
        outs = refs[n_in:n_in + n_out]
        gate_ref = extra.pop(0) if gate is not None else None
        resid_ref = extra.pop(0) if resid is not None else None

        def product():
            return lax.dot_general(a_ref[...].astype(BF), b_ref[...].astype(BF), dims, preferred_element_type=F32)

        def finish(r):
            if pre_dtype is not None:
                outs[0][...] = r.astype(pre_dtype)
            if gate_ref is not None:
                r = r * gate_ref[...]
            if resid_ref is not None:
                r = resid_ref[...] + r
            outs[-1][...] = r.astype(out_dtype)

        if nk == 1:
            finish(product())
        else:
            acc = refs[n_in + n_out]
            k = pl.program_id(2)

            @pl.when(k == 0)
            def _():
                acc[...] = product()

            @pl.when(k > 0)
            def _():
                acc[...] += product()

            @pl.when(k == nk - 1)
            def _():
                finish(acc[...])

    res = pl.pallas_call(
        body, name=name, grid=(m // tm, n // tn, nk),
        in_specs=in_specs, out_specs=out_specs, out_shape=out_shape,
        scratch_shapes=[pltpu.VMEM((tm, tn), F32)] if nk > 1 else [],
        compiler_params=_params(("parallel", "parallel", "arbitrary")),
    )(*args)
    return res if pre_dtype is not None else res[0]


def _ada_fwd(cond16, w, name):
    nl, d, n = w.shape

    def body(c_ref, w_ref, o_ref):
        o_ref[...] = jnp.dot(c_ref[...].astype(BF), w_ref[...].astype(BF), preferred_element_type=F32)

    return pl.pallas_call(
        body, name=name, grid=(nl,),
        in_specs=[pl.BlockSpec((16, d), lambda l: (0, 0)), pl.BlockSpec((None, d, n), lambda l: (l, 0, 0))],
        out_specs=pl.BlockSpec((None, 16, n), lambda l: (l, 0, 0)),
        out_shape=jax.ShapeDtypeStruct((nl, 16, n), F32),
        compiler_params=_params(("parallel",)),
    )(cond16, w)


def _ada_bwd(cond16, dmod, name):
    nl, _, n = dmod.shape
    d = cond16.shape[1]

    def body(c_ref, g_ref, o_ref):
        o_ref[...] = lax.dot_general(c_ref[...].astype(BF), g_ref[...].astype(BF), (((0,), (0,)), ((), ())),
                                     preferred_element_type=F32)

    return pl.pallas_call(
        body, name=name, grid=(nl,),
        in_specs=[pl.BlockSpec((16, d), lambda l: (0, 0)), pl.BlockSpec((None, 16, n), lambda l: (l, 0, 0))],
        out_specs=pl.BlockSpec((None, d, n), lambda l: (l, 0, 0)),
        out_shape=jax.ShapeDtypeStruct((nl, d, n), F32),
        compiler_params=_params(("parallel",)),
    )(cond16, dmod)


def _row_spec(tm, d):
    return pl.BlockSpec((tm, d), lambda i: (i, 0))


def _vec_spec(d):
    return pl.BlockSpec((1, d), lambda i: (0, 0))


def _norm_mod(x, g, sc, sh, name):
    s, d = x.shape
    tm = _pick(s, 512, 8)

    def body(x_ref, g_ref, sc_ref, sh_ref, o_ref):
        xv = x_ref[...]
        r = lax.rsqrt(jnp.mean(xv * xv, axis=-1, keepdims=True) + EPS)
        y = (xv * r) * g_ref[...]
        o_ref[...] = (y * (1.0 + sc_ref[...]) + sh_ref[...]).astype(BF)

    return pl.pallas_call(
        body, name=name, grid=(s // tm,),
        in_specs=[_row_spec(tm, d), _vec_spec(d), _vec_spec(d), _vec_spec(d)],
        out_specs=_row_spec(tm, d), out_shape=jax.ShapeDtypeStruct((s, d), BF),
        compiler_params=_params(("parallel",)),
    )(x, g, sc, sh)


def _norm_mod_bwd(x, dh, g, sc, dx_in, name):
    s, d = x.shape
    tm = _pick(s, 512, 8)

    def body(x_ref, dh_ref, g_ref, sc_ref, dxin_ref, dx_ref, dsh_ref, dw_ref):
        i = pl.program_id(0)
        xv = x_ref[...]
        dhv = dh_ref[...].astype(F32)
        r = lax.rsqrt(jnp.mean(xv * xv, axis=-1, keepdims=True) + EPS)
        xn = xv * r
        dxn = dhv * (g_ref[...] * (1.0 + sc_ref[...]))
        dx_ref[...] = dxin_ref[...] + r * (dxn - xn * jnp.mean(dxn * xn, axis=-1, keepdims=True))

        @pl.when(i == 0)
        def _():
            dsh_ref[...] = jnp.zeros_like(dsh_ref)
            dw_ref[...] = jnp.zeros_like(dw_ref)

        dsh_ref[...] += jnp.sum(dhv, axis=0, keepdims=True)
        dw_ref[...] += jnp.sum(dhv * xn, axis=0, keepdims=True)

    return pl.pallas_call(
        body, name=name, grid=(s // tm,),
        in_specs=[_row_spec(tm, d), _row_spec(tm, d), _vec_spec(d), _vec_spec(d), _row_spec(tm, d)],
        out_specs=[_row_spec(tm, d), _vec_spec(d), _vec_spec(d)],
        out_shape=[jax.ShapeDtypeStruct((s, d), F32), jax.ShapeDtypeStruct((1, d), F32),
                   jax.ShapeDtypeStruct((1, d), F32)],
        compiler_params=_params(("arbitrary",)),
    )(x, dh, g, sc, dx_in)


def _gate_bwd(dx, y, gate, name):
    s, d = dx.shape
    tm = _pick(s, 512, 8)

    def body(dx_ref, y_ref, g_ref, dy_ref, dg_ref):
        i = pl.program_id(0)
        dxv = dx_ref[...]
        dy_ref[...] = (dxv * g_ref[...]).astype(BF)

        @pl.when(i == 0)
        def _():
            dg_ref[...] = jnp.zeros_like(dg_ref)

        dg_ref[...] += jnp.sum(dxv * y_ref[...].astype(F32), axis=0, keepdims=True)

    return pl.pallas_call(
        body, name=name, grid=(s // tm,),
        in_specs=[_row_spec(tm, d), _row_spec(tm, d), _vec_spec(d)],
        out_specs=[_row_spec(tm, d), _vec_spec(d)],
        out_shape=[jax.ShapeDtypeStruct((s, d), BF), jax.ShapeDtypeStruct((1, d), F32)],
        compiler_params=_params(("arbitrary",)),
    )(dx, y, gate)


def _final_loss(x, tgt, g, name):
    s, d = x.shape
    tm = _pick(s, 512, 8)

    def body(x_ref, t_ref, g_ref, dx_ref, loss_ref, dg_ref):
        i = pl.program_id(0)
        xv = x_ref[...]
        gv = g_ref[...]
        r = lax.rsqrt(jnp.mean(xv * xv, axis=-1, keepdims=True) + EPS)
        xn = xv * r
        err = xn * gv - t_ref[...]
        dy = err * (1.0 / d)
        dxn = dy * gv
        dx_ref[...] = r * (dxn - xn * jnp.mean(dxn * xn, axis=-1, keepdims=True))

        @pl.when(i == 0)
        def _():
            loss_ref[...] = jnp.zeros_like(loss_ref)
            dg_ref[...] = jnp.zeros_like(dg_ref)

        part = 0.5 * jnp.sum(jnp.sum(err * err, axis=-1, keepdims=True) * (1.0 / d), axis=0, keepdims=True)
        loss_ref[...] += jnp.broadcast_to(part, loss_ref.shape)
        dg_ref[...] += jnp.sum(dy * xn, axis=0, keepdims=True)

    return pl.pallas_call(
        body, name=name, grid=(s // tm,),
        in_specs=[_row_spec(tm, d), _row_spec(tm, d), _vec_spec(d)],
        out_specs=[_row_spec(tm, d), pl.BlockSpec((8, LANES), lambda i: (0, 0)), _vec_spec(d)],
        out_shape=[jax.ShapeDtypeStruct((s, d), F32), jax.ShapeDtypeStruct((8, LANES), F32),
                   jax.ShapeDtypeStruct((1, d), F32)],
        compiler_params=_params(("arbitrary",)),
    )(x, tgt, g)


def _pool_counts(tm, gd, row0, w):
    t = lax.broadcasted_iota(jnp.int32, (tm, gd), 0) + row0
    return jnp.minimum(t + 1, w).astype(F32)


def _pool_fwd(u, name):
    s, d = u.shape
    tm = _pick(s, 256, POOL_HALO)
    gd = d // len(POOL_WINDOWS)
    per = tm // POOL_HALO

    def body(prev_ref, cur_ref, o_ref, ext):
        i = pl.program_id(0)
        ext[0:POOL_HALO, :] = jnp.where(i > 0, prev_ref[...], 0.0)
        ext[POOL_HALO:, :] = cur_ref[...]
        for g, w in enumerate(POOL_WINDOWS):
            cols = slice(g * gd, (g + 1) * gd)
            acc = ext[POOL_HALO:POOL_HALO + tm, cols]
            own = acc
            for k in range(1, w):
                acc = acc + ext[POOL_HALO - k:POOL_HALO - k + tm, cols]
            o_ref[:, cols] = (acc / _pool_counts(tm, gd, i * tm, w) - own).astype(BF)

    return pl.pallas_call(
        body, name=name, grid=(s // tm,),
        in_specs=[pl.BlockSpec((POOL_HALO, d), lambda i: (jnp.maximum(i * per - 1, 0), 0)), _row_spec(tm, d)],
        out_specs=_row_spec(tm, d), out_shape=jax.ShapeDtypeStruct((s, d), BF),
        scratch_shapes=[pltpu.VMEM((tm + POOL_HALO, d), F32)],
        compiler_params=_params(("parallel",)),
    )(u, u)


def _pool_bwd(dp, name):
    s, d = dp.shape
    tm = _pick(s, 256, POOL_HALO)
    gd = d // len(POOL_WINDOWS)
    per = tm // POOL_HALO
    nt = s // tm
    last_halo = s // POOL_HALO - 1

    def body(cur_ref, nxt_ref, o_ref, ext):
        i = pl.program_id(0)
        for g, w in enumerate(POOL_WINDOWS):
            cols = slice(g * gd, (g + 1) * gd)
            ext[0:tm, cols] = cur_ref[:, cols].astype(F32) / _pool_counts(tm, gd, i * tm, w)
            nxt = nxt_ref[:, cols].astype(F32) / _pool_counts(POOL_HALO, gd, (i + 1) * tm, w)
            ext[tm:, cols] = jnp.where(i < nt - 1, nxt, 0.0)
        for g, w in enumerate(POOL_WINDOWS):
            cols = slice(g * gd, (g + 1) * gd)
            acc = ext[0:tm, cols]
            for k in range(1, w):
                acc = acc + ext[k:k + tm, cols]
            o_ref[:, cols] = (acc - cur_ref[:, cols].astype(F32)).astype(BF)

    return pl.pallas_call(
        body, name=name, grid=(nt,),
        in_specs=[_row_spec(tm, d), pl.BlockSpec((POOL_HALO, d), lambda i: (jnp.minimum((i + 1) * per, last_halo), 0))],
        out_specs=_row_spec(tm, d), out_shape=jax.ShapeDtypeStruct((s, d), BF),
        scratch_shapes=[pltpu.VMEM((tm + POOL_HALO, d), F32)],
        compiler_params=_params(("parallel",)),
    )(dp, dp)


def _grp_fwd(p, w, scale, name):
    s, d = p.shape
    ng, gd, _ = w.shape
    tm = _pick(s, 1024, 8)

    def body(p_ref, w_ref, s_ref, z_ref, y_ref):
        z = jnp.dot(p_ref[...], w_ref[...].astype(BF), preferred_element_type=F32)
        z_ref[...] = z.astype(BF)
        y_ref[...] = (z * s_ref[...]).astype(BF)

    blk = pl.BlockSpec((tm, gd), lambda i, g: (i, g))
    return pl.pallas_call(
        body, name=name, grid=(s // tm, ng),
        in_specs=[blk, pl.BlockSpec((None, gd, gd), lambda i, g: (g, 0, 0)), pl.BlockSpec((1, gd), lambda i, g: (0, g))],
        out_specs=[blk, blk],
        out_shape=[jax.ShapeDtypeStruct((s, d), BF), jax.ShapeDtypeStruct((s, d), BF)],
        compiler_params=_params(("parallel", "parallel")),
    )(p, w, scale)


def _grp_bwd(dy, z, w, scale, name):
    s, d = dy.shape
    ng, gd, _ = w.shape
    tm = _pick(s, 1024, 8)

    def body(dy_ref, z_ref, w_ref, s_ref, dz_ref, dp_ref, ds_ref):
        i = pl.program_id(1)
        dyv = dy_ref[...].astype(F32)
        dz = (dyv * s_ref[...]).astype(BF)
        dz_ref[...] = dz
        dp_ref[...] = lax.dot_general(dz, w_ref[...].astype(BF), (((1,), (1,)), ((), ())),
                                      preferred_element_type=F32).astype(BF)

        @pl.when(i == 0)
        def _():
            ds_ref[...] = jnp.zeros_like(ds_ref)

        ds_ref[...] += jnp.sum(dyv * z_ref[...].astype(F32), axis=0, keepdims=True)

    blk = pl.BlockSpec((tm, gd), lambda g, i: (i, g))
    vec = pl.BlockSpec((1, gd), lambda g, i: (0, g))
    return pl.pallas_call(
        body, name=name, grid=(ng, s // tm),
        in_specs=[blk, blk, pl.BlockSpec((None, gd, gd), lambda g, i: (g, 0, 0)), vec],
        out_specs=[blk, blk, vec],
        out_shape=[jax.ShapeDtypeStruct((s, d), BF), jax.ShapeDtypeStruct((s, d), BF),
                   jax.ShapeDtypeStruct((1, d), F32)],
        compiler_params=_params(("parallel", "arbitrary")),
    )(dy, z, w, scale)


def _grp_dw(p, dz, ng, name):
    s, d = p.shape
    gd = d // ng
    tk = _pick(s, 1024, 8)

    def body(p_ref, dz_ref, o_ref):
        k = pl.program_id(1)

        @pl.when(k == 0)
        def _():
            o_ref[...] = jnp.zeros_like(o_ref)

        o_ref[...] += lax.dot_general(p_ref[...], dz_ref[...], (((0,), (0,)), ((), ())), preferred_element_type=F32)

    blk = pl.BlockSpec((tk, gd), lambda g, k: (k, g))
    return pl.pallas_call(
        body, name=name, grid=(ng, s // tk),
        in_specs=[blk, blk], out_specs=pl.BlockSpec((None, gd, gd), lambda g, k: (g, 0, 0)),
        out_shape=jax.ShapeDtypeStruct((ng, gd, gd), F32),
        compiler_params=_params(("parallel", "arbitrary")),
    )(p, dz)


def _sigmoid(a):
    return 1.0 / (1.0 + jnp.exp(-a))


def _ffn_act(up, cw, cb, name):
    _, nq, s, fq = up.shape
    tm = _pick(s, 512, CONV_HALO)
    per = tm // CONV_HALO

    def body(prev_ref, a_ref, v_ref, w_ref, b_ref, o_ref, ext):
        i = pl.program_id(1)
        ext[0:CONV_HALO, :] = jnp.where(i > 0, prev_ref[...].astype(F32), 0.0)
        ext[CONV_HALO:, :] = a_ref[...].astype(F32)
        a2 = b_ref[...] + ext[CONV_HALO - 2:CONV_HALO - 2 + tm, :] * w_ref[0:1, :]
        a2 = a2 + ext[CONV_HALO - 1:CONV_HALO - 1 + tm, :] * w_ref[1:2, :]
        a2 = a2 + ext[CONV_HALO:CONV_HALO + tm, :] * w_ref[2:3, :]
        o_ref[...] = (a2 * _sigmoid(a2) * v_ref[...].astype(F32)).astype(BF)

    return pl.pallas_call(
        body, name=name, grid=(nq, s // tm),
        in_specs=[pl.BlockSpec((None, None, CONV_HALO, fq), lambda q, i: (0, q, jnp.maximum(i * per - 1, 0), 0)),
                  pl.BlockSpec((None, None, tm, fq), lambda q, i: (0, q, i, 0)),
                  pl.BlockSpec((None, None, tm, fq), lambda q, i: (1, q, i, 0)),
                  pl.BlockSpec((None, CONV_WIDTH, fq), lambda q, i: (q, 0, 0)),
                  pl.BlockSpec((None, 1, fq), lambda q, i: (q, 0, 0))],
        out_specs=pl.BlockSpec((None, tm, fq), lambda q, i: (q, i, 0)),
        out_shape=jax.ShapeDtypeStruct((nq, s, fq), BF),
        scratch_shapes=[pltpu.VMEM((tm + CONV_HALO, fq), F32)],
        compiler_params=_params(("parallel", "parallel")),
    )(up, up, up, cw, cb)


def _ffn_act_bwd(up, dact, cw, cb, name):
    _, nq, s, fq = up.shape
    tm = _pick(s, 512, CONV_HALO)
    per = tm // CONV_HALO
    nt = s // tm
    last_halo = s // CONV_HALO - 1
    h = CONV_HALO
    te = tm + h

    def body(ap_ref, a_ref, an_ref, v_ref, vn_ref, d_ref, dn_ref, w_ref, b_ref, dup_ref, dc_ref, ext_a, dap):
        i = pl.program_id(1)
        ext_a[0:h, :] = jnp.where(i > 0, ap_ref[...].astype(F32), 0.0)
        ext_a[h:h + tm, :] = a_ref[...].astype(F32)
        ext_a[h + tm:, :] = an_ref[...].astype(F32)
        w0, w1, w2 = w_ref[0:1, :], w_ref[1:2, :], w_ref[2:3, :]
        a2 = b_ref[...] + ext_a[h - 2:h - 2 + te, :] * w0 + ext_a[h - 1:h - 1 + te, :] * w1 + ext_a[h:h + te, :] * w2
        sig = _sigmoid(a2)
        dsilu = sig * (1.0 + a2 * (1.0 - sig))
        d_cur = d_ref[...].astype(F32)
        d_nxt = jnp.where(i < nt - 1, dn_ref[...].astype(F32), 0.0)
        dap[0:tm, :] = d_cur * v_ref[...].astype(F32) * dsilu[0:tm, :]
        dap[tm:, :] = d_nxt * vn_ref[...].astype(F32) * dsilu[tm:, :]
        dup_ref[1] = (d_cur * (a2[0:tm, :] * sig[0:tm, :])).astype(BF)
        g0 = dap[0:tm, :]
        dup_ref[0] = (dap[2:2 + tm, :] * w0 + dap[1:1 + tm, :] * w1 + g0 * w2).astype(BF)

        @pl.when(i == 0)
        def _():
            dc_ref[...] = jnp.zeros_like(dc_ref)

        dc_ref[0:1, :] += jnp.sum(g0 * ext_a[h - 2:h - 2 + tm, :], axis=0, keepdims=True)
        dc_ref[1:2, :] += jnp.sum(g0 * ext_a[h - 1:h - 1 + tm, :], axis=0, keepdims=True)
        dc_ref[2:3, :] += jnp.sum(g0 * ext_a[h:h + tm, :], axis=0, keepdims=True)
        dc_ref[3:4, :] += jnp.sum(g0, axis=0, keepdims=True)

    def cur(half):
        return pl.BlockSpec((None, None, tm, fq), lambda q, i: (half, q, i, 0))

    def nxt(half):
        return pl.BlockSpec((None, None, h, fq), lambda q, i: (half, q, jnp.minimum((i + 1) * per, last_halo), 0))

    return pl.pallas_call(
        body, name=name, grid=(nq, nt),
        in_specs=[pl.BlockSpec((None, None, h, fq), lambda q, i: (0, q, jnp.maximum(i * per - 1, 0), 0)),
                  cur(0), nxt(0), cur(1), nxt(1),
                  pl.BlockSpec((None, tm, fq), lambda q, i: (q, i, 0)),
                  pl.BlockSpec((None, h, fq), lambda q, i: (q, jnp.minimum((i + 1) * per, last_halo), 0)),
                  pl.BlockSpec((None, CONV_WIDTH, fq), lambda q, i: (q, 0, 0)),
                  pl.BlockSpec((None, 1, fq), lambda q, i: (q, 0, 0))],
        out_specs=[pl.BlockSpec((2, None, tm, fq), lambda q, i: (0, q, i, 0)),
                   pl.BlockSpec((None, 8, fq), lambda q, i: (q, 0, 0))],
        out_shape=[jax.ShapeDtypeStruct((2, nq, s, fq), BF), jax.ShapeDtypeStruct((nq, 8, fq), F32)],
        scratch_shapes=[pltpu.VMEM((tm + 2 * h, fq), F32), pltpu.VMEM((te, fq), F32)],
        compiler_params=_params(("parallel", "arbitrary")),
    )(up, up, up, up, up, dact, dact, cw, cb)


def _band(blk, n_steps, dil, first):
    qi = lax.broadcasted_iota(jnp.int32, (blk, 2 * blk), 0) + blk
    ki = lax.broadcasted_iota(jnp.int32, (blk, 2 * blk), 1)
    delta = qi - ki
    valid = (delta >= 0) & (delta <= n_steps) & ((ki >= blk) | jnp.logical_not(first))
    return valid, (delta * dil).astype(F32)


def _attn_fwd(q_all, kv, g, slopes, d, name):
    window, dil = BRANCHES[g]
    n_steps = window // dil
    blk = max(ATTN_BLOCK, n_steps)
    s = q_all.shape[0]
    sub = s // dil
    nb = sub // blk
    assert nb * blk == sub
    nh = d // HEAD_DIM
    nbr = len(BRANCHES)
    qv = q_all.reshape(sub, dil * nbr * d)
    kvv = kv.reshape(sub, dil * 2 * nbr * d)
    scale = HEAD_DIM ** -0.5

    def body(q_ref, kp_ref, kc_ref, vp_ref, vc_ref, o_ref, l_ref):
        j = pl.program_id(1)
        valid, dist = _band(blk, n_steps, dil, j == 0)
        qb = q_ref[...]
        kb = jnp.concatenate([kp_ref[...], kc_ref[...]], axis=0)
        vb = jnp.concatenate([vp_ref[...], vc_ref[...]], axis=0)
        for h in range(nh):
            sl = slice(h * HEAD_DIM, (h + 1) * HEAD_DIM)
            sc = lax.dot_general(qb[:, sl], kb[:, sl], (((1,), (1,)), ((), ())), preferred_element_type=F32) * scale
            sc = jnp.where(valid, sc - float(slopes[h]) * dist, NEG)
            m = jnp.max(sc, axis=-1, keepdims=True)
            p = jnp.exp(sc - m)
            den = jnp.sum(p, axis=-1, keepdims=True)
            o = jnp.dot(p.astype(BF), vb[:, sl], preferred_element_type=F32) / den
            o_ref[:, sl] = o.astype(BF)
            l_ref[:, sl] = jnp.broadcast_to(m + jnp.log(den), (blk, HEAD_DIM))

    def spec(col, prev):
        if prev:
            return pl.BlockSpec((blk, d), lambda r, j: (jnp.maximum(j - 1, 0), r * col[0] + col[1]))
        return pl.BlockSpec((blk, d), lambda r, j: (j, r * col[0] + col[1]))

    qcol, kcol, vcol = (nbr, g), (2 * nbr, g), (2 * nbr, nbr + g)
    ospec = pl.BlockSpec((blk, d), lambda r, j: (j, r))
    o, lse = pl.pallas_call(
        body, name=name, grid=(dil, nb),
        in_specs=[spec(qcol, False), spec(kcol, True), spec(kcol, False), spec(vcol, True), spec(vcol, False)],
        out_specs=[ospec, ospec],
        out_shape=[jax.ShapeDtypeStruct((sub, dil * d), BF), jax.ShapeDtypeStruct((sub, dil * d), F32)],
        compiler_params=_params(("parallel", "parallel")),
    )(qv, kvv, kvv, kvv, kvv)
    return o.reshape(s, d), lse.reshape(s, d)


def _attn_combine(os, lses, name):
    s, d = os[0].shape
    tm = _pick(s, 512, 8)
    nbr = len(os)

    def body(*refs):
        o_refs, l_refs = refs[:nbr], refs[nbr:2 * nbr]
        o_ref, lt_ref = refs[2 * nbr], refs[2 * nbr + 1]
        ls = [r[...] for r in l_refs]
        m = ls[0]
        for v in ls[1:]:
            m = jnp.maximum(m, v)
        tot = jnp.exp(ls[0] - m)
        for v in ls[1:]:
            tot = tot + jnp.exp(v - m)
        lt = m + jnp.log(tot)
        acc = jnp.exp(ls[0] - lt) * o_refs[0][...].astype(F32)
        for v, r in zip(ls[1:], o_refs[1:]):
            acc = acc + jnp.exp(v - lt) * r[...].astype(F32)
        o_ref[...] = acc.astype(BF)
        lt_ref[...] = lt

    return pl.pallas_call(
        body, name=name, grid=(s // tm,),
        in_specs=[_row_spec(tm, d)] * (2 * nbr), out_specs=[_row_spec(tm, d), _row_spec(tm, d)],
        out_shape=[jax.ShapeDtypeStruct((s, d), BF), jax.ShapeDtypeStruct((s, d), F32)],
        compiler_params=_params(("parallel",)),
    )(*os, *lses)


def _attn_bwd(q_all, kv, do, o, lt, g, slopes, d, name, dk_in=None, dv_in=None):
    window, dil = BRANCHES[g]
    n_steps = window // dil
    blk = max(ATTN_BLOCK, n_steps)
    s = q_all.shape[0]
    sub = s // dil
    nb = sub // blk
    nh = d // HEAD_DIM
    nbr = len(BRANCHES)
    qv = q_all.reshape(sub, dil * nbr * d)
    kvv = kv.reshape(sub, dil * 2 * nbr * d)
    scale = HEAD_DIM ** -0.5
    acc_in = dk_in is not None

    def body(*refs):
        q_ref, do_ref, o_ref, lt_ref, kp_ref, kc_ref, vp_ref, vc_ref = refs[:8]
        n_in = 10 if acc_in else 8
        dkin_ref, dvin_ref = (refs[8], refs[9]) if acc_in else (None, None)
        dq_ref, dk_ref, dv_ref, keep_k, keep_v, part_k, part_v = refs[n_in:n_in + 7]
        t = pl.program_id(1)

        def emit(prev_k, prev_v):
            if acc_in:
                prev_k = prev_k + dkin_ref[...].astype(F32)
                prev_v = prev_v + dvin_ref[...].astype(F32)
            dk_ref[...] = prev_k.astype(BF)
            dv_ref[...] = prev_v.astype(BF)

        @pl.when(t < nb)
        def _():
            valid, dist = _band(blk, n_steps, dil, t == 0)
            qb = q_ref[...]
            dob = do_ref[...]
            kb = jnp.concatenate([kp_ref[...], kc_ref[...]], axis=0)
            vb = jnp.concatenate([vp_ref[...], vc_ref[...]], axis=0)
            for h in range(nh):
                sl = slice(h * HEAD_DIM, (h + 1) * HEAD_DIM)
                qh, kh, vh, doh = qb[:, sl], kb[:, sl], vb[:, sl], dob[:, sl]
                sc = lax.dot_general(qh, kh, (((1,), (1,)), ((), ())), preferred_element_type=F32) * scale
                sc = sc - float(slopes[h]) * dist
                p = jnp.where(valid, jnp.exp(jnp.minimum(sc - lt_ref[:, sl][:, 0:1], 30.0)), 0.0)
                dlt = jnp.sum(doh.astype(F32) * o_ref[:, sl].astype(F32), axis=-1, keepdims=True)
                dp = lax.dot_general(doh, vh, (((1,), (1,)), ((), ())), preferred_element_type=F32)
                ds = (p * (dp - dlt)).astype(BF)
                dq_ref[:, sl] = (jnp.dot(ds, kh, preferred_element_type=F32) * scale).astype(BF)
                part_k[:, sl] = lax.dot_general(ds, qh, (((0,), (0,)), ((), ())), preferred_element_type=F32) * scale
                part_v[:, sl] = lax.dot_general(p.astype(BF), doh, (((0,), (0,)), ((), ())),
                                                preferred_element_type=F32)

            @pl.when(t > 0)
            def _():
                emit(keep_k[...] + part_k[0:blk, :], keep_v[...] + part_v[0:blk, :])

            keep_k[...] = part_k[blk:, :]
            keep_v[...] = part_v[blk:, :]

        @pl.when(t == nb)
        def _():
            emit(keep_k[...], keep_v[...])

    def qspec(col):
        return pl.BlockSpec((blk, d), lambda r, t: (jnp.minimum(t, nb - 1), r * col[0] + col[1]))

    def kspec(col, prev):
        if prev:
            return pl.BlockSpec((blk, d), lambda r, t: (jnp.maximum(jnp.minimum(t, nb - 1) - 1, 0), r * col[0] + col[1]))
        return qspec(col)

    kout = pl.BlockSpec((blk, d), lambda r, t: (jnp.maximum(t - 1, 0), r))
    qcol, kcol, vcol, one = (nbr, g), (2 * nbr, g), (2 * nbr, nbr + g), (1, 0)
    in_specs = [qspec(qcol), qspec(one), qspec(one), qspec(one),
                kspec(kcol, True), kspec(kcol, False), kspec(vcol, True), kspec(vcol, False)]
    args = [qv, do.reshape(sub, dil * d), o.reshape(sub, dil * d), lt.reshape(sub, dil * d), kvv, kvv, kvv, kvv]
    if acc_in:
        in_specs += [kout, kout]
        args += [dk_in.reshape(sub, dil * d), dv_in.reshape(sub, dil * d)]
    shp = jax.ShapeDtypeStruct((sub, dil * d), BF)
    dq, dk, dv = pl.pallas_call(
        body, name=name, grid=(dil, nb + 1),
        in_specs=in_specs, out_specs=[qspec(one), kout, kout], out_shape=[shp, shp, shp],
        scratch_shapes=[pltpu.VMEM((blk, d), F32), pltpu.VMEM((blk, d), F32),
                        pltpu.VMEM((2 * blk, d), F32), pltpu.VMEM((2 * blk, d), F32)],
        compiler_params=_params(("parallel", "arbitrary")),
    )(*args)
    return dq.reshape(s, d), dk.reshape(s, d), dv.reshape(s, d)


def _adamw(parts_list, w, m, v, name):
    nl, r, c = w.shape
    assert len(parts_list) == nl
    npart = parts_list[0].shape[0]
    tr = _pick(r, 256, 16)
    c1 = 1.0 / (1.0 - ADAM_B1 ** ADAM_STEP)
    c2 = 1.0 / (1.0 - ADAM_B2 ** ADAM_STEP)

    def body(*refs):
        p_refs = refs[:nl]
        w_ref, m_ref, v_ref, g_ref, d_ref, nm_ref, nv_ref = refs[nl:]
        layer = pl.program_id(0)
        for idx in range(nl):
            @pl.when(layer == idx)
            def _(p_ref=p_refs[idx]):
                g = p_ref[0].astype(F32)
                for k in range(1, npart):
                    g = g + p_ref[k].astype(F32)
                nm = ADAM_B1 * m_ref[...] + (1.0 - ADAM_B1) * g
                nv = ADAM_B2 * v_ref[...] + (1.0 - ADAM_B2) * (g * g)
                g_ref[...] = g
                nm_ref[...] = nm
                nv_ref[...] = nv
                d_ref[...] = -ADAM_LR * ((nm * c1) / (jnp.sqrt(nv * c2) + ADAM_EPS) + ADAM_WD * w_ref[...])

    def part_spec(idx):
        return pl.BlockSpec((npart, tr, c), lambda l, i: (0, jnp.where(l == idx, i, 0), 0))

    blk = pl.BlockSpec((None, tr, c), lambda l, i: (l, i, 0))
    shp = jax.ShapeDtypeStruct((nl, r, c), F32)
    return pl.pallas_call(
        body, name=name, grid=(nl, r // tr),
        in_specs=[part_spec(idx) for idx in range(nl)] + [blk, blk, blk],
        out_specs=[blk, blk, blk, blk], out_shape=[shp, shp, shp, shp],
        compiler_params=_params(("parallel", "parallel")),
    )(*parts_list, w, m, v)


def _full_from_slots(slots, shard_shape, axis):
    a = slots.reshape((N_DEV,) + tuple(shard_shape))
    a = jnp.moveaxis(a, 0, axis)
    full = list(shard_shape)
    full[axis] *= N_DEV
    return a.reshape(full)


def kernel(x, c, ada_w, ada_b, norm1_g, norm2_g, pool_w_in, pool_w_grp, pool_scale, pool_w_out, kv_norm_g, kv_ada_w, kv_ada_b, w_kv, attn_w_q, attn_w_o, ffn_w_up, ffn_conv_w, ffn_conv_b, ffn_w_down, final_g, loss_target, m_ada_w, m_ada_b, m_norm1_g, m_norm2_g, m_pool_w_in, m_pool_w_grp, m_pool_scale, m_pool_w_out, m_kv_norm_g, m_kv_ada_w, m_kv_ada_b, m_w_kv, m_attn_w_q, m_attn_w_o, m_ffn_w_up, m_ffn_conv_w, m_ffn_conv_b, m_ffn_w_down, m_final_g, v_ada_w, v_ada_b, v_norm1_g, v_norm2_g, v_pool_w_in, v_pool_w_grp, v_pool_scale, v_pool_w_out, v_kv_norm_g, v_kv_ada_w, v_kv_ada_b, v_w_kv, v_attn_w_q, v_attn_w_o, v_ffn_w_up, v_ffn_conv_w, v_ffn_conv_b, v_ffn_w_down, v_final_g):
    weights = dict(ada_w=ada_w, ada_b=ada_b, norm1_g=norm1_g, norm2_g=norm2_g, pool_w_in=pool_w_in,
                   pool_w_grp=pool_w_grp, pool_scale=pool_scale, pool_w_out=pool_w_out, kv_norm_g=kv_norm_g,
                   kv_ada_w=kv_ada_w, kv_ada_b=kv_ada_b, w_kv=w_kv, attn_w_q=attn_w_q, attn_w_o=attn_w_o,
                   ffn_w_up=ffn_w_up, ffn_conv_w=ffn_conv_w, ffn_conv_b=ffn_conv_b, ffn_w_down=ffn_w_down,
                   final_g=final_g)
    mom1 = dict(ada_w=m_ada_w, ada_b=m_ada_b, norm1_g=m_norm1_g, norm2_g=m_norm2_g, pool_w_in=m_pool_w_in,
                pool_w_grp=m_pool_w_grp, pool_scale=m_pool_scale, pool_w_out=m_pool_w_out, kv_norm_g=m_kv_norm_g,
                kv_ada_w=m_kv_ada_w, kv_ada_b=m_kv_ada_b, w_kv=m_w_kv, attn_w_q=m_attn_w_q, attn_w_o=m_attn_w_o,
                ffn_w_up=m_ffn_w_up, ffn_conv_w=m_ffn_conv_w, ffn_conv_b=m_ffn_conv_b, ffn_w_down=m_ffn_w_down,
                final_g=m_final_g)
    mom2 = dict(ada_w=v_ada_w, ada_b=v_ada_b, norm1_g=v_norm1_g, norm2_g=v_norm2_g, pool_w_in=v_pool_w_in,
                pool_w_grp=v_pool_w_grp, pool_scale=v_pool_scale, pool_w_out=v_pool_w_out, kv_norm_g=v_kv_norm_g,
                kv_ada_w=v_kv_ada_w, kv_ada_b=v_kv_ada_b, w_kv=v_w_kv, attn_w_q=v_attn_w_q, attn_w_o=v_attn_w_o,
                ffn_w_up=v_ffn_w_up, ffn_conv_w=v_ffn_conv_w, ffn_conv_b=v_ffn_conv_b, ffn_w_down=v_ffn_w_down,
                final_g=v_final_g)
    order = list(weights)

    seq, d = x.shape[1], x.shape[2]
    depth = ada_w.shape[0]
    n_pool = pool_w_in.shape[0]
    f = ffn_conv_b.shape[1]
    nbr = len(BRANCHES)
    nh = d // HEAD_DIM
    slopes = _alibi_slopes(nbr * nh).reshape(nbr, nh)
    me = 4 * lax.axis_index("x") + 2 * lax.axis_index("y") + lax.axis_index("c")
    xs = x[0]
    tgt = loss_target[0]

    cond = c * (1.0 / (1.0 + jnp.exp(-c)))
    small_in = jnp.concatenate([cond.reshape(-1), ffn_conv_w.reshape(-1), pool_scale.reshape(-1)])
    n_small_in = small_in.shape[0]
    gath = _all_gather(_to_rows(small_in), "gather_small").reshape(N_DEV, -1)[:, :n_small_in]
    cond_all = gath[:, :d]
    o1 = d + ffn_conv_w.size
    conv_w_full = _full_from_slots(gath[:, d:o1], ffn_conv_w.shape, 2)
    pool_scale_full = _full_from_slots(gath[:, o1:], pool_scale.shape, 1)
    cond16 = jnp.concatenate([cond_all, jnp.zeros_like(cond_all)], axis=0)

    mod_part = _ada_fwd(cond16, ada_w, "ada_fwd")[:, :N_DEV]
    kv_part = _ada_fwd(cond16, kv_ada_w[None], "kv_ada_fwd")[0, :N_DEV]
    n_mod = depth * mod_part.shape[2] + kv_part.shape[1]
    send = jnp.concatenate([jnp.moveaxis(mod_part, 1, 0).reshape(N_DEV, -1), kv_part], axis=1)
    send_rows = jax.vmap(_to_rows)(send)
    got = _all_to_all(send_rows, "exchange_mod").reshape(N_DEV, -1)[:, :n_mod]
    ncol = mod_part.shape[2]
    mods = []
    for l in range(depth):
        row = got[:, l * ncol:(l + 1) * ncol].reshape(1, -1) + ada_b[l][None]
        mods.append([row[:, k * d:(k + 1) * d] for k in range(6)])
    kv_row = got[:, depth * ncol:].reshape(1, -1) + kv_ada_b[None]
    kv_shift, kv_scale = kv_row[:, :d], kv_row[:, d:]

    def vec(a):
        return a.reshape(1, -1)

    nq = 4
    fq = f // nq
    ng = len(POOL_WINDOWS)
    cw_slots = jnp.moveaxis(conv_w_full.reshape(depth, CONV_WIDTH, nq, fq), 2, 1)
    cb_slots = ffn_conv_b.reshape(depth, nq, 1, fq)

    def shard_list(l):
        if l < n_pool:
            srcs = [pool_w_in[l], pool_w_grp[l].reshape(-1, pool_w_grp.shape[-1]), pool_w_out[l]]
        else:
            j = l - n_pool
            srcs = [attn_w_q[j], attn_w_o[j]] + ([w_kv] if j == 0 else [])
        return [a.astype(BF) for a in srcs + [ffn_w_up[l], ffn_w_down[l]]]

    gathered = [_all_gather_many(shard_list(l), f"gather_weights_{l}") for l in range(depth)]

    saved = []
    xcur = xs
    kvs = None
    hkv = None
    x_kv = None
    w_kv_slots = None
    for l in range(depth):
        sh1, sc1, g1, sh2, sc2, g2 = mods[l]
        gw = gathered[l]
        w_up_slots, w_down = gw[-2], gw[-1].reshape(f, d)
        st = dict(x=xcur, w_up=w_up_slots, w_down=w_down)
        h = _norm_mod(xcur, vec(norm1_g[l]), sc1, sh1, f"norm1_{l}")
        st["h"] = h
        if l < n_pool:
            w_in = gw[0].reshape(d, d)
            w_grp = jnp.moveaxis(gw[1].reshape(N_DEV, ng, -1, d // ng), 0, 1).reshape(ng, d // ng, d // ng)
            w_out = gw[2].reshape(d, d)
            u = _mm(h, w_in, f"pool_in_{l}")
            pooled = _pool_fwd(u, f"pool_fwd_{l}")
            z, y = _grp_fwd(pooled, w_grp, vec(pool_scale_full[l]), f"grp_fwd_{l}")
            mix, x1 = _mm(y, w_out, f"pool_out_{l}", gate=g1, resid=xcur, pre_dtype=BF)
            st.update(pooled=pooled, z=z, y=y, w_in=w_in, w_grp=w_grp, w_out=w_out)
        else:
            j = l - n_pool
            w_q_slots, w_o = gw[0], gw[1].reshape(d, d)
            if j == 0:
                w_kv_slots = gw[2]
                x_kv = xcur
                hkv = _norm_mod(xcur, vec(kv_norm_g), kv_scale, kv_shift, "norm_kv")
                kvs = _mm(hkv, w_kv_slots, "kv_proj", bf="qkn", out_dtype=BF)
            q = _mm(h, w_q_slots, f"q_proj_{l}", bf="qkn", out_dtype=BF)
            outs, lses = [], []
            for g in range(nbr):
                og, lg = _attn_fwd(q, kvs, g, slopes[g], d, f"attn_fwd_{l}_{g}")
                outs.append(og)
                lses.append(lg)
            o, lt = _attn_combine(outs, lses, f"attn_mix_{l}")
            mix, x1 = _mm(o, w_o, f"attn_out_{l}", gate=g1, resid=xcur, pre_dtype=BF)
            st.update(q=q, o=o, lt=lt, w_q=w_q_slots, w_o=w_o)
        h2 = _norm_mod(x1, vec(norm2_g[l]), sc2, sh2, f"norm2_{l}")
        up = _mm(h2, w_up_slots, f"ffn_up_{l}", bf="qkn", of="qmn", out_dtype=BF).reshape(2, nq, seq, fq)
        act = _ffn_act(up, cw_slots[l], cb_slots[l], f"ffn_act_{l}")
        ffo, x2 = _mm(act, w_down, f"ffn_down_{l}", af="qmk", gate=g2, resid=x1, pre_dtype=BF)
        st.update(mix=mix, x1=x1, h2=h2, up=up, act=act, ffo=ffo)
        saved.append(st)
        xcur = x2

    dx, loss_blk, d_final_g = _final_loss(xcur, tgt, vec(final_g), "final_loss")
    loss = lax.psum(loss_blk[0, 0], ("x", "y", "c"))

    d_mod = [None] * depth
    d_n1 = [None] * depth
    d_n2 = [None] * depth
    d_conv = [None] * depth
    d_pscale = [None] * n_pool
    dk_acc = [None] * nbr
    dv_acc = [None] * nbr
    parts = [None] * depth
    for l in reversed(range(depth)):
        sh1, sc1, g1, sh2, sc2, g2 = mods[l]
        st = saved[l]
        dffo, dg2 = _gate_bwd(dx, st["ffo"], g2, f"gate2_bwd_{l}")
        g_down = _mm(st["act"], dffo, f"ffn_down_dw_{l}", af="qkm", out_dtype=BF)
        dact = _mm(dffo, st["w_down"], f"ffn_down_dx_{l}", bf="nk", of="qmn", tn=fq, out_dtype=BF)
        dup, dc = _ffn_act_bwd(st["up"], dact, cw_slots[l], cb_slots[l], f"ffn_act_bwd_{l}")
        dup = dup.reshape(N_DEV, seq, -1)
        d_conv[l] = dc
        g_up = _mm(st["h2"], dup, f"ffn_up_dw_{l}", af="km", bf="qkn", of="qmn", out_dtype=BF)
        dh2 = _mm(dup, st["w_up"], f"ffn_up_dx_{l}", af="qmk", bf="qnk")
        dx1, dsh2, dw2 = _norm_mod_bwd(st["x1"], dh2, vec(norm2_g[l]), sc2, dx, f"norm2_bwd_{l}")
        d_n2[l] = dw2 * (1.0 + sc2)
        dsc2 = dw2 * vec(norm2_g[l])

        dmix, dg1 = _gate_bwd(dx1, st["mix"], g1, f"gate1_bwd_{l}")
        if l < n_pool:
            g_out = _mm(st["y"], dmix, f"pool_out_dw_{l}", af="km", out_dtype=BF)
            dy = _mm(dmix, st["w_out"], f"pool_out_dx_{l}", bf="nk", out_dtype=BF)
            dz, dpool, dps = _grp_bwd(dy, st["z"], st["w_grp"], vec(pool_scale_full[l]), f"grp_bwd_{l}")
            d_pscale[l] = dps
            g_grp = _grp_dw(st["pooled"], dz, ng, f"grp_dw_{l}")
            du = _pool_bwd(dpool, f"pool_bwd_{l}")
            g_in = _mm(st["h"], du, f"pool_in_dw_{l}", af="km", out_dtype=BF)
            dh = _mm(du, st["w_in"], f"pool_in_dx_{l}", bf="nk")
            g_grp_slots = jnp.moveaxis(g_grp.reshape(ng, N_DEV, -1, d // ng), 1, 0).reshape(N_DEV, -1, d // ng)
            send = [g_in.reshape(N_DEV, -1, d), g_grp_slots.astype(BF), g_out.reshape(N_DEV, -1, d)]
        else:
            j = l - n_pool
            g_o = _mm(st["o"], dmix, f"attn_out_dw_{l}", af="km", out_dtype=BF)
            do = _mm(dmix, st["w_o"], f"attn_out_dx_{l}", bf="nk", out_dtype=BF)
            dqs = []
            for g in range(nbr):
                dq_g, dk_g, dv_g = _attn_bwd(st["q"], kvs, do, st["o"], st["lt"], g, slopes[g], d,
                                             f"attn_bwd_{l}_{g}", dk_in=dk_acc[g], dv_in=dv_acc[g])
                dqs.append(dq_g)
                dk_acc[g], dv_acc[g] = dk_g, dv_g
            dq = jnp.concatenate(dqs, axis=1)
            nqc = st["w_q"].shape[2]
            g_q = _mm(st["h"], dq, f"q_proj_dw_{l}", af="km", of="qmn", tn=nqc, out_dtype=BF)
            dh = _mm(dq, st["w_q"], f"q_proj_dx_{l}", bf="qnk")
            send = [g_q, g_o.reshape(N_DEV, -1, d)]
        dx0, dsh1, dw1 = _norm_mod_bwd(st["x"], dh, vec(norm1_g[l]), sc1, dx1, f"norm1_bwd_{l}")
        d_n1[l] = dw1 * (1.0 + sc1)
        dsc1 = dw1 * vec(norm1_g[l])
        d_mod[l] = jnp.concatenate([dsh1, dsc1, dg1, dsh2, dsc2, dg2], axis=1)
        dx = dx0
        if l == n_pool:
            dkv = jnp.concatenate(dk_acc + dv_acc, axis=1)
            nkc = w_kv_slots.shape[2]
            g_kv = _mm(hkv, dkv, "kv_proj_dw", af="km", of="qmn", tn=nkc, out_dtype=BF)
            dhkv = _mm(dkv, w_kv_slots, "kv_proj_dx", bf="qnk")
            dx, dsh_kv, dw_kv = _norm_mod_bwd(x_kv, dhkv, vec(kv_norm_g), kv_scale, dx, "norm_kv_bwd")
            d_kv_norm = dw_kv * (1.0 + kv_scale)
            d_kv_mod = jnp.concatenate([dsh_kv, dw_kv * vec(kv_norm_g)], axis=1)
            send.append(g_kv)
        send += [g_up, g_down.reshape(N_DEV, -1, d)]
        parts[l] = _all_to_all_many(send, f"exchange_grads_{l}")
    grad_x = dx[None]

    small = [jnp.concatenate(d_mod, axis=1).reshape(-1), d_kv_mod.reshape(-1),
             jnp.concatenate(d_n1, axis=0).reshape(-1), jnp.concatenate(d_n2, axis=0).reshape(-1),
             d_kv_norm.reshape(-1), d_final_g.reshape(-1),
             jnp.stack([dcl[:, 3, :] for dcl in d_conv]).reshape(-1),
             jnp.stack([jnp.moveaxis(dcl[:, 0:CONV_WIDTH, :], 0, 1) for dcl in d_conv]).reshape(-1),
             jnp.concatenate(d_pscale, axis=0).reshape(-1)]
    sizes = [a.shape[0] for a in small]
    small_rows = _to_rows(jnp.concatenate(small))
    small_all = _all_gather(small_rows, "gather_small_grads")
    dmod_all = small_all.reshape(N_DEV, -1)[:, :sizes[0] + sizes[1]]

    dmod16 = jnp.concatenate([dmod_all, jnp.zeros_like(dmod_all)], axis=0)
    dm = dmod16[:, :sizes[0]].reshape(16, depth, N_DEV, ncol)
    dm_mine = lax.dynamic_index_in_dim(dm, me, axis=2, keepdims=False)
    g_ada_w = _ada_bwd(cond16, jnp.moveaxis(dm_mine, 0, 1), "ada_bwd")
    nkv = kv_part.shape[1]
    dkm = dmod16[:, sizes[0]:].reshape(16, N_DEV, nkv)
    dkm_mine = lax.dynamic_index_in_dim(dkm, me, axis=1, keepdims=False)
    g_kv_ada_w = _ada_bwd(cond16, dkm_mine[None], "kv_ada_bwd")

    res = {}

    def update(n, parts_list, shape3):
        w3, m3, v3 = (a[n].reshape(shape3) for a in (weights, mom1, mom2))
        outs = _adamw(parts_list, w3, m3, v3, f"adamw_{n}")
        res[n] = [a.reshape(weights[n].shape) for a in outs]

    pool_layers, attn_layers = range(n_pool), range(n_pool, depth)
    update("pool_w_in", [parts[l][0] for l in pool_layers], (n_pool, -1, d))
    update("pool_w_grp", [parts[l][1] for l in pool_layers], (n_pool, -1, d // ng))
    update("pool_w_out", [parts[l][2] for l in pool_layers], (n_pool, -1, d))
    update("attn_w_q", [parts[l][0] for l in attn_layers], attn_w_q.shape)
    update("attn_w_o", [parts[l][1] for l in attn_layers], (depth - n_pool, -1, d))
    update("w_kv", [parts[n_pool][2]], (1,) + w_kv.shape)
    update("ffn_w_up", [parts[l][-2] for l in range(depth)], ffn_w_up.shape)
    update("ffn_w_down", [parts[l][-1] for l in range(depth)], ffn_w_down.shape)
    update("ada_w", [g_ada_w[l][None] for l in range(depth)], ada_w.shape)
    update("kv_ada_w", [g_kv_ada_w], (1,) + kv_ada_w.shape)

    tot = small_all.reshape(N_DEV, -1)
    offs = np.cumsum([0] + sizes)
    seg = {k: (int(offs[i]), int(offs[i + 1])) for i, k in enumerate(
        ["mod", "kv_mod", "n1", "n2", "kv_norm", "final", "conv_b", "conv_w", "pscale"])}

    def rows_of(a, b):
        return tot[:, a:b]

    nf8 = f // N_DEV
    conv_w_parts = lax.dynamic_slice_in_dim(
        rows_of(*seg["conv_w"]).reshape(N_DEV, depth, CONV_WIDTH, N_DEV, nf8), me, 1, axis=3).reshape(N_DEV, -1)
    nd8 = d // N_DEV
    pscale_parts = lax.dynamic_slice_in_dim(
        rows_of(*seg["pscale"]).reshape(N_DEV, n_pool, N_DEV, nd8), me, 1, axis=2).reshape(N_DEV, -1)
    small_names = ["ada_b", "norm1_g", "norm2_g", "pool_scale", "kv_norm_g", "kv_ada_b", "ffn_conv_w",
                   "ffn_conv_b", "final_g"]
    small_parts = [rows_of(*seg["mod"]), rows_of(*seg["n1"]), rows_of(*seg["n2"]), pscale_parts,
                   rows_of(*seg["kv_norm"]), rows_of(*seg["kv_mod"]), conv_w_parts, rows_of(*seg["conv_b"]),
                   rows_of(*seg["final"])]
    sp = jnp.concatenate(small_parts, axis=1)
    n_sp = sp.shape[1]
    sp_rows = jax.vmap(_to_rows)(sp)

    def packed(src):
        return _to_rows(jnp.concatenate([src[n].reshape(-1) for n in small_names]))[None]

    outs = _adamw([sp_rows], packed(weights), packed(mom1), packed(mom2), "adamw_small")
    outs = [a.reshape(-1)[:n_sp] for a in outs]
    off = 0
    for n in small_names:
        size = weights[n].size
        res[n] = [a[off:off + size].reshape(weights[n].shape) for a in outs]
        off += size

    grads = [res[n][0] for n in order]
    deltas = [res[n][1] for n in order]
    new_m = [res[n][2] for n in order]
    new_v = [res[n][3] for n in order]
    return (loss, grad_x, *grads, *deltas, *new_m, *new_v)
```

```python
import math

import numpy as np
import jax
import jax.numpy as jnp
from jax import lax
from jax.experimental import pallas as pl
from jax.experimental.pallas import tpu as pltpu

F32 = jnp.float32
BF = jnp.bfloat16

POOL_WINDOWS = (2, 4, 8, 16)
BRANCHES = ((128, 1), (512, 4), (2048, 16))
HEAD_DIM = 64
ATTN_BLOCK = 128
CONV_WIDTH = 3
EPS = 1e-6
ADAM_LR = 0.001
ADAM_B1 = 0.9
ADAM_B2 = 0.999
ADAM_EPS = 1e-08
ADAM_WD = 0.01
ADAM_STEP = 10

N_DEV = 8
LANES = 128
POOL_HALO = 16
CONV_HALO = 8
VMEM_LIMIT = 48 * 1024 * 1024
MM_TILE = 1024
NEG = -1e30

MESH = pl.DeviceIdType.MESH
ANY = pl.BlockSpec(memory_space=pl.ANY)


def _params(sem=None):
    if sem is None:
        return pltpu.CompilerParams(vmem_limit_bytes=VMEM_LIMIT)
    return pltpu.CompilerParams(dimension_semantics=sem, vmem_limit_bytes=VMEM_LIMIT)


def _pick(dim, pref, mult=LANES):
    if dim <= pref:
        return dim
    t = (pref // mult) * mult
    while t >= mult:
        if dim % t == 0:
            return t
        t -= mult
    return dim


def _alibi_slopes(n):
    def pow2(m):
        start = 2.0 ** (-(2.0 ** -(math.log2(m) - 3)))
        return [start ** (i + 1) for i in range(m)]
    if math.log2(n).is_integer():
        s = pow2(n)
    else:
        c = 2 ** math.floor(math.log2(n))
        s = pow2(c) + pow2(2 * c)[0::2][: n - c]
    s = np.asarray(s, dtype=np.float32)
    return -np.sort(-s)


def _my_place():
    return lax.axis_index("x"), lax.axis_index("y"), lax.axis_index("c")


def _all_gather_many(xs, name):
    n = len(xs)

    def body(*refs):
        x_refs, out_refs = refs[:n], refs[n:2 * n]
        send_sems, recv_sems, local_sems = refs[2 * n:]
        xi, yi, ci = _my_place()
        me, sibling = (xi, yi, ci), (xi, yi, 1 - ci)
        chips = [(1 - xi, yi), (xi, 1 - yi), (1 - xi, 1 - yi)]

        def slot(a, px, py, pc):
            return out_refs[a].at[4 * px + 2 * py + pc]

        def copy(a, k, block, to, src=None):
            return pltpu.make_async_remote_copy(
                src_ref=slot(a, *block) if src is None else src, dst_ref=slot(a, *block),
                send_sem=send_sems.at[7 * a + k], recv_sem=recv_sems.at[7 * a + k],
                device_id=to, device_id_type=MESH)

        mine = [pltpu.make_async_copy(x_refs[a], slot(a, *me), local_sems.at[a]) for a in range(n)]
        for cp in mine:
            cp.start()
        sent = []
        for a in range(n):
            first = [copy(a, 0, me, sibling, src=x_refs[a])]
            first += [copy(a, 1 + j, me, (*chip, ci), src=x_refs[a]) for j, chip in enumerate(chips)]
            for cp in first:
                cp.start()
            sent += first
        for a in range(n):
            for j, chip in enumerate(chips):
                copy(a, 1 + j, (*chip, ci), me).wait_recv()
                fwd = copy(a, 4 + j, (*chip, ci), sibling)
                fwd.start()
                sent.append(fwd)
        for a in range(n):
            copy(a, 0, sibling, me).wait_recv()
            for j, chip in enumerate(chips):
                copy(a, 4 + j, (*chip, 1 - ci), me).wait_recv()
        for cp in sent:
            cp.wait_send()
        for cp in mine:
            cp.wait()

    return pl.pallas_call(
        body, name=name,
        out_shape=[jax.ShapeDtypeStruct((N_DEV,) + x.shape, x.dtype) for x in xs],
        in_specs=[ANY] * n, out_specs=[ANY] * n,
        scratch_shapes=[pltpu.SemaphoreType.DMA((7 * n,)), pltpu.SemaphoreType.DMA((7 * n,)),
                        pltpu.SemaphoreType.DMA((n,))],
    )(*xs)


def _all_gather(x, name):
    return _all_gather_many([x], name)[0]


def _all_to_all_many(xs, name):
    n = len(xs)

    def body(*refs):
        x_refs, out_refs = refs[:n], refs[n:2 * n]
        send_sems, recv_sems, local_sems = refs[2 * n:]
        xi, yi, ci = _my_place()
        me = 4 * xi + 2 * yi + ci
        mine = [pltpu.make_async_copy(x_refs[a].at[me], out_refs[a].at[me], local_sems.at[a]) for a in range(n)]
        for cp in mine:
            cp.start()
        copies = []
        for a in range(n):
            for k in range(1, N_DEV):
                px = 1 - xi if k & 4 else xi
                py = 1 - yi if k & 2 else yi
                pc = 1 - ci if k & 1 else ci
                peer = 4 * px + 2 * py + pc
                cp = pltpu.make_async_remote_copy(
                    src_ref=x_refs[a].at[peer], dst_ref=out_refs[a].at[me],
                    send_sem=send_sems.at[7 * a + k - 1], recv_sem=recv_sems.at[7 * a + k - 1],
                    device_id=(px, py, pc), device_id_type=MESH)
                cp.start()
                copies.append(cp)
        for cp in copies:
            cp.wait()
        for cp in mine:
            cp.wait()

    return pl.pallas_call(
        body, name=name,
        out_shape=[jax.ShapeDtypeStruct(x.shape, x.dtype) for x in xs],
        in_specs=[ANY] * n, out_specs=[ANY] * n,
        scratch_shapes=[pltpu.SemaphoreType.DMA((7 * n,)), pltpu.SemaphoreType.DMA((7 * n,)),
                        pltpu.SemaphoreType.DMA((n,))],
    )(*xs)


def _all_to_all(x, name):
    return _all_to_all_many([x], name)[0]


HBM = pl.BlockSpec(memory_space=pltpu.HBM)
SEM = pl.BlockSpec(memory_space=pltpu.SEMAPHORE)
EFFECT = pltpu.SideEffectType.DATAFLOW_SIDE_EFFECTING


def _peer(k, xi, yi, ci):
    px = 1 - xi if k & 4 else xi
    py = 1 - yi if k & 2 else yi
    pc = 1 - ci if k & 1 else ci
    return (px, py, pc), 4 * px + 2 * py + pc


def _split_copies(x_refs, land_refs, send_sem, recv_sem, scatter):
    xi, yi, ci = _my_place()
    me = 4 * xi + 2 * yi + ci
    copies = []
    for p, (x_ref, land_ref) in enumerate(zip(x_refs, land_refs)):
        for k in range(1, N_DEV):
            place, peer = _peer(k, xi, yi, ci)
            copies.append(pltpu.make_async_remote_copy(
                src_ref=x_ref.at[peer] if scatter else x_ref, dst_ref=land_ref.at[me],
                send_sem=send_sem.at[7 * p + k - 1], recv_sem=recv_sem.at[7 * p + k - 1],
                device_id=place, device_id_type=MESH))
    return copies


def _comm_start(groups, name, scatter):
    sizes = [len(g) for g in groups]
    xs = [x for g in groups for x in g]
    n, ng = len(xs), len(groups)
    lands = [lax.empty(x.shape if scatter else (N_DEV,) + x.shape, x.dtype) for x in xs]
    starts = np.cumsum([0] + sizes)

    def body(*refs):
        x_refs, land_refs = refs[:n], refs[n:2 * n]
        send_sems, recv_sems = refs[2 * n:2 * n + ng], refs[2 * n + ng:2 * n + 2 * ng]
        token = refs[2 * n + 2 * ng + 2 * n]
        local_sems = refs[-1]
        xi, yi, ci = _my_place()
        me = 4 * xi + 2 * yi + ci
        mine = [pltpu.make_async_copy(x_refs[a].at[me] if scatter else x_refs[a], land_refs[a].at[me],
                                      local_sems.at[a]) for a in range(n)]
        for cp in mine:
            cp.start()
        for gi in range(ng):
            lo, hi = int(starts[gi]), int(starts[gi + 1])
            for cp in _split_copies(x_refs[lo:hi], land_refs[lo:hi], send_sems[gi], recv_sems[gi], scatter):
                cp.start()
        for cp in mine:
            cp.wait()
        token[...] = jnp.zeros_like(token)

    sem_shapes = [pltpu.SemaphoreType.DMA((7 * m,)) for m in sizes]
    thru = [pltpu.HBM(a.shape, a.dtype) for a in xs + lands]
    res = pl.pallas_call(
        body, name=name,
        out_shape=sem_shapes + sem_shapes + thru + [jax.ShapeDtypeStruct((8, LANES), F32)],
        in_specs=[HBM] * (2 * n),
        out_specs=[SEM] * (2 * ng) + [HBM] * (2 * n) + [pl.BlockSpec(memory_space=pltpu.VMEM)],
        input_output_aliases={i: 2 * ng + i for i in range(2 * n)},
        scratch_shapes=[pltpu.SemaphoreType.DMA((n,))],
        compiler_params=pltpu.CompilerParams(has_side_effects=EFFECT),
    )(*[pltpu.with_memory_space_constraint(a, pltpu.HBM) for a in xs + lands])
    send_sems, recv_sems = res[:ng], res[ng:2 * ng]
    x_thru, land_thru = res[2 * ng:2 * ng + n], res[2 * ng + n:2 * ng + 2 * n]
    handles = []
    for gi in range(ng):
        lo, hi = int(starts[gi]), int(starts[gi + 1])
        handles.append((send_sems[gi], recv_sems[gi], list(x_thru[lo:hi]), list(land_thru[lo:hi]), scatter))
    return handles, res[-1][0, 0]


def _comm_wait(handle, after, name):
    send_sem, recv_sem, x_thru, land_thru, scatter = handle
    m = len(x_thru)

    def body(*refs):
        x_refs, land_refs = refs[:m], refs[m:2 * m]
        for cp in _split_copies(x_refs, land_refs, refs[2 * m], refs[2 * m + 1], scatter):
            cp.wait_send()
            cp.wait_recv()

    res = pl.pallas_call(
        body, name=name,
        out_shape=[pltpu.HBM(a.shape, a.dtype) for a in x_thru + land_thru],
        in_specs=[HBM] * (2 * m) + [SEM, SEM, ANY], out_specs=[HBM] * (2 * m),
        input_output_aliases={i: i for i in range(2 * m)},
        compiler_params=pltpu.CompilerParams(has_side_effects=EFFECT),
    )(*x_thru, *land_thru, send_sem, recv_sem, after)
    return list(res[m:])


def _to_rows(vec):
    n = vec.shape[0]
    unit = 8 * LANES
    pad = (-n) % unit
    if pad:
        vec = jnp.concatenate([vec, jnp.zeros((pad,), vec.dtype)])
    return vec.reshape(-1, LANES)


def _mm(a, b, name, *, af="mk", bf="kn", of="mn", out_dtype=F32, tm=None, tn=None, tk=None,
        gate=None, resid=None, pre_dtype=None):
    if af == "mk":
        m, kk = a.shape
    elif af == "km":
        kk, m = a.shape
    elif af == "qmk":
        qa, m, tk = a.shape
        kk = qa * tk
    else:
        qa, kk, tm = a.shape
        m = qa * tm
    if bf == "kn":
        k2, n = b.shape
    elif bf == "nk":
        n, k2 = b.shape
    elif bf == "qkn":
        qb, k2, tn = b.shape
        n = qb * tn
    else:
        qb, n, tkb = b.shape
        k2 = qb * tkb
        assert af != "qmk" or tkb == tk
        tk = tkb
    assert kk == k2, (name, a.shape, b.shape, af, bf)
    tm = _pick(m, MM_TILE) if tm is None else tm
    tn = _pick(n, MM_TILE) if tn is None else tn
    tk = _pick(kk, MM_TILE) if tk is None else tk
    assert m % tm == 0 and n % tn == 0 and kk % tk == 0, (name, m, n, kk, tm, tn, tk)
    nk = kk // tk
    a_spec = {"mk": pl.BlockSpec((tm, tk), lambda i, j, k: (i, k)),
              "km": pl.BlockSpec((tk, tm), lambda i, j, k: (k, i)),
              "qmk": pl.BlockSpec((None, tm, tk), lambda i, j, k: (k, i, 0)),
              "qkm": pl.BlockSpec((None, tk, tm), lambda i, j, k: (i, k, 0))}[af]
    b_spec = {"kn": pl.BlockSpec((tk, tn), lambda i, j, k: (k, j)),
              "nk": pl.BlockSpec((tn, tk), lambda i, j, k: (j, k)),
              "qkn": pl.BlockSpec((None, tk, tn), lambda i, j, k: (j, k, 0)),
              "qnk": pl.BlockSpec((None, tn, tk), lambda i, j, k: (k, j, 0))}[bf]
    dims = (((1 if af in ("mk", "qmk") else 0,), (0 if bf in ("kn", "qkn") else 1,)), ((), ()))
    in_specs, args = [a_spec, b_spec], [a, b]
    if gate is not None:
        assert of == "mn"
        in_specs.append(pl.BlockSpec((1, tn), lambda i, j, k: (0, j)))
        args.append(gate)
    if resid is not None:
        assert of == "mn"
        in_specs.append(pl.BlockSpec((tm, tn), lambda i, j, k: (i, j)))
        args.append(resid)
    if of == "mn":
        o_spec, o_shape = pl.BlockSpec((tm, tn), lambda i, j, k: (i, j)), (m, n)
    else:
        o_spec, o_shape = pl.BlockSpec((None, tm, tn), lambda i, j, k: (j, i, 0)), (n // tn, m, tn)
    out_shape, out_specs = [jax.ShapeDtypeStruct(o_shape, out_dtype)], [o_spec]
    if pre_dtype is not None:
        out_shape.insert(0, jax.ShapeDtypeStruct(o_shape, pre_dtype))
        out_specs.insert(0, o_spec)
    n_in = len(args)
    n_out = len(out_shape)

    def body(*refs):
        a_ref, b_ref = refs[0], refs[1]
        extra = list(refs[2:n_in])
        outs = refs[n_in:n_in + n_out]
        gate_ref = extra.pop(0) if gate is not None else None
        resid_ref = extra.pop(0) if resid is not None else None

        def product():
            return lax.dot_general(a_ref[...].astype(BF), b_ref[...].astype(BF), dims, preferred_element_type=F32)

        def finish(r):
            if pre_dtype is not None:
                outs[0][...] = r.astype(pre_dtype)
            if gate_ref is not None:
                r = r * gate_ref[...]
            if resid_ref is not None:
                r = resid_ref[...] + r
            outs[-1][...] = r.astype(out_dtype)

        if nk == 1:
            finish(product())
        else:
            acc = refs[n_in + n_out]
            k = pl.program_id(2)

            @pl.when(k == 0)
            def _():
                acc[...] = product()

            @pl.when(k > 0)
            def _():
                acc[...] += product()

            @pl.when(k == nk - 1)
            def _():
                finish(acc[...])

    res = pl.pallas_call(
        body, name=name, grid=(m // tm, n // tn, nk),
        in_specs=in_specs, out_specs=out_specs, out_shape=out_shape,
        scratch_shapes=[pltpu.VMEM((tm, tn), F32)] if nk > 1 else [],
        compiler_params=_params(("parallel", "parallel", "arbitrary")),
    )(*args)
    return res if pre_dtype is not None else res[0]


def _ada_fwd(cond16, w, name):
    nl, d, n = w.shape

    def body(c_ref, w_ref, o_ref):
        o_ref[...] = jnp.dot(c_ref[...].astype(BF), w_ref[...].astype(BF), preferred_element_type=F32)

    return pl.pallas_call(
        body, name=name, grid=(nl,),
        in_specs=[pl.BlockSpec((16, d), lambda l: (0, 0)), pl.BlockSpec((None, d, n), lambda l: (l, 0, 0))],
        out_specs=pl.BlockSpec((None, 16, n), lambda l: (l, 0, 0)),
        out_shape=jax.ShapeDtypeStruct((nl, 16, n), F32),
        compiler_params=_params(("parallel",)),
    )(cond16, w)


def _ada_bwd(cond16, dmod, name):
    nl, _, n = dmod.shape
    d = cond16.shape[1]

    def body(c_ref, g_ref, o_ref):
        o_ref[...] = lax.dot_general(c_ref[...].astype(BF), g_ref[...].astype(BF), (((0,), (0,)), ((), ())),
                                     preferred_element_type=F32)

    return pl.pallas_call(
        body, name=name, grid=(nl,),
        in_specs=[pl.BlockSpec((16, d), lambda l: (0, 0)), pl.BlockSpec((None, 16, n), lambda l: (l, 0, 0))],
        out_specs=pl.BlockSpec((None, d, n), lambda l: (l, 0, 0)),
        out_shape=jax.ShapeDtypeStruct((nl, d, n), F32),
        compiler_params=_params(("parallel",)),
    )(cond16, dmod)


def _row_spec(tm, d):
    return pl.BlockSpec((tm, d), lambda i: (i, 0))


def _vec_spec(d):
    return pl.BlockSpec((1, d), lambda i: (0, 0))


def _norm_mod(x, g, sc, sh, name):
    s, d = x.shape
    tm = _pick(s, 512, 8)

    def body(x_ref, g_ref, sc_ref, sh_ref, o_ref):
        xv = x_ref[...]
        r = lax.rsqrt(jnp.mean(xv * xv, axis=-1, keepdims=True) + EPS)
        y = (xv * r) * g_ref[...]
        o_ref[...] = (y * (1.0 + sc_ref[...]) + sh_ref[...]).astype(BF)

    return pl.pallas_call(
        body, name=name, grid=(s // tm,),
        in_specs=[_row_spec(tm, d), _vec_spec(d), _vec_spec(d), _vec_spec(d)],
        out_specs=_row_spec(tm, d), out_shape=jax.ShapeDtypeStruct((s, d), BF),
        compiler_params=_params(("parallel",)),
    )(x, g, sc, sh)


def _norm_mod_bwd(x, dh, g, sc, dx_in, name):
    s, d = x.shape
    tm = _pick(s, 512, 8)

    def body(x_ref, dh_ref, g_ref, sc_ref, dxin_ref, dx_ref, dsh_ref, dw_ref):
        i = pl.program_id(0)
        xv = x_ref[...]
        dhv = dh_ref[...].astype(F32)
        r = lax.rsqrt(jnp.mean(xv * xv, axis=-1, keepdims=True) + EPS)
        xn = xv * r
        dxn = dhv * (g_ref[...] * (1.0 + sc_ref[...]))
        dx_ref[...] = dxin_ref[...] + r * (dxn - xn * jnp.mean(dxn * xn, axis=-1, keepdims=True))

        @pl.when(i == 0)
        def _():
            dsh_ref[...] = jnp.zeros_like(dsh_ref)
            dw_ref[...] = jnp.zeros_like(dw_ref)

        dsh_ref[...] += jnp.sum(dhv, axis=0, keepdims=True)
        dw_ref[...] += jnp.sum(dhv * xn, axis=0, keepdims=True)

    return pl.pallas_call(
        body, name=name, grid=(s // tm,),
        in_specs=[_row_spec(tm, d), _row_spec(tm, d), _vec_spec(d), _vec_spec(d), _row_spec(tm, d)],
        out_specs=[_row_spec(tm, d), _vec_spec(d), _vec_spec(d)],
        out_shape=[jax.ShapeDtypeStruct((s, d), F32), jax.ShapeDtypeStruct((1, d), F32),
                   jax.ShapeDtypeStruct((1, d), F32)],
        compiler_params=_params(("arbitrary",)),
    )(x, dh, g, sc, dx_in)


def _gate_bwd(dx, y, gate, name):
    s, d = dx.shape
    tm = _pick(s, 512, 8)

    def body(dx_ref, y_ref, g_ref, dy_ref, dg_ref):
        i = pl.program_id(0)
        dxv = dx_ref[...]
        dy_ref[...] = (dxv * g_ref[...]).astype(BF)

        @pl.when(i == 0)
        def _():
            dg_ref[...] = jnp.zeros_like(dg_ref)

        dg_ref[...] += jnp.sum(dxv * y_ref[...].astype(F32), axis=0, keepdims=True)

    return pl.pallas_call(
        body, name=name, grid=(s // tm,),
        in_specs=[_row_spec(tm, d), _row_spec(tm, d), _vec_spec(d)],
        out_specs=[_row_spec(tm, d), _vec_spec(d)],
        out_shape=[jax.ShapeDtypeStruct((s, d), BF), jax.ShapeDtypeStruct((1, d), F32)],
        compiler_params=_params(("arbitrary",)),
    )(dx, y, gate)


def _final_loss(x, tgt, g, name):
    s, d = x.shape
    tm = _pick(s, 512, 8)

    def body(x_ref, t_ref, g_ref, dx_ref, loss_ref, dg_ref):
        i = pl.program_id(0)
        xv = x_ref[...]
        gv = g_ref[...]
        r = lax.rsqrt(jnp.mean(xv * xv, axis=-1, keepdims=True) + EPS)
        xn = xv * r
        err = xn * gv - t_ref[...]
        dy = err * (1.0 / d)
        dxn = dy * gv
        dx_ref[...] = r * (dxn - xn * jnp.mean(dxn * xn, axis=-1, keepdims=True))

        @pl.when(i == 0)
        def _():
            loss_ref[...] = jnp.zeros_like(loss_ref)
            dg_ref[...] = jnp.zeros_like(dg_ref)

        part = 0.5 * jnp.sum(jnp.sum(err * err, axis=-1, keepdims=True) * (1.0 / d), axis=0, keepdims=True)
        loss_ref[...] += jnp.broadcast_to(part, loss_ref.shape)
        dg_ref[...] += jnp.sum(dy * xn, axis=0, keepdims=True)

    return pl.pallas_call(
        body, name=name, grid=(s // tm,),
        in_specs=[_row_spec(tm, d), _row_spec(tm, d), _vec_spec(d)],
        out_specs=[_row_spec(tm, d), pl.BlockSpec((8, LANES), lambda i: (0, 0)), _vec_spec(d)],
        out_shape=[jax.ShapeDtypeStruct((s, d), F32), jax.ShapeDtypeStruct((8, LANES), F32),
                   jax.ShapeDtypeStruct((1, d), F32)],
        compiler_params=_params(("arbitrary",)),
    )(x, tgt, g)


def _pool_counts(tm, gd, row0, w):
    t = lax.broadcasted_iota(jnp.int32, (tm, gd), 0) + row0
    return jnp.minimum(t + 1, w).astype(F32)


def _pool_fwd(u, name):
    s, d = u.shape
    tm = _pick(s, 256, POOL_HALO)
    gd = d // len(POOL_WINDOWS)
    per = tm // POOL_HALO

    def body(prev_ref, cur_ref, o_ref, ext):
        i = pl.program_id(0)
        ext[0:POOL_HALO, :] = jnp.where(i > 0, prev_ref[...], 0.0)
        ext[POOL_HALO:, :] = cur_ref[...]
        for g, w in enumerate(POOL_WINDOWS):
            cols = slice(g * gd, (g + 1) * gd)
            acc = ext[POOL_HALO:POOL_HALO + tm, cols]
            own = acc
            for k in range(1, w):
                acc = acc + ext[POOL_HALO - k:POOL_HALO - k + tm, cols]
            o_ref[:, cols] = (acc / _pool_counts(tm, gd, i * tm, w) - own).astype(BF)

    return pl.pallas_call(
        body, name=name, grid=(s // tm,),
        in_specs=[pl.BlockSpec((POOL_HALO, d), lambda i: (jnp.maximum(i * per - 1, 0), 0)), _row_spec(tm, d)],
        out_specs=_row_spec(tm, d), out_shape=jax.ShapeDtypeStruct((s, d), BF),
        scratch_shapes=[pltpu.VMEM((tm + POOL_HALO, d), F32)],
        compiler_params=_params(("parallel",)),
    )(u, u)


def _pool_bwd(dp, name):
    s, d = dp.shape
    tm = _pick(s, 256, POOL_HALO)
    gd = d // len(POOL_WINDOWS)
    per = tm // POOL_HALO
    nt = s // tm
    last_halo = s // POOL_HALO - 1

    def body(cur_ref, nxt_ref, o_ref, ext):
        i = pl.program_id(0)
        for g, w in enumerate(POOL_WINDOWS):
            cols = slice(g * gd, (g + 1) * gd)
            ext[0:tm, cols] = cur_ref[:, cols].astype(F32) / _pool_counts(tm, gd, i * tm, w)
            nxt = nxt_ref[:, cols].astype(F32) / _pool_counts(POOL_HALO, gd, (i + 1) * tm, w)
            ext[tm:, cols] = jnp.where(i < nt - 1, nxt, 0.0)
        for g, w in enumerate(POOL_WINDOWS):
            cols = slice(g * gd, (g + 1) * gd)
            acc = ext[0:tm, cols]
            for k in range(1, w):
                acc = acc + ext[k:k + tm, cols]
            o_ref[:, cols] = (acc - cur_ref[:, cols].astype(F32)).astype(BF)

    return pl.pallas_call(
        body, name=name, grid=(nt,),
        in_specs=[_row_spec(tm, d), pl.BlockSpec((POOL_HALO, d), lambda i: (jnp.minimum((i + 1) * per, last_halo), 0))],
        out_specs=_row_spec(tm, d), out_shape=jax.ShapeDtypeStruct((s, d), BF),
        scratch_shapes=[pltpu.VMEM((tm + POOL_HALO, d), F32)],
        compiler_params=_params(("parallel",)),
    )(dp, dp)


def _grp_fwd(p, w, scale, name):
    s, d = p.shape
    ng, gd, _ = w.shape
    tm = _pick(s, 1024, 8)

    def body(p_ref, w_ref, s_ref, z_ref, y_ref):
        z = jnp.dot(p_ref[...], w_ref[...].astype(BF), preferred_element_type=F32)
        z_ref[...] = z.astype(BF)
        y_ref[...] = (z * s_ref[...]).astype(BF)

    blk = pl.BlockSpec((tm, gd), lambda i, g: (i, g))
    return pl.pallas_call(
        body, name=name, grid=(s // tm, ng),
        in_specs=[blk, pl.BlockSpec((None, gd, gd), lambda i, g: (g, 0, 0)), pl.BlockSpec((1, gd), lambda i, g: (0, g))],
        out_specs=[blk, blk],
        out_shape=[jax.ShapeDtypeStruct((s, d), BF), jax.ShapeDtypeStruct((s, d), BF)],
        compiler_params=_params(("parallel", "parallel")),
    )(p, w, scale)


def _grp_bwd(dy, z, w, scale, name):
    s, d = dy.shape
    ng, gd, _ = w.shape
    tm = _pick(s, 1024, 8)

    def body(dy_ref, z_ref, w_ref, s_ref, dz_ref, dp_ref, ds_ref):
        i = pl.program_id(1)
        dyv = dy_ref[...].astype(F32)
        dz = (dyv * s_ref[...]).astype(BF)
        dz_ref[...] = dz
        dp_ref[...] = lax.dot_general(dz, w_ref[...].astype(BF), (((1,), (1,)), ((), ())),
                                      preferred_element_type=F32).astype(BF)

        @pl.when(i == 0)
        def _():
            ds_ref[...] = jnp.zeros_like(ds_ref)

        ds_ref[...] += jnp.sum(dyv * z_ref[...].astype(F32), axis=0, keepdims=True)

    blk = pl.BlockSpec((tm, gd), lambda g, i: (i, g))
    vec = pl.BlockSpec((1, gd), lambda g, i: (0, g))
    return pl.pallas_call(
        body, name=name, grid=(ng, s // tm),
        in_specs=[blk, blk, pl.BlockSpec((None, gd, gd), lambda g, i: (g, 0, 0)), vec],
        out_specs=[blk, blk, vec],
        out_shape=[jax.ShapeDtypeStruct((s, d), BF), jax.ShapeDtypeStruct((s, d), BF),
                   jax.ShapeDtypeStruct((1, d), F32)],
        compiler_params=_params(("parallel", "arbitrary")),
    )(dy, z, w, scale)


def _grp_dw(p, dz, ng, name):
    s, d = p.shape
    gd = d // ng
    tk = _pick(s, 1024, 8)

    def body(p_ref, dz_ref, o_ref):
        k = pl.program_id(1)

        @pl.when(k == 0)
        def _():
            o_ref[...] = jnp.zeros_like(o_ref)

        o_ref[...] += lax.dot_general(p_ref[...], dz_ref[...], (((0,), (0,)), ((), ())), preferred_element_type=F32)

    blk = pl.BlockSpec((tk, gd), lambda g, k: (k, g))
    return pl.pallas_call(
        body, name=name, grid=(ng, s // tk),
        in_specs=[blk, blk], out_specs=pl.BlockSpec((None, gd, gd), lambda g, k: (g, 0, 0)),
        out_shape=jax.ShapeDtypeStruct((ng, gd, gd), F32),
        compiler_params=_params(("parallel", "arbitrary")),
    )(p, dz)


def _sigmoid(a):
    return 1.0 / (1.0 + jnp.exp(-a))


def _ffn_act(up, cw, cb, name):
    _, nq, s, fq = up.shape
    tm = _pick(s, 512, CONV_HALO)
    per = tm // CONV_HALO

    def body(prev_ref, a_ref, v_ref, w_ref, b_ref, o_ref, ext):
        i = pl.program_id(1)
        ext[0:CONV_HALO, :] = jnp.where(i > 0, prev_ref[...].astype(F32), 0.0)
        ext[CONV_HALO:, :] = a_ref[...].astype(F32)
        a2 = b_ref[...] + ext[CONV_HALO - 2:CONV_HALO - 2 + tm, :] * w_ref[0:1, :]
        a2 = a2 + ext[CONV_HALO - 1:CONV_HALO - 1 + tm, :] * w_ref[1:2, :]
        a2 = a2 + ext[CONV_HALO:CONV_HALO + tm, :] * w_ref[2:3, :]
        o_ref[...] = (a2 * _sigmoid(a2) * v_ref[...].astype(F32)).astype(BF)

    return pl.pallas_call(
        body, name=name, grid=(nq, s // tm),
        in_specs=[pl.BlockSpec((None, None, CONV_HALO, fq), lambda q, i: (0, q, jnp.maximum(i * per - 1, 0), 0)),
                  pl.BlockSpec((None, None, tm, fq), lambda q, i: (0, q, i, 0)),
                  pl.BlockSpec((None, None, tm, fq), lambda q, i: (1, q, i, 0)),
                  pl.BlockSpec((None, CONV_WIDTH, fq), lambda q, i: (q, 0, 0)),
                  pl.BlockSpec((None, 1, fq), lambda q, i: (q, 0, 0))],
        out_specs=pl.BlockSpec((None, tm, fq), lambda q, i: (q, i, 0)),
        out_shape=jax.ShapeDtypeStruct((nq, s, fq), BF),
        scratch_shapes=[pltpu.VMEM((tm + CONV_HALO, fq), F32)],
        compiler_params=_params(("parallel", "parallel")),
    )(up, up, up, cw, cb)


def _ffn_act_bwd(up, dact, cw, cb, name):
    _, nq, s, fq = up.shape
    tm = _pick(s, 512, CONV_HALO)
    per = tm // CONV_HALO
    nt = s // tm
    last_halo = s // CONV_HALO - 1
    h = CONV_HALO
    te = tm + h

    def body(ap_ref, a_ref, an_ref, v_ref, vn_ref, d_ref, dn_ref, w_ref, b_ref, dup_ref, dc_ref, ext_a, dap):
        i = pl.program_id(1)
        ext_a[0:h, :] = jnp.where(i > 0, ap_ref[...].astype(F32), 0.0)
        ext_a[h:h + tm, :] = a_ref[...].astype(F32)
        ext_a[h + tm:, :] = an_ref[...].astype(F32)
        w0, w1, w2 = w_ref[0:1, :], w_ref[1:2, :], w_ref[2:3, :]
        a2 = b_ref[...] + ext_a[h - 2:h - 2 + te, :] * w0 + ext_a[h - 1:h - 1 + te, :] * w1 + ext_a[h:h + te, :] * w2
        sig = _sigmoid(a2)
        dsilu = sig * (1.0 + a2 * (1.0 - sig))
        d_cur = d_ref[...].astype(F32)
        d_nxt = jnp.where(i < nt - 1, dn_ref[...].astype(F32), 0.0)
        dap[0:tm, :] = d_cur * v_ref[...].astype(F32) * dsilu[0:tm, :]
        dap[tm:, :] = d_nxt * vn_ref[...].astype(F32) * dsilu[tm:, :]
        dup_ref[1] = (d_cur * (a2[0:tm, :] * sig[0:tm, :])).astype(BF)
        g0 = dap[0:tm, :]
        dup_ref[0] = (dap[2:2 + tm, :] * w0 + dap[1:1 + tm, :] * w1 + g0 * w2).astype(BF)

        @pl.when(i == 0)
        def _():
            dc_ref[...] = jnp.zeros_like(dc_ref)

        dc_ref[0:1, :] += jnp.sum(g0 * ext_a[h - 2:h - 2 + tm, :], axis=0, keepdims=True)
        dc_ref[1:2, :] += jnp.sum(g0 * ext_a[h - 1:h - 1 + tm, :], axis=0, keepdims=True)
        dc_ref[2:3, :] += jnp.sum(g0 * ext_a[h:h + tm, :], axis=0, keepdims=True)
        dc_ref[3:4, :] += jnp.sum(g0, axis=0, keepdims=True)

    def cur(half):
        return pl.BlockSpec((None, None, tm, fq), lambda q, i: (half, q, i, 0))

    def nxt(half):
        return pl.BlockSpec((None, None, h, fq), lambda q, i: (half, q, jnp.minimum((i + 1) * per, last_halo), 0))

    return pl.pallas_call(
        body, name=name, grid=(nq, nt),
        in_specs=[pl.BlockSpec((None, None, h, fq), lambda q, i: (0, q, jnp.maximum(i * per - 1, 0), 0)),
                  cur(0), nxt(0), cur(1), nxt(1),
                  pl.BlockSpec((None, tm, fq), lambda q, i: (q, i, 0)),
                  pl.BlockSpec((None, h, fq), lambda q, i: (q, jnp.minimum((i + 1) * per, last_halo), 0)),
                  pl.BlockSpec((None, CONV_WIDTH, fq), lambda q, i: (q, 0, 0)),
                  pl.BlockSpec((None, 1, fq), lambda q, i: (q, 0, 0))],
        out_specs=[pl.BlockSpec((2, None, tm, fq), lambda q, i: (0, q, i, 0)),
                   pl.BlockSpec((None, 8, fq), lambda q, i: (q, 0, 0))],
        out_shape=[jax.ShapeDtypeStruct((2, nq, s, fq), BF), jax.ShapeDtypeStruct((nq, 8, fq), F32)],
        scratch_shapes=[pltpu.VMEM((tm + 2 * h, fq), F32), pltpu.VMEM((te, fq), F32)],
        compiler_params=_params(("parallel", "arbitrary")),
    )(up, up, up, up, up, dact, dact, cw, cb)


def _band(blk, n_steps, dil, first):
    qi = lax.broadcasted_iota(jnp.int32, (blk, 2 * blk), 0) + blk
    ki = lax.broadcasted_iota(jnp.int32, (blk, 2 * blk), 1)
    delta = qi - ki
    valid = (delta >= 0) & (delta <= n_steps) & ((ki >= blk) | jnp.logical_not(first))
    return valid, (delta * dil).astype(F32)


def _attn_fwd(q_all, kv, g, slopes, d, name):
    window, dil = BRANCHES[g]
    n_steps = window // dil
    blk = max(ATTN_BLOCK, n_steps)
    s = q_all.shape[0]
    sub = s // dil
    nb = sub // blk
    assert nb * blk == sub
    nh = d // HEAD_DIM
    nbr = len(BRANCHES)
    qv = q_all.reshape(sub, dil * nbr * d)
    kvv = kv.reshape(sub, dil * 2 * nbr * d)
    scale = HEAD_DIM ** -0.5

    def body(q_ref, kp_ref, kc_ref, vp_ref, vc_ref, o_ref, l_ref):
        j = pl.program_id(1)
        valid, dist = _band(blk, n_steps, dil, j == 0)
        qb = q_ref[...]
        kb = jnp.concatenate([kp_ref[...], kc_ref[...]], axis=0)
        vb = jnp.concatenate([vp_ref[...], vc_ref[...]], axis=0)
        for h in range(nh):
            sl = slice(h * HEAD_DIM, (h + 1) * HEAD_DIM)
            sc = lax.dot_general(qb[:, sl], kb[:, sl], (((1,), (1,)), ((), ())), preferred_element_type=F32) * scale
            sc = jnp.where(valid, sc - float(slopes[h]) * dist, NEG)
            m = jnp.max(sc, axis=-1, keepdims=True)
            p = jnp.exp(sc - m)
            den = jnp.sum(p, axis=-1, keepdims=True)
            o = jnp.dot(p.astype(BF), vb[:, sl], preferred_element_type=F32) / den
            o_ref[:, sl] = o.astype(BF)
            l_ref[:, sl] = jnp.broadcast_to(m + jnp.log(den), (blk, HEAD_DIM))

    def spec(col, prev):
        if prev:
            return pl.BlockSpec((blk, d), lambda r, j: (jnp.maximum(j - 1, 0), r * col[0] + col[1]))
        return pl.BlockSpec((blk, d), lambda r, j: (j, r * col[0] + col[1]))

    qcol, kcol, vcol = (nbr, g), (2 * nbr, g), (2 * nbr, nbr + g)
    ospec = pl.BlockSpec((blk, d), lambda r, j: (j, r))
    o, lse = pl.pallas_call(
        body, name=name, grid=(dil, nb),
        in_specs=[spec(qcol, False), spec(kcol, True), spec(kcol, False), spec(vcol, True), spec(vcol, False)],
        out_specs=[ospec, ospec],
        out_shape=[jax.ShapeDtypeStruct((sub, dil * d), BF), jax.ShapeDtypeStruct((sub, dil * d), F32)],
        compiler_params=_params(("parallel", "parallel")),
    )(qv, kvv, kvv, kvv, kvv)
    return o.reshape(s, d), lse.reshape(s, d)


def _attn_combine(os, lses, name):
    s, d = os[0].shape
    tm = _pick(s, 512, 8)
    nbr = len(os)

    def body(*refs):
        o_refs, l_refs = refs[:nbr], refs[nbr:2 * nbr]
        o_ref, lt_ref = refs[2 * nbr], refs[2 * nbr + 1]
        ls = [r[...] for r in l_refs]
        m = ls[0]
        for v in ls[1:]:
            m = jnp.maximum(m, v)
        tot = jnp.exp(ls[0] - m)
        for v in ls[1:]:
            tot = tot + jnp.exp(v - m)
        lt = m + jnp.log(tot)
        acc = jnp.exp(ls[0] - lt) * o_refs[0][...].astype(F32)
        for v, r in zip(ls[1:], o_refs[1:]):
            acc = acc + jnp.exp(v - lt) * r[...].astype(F32)
        o_ref[...] = acc.astype(BF)
        lt_ref[...] = lt

    return pl.pallas_call(
        body, name=name, grid=(s // tm,),
        in_specs=[_row_spec(tm, d)] * (2 * nbr), out_specs=[_row_spec(tm, d), _row_spec(tm, d)],
        out_shape=[jax.ShapeDtypeStruct((s, d), BF), jax.ShapeDtypeStruct((s, d), F32)],
        compiler_params=_params(("parallel",)),
    )(*os, *lses)


def _attn_bwd(q_all, kv, do, o, lt, g, slopes, d, name, dk_in=None, dv_in=None):
    window, dil = BRANCHES[g]
    n_steps = window // dil
    blk = max(ATTN_BLOCK, n_steps)
    s = q_all.shape[0]
    sub = s // dil
    nb = sub // blk
    nh = d // HEAD_DIM
    nbr = len(BRANCHES)
    qv = q_all.reshape(sub, dil * nbr * d)
    kvv = kv.reshape(sub, dil * 2 * nbr * d)
    scale = HEAD_DIM ** -0.5
    acc_in = dk_in is not None

    def body(*refs):
        q_ref, do_ref, o_ref, lt_ref, kp_ref, kc_ref, vp_ref, vc_ref = refs[:8]
        n_in = 10 if acc_in else 8
        dkin_ref, dvin_ref = (refs[8], refs[9]) if acc_in else (None, None)
        dq_ref, dk_ref, dv_ref, keep_k, keep_v, part_k, part_v = refs[n_in:n_in + 7]
        t = pl.program_id(1)

        def emit(prev_k, prev_v):
            if acc_in:
                prev_k = prev_k + dkin_ref[...].astype(F32)
                prev_v = prev_v + dvin_ref[...].astype(F32)
            dk_ref[...] = prev_k.astype(BF)
            dv_ref[...] = prev_v.astype(BF)

        @pl.when(t < nb)
        def _():
            valid, dist = _band(blk, n_steps, dil, t == 0)
            qb = q_ref[...]
            dob = do_ref[...]
            kb = jnp.concatenate([kp_ref[...], kc_ref[...]], axis=0)
            vb = jnp.concatenate([vp_ref[...], vc_ref[...]], axis=0)
            for h in range(nh):
                sl = slice(h * HEAD_DIM, (h + 1) * HEAD_DIM)
                qh, kh, vh, doh = qb[:, sl], kb[:, sl], vb[:, sl], dob[:, sl]
                sc = lax.dot_general(qh, kh, (((1,), (1,)), ((), ())), preferred_element_type=F32) * scale
                sc = sc - float(slopes[h]) * dist
                p = jnp.where(valid, jnp.exp(jnp.minimum(sc - lt_ref[:, sl][:, 0:1], 30.0)), 0.0)
                dlt = jnp.sum(doh.astype(F32) * o_ref[:, sl].astype(F32), axis=-1, keepdims=True)
                dp = lax.dot_general(doh, vh, (((1,), (1,)), ((), ())), preferred_element_type=F32)
                ds = (p * (dp - dlt)).astype(BF)
                dq_ref[:, sl] = (jnp.dot(ds, kh, preferred_element_type=F32) * scale).astype(BF)
                part_k[:, sl] = lax.dot_general(ds, qh, (((0,), (0,)), ((), ())), preferred_element_type=F32) * scale
                part_v[:, sl] = lax.dot_general(p.astype(BF), doh, (((0,), (0,)), ((), ())),
                                                preferred_element_type=F32)

            @pl.when(t > 0)
            def _():
                emit(keep_k[...] + part_k[0:blk, :], keep_v[...] + part_v[0:blk, :])

            keep_k[...] = part_k[blk:, :]
            keep_v[...] = part_v[blk:, :]

        @pl.when(t == nb)
        def _():
            emit(keep_k[...], keep_v[...])

    def qspec(col):
        return pl.BlockSpec((blk, d), lambda r, t: (jnp.minimum(t, nb - 1), r * col[0] + col[1]))

    def kspec(col, prev):
        if prev:
            return pl.BlockSpec((blk, d), lambda r, t: (jnp.maximum(jnp.minimum(t, nb - 1) - 1, 0), r * col[0] + col[1]))
        return qspec(col)

    kout = pl.BlockSpec((blk, d), lambda r, t: (jnp.maximum(t - 1, 0), r))
    qcol, kcol, vcol, one = (nbr, g), (2 * nbr, g), (2 * nbr, nbr + g), (1, 0)
    in_specs = [qspec(qcol), qspec(one), qspec(one), qspec(one),
                kspec(kcol, True), kspec(kcol, False), kspec(vcol, True), kspec(vcol, False)]
    args = [qv, do.reshape(sub, dil * d), o.reshape(sub, dil * d), lt.reshape(sub, dil * d), kvv, kvv, kvv, kvv]
    if acc_in:
        in_specs += [kout, kout]
        args += [dk_in.reshape(sub, dil * d), dv_in.reshape(sub, dil * d)]
    shp = jax.ShapeDtypeStruct((sub, dil * d), BF)
    dq, dk, dv = pl.pallas_call(
        body, name=name, grid=(dil, nb + 1),
        in_specs=in_specs, out_specs=[qspec(one), kout, kout], out_shape=[shp, shp, shp],
        scratch_shapes=[pltpu.VMEM((blk, d), F32), pltpu.VMEM((blk, d), F32),
                        pltpu.VMEM((2 * blk, d), F32), pltpu.VMEM((2 * blk, d), F32)],
        compiler_params=_params(("parallel", "arbitrary")),
    )(*args)
    return dq.reshape(s, d), dk.reshape(s, d), dv.reshape(s, d)


def _adamw(parts_list, w, m, v, name):
    nl, r, c = w.shape
    assert len(parts_list) == nl
    npart = parts_list[0].shape[0]
    tr = _pick(r, 256, 16)
    c1 = 1.0 / (1.0 - ADAM_B1 ** ADAM_STEP)
    c2 = 1.0 / (1.0 - ADAM_B2 ** ADAM_STEP)

    def body(*refs):
        p_refs = refs[:nl]
        w_ref, m_ref, v_ref, g_ref, d_ref, nm_ref, nv_ref = refs[nl:]
        layer = pl.program_id(0)
        for idx in range(nl):
            @pl.when(layer == idx)
            def _(p_ref=p_refs[idx]):
                g = p_ref[0].astype(F32)
                for k in range(1, npart):
                    g = g + p_ref[k].astype(F32)
                nm = ADAM_B1 * m_ref[...] + (1.0 - ADAM_B1) * g
                nv = ADAM_B2 * v_ref[...] + (1.0 - ADAM_B2) * (g * g)
                g_ref[...] = g
                nm_ref[...] = nm
                nv_ref[...] = nv
                d_ref[...] = -ADAM_LR * ((nm * c1) / (jnp.sqrt(nv * c2) + ADAM_EPS) + ADAM_WD * w_ref[...])

    def part_spec(idx):
        return pl.BlockSpec((npart, tr, c), lambda l, i: (0, jnp.where(l == idx, i, 0), 0))

    blk = pl.BlockSpec((None, tr, c), lambda l, i: (l, i, 0))
    shp = jax.ShapeDtypeStruct((nl, r, c), F32)
    return pl.pallas_call(
        body, name=name, grid=(nl, r // tr),
        in_specs=[part_spec(idx) for idx in range(nl)] + [blk, blk, blk],
        out_specs=[blk, blk, blk, blk], out_shape=[shp, shp, shp, shp],
        compiler_params=_params(("parallel", "parallel")),
    )(*parts_list, w, m, v)


def _full_from_slots(slots, shard_shape, axis):
    a = slots.reshape((N_DEV,) + tuple(shard_shape))
    a = jnp.moveaxis(a, 0, axis)
    full = list(shard_shape)
    full[axis] *= N_DEV
    return a.reshape(full)


def kernel(x, c, ada_w, ada_b, norm1_g, norm2_g, pool_w_in, pool_w_grp, pool_scale, pool_w_out, kv_norm_g, kv_ada_w, kv_ada_b, w_kv, attn_w_q, attn_w_o, ffn_w_up, ffn_conv_w, ffn_conv_b, ffn_w_down, final_g, loss_target, m_ada_w, m_ada_b, m_norm1_g, m_norm2_g, m_pool_w_in, m_pool_w_grp, m_pool_scale, m_pool_w_out, m_kv_norm_g, m_kv_ada_w, m_kv_ada_b, m_w_kv, m_attn_w_q, m_attn_w_o, m_ffn_w_up, m_ffn_conv_w, m_ffn_conv_b, m_ffn_w_down, m_final_g, v_ada_w, v_ada_b, v_norm1_g, v_norm2_g, v_pool_w_in, v_pool_w_grp, v_pool_scale, v_pool_w_out, v_kv_norm_g, v_kv_ada_w, v_kv_ada_b, v_w_kv, v_attn_w_q, v_attn_w_o, v_ffn_w_up, v_ffn_conv_w, v_ffn_conv_b, v_ffn_w_down, v_final_g):
    weights = dict(ada_w=ada_w, ada_b=ada_b, norm1_g=norm1_g, norm2_g=norm2_g, pool_w_in=pool_w_in,
                   pool_w_grp=pool_w_grp, pool_scale=pool_scale, pool_w_out=pool_w_out, kv_norm_g=kv_norm_g,
                   kv_ada_w=kv_ada_w, kv_ada_b=kv_ada_b, w_kv=w_kv, attn_w_q=attn_w_q, attn_w_o=attn_w_o,
                   ffn_w_up=ffn_w_up, ffn_conv_w=ffn_conv_w, ffn_conv_b=ffn_conv_b, ffn_w_down=ffn_w_down,
                   final_g=final_g)
    mom1 = dict(ada_w=m_ada_w, ada_b=m_ada_b, norm1_g=m_norm1_g, norm2_g=m_norm2_g, pool_w_in=m_pool_w_in,
                pool_w_grp=m_pool_w_grp, pool_scale=m_pool_scale, pool_w_out=m_pool_w_out, kv_norm_g=m_kv_norm_g,
                kv_ada_w=m_kv_ada_w, kv_ada_b=m_kv_ada_b, w_kv=m_w_kv, attn_w_q=m_attn_w_q, attn_w_o=m_attn_w_o,
                ffn_w_up=m_ffn_w_up, ffn_conv_w=m_ffn_conv_w, ffn_conv_b=m_ffn_conv_b, ffn_w_down=m_ffn_w_down,
                final_g=m_final_g)
    mom2 = dict(ada_w=v_ada_w, ada_b=v_ada_b, norm1_g=v_norm1_g, norm2_g=v_norm2_g, pool_w_in=v_pool_w_in,
                pool_w_grp=v_pool_w_grp, pool_scale=v_pool_scale, pool_w_out=v_pool_w_out, kv_norm_g=v_kv_norm_g,
                kv_ada_w=v_kv_ada_w, kv_ada_b=v_kv_ada_b, w_kv=v_w_kv, attn_w_q=v_attn_w_q, attn_w_o=v_attn_w_o,
                ffn_w_up=v_ffn_w_up, ffn_conv_w=v_ffn_conv_w, ffn_conv_b=v_ffn_conv_b, ffn_w_down=v_ffn_w_down,
                final_g=v_final_g)
    order = list(weights)

    seq, d = x.shape[1], x.shape[2]
    depth = ada_w.shape[0]
    n_pool = pool_w_in.shape[0]
    f = ffn_conv_b.shape[1]
    nbr = len(BRANCHES)
    nh = d // HEAD_DIM
    slopes = _alibi_slopes(nbr * nh).reshape(nbr, nh)
    me = 4 * lax.axis_index("x") + 2 * lax.axis_index("y") + lax.axis_index("c")
    xs = x[0]
    tgt = loss_target[0]

    groups = []
    for l in range(depth):
        if l < n_pool:
            mixer = [pool_w_in[l], pool_w_grp[l].reshape(-1, pool_w_grp.shape[-1]), pool_w_out[l]]
        else:
            j = l - n_pool
            mixer = [attn_w_q[j], attn_w_o[j]] + ([w_kv] if j == 0 else [])
        groups += [[a.astype(BF) for a in mixer], [ffn_w_up[l].astype(BF), ffn_w_down[l].astype(BF)]]
    gather_handles, tok = _comm_start(groups, "gather_start", scatter=False)

    cond = c * (1.0 / (1.0 + jnp.exp(-c))) + tok
    small_in = jnp.concatenate([cond.reshape(-1), ffn_conv_w.reshape(-1), pool_scale.reshape(-1)])
    n_small_in = small_in.shape[0]
    gath = _all_gather(_to_rows(small_in), "gather_small").reshape(N_DEV, -1)[:, :n_small_in]
    cond_all = gath[:, :d]
    o1 = d + ffn_conv_w.size
    conv_w_full = _full_from_slots(gath[:, d:o1], ffn_conv_w.shape, 2)
    pool_scale_full = _full_from_slots(gath[:, o1:], pool_scale.shape, 1)
    cond16 = jnp.concatenate([cond_all, jnp.zeros_like(cond_all)], axis=0)

    mod_part = _ada_fwd(cond16, ada_w, "ada_fwd")[:, :N_DEV]
    kv_part = _ada_fwd(cond16, kv_ada_w[None], "kv_ada_fwd")[0, :N_DEV]
    n_mod = depth * mod_part.shape[2] + kv_part.shape[1]
    send = jnp.concatenate([jnp.moveaxis(mod_part, 1, 0).reshape(N_DEV, -1), kv_part], axis=1)
    send_rows = jax.vmap(_to_rows)(send)
    got = _all_to_all(send_rows, "exchange_mod").reshape(N_DEV, -1)[:, :n_mod]
    ncol = mod_part.shape[2]
    mods = []
    for l in range(depth):
        row = got[:, l * ncol:(l + 1) * ncol].reshape(1, -1) + ada_b[l][None]
        mods.append([row[:, k * d:(k + 1) * d] for k in range(6)])
    kv_row = got[:, depth * ncol:].reshape(1, -1) + kv_ada_b[None]
    kv_shift, kv_scale = kv_row[:, :d], kv_row[:, d:]

    def vec(a):
        return a.reshape(1, -1)

    nq = 4
    fq = f // nq
    ng = len(POOL_WINDOWS)
    cw_slots = jnp.moveaxis(conv_w_full.reshape(depth, CONV_WIDTH, nq, fq), 2, 1)
    cb_slots = ffn_conv_b.reshape(depth, nq, 1, fq)

    saved = []
    xcur = xs
    kvs = None
    hkv = None
    x_kv = None
    w_kv_slots = None
    for l in range(depth):
        sh1, sc1, g1, sh2, sc2, g2 = mods[l]
        st = dict(x=xcur)
        h = _norm_mod(xcur, vec(norm1_g[l]), sc1, sh1, f"norm1_{l}")
        st["h"] = h
        gw = _comm_wait(gather_handles[2 * l], h, f"gather_wait_mixer_{l}")
        if l < n_pool:
            w_in = gw[0].reshape(d, d)
            w_grp = jnp.moveaxis(gw[1].reshape(N_DEV, ng, -1, d // ng), 0, 1).reshape(ng, d // ng, d // ng)
            w_out = gw[2].reshape(d, d)
            u = _mm(h, w_in, f"pool_in_{l}")
            pooled = _pool_fwd(u, f"pool_fwd_{l}")
            z, y = _grp_fwd(pooled, w_grp, vec(pool_scale_full[l]), f"grp_fwd_{l}")
            mix, x1 = _mm(y, w_out, f"pool_out_{l}", gate=g1, resid=xcur, pre_dtype=BF)
            st.update(pooled=pooled, z=z, y=y, w_in=w_in, w_grp=w_grp, w_out=w_out)
        else:
            j = l - n_pool
            w_q_slots, w_o = gw[0], gw[1].reshape(d, d)
            if j == 0:
                w_kv_slots = gw[2]
                x_kv = xcur
                hkv = _norm_mod(xcur, vec(kv_norm_g), kv_scale, kv_shift, "norm_kv")
                kvs = _mm(hkv, w_kv_slots, "kv_proj", bf="qkn", out_dtype=BF)
            q = _mm(h, w_q_slots, f"q_proj_{l}", bf="qkn", out_dtype=BF)
            outs, lses = [], []
            for g in range(nbr):
                og, lg = _attn_fwd(q, kvs, g, slopes[g], d, f"attn_fwd_{l}_{g}")
                outs.append(og)
                lses.append(lg)
            o, lt = _attn_combine(outs, lses, f"attn_mix_{l}")
            mix, x1 = _mm(o, w_o, f"attn_out_{l}", gate=g1, resid=xcur, pre_dtype=BF)
            st.update(q=q, o=o, lt=lt, w_q=w_q_slots, w_o=w_o)
        h2 = _norm_mod(x1, vec(norm2_g[l]), sc2, sh2, f"norm2_{l}")
        w_up_slots, w_down = _comm_wait(gather_handles[2 * l + 1], h2, f"gather_wait_ffn_{l}")
        w_down = w_down.reshape(f, d)
        st.update(w_up=w_up_slots, w_down=w_down)
        up = _mm(h2, w_up_slots, f"ffn_up_{l}", bf="qkn", of="qmn", out_dtype=BF).reshape(2, nq, seq, fq)
        act = _ffn_act(up, cw_slots[l], cb_slots[l], f"ffn_act_{l}")
        ffo, x2 = _mm(act, w_down, f"ffn_down_{l}", af="qmk", gate=g2, resid=x1, pre_dtype=BF)
        st.update(mix=mix, x1=x1, h2=h2, up=up, act=act, ffo=ffo)
        saved.append(st)
        xcur = x2

    dx, loss_blk, d_final_g = _final_loss(xcur, tgt, vec(final_g), "final_loss")
    loss = lax.psum(loss_blk[0, 0], ("x", "y", "c"))

    d_mod = [None] * depth
    d_n1 = [None] * depth
    d_n2 = [None] * depth
    d_conv = [None] * depth
    d_pscale = [None] * n_pool
    dk_acc = [None] * nbr
    dv_acc = [None] * nbr
    ffn_handles = [None] * depth
    mixer_handles = [None] * depth
    tok = 0.0
    for l in reversed(range(depth)):
        sh1, sc1, g1, sh2, sc2, g2 = mods[l]
        st = saved[l]
        dffo, dg2 = _gate_bwd(dx, st["ffo"], g2 + tok, f"gate2_bwd_{l}")
        g_down = _mm(st["act"], dffo, f"ffn_down_dw_{l}", af="qkm", out_dtype=BF)
        dact = _mm(dffo, st["w_down"], f"ffn_down_dx_{l}", bf="nk", of="qmn", tn=fq, out_dtype=BF)
        dup, dc = _ffn_act_bwd(st["up"], dact, cw_slots[l], cb_slots[l], f"ffn_act_bwd_{l}")
        dup = dup.reshape(N_DEV, seq, -1)
        d_conv[l] = dc
        g_up = _mm(st["h2"], dup, f"ffn_up_dw_{l}", af="km", bf="qkn", of="qmn", out_dtype=BF)
        dh2 = _mm(dup, st["w_up"], f"ffn_up_dx_{l}", af="qmk", bf="qnk")
        (ffn_handles[l],), tok = _comm_start([[g_up, g_down.reshape(N_DEV, -1, d)]], f"exchange_start_ffn_{l}",
                                             scatter=True)
        dx1, dsh2, dw2 = _norm_mod_bwd(st["x1"], dh2, vec(norm2_g[l]), sc2 + tok, dx, f"norm2_bwd_{l}")
        d_n2[l] = dw2 * (1.0 + sc2)
        dsc2 = dw2 * vec(norm2_g[l])

        dmix, dg1 = _gate_bwd(dx1, st["mix"], g1, f"gate1_bwd_{l}")
        if l < n_pool:
            g_out = _mm(st["y"], dmix, f"pool_out_dw_{l}", af="km", out_dtype=BF)
            dy = _mm(dmix, st["w_out"], f"pool_out_dx_{l}", bf="nk", out_dtype=BF)
            dz, dpool, dps = _grp_bwd(dy, st["z"], st["w_grp"], vec(pool_scale_full[l]), f"grp_bwd_{l}")
            d_pscale[l] = dps
            g_grp = _grp_dw(st["pooled"], dz, ng, f"grp_dw_{l}")
            du = _pool_bwd(dpool, f"pool_bwd_{l}")
            g_in = _mm(st["h"], du, f"pool_in_dw_{l}", af="km", out_dtype=BF)
            dh = _mm(du, st["w_in"], f"pool_in_dx_{l}", bf="nk")
            g_grp_slots = jnp.moveaxis(g_grp.reshape(ng, N_DEV, -1, d // ng), 1, 0).reshape(N_DEV, -1, d // ng)
            send = [g_in.reshape(N_DEV, -1, d), g_grp_slots.astype(BF), g_out.reshape(N_DEV, -1, d)]
        else:
            j = l - n_pool
            g_o = _mm(st["o"], dmix, f"attn_out_dw_{l}", af="km", out_dtype=BF)
            do = _mm(dmix, st["w_o"], f"attn_out_dx_{l}", bf="nk", out_dtype=BF)
            dqs = []
            for g in range(nbr):
                dq_g, dk_g, dv_g = _attn_bwd(st["q"], kvs, do, st["o"], st["lt"], g, slopes[g], d,
                                             f"attn_bwd_{l}_{g}", dk_in=dk_acc[g], dv_in=dv_acc[g])
                dqs.append(dq_g)
                dk_acc[g], dv_acc[g] = dk_g, dv_g
            dq = jnp.concatenate(dqs, axis=1)
            nqc = st["w_q"].shape[2]
            g_q = _mm(st["h"], dq, f"q_proj_dw_{l}", af="km", of="qmn", tn=nqc, out_dtype=BF)
            dh = _mm(dq, st["w_q"], f"q_proj_dx_{l}", bf="qnk")
            send = [g_q, g_o.reshape(N_DEV, -1, d)]
        dx0, dsh1, dw1 = _norm_mod_bwd(st["x"], dh, vec(norm1_g[l]), sc1, dx1, f"norm1_bwd_{l}")
        d_n1[l] = dw1 * (1.0 + sc1)
        dsc1 = dw1 * vec(norm1_g[l])
        d_mod[l] = jnp.concatenate([dsh1, dsc1, dg1, dsh2, dsc2, dg2], axis=1)
        dx = dx0
        if l == n_pool:
            dkv = jnp.concatenate(dk_acc + dv_acc, axis=1)
            nkc = w_kv_slots.shape[2]
            g_kv = _mm(hkv, dkv, "kv_proj_dw", af="km", of="qmn", tn=nkc, out_dtype=BF)
            dhkv = _mm(dkv, w_kv_slots, "kv_proj_dx", bf="qnk")
            dx, dsh_kv, dw_kv = _norm_mod_bwd(x_kv, dhkv, vec(kv_norm_g), kv_scale, dx, "norm_kv_bwd")
            d_kv_norm = dw_kv * (1.0 + kv_scale)
            d_kv_mod = jnp.concatenate([dsh_kv, dw_kv * vec(kv_norm_g)], axis=1)
            send.append(g_kv)
        (mixer_handles[l],), tok = _comm_start([send], f"exchange_start_mixer_{l}", scatter=True)
    grad_x = dx[None]
    d_final_g = d_final_g + tok

    small = [jnp.concatenate(d_mod, axis=1).reshape(-1), d_kv_mod.reshape(-1),
             jnp.concatenate(d_n1, axis=0).reshape(-1), jnp.concatenate(d_n2, axis=0).reshape(-1),
             d_kv_norm.reshape(-1), d_final_g.reshape(-1),
             jnp.stack([dcl[:, 3, :] for dcl in d_conv]).reshape(-1),
             jnp.stack([jnp.moveaxis(dcl[:, 0:CONV_WIDTH, :], 0, 1) for dcl in d_conv]).reshape(-1),
             jnp.concatenate(d_pscale, axis=0).reshape(-1)]
    sizes = [a.shape[0] for a in small]
    small_rows = _to_rows(jnp.concatenate(small))
    small_all = _all_gather(small_rows, "gather_small_grads")
    dmod_all = small_all.reshape(N_DEV, -1)[:, :sizes[0] + sizes[1]]

    dmod16 = jnp.concatenate([dmod_all, jnp.zeros_like(dmod_all)], axis=0)
    dm = dmod16[:, :sizes[0]].reshape(16, depth, N_DEV, ncol)
    dm_mine = lax.dynamic_index_in_dim(dm, me, axis=2, keepdims=False)
    g_ada_w = _ada_bwd(cond16, jnp.moveaxis(dm_mine, 0, 1), "ada_bwd")
    nkv = kv_part.shape[1]
    dkm = dmod16[:, sizes[0]:].reshape(16, N_DEV, nkv)
    dkm_mine = lax.dynamic_index_in_dim(dkm, me, axis=1, keepdims=False)
    g_kv_ada_w = _ada_bwd(cond16, dkm_mine[None], "kv_ada_bwd")

    res = {}

    def update(n, parts_list, shape3):
        w3, m3, v3 = (a[n].reshape(shape3) for a in (weights, mom1, mom2))
        outs = _adamw(parts_list, w3, m3, v3, f"adamw_{n}")
        res[n] = [a.reshape(weights[n].shape) for a in outs]

    update("ada_w", [g_ada_w[l][None] for l in range(depth)], ada_w.shape)
    update("kv_ada_w", [g_kv_ada_w], (1,) + kv_ada_w.shape)
    after = res["ada_w"][0]
    parts_ffn = [_comm_wait(ffn_handles[l], after, f"exchange_wait_ffn_{l}") for l in reversed(range(depth))][::-1]
    parts = [_comm_wait(mixer_handles[l], after, f"exchange_wait_mixer_{l}") for l in reversed(range(depth))][::-1]
    pool_layers, attn_layers = range(n_pool), range(n_pool, depth)
    update("ffn_w_up", [parts_ffn[l][0] for l in range(depth)], ffn_w_up.shape)
    update("ffn_w_down", [parts_ffn[l][1] for l in range(depth)], ffn_w_down.shape)
    update("attn_w_q", [parts[l][0] for l in attn_layers], attn_w_q.shape)
    update("attn_w_o", [parts[l][1] for l in attn_layers], (depth - n_pool, -1, d))
    update("w_kv", [parts[n_pool][2]], (1,) + w_kv.shape)
    update("pool_w_in", [parts[l][0] for l in pool_layers], (n_pool, -1, d))
    update("pool_w_grp", [parts[l][1] for l in pool_layers], (n_pool, -1, d // ng))
    update("pool_w_out", [parts[l][2] for l in pool_layers], (n_pool, -1, d))

    tot = small_all.reshape(N_DEV, -1)
    offs = np.cumsum([0] + sizes)
    seg = {k: (int(offs[i]), int(offs[i + 1])) for i, k in enumerate(
        ["mod", "kv_mod", "n1", "n2", "kv_norm", "final", "conv_b", "conv_w", "pscale"])}

    def rows_of(a, b):
        return tot[:, a:b]

    nf8 = f // N_DEV
    conv_w_parts = lax.dynamic_slice_in_dim(
        rows_of(*seg["conv_w"]).reshape(N_DEV, depth, CONV_WIDTH, N_DEV, nf8), me, 1, axis=3).reshape(N_DEV, -1)
    nd8 = d // N_DEV
    pscale_parts = lax.dynamic_slice_in_dim(
        rows_of(*seg["pscale"]).reshape(N_DEV, n_pool, N_DEV, nd8), me, 1, axis=2).reshape(N_DEV, -1)
    small_names = ["ada_b", "norm1_g", "norm2_g", "pool_scale", "kv_norm_g", "kv_ada_b", "ffn_conv_w",
                   "ffn_conv_b", "final_g"]
    small_parts = [rows_of(*seg["mod"]), rows_of(*seg["n1"]), rows_of(*seg["n2"]), pscale_parts,
                   rows_of(*seg["kv_norm"]), rows_of(*seg["kv_mod"]), conv_w_parts, rows_of(*seg["conv_b"]),
                   rows_of(*seg["final"])]
    sp = jnp.concatenate(small_parts, axis=1)
    n_sp = sp.shape[1]
    sp_rows = jax.vmap(_to_rows)(sp)

    def packed(src):
        return _to_rows(jnp.concatenate([src[n].reshape(-1) for n in small_names]))[None]

    outs = _adamw([sp_rows], packed(weights), packed(mom1), packed(mom2), "adamw_small")
    outs = [a.reshape(-1)[:n_sp] for a in outs]
    off = 0
    for n in small_names:
        size = weights[n].size
        res[n] = [a[off:off + size].reshape(weights[n].shape) for a in outs]
        off += size

    grads = [res[n][0] for n in order]
    deltas = [res[n][1] for n in order]
    new_m = [res[n][2] for n in order]
    new_v = [res[n][3] for n in order]
    return (loss, grad_x, *grads, *deltas, *new_m, *new_v)
```

```python
import math

import numpy as np
import jax
import jax.numpy as jnp
from jax import lax
from jax.experimental import pallas as pl
from jax.experimental.pallas import tpu as pltpu

F32 = jnp.float32
BF = jnp.bfloat16

POOL_WINDOWS = (2, 4, 8, 16)
BRANCHES = ((128, 1), (512, 4), (2048, 16))
HEAD_DIM = 64
ATTN_BLOCK = 128
CONV_WIDTH = 3
EPS = 1e-6
ADAM_LR = 0.001
ADAM_B1 = 0.9
ADAM_B2 = 0.999
ADAM_EPS = 1e-08
ADAM_WD = 0.01
ADAM_STEP = 10

N_DEV = 8
LANES = 128
POOL_HALO = 16
CONV_HALO = 8
FFN_ROWS = 16
VMEM_LIMIT = 48 * 1024 * 1024
MM_TILE = 1024
NEG = -1e30

MESH = pl.DeviceIdType.MESH
ANY = pl.BlockSpec(memory_space=pl.ANY)


def _params(sem=None):
    if sem is None:
        return pltpu.CompilerParams(vmem_limit_bytes=VMEM_LIMIT)
    return pltpu.CompilerParams(dimension_semantics=sem, vmem_limit_bytes=VMEM_LIMIT)


def _pick(dim, pref, mult=LANES):
    if dim <= pref:
        return dim
    t = (pref // mult) * mult
    while t >= mult:
        if dim % t == 0:
            return t
        t -= mult
    return dim


def _alibi_slopes(n):
    def pow2(m):
        start = 2.0 ** (-(2.0 ** -(math.log2(m) - 3)))
        return [start ** (i + 1) for i in range(m)]
    if math.log2(n).is_integer():
        s = pow2(n)
    else:
        c = 2 ** math.floor(math.log2(n))
        s = pow2(c) + pow2(2 * c)[0::2][: n - c]
    s = np.asarray(s, dtype=np.float32)
    return -np.sort(-s)


def _my_place():
    return lax.axis_index("x"), lax.axis_index("y"), lax.axis_index("c")


def _all_gather_many(xs, name):
    n = len(xs)

    def body(*refs):
        x_refs, out_refs = refs[:n], refs[n:2 * n]
        send_sems, recv_sems, local_sems = refs[2 * n:]
        xi, yi, ci = _my_place()
        me, sibling = (xi, yi, ci), (xi, yi, 1 - ci)
        chips = [(1 - xi, yi), (xi, 1 - yi), (1 - xi, 1 - yi)]

        def slot(a, px, py, pc):
            return out_refs[a].at[4 * px + 2 * py + pc]

        def copy(a, k, block, to, src=None):
            return pltpu.make_async_remote_copy(
                src_ref=slot(a, *block) if src is None else src, dst_ref=slot(a, *block),
                send_sem=send_sems.at[7 * a + k], recv_sem=recv_sems.at[7 * a + k],
                device_id=to, device_id_type=MESH)

        mine = [pltpu.make_async_copy(x_refs[a], slot(a, *me), local_sems.at[a]) for a in range(n)]
        for cp in mine:
            cp.start()
        sent = []
        for a in range(n):
            first = [copy(a, 0, me, sibling, src=x_refs[a])]
            first += [copy(a, 1 + j, me, (*chip, ci), src=x_refs[a]) for j, chip in enumerate(chips)]
            for cp in first:
                cp.start()
            sent += first
        for a in range(n):
            for j, chip in enumerate(chips):
                copy(a, 1 + j, (*chip, ci), me).wait_recv()
                fwd = copy(a, 4 + j, (*chip, ci), sibling)
                fwd.start()
                sent.append(fwd)
        for a in range(n):
            copy(a, 0, sibling, me).wait_recv()
            for j, chip in enumerate(chips):
                copy(a, 4 + j, (*chip, 1 - ci), me).wait_recv()
        for cp in sent:
            cp.wait_send()
        for cp in mine:
            cp.wait()

    return pl.pallas_call(
        body, name=name,
        out_shape=[jax.ShapeDtypeStruct((N_DEV,) + x.shape, x.dtype) for x in xs],
        in_specs=[ANY] * n, out_specs=[ANY] * n,
        scratch_shapes=[pltpu.SemaphoreType.DMA((7 * n,)), pltpu.SemaphoreType.DMA((7 * n,)),
                        pltpu.SemaphoreType.DMA((n,))],
    )(*xs)


def _all_gather(x, name):
    return _all_gather_many([x], name)[0]


def _all_to_all_many(xs, name):
    n = len(xs)

    def body(*refs):
        x_refs, out_refs = refs[:n], refs[n:2 * n]
        send_sems, recv_sems, local_sems = refs[2 * n:]
        xi, yi, ci = _my_place()
        me = 4 * xi + 2 * yi + ci
        mine = [pltpu.make_async_copy(x_refs[a].at[me], out_refs[a].at[me], local_sems.at[a]) for a in range(n)]
        for cp in mine:
            cp.start()
        copies = []
        for a in range(n):
            for k in range(1, N_DEV):
                px = 1 - xi if k & 4 else xi
                py = 1 - yi if k & 2 else yi
                pc = 1 - ci if k & 1 else ci
                peer = 4 * px + 2 * py + pc
                cp = pltpu.make_async_remote_copy(
                    src_ref=x_refs[a].at[peer], dst_ref=out_refs[a].at[me],
                    send_sem=send_sems.at[7 * a + k - 1], recv_sem=recv_sems.at[7 * a + k - 1],
                    device_id=(px, py, pc), device_id_type=MESH)
                cp.start()
                copies.append(cp)
        for cp in copies:
            cp.wait()
        for cp in mine:
            cp.wait()

    return pl.pallas_call(
        body, name=name,
        out_shape=[jax.ShapeDtypeStruct(x.shape, x.dtype) for x in xs],
        in_specs=[ANY] * n, out_specs=[ANY] * n,
        scratch_shapes=[pltpu.SemaphoreType.DMA((7 * n,)), pltpu.SemaphoreType.DMA((7 * n,)),
                        pltpu.SemaphoreType.DMA((n,))],
    )(*xs)


def _all_to_all(x, name):
    return _all_to_all_many([x], name)[0]


HBM = pl.BlockSpec(memory_space=pltpu.HBM)
SEM = pl.BlockSpec(memory_space=pltpu.SEMAPHORE)
EFFECT = pltpu.SideEffectType.DATAFLOW_SIDE_EFFECTING


def _peer(k, xi, yi, ci):
    px = 1 - xi if k & 4 else xi
    py = 1 - yi if k & 2 else yi
    pc = 1 - ci if k & 1 else ci
    return (px, py, pc), 4 * px + 2 * py + pc


def _split_copies(x_refs, land_refs, send_sem, recv_sem, scatter):
    xi, yi, ci = _my_place()
    me = 4 * xi + 2 * yi + ci
    copies = []
    for p, (x_ref, land_ref) in enumerate(zip(x_refs, land_refs)):
        for k in range(1, N_DEV):
            place, peer = _peer(k, xi, yi, ci)
            copies.append(pltpu.make_async_remote_copy(
                src_ref=x_ref.at[peer] if scatter else x_ref, dst_ref=land_ref.at[me],
                send_sem=send_sem.at[7 * p + k - 1], recv_sem=recv_sem.at[7 * p + k - 1],
                device_id=place, device_id_type=MESH))
    return copies


def _comm_start(groups, name, scatter):
    sizes = [len(g) for g in groups]
    xs = [x for g in groups for x in g]
    n, ng = len(xs), len(groups)
    lands = [lax.empty(x.shape if scatter else (N_DEV,) + x.shape, x.dtype) for x in xs]
    starts = np.cumsum([0] + sizes)

    def body(*refs):
        x_refs, land_refs = refs[:n], refs[n:2 * n]
        send_sems, recv_sems = refs[2 * n:2 * n + ng], refs[2 * n + ng:2 * n + 2 * ng]
        token = refs[2 * n + 2 * ng + 2 * n]
        for gi in range(ng):
            lo, hi = int(starts[gi]), int(starts[gi + 1])
            for cp in _split_copies(x_refs[lo:hi], land_refs[lo:hi], send_sems[gi], recv_sems[gi], scatter):
                cp.start()
        token[...] = jnp.zeros_like(token)

    sem_shapes = [pltpu.SemaphoreType.DMA((7 * m,)) for m in sizes]
    thru = [pltpu.HBM(a.shape, a.dtype) for a in xs + lands]
    res = pl.pallas_call(
        body, name=name,
        out_shape=sem_shapes + sem_shapes + thru + [jax.ShapeDtypeStruct((8, LANES), F32)],
        in_specs=[HBM] * (2 * n),
        out_specs=[SEM] * (2 * ng) + [HBM] * (2 * n) + [pl.BlockSpec(memory_space=pltpu.VMEM)],
        input_output_aliases={i: 2 * ng + i for i in range(2 * n)},
        compiler_params=pltpu.CompilerParams(has_side_effects=EFFECT),
    )(*[pltpu.with_memory_space_constraint(a, pltpu.HBM) for a in xs + lands])
    send_sems, recv_sems = res[:ng], res[ng:2 * ng]
    x_thru, land_thru = res[2 * ng:2 * ng + n], res[2 * ng + n:2 * ng + 2 * n]
    handles = []
    for gi in range(ng):
        lo, hi = int(starts[gi]), int(starts[gi + 1])
        handles.append((send_sems[gi], recv_sems[gi], list(x_thru[lo:hi]), list(land_thru[lo:hi]), scatter))
    return handles, res[-1][0, 0]


def _comm_wait(handle, after, name):
    send_sem, recv_sem, x_thru, land_thru, scatter = handle
    m = len(x_thru)

    def body(*refs):
        x_refs, land_refs = refs[:m], refs[m:2 * m]
        local_sems = refs[-1]
        xi, yi, ci = _my_place()
        me = 4 * xi + 2 * yi + ci
        mine = [pltpu.make_async_copy(x_refs[p].at[me] if scatter else x_refs[p], land_refs[p].at[me],
                                      local_sems.at[p]) for p in range(m)]
        for cp in mine:
            cp.start()
        for cp in _split_copies(x_refs, land_refs, refs[2 * m], refs[2 * m + 1], scatter):
            cp.wait_send()
            cp.wait_recv()
        for cp in mine:
            cp.wait()

    res = pl.pallas_call(
        body, name=name,
        out_shape=[pltpu.HBM(a.shape, a.dtype) for a in x_thru + land_thru],
        in_specs=[HBM] * (2 * m) + [SEM, SEM, ANY], out_specs=[HBM] * (2 * m),
        input_output_aliases={i: i for i in range(2 * m)},
        scratch_shapes=[pltpu.SemaphoreType.DMA((m,))],
        compiler_params=pltpu.CompilerParams(has_side_effects=EFFECT),
    )(*x_thru, *land_thru, send_sem, recv_sem, after)
    return list(res[m:])


def _to_rows(vec):
    n = vec.shape[0]
    unit = 8 * LANES
    pad = (-n) % unit
    if pad:
        vec = jnp.concatenate([vec, jnp.zeros((pad,), vec.dtype)])
    return vec.reshape(-1, LANES)


def _mm(a, b, name, *, af="mk", bf="kn", of="mn", out_dtype=F32, tm=None, tn=None, tk=None,
        gate=None, resid=None, pre_dtype=None):
    if af == "mk":
        m, kk = a.shape
    elif af == "km":
        kk, m = a.shape
    elif af == "qmk":
        qa, m, tk = a.shape
        kk = qa * tk
    else:
        qa, kk, tm = a.shape
        m = qa * tm
    if bf == "kn":
        k2, n = b.shape
    elif bf == "nk":
        n, k2 = b.shape
    elif bf == "qkn":
        qb, k2, tn = b.shape
        n = qb * tn
    else:
        qb, n, tkb = b.shape
        k2 = qb * tkb
        assert af != "qmk" or tkb == tk
        tk = tkb
    assert kk == k2, (name, a.shape, b.shape, af, bf)
    tm = _pick(m, MM_TILE) if tm is None else tm
    tn = _pick(n, MM_TILE) if tn is None else tn
    tk = _pick(kk, MM_TILE) if tk is None else tk
    assert m % tm == 0 and n % tn == 0 and kk % tk == 0, (name, m, n, kk, tm, tn, tk)
    nk = kk // tk
    a_spec = {"mk": pl.BlockSpec((tm, tk), lambda i, j, k: (i, k)),
              "km": pl.BlockSpec((tk, tm), lambda i, j, k: (k, i)),
              "qmk": pl.BlockSpec((None, tm, tk), lambda i, j, k: (k, i, 0)),
              "qkm": pl.BlockSpec((None, tk, tm), lambda i, j, k: (i, k, 0))}[af]
    b_spec = {"kn": pl.BlockSpec((tk, tn), lambda i, j, k: (k, j)),
              "nk": pl.BlockSpec((tn, tk), lambda i, j, k: (j, k)),
              "qkn": pl.BlockSpec((None, tk, tn), lambda i, j, k: (j, k, 0)),
              "qnk": pl.BlockSpec((None, tn, tk), lambda i, j, k: (k, j, 0))}[bf]
    dims = (((1 if af in ("mk", "qmk") else 0,), (0 if bf in ("kn", "qkn") else 1,)), ((), ()))
    in_specs, args = [a_spec, b_spec], [a, b]
    if gate is not None:
        assert of == "mn"
        in_specs.append(pl.BlockSpec((1, tn), lambda i, j, k: (0, j)))
        args.append(gate)
    if resid is not None:
        assert of == "mn"
        in_specs.append(pl.BlockSpec((tm, tn), lambda i, j, k: (i, j)))
        args.append(resid)
    if of == "mn":
        o_spec, o_shape = pl.BlockSpec((tm, tn), lambda i, j, k: (i, j)), (m, n)
    else:
        o_spec, o_shape = pl.BlockSpec((None, tm, tn), lambda i, j, k: (j, i, 0)), (n // tn, m, tn)
    out_shape, out_specs = [jax.ShapeDtypeStruct(o_shape, out_dtype)], [o_spec]
    if pre_dtype is not None:
        out_shape.insert(0, jax.ShapeDtypeStruct(o_shape, pre_dtype))
        out_specs.insert(0, o_spec)
    n_in = len(args)
    n_out = len(out_shape)

    def body(*refs):
        a_ref, b_ref = refs[0], refs[1]
        extra = list(refs[2:n_in])
        outs = refs[n_in:n_in + n_out]
        gate_ref = extra.pop(0) if gate is not None else None
        resid_ref = extra.pop(0) if resid is not None else None

        def product():
            return lax.dot_general(a_ref[...].astype(BF), b_ref[...].astype(BF), dims, preferred_element_type=F32)

        def finish(r):
            if pre_dtype is not None:
                outs[0][...] = r.astype(pre_dtype)
            if gate_ref is not None:
                r = r * gate_ref[...]
            if resid_ref is not None:
                r = resid_ref[...] + r
            outs[-1][...] = r.astype(out_dtype)

        if nk == 1:
            finish(product())
        else:
            acc = refs[n_in + n_out]
            k = pl.program_id(2)

            @pl.when(k == 0)
            def _():
                acc[...] = product()

            @pl.when(k > 0)
            def _():
                acc[...] += product()

            @pl.when(k == nk - 1)
            def _():
                finish(acc[...])

    res = pl.pallas_call(
        body, name=name, grid=(m // tm, n // tn, nk),
        in_specs=in_specs, out_specs=out_specs, out_shape=out_shape,
        scratch_shapes=[pltpu.VMEM((tm, tn), F32)] if nk > 1 else [],
        compiler_params=_params(("parallel", "parallel", "arbitrary")),
    )(*args)
    return res if pre_dtype is not None else res[0]


def _ada_fwd(cond16, w, name):
    nl, d, n = w.shape

    def body(c_ref, w_ref, o_ref):
        o_ref[...] = jnp.dot(c_ref[...].astype(BF), w_ref[...].astype(BF), preferred_element_type=F32)

    return pl.pallas_call(
        body, name=name, grid=(nl,),
        in_specs=[pl.BlockSpec((16, d), lambda l: (0, 0)), pl.BlockSpec((None, d, n), lambda l: (l, 0, 0))],
        out_specs=pl.BlockSpec((None, 16, n), lambda l: (l, 0, 0)),
        out_shape=jax.ShapeDtypeStruct((nl, 16, n), F32),
        compiler_params=_params(("parallel",)),
    )(cond16, w)


def _ada_bwd(cond16, dmod, name):
    nl, _, n = dmod.shape
    d = cond16.shape[1]

    def body(c_ref, g_ref, o_ref):
        o_ref[...] = lax.dot_general(c_ref[...].astype(BF), g_ref[...].astype(BF), (((0,), (0,)), ((), ())),
                                     preferred_element_type=F32)

    return pl.pallas_call(
        body, name=name, grid=(nl,),
        in_specs=[pl.BlockSpec((16, d), lambda l: (0, 0)), pl.BlockSpec((None, 16, n), lambda l: (l, 0, 0))],
        out_specs=pl.BlockSpec((None, d, n), lambda l: (l, 0, 0)),
        out_shape=jax.ShapeDtypeStruct((nl, d, n), F32),
        compiler_params=_params(("parallel",)),
    )(cond16, dmod)


def _row_spec(tm, d):
    return pl.BlockSpec((tm, d), lambda i: (i, 0))


def _vec_spec(d):
    return pl.BlockSpec((1, d), lambda i: (0, 0))


def _norm_mod(x, g, sc, sh, name):
    s, d = x.shape
    tm = _pick(s, 512, 8)

    def body(x_ref, g_ref, sc_ref, sh_ref, o_ref):
        xv = x_ref[...]
        r = lax.rsqrt(jnp.mean(xv * xv, axis=-1, keepdims=True) + EPS)
        y = (xv * r) * g_ref[...]
        o_ref[...] = (y * (1.0 + sc_ref[...]) + sh_ref[...]).astype(BF)

    return pl.pallas_call(
        body, name=name, grid=(s // tm,),
        in_specs=[_row_spec(tm, d), _vec_spec(d), _vec_spec(d), _vec_spec(d)],
        out_specs=_row_spec(tm, d), out_shape=jax.ShapeDtypeStruct((s, d), BF),
        compiler_params=_params(("parallel",)),
    )(x, g, sc, sh)


def _norm_mod_bwd(x, dh, g, sc, dx_in, name):
    s, d = x.shape
    tm = _pick(s, 512, 8)

    def body(x_ref, dh_ref, g_ref, sc_ref, dxin_ref, dx_ref, dsh_ref, dw_ref):
        i = pl.program_id(0)
        xv = x_ref[...]
        dhv = dh_ref[...].astype(F32)
        r = lax.rsqrt(jnp.mean(xv * xv, axis=-1, keepdims=True) + EPS)
        xn = xv * r
        dxn = dhv * (g_ref[...] * (1.0 + sc_ref[...]))
        dx_ref[...] = dxin_ref[...] + r * (dxn - xn * jnp.mean(dxn * xn, axis=-1, keepdims=True))

        @pl.when(i == 0)
        def _():
            dsh_ref[...] = jnp.zeros_like(dsh_ref)
            dw_ref[...] = jnp.zeros_like(dw_ref)

        dsh_ref[...] += jnp.sum(dhv, axis=0, keepdims=True)
        dw_ref[...] += jnp.sum(dhv * xn, axis=0, keepdims=True)

    return pl.pallas_call(
        body, name=name, grid=(s // tm,),
        in_specs=[_row_spec(tm, d), _row_spec(tm, d), _vec_spec(d), _vec_spec(d), _row_spec(tm, d)],
        out_specs=[_row_spec(tm, d), _vec_spec(d), _vec_spec(d)],
        out_shape=[jax.ShapeDtypeStruct((s, d), F32), jax.ShapeDtypeStruct((1, d), F32),
                   jax.ShapeDtypeStruct((1, d), F32)],
        compiler_params=_params(("arbitrary",)),
    )(x, dh, g, sc, dx_in)


def _gate_bwd(dx, y, gate, name):
    s, d = dx.shape
    tm = _pick(s, 512, 8)

    def body(dx_ref, y_ref, g_ref, dy_ref, dg_ref):
        i = pl.program_id(0)
        dxv = dx_ref[...]
        dy_ref[...] = (dxv * g_ref[...]).astype(BF)

        @pl.when(i == 0)
        def _():
            dg_ref[...] = jnp.zeros_like(dg_ref)

        dg_ref[...] += jnp.sum(dxv * y_ref[...].astype(F32), axis=0, keepdims=True)

    return pl.pallas_call(
        body, name=name, grid=(s // tm,),
        in_specs=[_row_spec(tm, d), _row_spec(tm, d), _vec_spec(d)],
        out_specs=[_row_spec(tm, d), _vec_spec(d)],
        out_shape=[jax.ShapeDtypeStruct((s, d), BF), jax.ShapeDtypeStruct((1, d), F32)],
        compiler_params=_params(("arbitrary",)),
    )(dx, y, gate)


def _final_loss(x, tgt, g, name):
    s, d = x.shape
    tm = _pick(s, 512, 8)

    def body(x_ref, t_ref, g_ref, dx_ref, loss_ref, dg_ref):
        i = pl.program_id(0)
        xv = x_ref[...]
        gv = g_ref[...]
        r = lax.rsqrt(jnp.mean(xv * xv, axis=-1, keepdims=True) + EPS)
        xn = xv * r
        err = xn * gv - t_ref[...]
        dy = err * (1.0 / d)
        dxn = dy * gv
        dx_ref[...] = r * (dxn - xn * jnp.mean(dxn * xn, axis=-1, keepdims=True))

        @pl.when(i == 0)
        def _():
            loss_ref[...] = jnp.zeros_like(loss_ref)
            dg_ref[...] = jnp.zeros_like(dg_ref)

        part = 0.5 * jnp.sum(jnp.sum(err * err, axis=-1, keepdims=True) * (1.0 / d), axis=0, keepdims=True)
        loss_ref[...] += jnp.broadcast_to(part, loss_ref.shape)
        dg_ref[...] += jnp.sum(dy * xn, axis=0, keepdims=True)

    return pl.pallas_call(
        body, name=name, grid=(s // tm,),
        in_specs=[_row_spec(tm, d), _row_spec(tm, d), _vec_spec(d)],
        out_specs=[_row_spec(tm, d), pl.BlockSpec((8, LANES), lambda i: (0, 0)), _vec_spec(d)],
        out_shape=[jax.ShapeDtypeStruct((s, d), F32), jax.ShapeDtypeStruct((8, LANES), F32),
                   jax.ShapeDtypeStruct((1, d), F32)],
        compiler_params=_params(("arbitrary",)),
    )(x, tgt, g)


def _pool_counts(tm, gd, row0, w):
    t = lax.broadcasted_iota(jnp.int32, (tm, gd), 0) + row0
    return jnp.minimum(t + 1, w).astype(F32)


def _pool_fwd(u, name):
    s, d = u.shape
    tm = _pick(s, 256, POOL_HALO)
    gd = d // len(POOL_WINDOWS)
    per = tm // POOL_HALO

    def body(prev_ref, cur_ref, o_ref, ext):
        i = pl.program_id(0)
        ext[0:POOL_HALO, :] = jnp.where(i > 0, prev_ref[...], 0.0)
        ext[POOL_HALO:, :] = cur_ref[...]
        for g, w in enumerate(POOL_WINDOWS):
            cols = slice(g * gd, (g + 1) * gd)
            acc = ext[POOL_HALO:POOL_HALO + tm, cols]
            own = acc
            for k in range(1, w):
                acc = acc + ext[POOL_HALO - k:POOL_HALO - k + tm, cols]
            o_ref[:, cols] = (acc / _pool_counts(tm, gd, i * tm, w) - own).astype(BF)

    return pl.pallas_call(
        body, name=name, grid=(s // tm,),
        in_specs=[pl.BlockSpec((POOL_HALO, d), lambda i: (jnp.maximum(i * per - 1, 0), 0)), _row_spec(tm, d)],
        out_specs=_row_spec(tm, d), out_shape=jax.ShapeDtypeStruct((s, d), BF),
        scratch_shapes=[pltpu.VMEM((tm + POOL_HALO, d), F32)],
        compiler_params=_params(("parallel",)),
    )(u, u)


def _pool_bwd(dp, name):
    s, d = dp.shape
    tm = _pick(s, 256, POOL_HALO)
    gd = d // len(POOL_WINDOWS)
    per = tm // POOL_HALO
    nt = s // tm
    last_halo = s // POOL_HALO - 1

    def body(cur_ref, nxt_ref, o_ref, ext):
        i = pl.program_id(0)
        for g, w in enumerate(POOL_WINDOWS):
            cols = slice(g * gd, (g + 1) * gd)
            ext[0:tm, cols] = cur_ref[:, cols].astype(F32) / _pool_counts(tm, gd, i * tm, w)
            nxt = nxt_ref[:, cols].astype(F32) / _pool_counts(POOL_HALO, gd, (i + 1) * tm, w)
            ext[tm:, cols] = jnp.where(i < nt - 1, nxt, 0.0)
        for g, w in enumerate(POOL_WINDOWS):
            cols = slice(g * gd, (g + 1) * gd)
            acc = ext[0:tm, cols]
            for k in range(1, w):
                acc = acc + ext[k:k + tm, cols]
            o_ref[:, cols] = (acc - cur_ref[:, cols].astype(F32)).astype(BF)

    return pl.pallas_call(
        body, name=name, grid=(nt,),
        in_specs=[_row_spec(tm, d), pl.BlockSpec((POOL_HALO, d), lambda i: (jnp.minimum((i + 1) * per, last_halo), 0))],
        out_specs=_row_spec(tm, d), out_shape=jax.ShapeDtypeStruct((s, d), BF),
        scratch_shapes=[pltpu.VMEM((tm + POOL_HALO, d), F32)],
        compiler_params=_params(("parallel",)),
    )(dp, dp)


def _grp_fwd(p, w, scale, name):
    s, d = p.shape
    ng, gd, _ = w.shape
    tm = _pick(s, 1024, 8)

    def body(p_ref, w_ref, s_ref, z_ref, y_ref):
        z = jnp.dot(p_ref[...], w_ref[...].astype(BF), preferred_element_type=F32)
        z_ref[...] = z.astype(BF)
        y_ref[...] = (z * s_ref[...]).astype(BF)

    blk = pl.BlockSpec((tm, gd), lambda i, g: (i, g))
    return pl.pallas_call(
        body, name=name, grid=(s // tm, ng),
        in_specs=[blk, pl.BlockSpec((None, gd, gd), lambda i, g: (g, 0, 0)), pl.BlockSpec((1, gd), lambda i, g: (0, g))],
        out_specs=[blk, blk],
        out_shape=[jax.ShapeDtypeStruct((s, d), BF), jax.ShapeDtypeStruct((s, d), BF)],
        compiler_params=_params(("parallel", "parallel")),
    )(p, w, scale)


def _grp_bwd(dy, z, w, scale, name):
    s, d = dy.shape
    ng, gd, _ = w.shape
    tm = _pick(s, 1024, 8)

    def body(dy_ref, z_ref, w_ref, s_ref, dz_ref, dp_ref, ds_ref):
        i = pl.program_id(1)
        dyv = dy_ref[...].astype(F32)
        dz = (dyv * s_ref[...]).astype(BF)
        dz_ref[...] = dz
        dp_ref[...] = lax.dot_general(dz, w_ref[...].astype(BF), (((1,), (1,)), ((), ())),
                                      preferred_element_type=F32).astype(BF)

        @pl.when(i == 0)
        def _():
            ds_ref[...] = jnp.zeros_like(ds_ref)

        ds_ref[...] += jnp.sum(dyv * z_ref[...].astype(F32), axis=0, keepdims=True)

    blk = pl.BlockSpec((tm, gd), lambda g, i: (i, g))
    vec = pl.BlockSpec((1, gd), lambda g, i: (0, g))
    return pl.pallas_call(
        body, name=name, grid=(ng, s // tm),
        in_specs=[blk, blk, pl.BlockSpec((None, gd, gd), lambda g, i: (g, 0, 0)), vec],
        out_specs=[blk, blk, vec],
        out_shape=[jax.ShapeDtypeStruct((s, d), BF), jax.ShapeDtypeStruct((s, d), BF),
                   jax.ShapeDtypeStruct((1, d), F32)],
        compiler_params=_params(("parallel", "arbitrary")),
    )(dy, z, w, scale)


def _grp_dw(p, dz, ng, name):
    s, d = p.shape
    gd = d // ng
    tk = _pick(s, 1024, 8)

    def body(p_ref, dz_ref, o_ref):
        k = pl.program_id(1)

        @pl.when(k == 0)
        def _():
            o_ref[...] = jnp.zeros_like(o_ref)

        o_ref[...] += lax.dot_general(p_ref[...], dz_ref[...], (((0,), (0,)), ((), ())), preferred_element_type=F32)

    blk = pl.BlockSpec((tk, gd), lambda g, k: (k, g))
    return pl.pallas_call(
        body, name=name, grid=(ng, s // tk),
        in_specs=[blk, blk], out_specs=pl.BlockSpec((None, gd, gd), lambda g, k: (g, 0, 0)),
        out_shape=jax.ShapeDtypeStruct((ng, gd, gd), F32),
        compiler_params=_params(("parallel", "arbitrary")),
    )(p, dz)


def _sigmoid(a):
    return 1.0 / (1.0 + jnp.exp(-a))


def _ffn_act(up, cw, cb, name):
    _, nq, s, fq = up.shape
    tm = _pick(s, 512, CONV_HALO)
    per = tm // CONV_HALO

    h = CONV_HALO
    rows = _pick(tm, FFN_ROWS, h)

    def body(prev_ref, a_ref, v_ref, w_ref, b_ref, o_ref, ext):
        i = pl.program_id(1)
        ext[0:h, :] = jnp.where(i > 0, prev_ref[...].astype(F32), 0.0)
        ext[h:, :] = a_ref[...].astype(F32)

        def step(c, carry):
            r0 = pl.multiple_of(c * rows, rows)
            e = ext[pl.ds(r0, rows + h), :]
            a2 = (b_ref[...] + e[h - 2:h - 2 + rows] * w_ref[0:1, :] + e[h - 1:h - 1 + rows] * w_ref[1:2, :]
                  + e[h:h + rows] * w_ref[2:3, :])
            vv = v_ref[pl.ds(r0, rows), :].astype(F32)
            o_ref[pl.ds(r0, rows), :] = (a2 * _sigmoid(a2) * vv).astype(BF)
            return carry

        lax.fori_loop(0, tm // rows, step, 0)

    return pl.pallas_call(
        body, name=name, grid=(nq, s // tm),
        in_specs=[pl.BlockSpec((None, None, CONV_HALO, fq), lambda q, i: (0, q, jnp.maximum(i * per - 1, 0), 0)),
                  pl.BlockSpec((None, None, tm, fq), lambda q, i: (0, q, i, 0)),
                  pl.BlockSpec((None, None, tm, fq), lambda q, i: (1, q, i, 0)),
                  pl.BlockSpec((None, CONV_WIDTH, fq), lambda q, i: (q, 0, 0)),
                  pl.BlockSpec((None, 1, fq), lambda q, i: (q, 0, 0))],
        out_specs=pl.BlockSpec((None, tm, fq), lambda q, i: (q, i, 0)),
        out_shape=jax.ShapeDtypeStruct((nq, s, fq), BF),
        scratch_shapes=[pltpu.VMEM((tm + CONV_HALO, fq), F32)],
        compiler_params=_params(("parallel", "parallel")),
    )(up, up, up, cw, cb)


def _ffn_act_bwd(up, dact, cw, cb, name):
    _, nq, s, fq = up.shape
    tm = _pick(s, 512, CONV_HALO)
    per = tm // CONV_HALO
    nt = s // tm
    last_halo = s // CONV_HALO - 1
    h = CONV_HALO
    te = tm + h

    rows = _pick(tm, FFN_ROWS, h)

    def body(ap_ref, a_ref, an_ref, v_ref, vn_ref, d_ref, dn_ref, w_ref, b_ref, dup_ref, dc_ref, ext_a, dap, sums):
        i = pl.program_id(1)
        ext_a[0:h, :] = jnp.where(i > 0, ap_ref[...].astype(F32), 0.0)
        ext_a[h:h + tm, :] = a_ref[...].astype(F32)
        ext_a[h + tm:, :] = an_ref[...].astype(F32)

        def pre_act(e, n):
            return (b_ref[...] + e[h - 2:h - 2 + n] * w_ref[0:1, :] + e[h - 1:h - 1 + n] * w_ref[1:2, :]
                    + e[h:h + n] * w_ref[2:3, :])

        def through_gate(a2, dd, vv):
            sig = _sigmoid(a2)
            return dd * vv * (sig * (1.0 + a2 * (1.0 - sig))), dd * (a2 * sig)

        def step1(c, carry):
            r0 = pl.multiple_of(c * rows, rows)
            a2 = pre_act(ext_a[pl.ds(r0, rows + h), :], rows)
            g, dgate = through_gate(a2, d_ref[pl.ds(r0, rows), :].astype(F32), v_ref[pl.ds(r0, rows), :].astype(F32))
            dap[pl.ds(r0, rows), :] = g
            dup_ref[1, pl.ds(r0, rows), :] = dgate.astype(BF)
            return carry

        lax.fori_loop(0, tm // rows, step1, 0)
        d_nxt = jnp.where(i < nt - 1, dn_ref[...].astype(F32), 0.0)
        g, _ = through_gate(pre_act(ext_a[tm:tm + 2 * h, :], h), d_nxt, vn_ref[...].astype(F32))
        dap[tm:, :] = g
        sums[...] = jnp.zeros_like(sums)

        def fold(t):
            acc = t[0:8]
            for k in range(8, rows, 8):
                acc = acc + t[k:k + 8]
            return acc

        def step2(c, carry):
            r0 = pl.multiple_of(c * rows, rows)
            gch = dap[pl.ds(r0, rows + h), :]
            g0 = gch[0:rows]
            dup_ref[0, pl.ds(r0, rows), :] = (gch[2:2 + rows] * w_ref[0:1, :] + gch[1:1 + rows] * w_ref[1:2, :]
                                              + g0 * w_ref[2:3, :]).astype(BF)
            e = ext_a[pl.ds(r0, rows + h), :]
            sums[0] += fold(g0 * e[h - 2:h - 2 + rows])
            sums[1] += fold(g0 * e[h - 1:h - 1 + rows])
            sums[2] += fold(g0 * e[h:h + rows])
            sums[3] += fold(g0)
            return carry

        lax.fori_loop(0, tm // rows, step2, 0)

        @pl.when(i == 0)
        def _():
            dc_ref[...] = jnp.zeros_like(dc_ref)

        for k in range(4):
            dc_ref[k:k + 1, :] += jnp.sum(sums[k], axis=0, keepdims=True)

    def cur(half):
        return pl.BlockSpec((None, None, tm, fq), lambda q, i: (half, q, i, 0))

    def nxt(half):
        return pl.BlockSpec((None, None, h, fq), lambda q, i: (half, q, jnp.minimum((i + 1) * per, last_halo), 0))

    return pl.pallas_call(
        body, name=name, grid=(nq, nt),
        in_specs=[pl.BlockSpec((None, None, h, fq), lambda q, i: (0, q, jnp.maximum(i * per - 1, 0), 0)),
                  cur(0), nxt(0), cur(1), nxt(1),
                  pl.BlockSpec((None, tm, fq), lambda q, i: (q, i, 0)),
                  pl.BlockSpec((None, h, fq), lambda q, i: (q, jnp.minimum((i + 1) * per, last_halo), 0)),
                  pl.BlockSpec((None, CONV_WIDTH, fq), lambda q, i: (q, 0, 0)),
                  pl.BlockSpec((None, 1, fq), lambda q, i: (q, 0, 0))],
        out_specs=[pl.BlockSpec((2, None, tm, fq), lambda q, i: (0, q, i, 0)),
                   pl.BlockSpec((None, 8, fq), lambda q, i: (q, 0, 0))],
        out_shape=[jax.ShapeDtypeStruct((2, nq, s, fq), BF), jax.ShapeDtypeStruct((nq, 8, fq), F32)],
        scratch_shapes=[pltpu.VMEM((tm + 2 * h, fq), F32), pltpu.VMEM((te, fq), F32), pltpu.VMEM((4, 8, fq), F32)],
        compiler_params=_params(("parallel", "arbitrary")),
    )(up, up, up, up, up, dact, dact, cw, cb)


def _band(blk, n_steps, dil, first):
    qi = lax.broadcasted_iota(jnp.int32, (blk, 2 * blk), 0) + blk
    ki = lax.broadcasted_iota(jnp.int32, (blk, 2 * blk), 1)
    delta = qi - ki
    valid = (delta >= 0) & (delta <= n_steps) & ((ki >= blk) | jnp.logical_not(first))
    return valid, (delta * dil).astype(F32)


def _attn_fwd(q_all, kv, g, slopes, d, name):
    window, dil = BRANCHES[g]
    n_steps = window // dil
    blk = max(ATTN_BLOCK, n_steps)
    s = q_all.shape[0]
    sub = s // dil
    nb = sub // blk
    assert nb * blk == sub
    nh = d // HEAD_DIM
    nbr = len(BRANCHES)
    qv = q_all.reshape(sub, dil * nbr * d)
    kvv = kv.reshape(sub, dil * 2 * nbr * d)
    scale = HEAD_DIM ** -0.5

    def body(q_ref, kp_ref, kc_ref, vp_ref, vc_ref, o_ref, l_ref):
        j = pl.program_id(1)
        valid, dist = _band(blk, n_steps, dil, j == 0)
        qb = q_ref[...]
        kb = jnp.concatenate([kp_ref[...], kc_ref[...]], axis=0)
        vb = jnp.concatenate([vp_ref[...], vc_ref[...]], axis=0)
        for h in range(nh):
            sl = slice(h * HEAD_DIM, (h + 1) * HEAD_DIM)
            sc = lax.dot_general(qb[:, sl], kb[:, sl], (((1,), (1,)), ((), ())), preferred_element_type=F32) * scale
            sc = jnp.where(valid, sc - float(slopes[h]) * dist, NEG)
            m = jnp.max(sc, axis=-1, keepdims=True)
            p = jnp.exp(sc - m)
            den = jnp.sum(p, axis=-1, keepdims=True)
            o = jnp.dot(p.astype(BF), vb[:, sl], preferred_element_type=F32) / den
            o_ref[:, sl] = o.astype(BF)
            l_ref[:, sl] = jnp.broadcast_to(m + jnp.log(den), (blk, HEAD_DIM))

    def spec(col, prev):
        if prev:
            return pl.BlockSpec((blk, d), lambda r, j: (jnp.maximum(j - 1, 0), r * col[0] + col[1]))
        return pl.BlockSpec((blk, d), lambda r, j: (j, r * col[0] + col[1]))

    qcol, kcol, vcol = (nbr, g), (2 * nbr, g), (2 * nbr, nbr + g)
    ospec = pl.BlockSpec((blk, d), lambda r, j: (j, r))
    o, lse = pl.pallas_call(
        body, name=name, grid=(dil, nb),
        in_specs=[spec(qcol, False), spec(kcol, True), spec(kcol, False), spec(vcol, True), spec(vcol, False)],
        out_specs=[ospec, ospec],
        out_shape=[jax.ShapeDtypeStruct((sub, dil * d), BF), jax.ShapeDtypeStruct((sub, dil * d), F32)],
        compiler_params=_params(("parallel", "parallel")),
    )(qv, kvv, kvv, kvv, kvv)
    return o.reshape(s, d), lse.reshape(s, d)


def _attn_combine(os, lses, name):
    s, d = os[0].shape
    tm = _pick(s, 512, 8)
    nbr = len(os)

    def body(*refs):
        o_refs, l_refs = refs[:nbr], refs[nbr:2 * nbr]
        o_ref, lt_ref = refs[2 * nbr], refs[2 * nbr + 1]
        ls = [r[...] for r in l_refs]
        m = ls[0]
        for v in ls[1:]:
            m = jnp.maximum(m, v)
        tot = jnp.exp(ls[0] - m)
        for v in ls[1:]:
            tot = tot + jnp.exp(v - m)
        lt = m + jnp.log(tot)
        acc = jnp.exp(ls[0] - lt) * o_refs[0][...].astype(F32)
        for v, r in zip(ls[1:], o_refs[1:]):
            acc = acc + jnp.exp(v - lt) * r[...].astype(F32)
        o_ref[...] = acc.astype(BF)
        lt_ref[...] = lt

    return pl.pallas_call(
        body, name=name, grid=(s // tm,),
        in_specs=[_row_spec(tm, d)] * (2 * nbr), out_specs=[_row_spec(tm, d), _row_spec(tm, d)],
        out_shape=[jax.ShapeDtypeStruct((s, d), BF), jax.ShapeDtypeStruct((s, d), F32)],
        compiler_params=_params(("parallel",)),
    )(*os, *lses)


def _attn_bwd(q_all, kv, do, o, lt, g, slopes, d, name, dk_in=None, dv_in=None):
    window, dil = BRANCHES[g]
    n_steps = window // dil
    blk = max(ATTN_BLOCK, n_steps)
    s = q_all.shape[0]
    sub = s // dil
    nb = sub // blk
    nh = d // HEAD_DIM
    nbr = len(BRANCHES)
    qv = q_all.reshape(sub, dil * nbr * d)
    kvv = kv.reshape(sub, dil * 2 * nbr * d)
    scale = HEAD_DIM ** -0.5
    acc_in = dk_in is not None

    def body(*refs):
        q_ref, do_ref, o_ref, lt_ref, kp_ref, kc_ref, vp_ref, vc_ref = refs[:8]
        n_in = 10 if acc_in else 8
        dkin_ref, dvin_ref = (refs[8], refs[9]) if acc_in else (None, None)
        dq_ref, dk_ref, dv_ref, keep_k, keep_v, part_k, part_v = refs[n_in:n_in + 7]
        t = pl.program_id(1)

        def emit(prev_k, prev_v):
            if acc_in:
                prev_k = prev_k + dkin_ref[...].astype(F32)
                prev_v = prev_v + dvin_ref[...].astype(F32)
            dk_ref[...] = prev_k.astype(BF)
            dv_ref[...] = prev_v.astype(BF)

        @pl.when(t < nb)
        def _():
            valid, dist = _band(blk, n_steps, dil, t == 0)
            qb = q_ref[...]
            dob = do_ref[...]
            kb = jnp.concatenate([kp_ref[...], kc_ref[...]], axis=0)
            vb = jnp.concatenate([vp_ref[...], vc_ref[...]], axis=0)
            for h in range(nh):
                sl = slice(h * HEAD_DIM, (h + 1) * HEAD_DIM)
                qh, kh, vh, doh = qb[:, sl], kb[:, sl], vb[:, sl], dob[:, sl]
                sc = lax.dot_general(qh, kh, (((1,), (1,)), ((), ())), preferred_element_type=F32) * scale
                sc = sc - float(slopes[h]) * dist
                p = jnp.where(valid, jnp.exp(jnp.minimum(sc - lt_ref[:, sl][:, 0:1], 30.0)), 0.0)
                dlt = jnp.sum(doh.astype(F32) * o_ref[:, sl].astype(F32), axis=-1, keepdims=True)
                dp = lax.dot_general(doh, vh, (((1,), (1,)), ((), ())), preferred_element_type=F32)
                ds = (p * (dp - dlt)).astype(BF)
                dq_ref[:, sl] = (jnp.dot(ds, kh, preferred_element_type=F32) * scale).astype(BF)
                part_k[:, sl] = lax.dot_general(ds, qh, (((0,), (0,)), ((), ())), preferred_element_type=F32) * scale
                part_v[:, sl] = lax.dot_general(p.astype(BF), doh, (((0,), (0,)), ((), ())),
                                                preferred_element_type=F32)

            @pl.when(t > 0)
            def _():
                emit(keep_k[...] + part_k[0:blk, :], keep_v[...] + part_v[0:blk, :])

            keep_k[...] = part_k[blk:, :]
            keep_v[...] = part_v[blk:, :]

        @pl.when(t == nb)
        def _():
            emit(keep_k[...], keep_v[...])

    def qspec(col):
        return pl.BlockSpec((blk, d), lambda r, t: (jnp.minimum(t, nb - 1), r * col[0] + col[1]))

    def kspec(col, prev):
        if prev:
            return pl.BlockSpec((blk, d), lambda r, t: (jnp.maximum(jnp.minimum(t, nb - 1) - 1, 0), r * col[0] + col[1]))
        return qspec(col)

    kout = pl.BlockSpec((blk, d), lambda r, t: (jnp.maximum(t - 1, 0), r))
    qcol, kcol, vcol, one = (nbr, g), (2 * nbr, g), (2 * nbr, nbr + g), (1, 0)
    in_specs = [qspec(qcol), qspec(one), qspec(one), qspec(one),
                kspec(kcol, True), kspec(kcol, False), kspec(vcol, True), kspec(vcol, False)]
    args = [qv, do.reshape(sub, dil * d), o.reshape(sub, dil * d), lt.reshape(sub, dil * d), kvv, kvv, kvv, kvv]
    if acc_in:
        in_specs += [kout, kout]
        args += [dk_in.reshape(sub, dil * d), dv_in.reshape(sub, dil * d)]
    shp = jax.ShapeDtypeStruct((sub, dil * d), BF)
    dq, dk, dv = pl.pallas_call(
        body, name=name, grid=(dil, nb + 1),
        in_specs=in_specs, out_specs=[qspec(one), kout, kout], out_shape=[shp, shp, shp],
        scratch_shapes=[pltpu.VMEM((blk, d), F32), pltpu.VMEM((blk, d), F32),
                        pltpu.VMEM((2 * blk, d), F32), pltpu.VMEM((2 * blk, d), F32)],
        compiler_params=_params(("parallel", "arbitrary")),
    )(*args)
    return dq.reshape(s, d), dk.reshape(s, d), dv.reshape(s, d)


def _adamw(parts_list, w, m, v, name):
    nl, r, c = w.shape
    assert len(parts_list) == nl
    npart = parts_list[0].shape[0]
    tr = _pick(r, 256, 16)
    c1 = 1.0 / (1.0 - ADAM_B1 ** ADAM_STEP)
    c2 = 1.0 / (1.0 - ADAM_B2 ** ADAM_STEP)

    def body(*refs):
        p_refs = refs[:nl]
        w_ref, m_ref, v_ref, g_ref, d_ref, nm_ref, nv_ref = refs[nl:]
        layer = pl.program_id(0)
        for idx in range(nl):
            @pl.when(layer == idx)
            def _(p_ref=p_refs[idx]):
                g = p_ref[0].astype(F32)
                for k in range(1, npart):
                    g = g + p_ref[k].astype(F32)
                nm = ADAM_B1 * m_ref[...] + (1.0 - ADAM_B1) * g
                nv = ADAM_B2 * v_ref[...] + (1.0 - ADAM_B2) * (g * g)
                g_ref[...] = g
                nm_ref[...] = nm
                nv_ref[...] = nv
                d_ref[...] = -ADAM_LR * ((nm * c1) / (jnp.sqrt(nv * c2) + ADAM_EPS) + ADAM_WD * w_ref[...])

    def part_spec(idx):
        return pl.BlockSpec((npart, tr, c), lambda l, i: (0, jnp.where(l == idx, i, 0), 0))

    blk = pl.BlockSpec((None, tr, c), lambda l, i: (l, i, 0))
    shp = jax.ShapeDtypeStruct((nl, r, c), F32)
    return pl.pallas_call(
        body, name=name, grid=(nl, r // tr),
        in_specs=[part_spec(idx) for idx in range(nl)] + [blk, blk, blk],
        out_specs=[blk, blk, blk, blk], out_shape=[shp, shp, shp, shp],
        compiler_params=_params(("parallel", "parallel")),
    )(*parts_list, w, m, v)


def _full_from_slots(slots, shard_shape, axis):
    a = slots.reshape((N_DEV,) + tuple(shard_shape))
    a = jnp.moveaxis(a, 0, axis)
    full = list(shard_shape)
    full[axis] *= N_DEV
    return a.reshape(full)


def kernel(x, c, ada_w, ada_b, norm1_g, norm2_g, pool_w_in, pool_w_grp, pool_scale, pool_w_out, kv_norm_g, kv_ada_w, kv_ada_b, w_kv, attn_w_q, attn_w_o, ffn_w_up, ffn_conv_w, ffn_conv_b, ffn_w_down, final_g, loss_target, m_ada_w, m_ada_b, m_norm1_g, m_norm2_g, m_pool_w_in, m_pool_w_grp, m_pool_scale, m_pool_w_out, m_kv_norm_g, m_kv_ada_w, m_kv_ada_b, m_w_kv, m_attn_w_q, m_attn_w_o, m_ffn_w_up, m_ffn_conv_w, m_ffn_conv_b, m_ffn_w_down, m_final_g, v_ada_w, v_ada_b, v_norm1_g, v_norm2_g, v_pool_w_in, v_pool_w_grp, v_pool_scale, v_pool_w_out, v_kv_norm_g, v_kv_ada_w, v_kv_ada_b, v_w_kv, v_attn_w_q, v_attn_w_o, v_ffn_w_up, v_ffn_conv_w, v_ffn_conv_b, v_ffn_w_down, v_final_g):
    weights = dict(ada_w=ada_w, ada_b=ada_b, norm1_g=norm1_g, norm2_g=norm2_g, pool_w_in=pool_w_in,
                   pool_w_grp=pool_w_grp, pool_scale=pool_scale, pool_w_out=pool_w_out, kv_norm_g=kv_norm_g,
                   kv_ada_w=kv_ada_w, kv_ada_b=kv_ada_b, w_kv=w_kv, attn_w_q=attn_w_q, attn_w_o=attn_w_o,
                   ffn_w_up=ffn_w_up, ffn_conv_w=ffn_conv_w, ffn_conv_b=ffn_conv_b, ffn_w_down=ffn_w_down,
                   final_g=final_g)
    mom1 = dict(ada_w=m_ada_w, ada_b=m_ada_b, norm1_g=m_norm1_g, norm2_g=m_norm2_g, pool_w_in=m_pool_w_in,
                pool_w_grp=m_pool_w_grp, pool_scale=m_pool_scale, pool_w_out=m_pool_w_out, kv_norm_g=m_kv_norm_g,
                kv_ada_w=m_kv_ada_w, kv_ada_b=m_kv_ada_b, w_kv=m_w_kv, attn_w_q=m_attn_w_q, attn_w_o=m_attn_w_o,
                ffn_w_up=m_ffn_w_up, ffn_conv_w=m_ffn_conv_w, ffn_conv_b=m_ffn_conv_b, ffn_w_down=m_ffn_w_down,
                final_g=m_final_g)
    mom2 = dict(ada_w=v_ada_w, ada_b=v_ada_b, norm1_g=v_norm1_g, norm2_g=v_norm2_g, pool_w_in=v_pool_w_in,
                pool_w_grp=v_pool_w_grp, pool_scale=v_pool_scale, pool_w_out=v_pool_w_out, kv_norm_g=v_kv_norm_g,
                kv_ada_w=v_kv_ada_w, kv_ada_b=v_kv_ada_b, w_kv=v_w_kv, attn_w_q=v_attn_w_q, attn_w_o=v_attn_w_o,
                ffn_w_up=v_ffn_w_up, ffn_conv_w=v_ffn_conv_w, ffn_conv_b=v_ffn_conv_b, ffn_w_down=v_ffn_w_down,
                final_g=v_final_g)
    order = list(weights)

    seq, d = x.shape[1], x.shape[2]
    depth = ada_w.shape[0]
    n_pool = pool_w_in.shape[0]
    f = ffn_conv_b.shape[1]
    nbr = len(BRANCHES)
    nh = d // HEAD_DIM
    slopes = _alibi_slopes(nbr * nh).reshape(nbr, nh)
    me = 4 * lax.axis_index("x") + 2 * lax.axis_index("y") + lax.axis_index("c")
    xs = x[0]
    tgt = loss_target[0]

    def start_gather(l, after):
        if l < n_pool:
            mixer = [pool_w_in[l], pool_w_grp[l].reshape(-1, pool_w_grp.shape[-1]), pool_w_out[l]]
        else:
            j = l - n_pool
            mixer = [attn_w_q[j], attn_w_o[j]] + ([w_kv] if j == 0 else [])
        srcs = [a.astype(BF) for a in mixer + [ffn_w_up[l], ffn_w_down[l]]]
        *srcs, _ = lax.optimization_barrier((*srcs, after))
        return _comm_start([srcs[:-2], srcs[-2:]], f"gather_start_{l}", scatter=False)

    cond = c * (1.0 / (1.0 + jnp.exp(-c)))
    small_in = jnp.concatenate([cond.reshape(-1), ffn_conv_w.reshape(-1), pool_scale.reshape(-1)])
    n_small_in = small_in.shape[0]
    gath = _all_gather(_to_rows(small_in), "gather_small").reshape(N_DEV, -1)[:, :n_small_in]
    cond_all = gath[:, :d]
    o1 = d + ffn_conv_w.size
    conv_w_full = _full_from_slots(gath[:, d:o1], ffn_conv_w.shape, 2)
    pool_scale_full = _full_from_slots(gath[:, o1:], pool_scale.shape, 1)
    cond16 = jnp.concatenate([cond_all, jnp.zeros_like(cond_all)], axis=0)

    mod_part = _ada_fwd(cond16, ada_w, "ada_fwd")[:, :N_DEV]
    kv_part = _ada_fwd(cond16, kv_ada_w[None], "kv_ada_fwd")[0, :N_DEV]
    n_mod = depth * mod_part.shape[2] + kv_part.shape[1]
    send = jnp.concatenate([jnp.moveaxis(mod_part, 1, 0).reshape(N_DEV, -1), kv_part], axis=1)
    send_rows = jax.vmap(_to_rows)(send)
    got = _all_to_all(send_rows, "exchange_mod").reshape(N_DEV, -1)[:, :n_mod]
    ncol = mod_part.shape[2]
    mods = []
    for l in range(depth):
        row = got[:, l * ncol:(l + 1) * ncol].reshape(1, -1) + ada_b[l][None]
        mods.append([row[:, k * d:(k + 1) * d] for k in range(6)])
    kv_row = got[:, depth * ncol:].reshape(1, -1) + kv_ada_b[None]
    kv_shift, kv_scale = kv_row[:, :d], kv_row[:, d:]
    gather_handles = [None] * depth
    gather_handles[0], _ = start_gather(0, got)

    def vec(a):
        return a.reshape(1, -1)

    nq = 4
    fq = f // nq
    ng = len(POOL_WINDOWS)
    cw_slots = jnp.moveaxis(conv_w_full.reshape(depth, CONV_WIDTH, nq, fq), 2, 1)
    cb_slots = ffn_conv_b.reshape(depth, nq, 1, fq)

    saved = []
    xcur = xs
    kvs = None
    hkv = None
    x_kv = None
    w_kv_slots = None
    for l in range(depth):
        sh1, sc1, g1, sh2, sc2, g2 = mods[l]
        st = dict(x=xcur)
        h = _norm_mod(xcur, vec(norm1_g[l]), sc1, sh1, f"norm1_{l}")
        if l + 1 < depth:
            gather_handles[l + 1], tok = start_gather(l + 1, h)
            h, _ = lax.optimization_barrier((h, tok))
        st["h"] = h
        gw = _comm_wait(gather_handles[l][0], h, f"gather_wait_mixer_{l}")
        if l < n_pool:
            w_in = gw[0].reshape(d, d)
            w_grp = jnp.moveaxis(gw[1].reshape(N_DEV, ng, -1, d // ng), 0, 1).reshape(ng, d // ng, d // ng)
            w_out = gw[2].reshape(d, d)
            u = _mm(h, w_in, f"pool_in_{l}")
            pooled = _pool_fwd(u, f"pool_fwd_{l}")
            z, y = _grp_fwd(pooled, w_grp, vec(pool_scale_full[l]), f"grp_fwd_{l}")
            mix, x1 = _mm(y, w_out, f"pool_out_{l}", gate=g1, resid=xcur, pre_dtype=BF)
            st.update(pooled=pooled, z=z, y=y, w_in=w_in, w_grp=w_grp, w_out=w_out)
        else:
            j = l - n_pool
            w_q_slots, w_o = gw[0], gw[1].reshape(d, d)
            if j == 0:
                w_kv_slots = gw[2]
                x_kv = xcur
                hkv = _norm_mod(xcur, vec(kv_norm_g), kv_scale, kv_shift, "norm_kv")
                kvs = _mm(hkv, w_kv_slots, "kv_proj", bf="qkn", out_dtype=BF)
            q = _mm(h, w_q_slots, f"q_proj_{l}", bf="qkn", out_dtype=BF)
            outs, lses = [], []
            for g in range(nbr):
                og, lg = _attn_fwd(q, kvs, g, slopes[g], d, f"attn_fwd_{l}_{g}")
                outs.append(og)
                lses.append(lg)
            o, lt = _attn_combine(outs, lses, f"attn_mix_{l}")
            mix, x1 = _mm(o, w_o, f"attn_out_{l}", gate=g1, resid=xcur, pre_dtype=BF)
            st.update(q=q, o=o, lt=lt, w_q=w_q_slots, w_o=w_o)
        h2 = _norm_mod(x1, vec(norm2_g[l]), sc2, sh2, f"norm2_{l}")
        w_up_slots, w_down = _comm_wait(gather_handles[l][1], h2, f"gather_wait_ffn_{l}")
        w_down = w_down.reshape(f, d)
        st.update(w_up=w_up_slots, w_down=w_down)
        up = _mm(h2, w_up_slots, f"ffn_up_{l}", bf="qkn", of="qmn", out_dtype=BF).reshape(2, nq, seq, fq)
        act = _ffn_act(up, cw_slots[l], cb_slots[l], f"ffn_act_{l}")
        ffo, x2 = _mm(act, w_down, f"ffn_down_{l}", af="qmk", gate=g2, resid=x1, pre_dtype=BF)
        st.update(mix=mix, x1=x1, h2=h2, up=up, act=act, ffo=ffo)
        saved.append(st)
        xcur = x2

    dx, loss_blk, d_final_g = _final_loss(xcur, tgt, vec(final_g), "final_loss")
    loss = lax.psum(loss_blk[0, 0], ("x", "y", "c"))

    d_mod = [None] * depth
    d_n1 = [None] * depth
    d_n2 = [None] * depth
    d_conv = [None] * depth
    d_pscale = [None] * n_pool
    dk_acc = [None] * nbr
    dv_acc = [None] * nbr
    ffn_handles = [None] * depth
    mixer_handles = [None] * depth
    tok = 0.0
    for l in reversed(range(depth)):
        sh1, sc1, g1, sh2, sc2, g2 = mods[l]
        st = saved[l]
        dffo, dg2 = _gate_bwd(dx, st["ffo"], g2 + tok, f"gate2_bwd_{l}")
        g_down = _mm(st["act"], dffo, f"ffn_down_dw_{l}", af="qkm", out_dtype=BF)
        dact = _mm(dffo, st["w_down"], f"ffn_down_dx_{l}", bf="nk", of="qmn", tn=fq, out_dtype=BF)
        dup, dc = _ffn_act_bwd(st["up"], dact, cw_slots[l], cb_slots[l], f"ffn_act_bwd_{l}")
        dup = dup.reshape(N_DEV, seq, -1)
        d_conv[l] = dc
        g_up = _mm(st["h2"], dup, f"ffn_up_dw_{l}", af="km", bf="qkn", of="qmn", out_dtype=BF)
        dh2 = _mm(dup, st["w_up"], f"ffn_up_dx_{l}", af="qmk", bf="qnk")
        (ffn_handles[l],), tok = _comm_start([[g_up, g_down.reshape(N_DEV, -1, d)]], f"exchange_start_ffn_{l}",
                                             scatter=True)
        dx1, dsh2, dw2 = _norm_mod_bwd(st["x1"], dh2, vec(norm2_g[l]), sc2 + tok, dx, f"norm2_bwd_{l}")
        d_n2[l] = dw2 * (1.0 + sc2)
        dsc2 = dw2 * vec(norm2_g[l])

        dmix, dg1 = _gate_bwd(dx1, st["mix"], g1, f"gate1_bwd_{l}")
        if l < n_pool:
            g_out = _mm(st["y"], dmix, f"pool_out_dw_{l}", af="km", out_dtype=BF)
            dy = _mm(dmix, st["w_out"], f"pool_out_dx_{l}", bf="nk", out_dtype=BF)
            dz, dpool, dps = _grp_bwd(dy, st["z"], st["w_grp"], vec(pool_scale_full[l]), f"grp_bwd_{l}")
            d_pscale[l] = dps
            g_grp = _grp_dw(st["pooled"], dz, ng, f"grp_dw_{l}")
            du = _pool_bwd(dpool, f"pool_bwd_{l}")
            g_in = _mm(st["h"], du, f"pool_in_dw_{l}", af="km", out_dtype=BF)
            dh = _mm(du, st["w_in"], f"pool_in_dx_{l}", bf="nk")
            g_grp_slots = jnp.moveaxis(g_grp.reshape(ng, N_DEV, -1, d // ng), 1, 0).reshape(N_DEV, -1, d // ng)
            send = [g_in.reshape(N_DEV, -1, d), g_grp_slots.astype(BF), g_out.reshape(N_DEV, -1, d)]
        else:
            j = l - n_pool
            g_o = _mm(st["o"], dmix, f"attn_out_dw_{l}", af="km", out_dtype=BF)
            do = _mm(dmix, st["w_o"], f"attn_out_dx_{l}", bf="nk", out_dtype=BF)
            dqs = []
            for g in range(nbr):
                dq_g, dk_g, dv_g = _attn_bwd(st["q"], kvs, do, st["o"], st["lt"], g, slopes[g], d,
                                             f"attn_bwd_{l}_{g}", dk_in=dk_acc[g], dv_in=dv_acc[g])
                dqs.append(dq_g)
                dk_acc[g], dv_acc[g] = dk_g, dv_g
            dq = jnp.concatenate(dqs, axis=1)
            nqc = st["w_q"].shape[2]
            g_q = _mm(st["h"], dq, f"q_proj_dw_{l}", af="km", of="qmn", tn=nqc, out_dtype=BF)
            dh = _mm(dq, st["w_q"], f"q_proj_dx_{l}", bf="qnk")
            send = [g_q, g_o.reshape(N_DEV, -1, d)]
        dx0, dsh1, dw1 = _norm_mod_bwd(st["x"], dh, vec(norm1_g[l]), sc1, dx1, f"norm1_bwd_{l}")
        d_n1[l] = dw1 * (1.0 + sc1)
        dsc1 = dw1 * vec(norm1_g[l])
        d_mod[l] = jnp.concatenate([dsh1, dsc1, dg1, dsh2, dsc2, dg2], axis=1)
        dx = dx0
        if l == n_pool:
            dkv = jnp.concatenate(dk_acc + dv_acc, axis=1)
            nkc = w_kv_slots.shape[2]
            g_kv = _mm(hkv, dkv, "kv_proj_dw", af="km", of="qmn", tn=nkc, out_dtype=BF)
            dhkv = _mm(dkv, w_kv_slots, "kv_proj_dx", bf="qnk")
            dx, dsh_kv, dw_kv = _norm_mod_bwd(x_kv, dhkv, vec(kv_norm_g), kv_scale, dx, "norm_kv_bwd")
            d_kv_norm = dw_kv * (1.0 + kv_scale)
            d_kv_mod = jnp.concatenate([dsh_kv, dw_kv * vec(kv_norm_g)], axis=1)
            send.append(g_kv)
        (mixer_handles[l],), tok = _comm_start([send], f"exchange_start_mixer_{l}", scatter=True)
    grad_x = dx[None]
    d_final_g = d_final_g + tok

    small = [jnp.concatenate(d_mod, axis=1).reshape(-1), d_kv_mod.reshape(-1),
             jnp.concatenate(d_n1, axis=0).reshape(-1), jnp.concatenate(d_n2, axis=0).reshape(-1),
             d_kv_norm.reshape(-1), d_final_g.reshape(-1),
             jnp.stack([dcl[:, 3, :] for dcl in d_conv]).reshape(-1),
             jnp.stack([jnp.moveaxis(dcl[:, 0:CONV_WIDTH, :], 0, 1) for dcl in d_conv]).reshape(-1),
             jnp.concatenate(d_pscale, axis=0).reshape(-1)]
    sizes = [a.shape[0] for a in small]
    small_rows = _to_rows(jnp.concatenate(small))
    small_all = _all_gather(small_rows, "gather_small_grads")
    dmod_all = small_all.reshape(N_DEV, -1)[:, :sizes[0] + sizes[1]]

    dmod16 = jnp.concatenate([dmod_all, jnp.zeros_like(dmod_all)], axis=0)
    dm = dmod16[:, :sizes[0]].reshape(16, depth, N_DEV, ncol)
    dm_mine = lax.dynamic_index_in_dim(dm, me, axis=2, keepdims=False)
    g_ada_w = _ada_bwd(cond16, jnp.moveaxis(dm_mine, 0, 1), "ada_bwd")
    nkv = kv_part.shape[1]
    dkm = dmod16[:, sizes[0]:].reshape(16, N_DEV, nkv)
    dkm_mine = lax.dynamic_index_in_dim(dkm, me, axis=1, keepdims=False)
    g_kv_ada_w = _ada_bwd(cond16, dkm_mine[None], "kv_ada_bwd")

    res = {}

    def update(n, parts_list, shape3):
        w3, m3, v3 = (a[n].reshape(shape3) for a in (weights, mom1, mom2))
        outs = _adamw(parts_list, w3, m3, v3, f"adamw_{n}")
        res[n] = [a.reshape(weights[n].shape) for a in outs]

    update("ada_w", [g_ada_w[l][None] for l in range(depth)], ada_w.shape)
    update("kv_ada_w", [g_kv_ada_w], (1,) + kv_ada_w.shape)
    after = res["ada_w"][0]
    parts_ffn = [_comm_wait(ffn_handles[l], after, f"exchange_wait_ffn_{l}") for l in reversed(range(depth))][::-1]
    parts = [_comm_wait(mixer_handles[l], after, f"exchange_wait_mixer_{l}") for l in reversed(range(depth))][::-1]
    pool_layers, attn_layers = range(n_pool), range(n_pool, depth)
    update("ffn_w_up", [parts_ffn[l][0] for l in range(depth)], ffn_w_up.shape)
    update("ffn_w_down", [parts_ffn[l][1] for l in range(depth)], ffn_w_down.shape)
    update("attn_w_q", [parts[l][0] for l in attn_layers], attn_w_q.shape)
    update("attn_w_o", [parts[l][1] for l in attn_layers], (depth - n_pool, -1, d))
    update("w_kv", [parts[n_pool][2]], (1,) + w_kv.shape)
    update("pool_w_in", [parts[l][0] for l in pool_layers], (n_pool, -1, d))
    update("pool_w_grp", [parts[l][1] for l in pool_layers], (n_pool, -1, d // ng))
    update("pool_w_out", [parts[l][2] for l in pool_layers], (n_pool, -1, d))

    tot = small_all.reshape(N_DEV, -1)
    offs = np.cumsum([0] + sizes)
    seg = {k: (int(offs[i]), int(offs[i + 1])) for i, k in enumerate(
        ["mod", "kv_mod", "n1", "n2", "kv_norm", "final", "conv_b", "conv_w", "pscale"])}

    def rows_of(a, b):
        return tot[:, a:b]

    nf8 = f // N_DEV
    conv_w_parts = lax.dynamic_slice_in_dim(
        rows_of(*seg["conv_w"]).reshape(N_DEV, depth, CONV_WIDTH, N_DEV, nf8), me, 1, axis=3).reshape(N_DEV, -1)
    nd8 = d // N_DEV
    pscale_parts = lax.dynamic_slice_in_dim(
        rows_of(*seg["pscale"]).reshape(N_DEV, n_pool, N_DEV, nd8), me, 1, axis=2).reshape(N_DEV, -1)
    small_names = ["ada_b", "norm1_g", "norm2_g", "pool_scale", "kv_norm_g", "kv_ada_b", "ffn_conv_w",
                   "ffn_conv_b", "final_g"]
    small_parts = [rows_of(*seg["mod"]), rows_of(*seg["n1"]), rows_of(*seg["n2"]), pscale_parts,
                   rows_of(*seg["kv_norm"]), rows_of(*seg["kv_mod"]), conv_w_parts, rows_of(*seg["conv_b"]),
                   rows_of(*seg["final"])]
    sp = jnp.concatenate(small_parts, axis=1)
    n_sp = sp.shape[1]
    sp_rows = jax.vmap(_to_rows)(sp)

    def packed(src):
        return _to_rows(jnp.concatenate([src[n].reshape(-1) for n in small_names]))[None]

    outs = _adamw([sp_rows], packed(weights), packed(mom1), packed(mom2), "adamw_small")
    outs = [a.reshape(-1)[:n_sp] for a in outs]
    off = 0
    for n in small_names:
        size = weights[n].size
        res[n] = [a[off:off + size].reshape(weights[n].shape) for a in outs]
        off += size

    grads = [res[n][0] for n in order]
    deltas = [res[n][1] for n in order]
    new_m = [res[n][2] for n in order]
    new_v = [res[n][3] for n in order]
    return (loss, grad_x, *grads, *deltas, *new_m, *new_v)
```

```python
import math

import numpy as np
import jax
import jax.numpy as jnp
from jax import lax
from jax.experimental import pallas as pl
from jax.experimental.pallas import tpu as pltpu

F32 = jnp.float32
BF = jnp.bfloat16

POOL_WINDOWS = (2, 4, 8, 16)
BRANCHES = ((128, 1), (512, 4), (2048, 16))
HEAD_DIM = 64
ATTN_BLOCK = 128
CONV_WIDTH = 3
EPS = 1e-6
ADAM_LR = 0.001
ADAM_B1 = 0.9
ADAM_B2 = 0.999
ADAM_EPS = 1e-08
ADAM_WD = 0.01
ADAM_STEP = 10

N_DEV = 8
LANES = 128
POOL_HALO = 16
CONV_HALO = 8
FFN_ROWS = 16
VMEM_LIMIT = 48 * 1024 * 1024
MM_TILE = 1024
NEG = -1e30

MESH = pl.DeviceIdType.MESH
ANY = pl.BlockSpec(memory_space=pl.ANY)


def _params(sem=None):
    if sem is None:
        return pltpu.CompilerParams(vmem_limit_bytes=VMEM_LIMIT)
    return pltpu.CompilerParams(dimension_semantics=sem, vmem_limit_bytes=VMEM_LIMIT)


def _pick(dim, pref, mult=LANES):
    if dim <= pref:
        return dim
    t = (pref // mult) * mult
    while t >= mult:
        if dim % t == 0:
            return t
        t -= mult
    return dim


def _alibi_slopes(n):
    def pow2(m):
        start = 2.0 ** (-(2.0 ** -(math.log2(m) - 3)))
        return [start ** (i + 1) for i in range(m)]
    if math.log2(n).is_integer():
        s = pow2(n)
    else:
        c = 2 ** math.floor(math.log2(n))
        s = pow2(c) + pow2(2 * c)[0::2][: n - c]
    s = np.asarray(s, dtype=np.float32)
    return -np.sort(-s)


def _my_place():
    return lax.axis_index("x"), lax.axis_index("y"), lax.axis_index("c")


def _all_gather_many(xs, name):
    n = len(xs)

    def body(*refs):
        x_refs, out_refs = refs[:n], refs[n:2 * n]
        send_sems, recv_sems, local_sems = refs[2 * n:]
        xi, yi, ci = _my_place()
        me, sibling = (xi, yi, ci), (xi, yi, 1 - ci)
        chips = [(1 - xi, yi), (xi, 1 - yi), (1 - xi, 1 - yi)]

        def slot(a, px, py, pc):
            return out_refs[a].at[4 * px + 2 * py + pc]

        def copy(a, k, block, to, src=None):
            return pltpu.make_async_remote_copy(
                src_ref=slot(a, *block) if src is None else src, dst_ref=slot(a, *block),
                send_sem=send_sems.at[7 * a + k], recv_sem=recv_sems.at[7 * a + k],
                device_id=to, device_id_type=MESH)

        mine = [pltpu.make_async_copy(x_refs[a], slot(a, *me), local_sems.at[a]) for a in range(n)]
        for cp in mine:
            cp.start()
        sent = []
        for a in range(n):
            first = [copy(a, 0, me, sibling, src=x_refs[a])]
            first += [copy(a, 1 + j, me, (*chip, ci), src=x_refs[a]) for j, chip in enumerate(chips)]
            for cp in first:
                cp.start()
            sent += first
        for a in range(n):
            for j, chip in enumerate(chips):
                copy(a, 1 + j, (*chip, ci), me).wait_recv()
                fwd = copy(a, 4 + j, (*chip, ci), sibling)
                fwd.start()
                sent.append(fwd)
        for a in range(n):
            copy(a, 0, sibling, me).wait_recv()
            for j, chip in enumerate(chips):
                copy(a, 4 + j, (*chip, 1 - ci), me).wait_recv()
        for cp in sent:
            cp.wait_send()
        for cp in mine:
            cp.wait()

    return pl.pallas_call(
        body, name=name,
        out_shape=[jax.ShapeDtypeStruct((N_DEV,) + x.shape, x.dtype) for x in xs],
        in_specs=[ANY] * n, out_specs=[ANY] * n,
        scratch_shapes=[pltpu.SemaphoreType.DMA((7 * n,)), pltpu.SemaphoreType.DMA((7 * n,)),
                        pltpu.SemaphoreType.DMA((n,))],
    )(*xs)


def _all_gather(x, name):
    return _all_gather_many([x], name)[0]


def _all_to_all_many(xs, name):
    n = len(xs)

    def body(*refs):
        x_refs, out_refs = refs[:n], refs[n:2 * n]
        send_sems, recv_sems, local_sems = refs[2 * n:]
        xi, yi, ci = _my_place()
        me = 4 * xi + 2 * yi + ci
        mine = [pltpu.make_async_copy(x_refs[a].at[me], out_refs[a].at[me], local_sems.at[a]) for a in range(n)]
        for cp in mine:
            cp.start()
        copies = []
        for a in range(n):
            for k in range(1, N_DEV):
                px = 1 - xi if k & 4 else xi
                py = 1 - yi if k & 2 else yi
                pc = 1 - ci if k & 1 else ci
                peer = 4 * px + 2 * py + pc
                cp = pltpu.make_async_remote_copy(
                    src_ref=x_refs[a].at[peer], dst_ref=out_refs[a].at[me],
                    send_sem=send_sems.at[7 * a + k - 1], recv_sem=recv_sems.at[7 * a + k - 1],
                    device_id=(px, py, pc), device_id_type=MESH)
                cp.start()
                copies.append(cp)
        for cp in copies:
            cp.wait()
        for cp in mine:
            cp.wait()

    return pl.pallas_call(
        body, name=name,
        out_shape=[jax.ShapeDtypeStruct(x.shape, x.dtype) for x in xs],
        in_specs=[ANY] * n, out_specs=[ANY] * n,
        scratch_shapes=[pltpu.SemaphoreType.DMA((7 * n,)), pltpu.SemaphoreType.DMA((7 * n,)),
                        pltpu.SemaphoreType.DMA((n,))],
    )(*xs)


def _all_to_all(x, name):
    return _all_to_all_many([x], name)[0]


HBM = pl.BlockSpec(memory_space=pltpu.HBM)
SEM = pl.BlockSpec(memory_space=pltpu.SEMAPHORE)
EFFECT = pltpu.SideEffectType.DATAFLOW_SIDE_EFFECTING


def _peer(k, xi, yi, ci):
    px = 1 - xi if k & 4 else xi
    py = 1 - yi if k & 2 else yi
    pc = 1 - ci if k & 1 else ci
    return (px, py, pc), 4 * px + 2 * py + pc


def _split_copies(x_refs, land_refs, send_sem, recv_sem, scatter):
    xi, yi, ci = _my_place()
    me = 4 * xi + 2 * yi + ci
    copies = []
    for p, (x_ref, land_ref) in enumerate(zip(x_refs, land_refs)):
        for k in range(1, N_DEV):
            place, peer = _peer(k, xi, yi, ci)
            copies.append(pltpu.make_async_remote_copy(
                src_ref=x_ref.at[peer] if scatter else x_ref, dst_ref=land_ref.at[me],
                send_sem=send_sem.at[7 * p + k - 1], recv_sem=recv_sem.at[7 * p + k - 1],
                device_id=place, device_id_type=MESH))
    return copies


def _comm_start(groups, name, scatter):
    sizes = [len(g) for g in groups]
    xs = [x for g in groups for x in g]
    n, ng = len(xs), len(groups)
    lands = [lax.empty(x.shape if scatter else (N_DEV,) + x.shape, x.dtype) for x in xs]
    starts = np.cumsum([0] + sizes)

    def body(*refs):
        x_refs, land_refs = refs[:n], refs[n:2 * n]
        send_sems, recv_sems = refs[2 * n:2 * n + ng], refs[2 * n + ng:2 * n + 2 * ng]
        token = refs[2 * n + 2 * ng + 2 * n]
        for gi in range(ng):
            lo, hi = int(starts[gi]), int(starts[gi + 1])
            for cp in _split_copies(x_refs[lo:hi], land_refs[lo:hi], send_sems[gi], recv_sems[gi], scatter):
                cp.start()
        token[...] = jnp.zeros_like(token)

    sem_shapes = [pltpu.SemaphoreType.DMA((7 * m,)) for m in sizes]
    thru = [pltpu.HBM(a.shape, a.dtype) for a in xs + lands]
    res = pl.pallas_call(
        body, name=name,
        out_shape=sem_shapes + sem_shapes + thru + [jax.ShapeDtypeStruct((8, LANES), F32)],
        in_specs=[HBM] * (2 * n),
        out_specs=[SEM] * (2 * ng) + [HBM] * (2 * n) + [pl.BlockSpec(memory_space=pltpu.VMEM)],
        input_output_aliases={i: 2 * ng + i for i in range(2 * n)},
        compiler_params=pltpu.CompilerParams(has_side_effects=EFFECT),
    )(*[pltpu.with_memory_space_constraint(a, pltpu.HBM) for a in xs + lands])
    send_sems, recv_sems = res[:ng], res[ng:2 * ng]
    x_thru, land_thru = res[2 * ng:2 * ng + n], res[2 * ng + n:2 * ng + 2 * n]
    handles = []
    for gi in range(ng):
        lo, hi = int(starts[gi]), int(starts[gi + 1])
        handles.append((send_sems[gi], recv_sems[gi], list(x_thru[lo:hi]), list(land_thru[lo:hi]), scatter))
    return handles, res[-1][0, 0]


def _comm_wait(handle, after, name):
    send_sem, recv_sem, x_thru, land_thru, scatter = handle
    m = len(x_thru)

    blocks = [a.shape[1:] if scatter else a.shape for a in x_thru]

    def body(*refs):
        x_refs, land_refs = refs[:m], refs[m:2 * m]
        local_sems, stage = refs[4 * m + 3], refs[4 * m + 4:]
        xi, yi, ci = _my_place()
        me = 4 * xi + 2 * yi + ci
        load = [pltpu.make_async_copy(x_refs[p].at[me] if scatter else x_refs[p], stage[p], local_sems.at[2 * p])
                for p in range(m)]
        store = [pltpu.make_async_copy(stage[p], land_refs[p].at[me], local_sems.at[2 * p + 1]) for p in range(m)]
        for cp in load:
            cp.start()
        for p in range(m):
            load[p].wait()
            store[p].start()
        for cp in _split_copies(x_refs, land_refs, refs[2 * m], refs[2 * m + 1], scatter):
            cp.wait_send()
            cp.wait_recv()
        for cp in store:
            cp.wait()

    res = pl.pallas_call(
        body, name=name,
        out_shape=[pltpu.HBM(a.shape, a.dtype) for a in x_thru + land_thru],
        in_specs=[HBM] * (2 * m) + [SEM, SEM, ANY], out_specs=[HBM] * (2 * m),
        input_output_aliases={i: i for i in range(2 * m)},
        scratch_shapes=[pltpu.SemaphoreType.DMA((2 * m,))] + [pltpu.VMEM(b, a.dtype) for b, a in zip(blocks, x_thru)],
        compiler_params=pltpu.CompilerParams(has_side_effects=EFFECT),
    )(*x_thru, *land_thru, send_sem, recv_sem, after)
    return list(res[m:])


def _to_rows(vec):
    n = vec.shape[0]
    unit = 8 * LANES
    pad = (-n) % unit
    if pad:
        vec = jnp.concatenate([vec, jnp.zeros((pad,), vec.dtype)])
    return vec.reshape(-1, LANES)


def _mm(a, b, name, *, af="mk", bf="kn", of="mn", out_dtype=F32, tm=None, tn=None, tk=None,
        gate=None, resid=None, pre_dtype=None):
    if af == "mk":
        m, kk = a.shape
    elif af == "km":
        kk, m = a.shape
    elif af == "qmk":
        qa, m, tk = a.shape
        kk = qa * tk
    else:
        qa, kk, tm = a.shape
        m = qa * tm
    if bf == "kn":
        k2, n = b.shape
    elif bf == "nk":
        n, k2 = b.shape
    elif bf == "qkn":
        qb, k2, tn = b.shape
        n = qb * tn
    else:
        qb, n, tkb = b.shape
        k2 = qb * tkb
        assert af != "qmk" or tkb == tk
        tk = tkb
    assert kk == k2, (name, a.shape, b.shape, af, bf)
    tm = _pick(m, MM_TILE) if tm is None else tm
    tn = _pick(n, MM_TILE) if tn is None else tn
    tk = _pick(kk, MM_TILE) if tk is None else tk
    assert m % tm == 0 and n % tn == 0 and kk % tk == 0, (name, m, n, kk, tm, tn, tk)
    nk = kk // tk
    a_spec = {"mk": pl.BlockSpec((tm, tk), lambda i, j, k: (i, k)),
              "km": pl.BlockSpec((tk, tm), lambda i, j, k: (k, i)),
              "qmk": pl.BlockSpec((None, tm, tk), lambda i, j, k: (k, i, 0)),
              "qkm": pl.BlockSpec((None, tk, tm), lambda i, j, k: (i, k, 0))}[af]
    b_spec = {"kn": pl.BlockSpec((tk, tn), lambda i, j, k: (k, j)),
              "nk": pl.BlockSpec((tn, tk), lambda i, j, k: (j, k)),
              "qkn": pl.BlockSpec((None, tk, tn), lambda i, j, k: (j, k, 0)),
              "qnk": pl.BlockSpec((None, tn, tk), lambda i, j, k: (k, j, 0))}[bf]
    dims = (((1 if af in ("mk", "qmk") else 0,), (0 if bf in ("kn", "qkn") else 1,)), ((), ()))
    in_specs, args = [a_spec, b_spec], [a, b]
    if gate is not None:
        assert of == "mn"
        in_specs.append(pl.BlockSpec((1, tn), lambda i, j, k: (0, j)))
        args.append(gate)
    if resid is not None:
        assert of == "mn"
        in_specs.append(pl.BlockSpec((tm, tn), lambda i, j, k: (i, j)))
        args.append(resid)
    if of == "mn":
        o_spec, o_shape = pl.BlockSpec((tm, tn), lambda i, j, k: (i, j)), (m, n)
    else:
        o_spec, o_shape = pl.BlockSpec((None, tm, tn), lambda i, j, k: (j, i, 0)), (n // tn, m, tn)
    out_shape, out_specs = [jax.ShapeDtypeStruct(o_shape, out_dtype)], [o_spec]
    if pre_dtype is not None:
        out_shape.insert(0, jax.ShapeDtypeStruct(o_shape, pre_dtype))
        out_specs.insert(0, o_spec)
    n_in = len(args)
    n_out = len(out_shape)

    def body(*refs):
        a_ref, b_ref = refs[0], refs[1]
        extra = list(refs[2:n_in])
        outs = refs[n_in:n_in + n_out]
        gate_ref = extra.pop(0) if gate is not None else None
        resid_ref = extra.pop(0) if resid is not None else None

        def product():
            return lax.dot_general(a_ref[...].astype(BF), b_ref[...].astype(BF), dims, preferred_element_type=F32)

        def finish(r):
            if pre_dtype is not None:
                outs[0][...] = r.astype(pre_dtype)
            if gate_ref is not None:
                r = r * gate_ref[...]
            if resid_ref is not None:
                r = resid_ref[...] + r
            outs[-1][...] = r.astype(out_dtype)

        if nk == 1:
            finish(product())
        else:
            acc = refs[n_in + n_out]
            k = pl.program_id(2)

            @pl.when(k == 0)
            def _():
                acc[...] = product()

            @pl.when(k > 0)
            def _():
                acc[...] += product()

            @pl.when(k == nk - 1)
            def _():
                finish(acc[...])

    res = pl.pallas_call(
        body, name=name, grid=(m // tm, n // tn, nk),
        in_specs=in_specs, out_specs=out_specs, out_shape=out_shape,
        scratch_shapes=[pltpu.VMEM((tm, tn), F32)] if nk > 1 else [],
        compiler_params=_params(("parallel", "parallel", "arbitrary")),
    )(*args)
    return res if pre_dtype is not None else res[0]


def _ada_fwd(cond16, w, name):
    nl, d, n = w.shape

    def body(c_ref, w_ref, o_ref):
        o_ref[...] = jnp.dot(c_ref[...].astype(BF), w_ref[...].astype(BF), preferred_element_type=F32)

    return pl.pallas_call(
        body, name=name, grid=(nl,),
        in_specs=[pl.BlockSpec((16, d), lambda l: (0, 0)), pl.BlockSpec((None, d, n), lambda l: (l, 0, 0))],
        out_specs=pl.BlockSpec((None, 16, n), lambda l: (l, 0, 0)),
        out_shape=jax.ShapeDtypeStruct((nl, 16, n), F32),
        compiler_params=_params(("parallel",)),
    )(cond16, w)


def _ada_bwd(cond16, dmod, name):
    nl, _, n = dmod.shape
    d = cond16.shape[1]

    def body(c_ref, g_ref, o_ref):
        o_ref[...] = lax.dot_general(c_ref[...].astype(BF), g_ref[...].astype(BF), (((0,), (0,)), ((), ())),
                                     preferred_element_type=F32)

    return pl.pallas_call(
        body, name=name, grid=(nl,),
        in_specs=[pl.BlockSpec((16, d), lambda l: (0, 0)), pl.BlockSpec((None, 16, n), lambda l: (l, 0, 0))],
        out_specs=pl.BlockSpec((None, d, n), lambda l: (l, 0, 0)),
        out_shape=jax.ShapeDtypeStruct((nl, d, n), F32),
        compiler_params=_params(("parallel",)),
    )(cond16, dmod)


def _row_spec(tm, d):
    return pl.BlockSpec((tm, d), lambda i: (i, 0))


def _vec_spec(d):
    return pl.BlockSpec((1, d), lambda i: (0, 0))


def _norm_mod(x, g, sc, sh, name):
    s, d = x.shape
    tm = _pick(s, 512, 8)

    def body(x_ref, g_ref, sc_ref, sh_ref, o_ref):
        xv = x_ref[...]
        r = lax.rsqrt(jnp.mean(xv * xv, axis=-1, keepdims=True) + EPS)
        y = (xv * r) * g_ref[...]
        o_ref[...] = (y * (1.0 + sc_ref[...]) + sh_ref[...]).astype(BF)

    return pl.pallas_call(
        body, name=name, grid=(s // tm,),
        in_specs=[_row_spec(tm, d), _vec_spec(d), _vec_spec(d), _vec_spec(d)],
        out_specs=_row_spec(tm, d), out_shape=jax.ShapeDtypeStruct((s, d), BF),
        compiler_params=_params(("parallel",)),
    )(x, g, sc, sh)


def _norm_mod_bwd(x, dh, g, sc, dx_in, name):
    s, d = x.shape
    tm = _pick(s, 512, 8)

    def body(x_ref, dh_ref, g_ref, sc_ref, dxin_ref, dx_ref, dsh_ref, dw_ref):
        i = pl.program_id(0)
        xv = x_ref[...]
        dhv = dh_ref[...].astype(F32)
        r = lax.rsqrt(jnp.mean(xv * xv, axis=-1, keepdims=True) + EPS)
        xn = xv * r
        dxn = dhv * (g_ref[...] * (1.0 + sc_ref[...]))
        dx_ref[...] = dxin_ref[...] + r * (dxn - xn * jnp.mean(dxn * xn, axis=-1, keepdims=True))

        @pl.when(i == 0)
        def _():
            dsh_ref[...] = jnp.zeros_like(dsh_ref)
            dw_ref[...] = jnp.zeros_like(dw_ref)

        dsh_ref[...] += jnp.sum(dhv, axis=0, keepdims=True)
        dw_ref[...] += jnp.sum(dhv * xn, axis=0, keepdims=True)

    return pl.pallas_call(
        body, name=name, grid=(s // tm,),
        in_specs=[_row_spec(tm, d), _row_spec(tm, d), _vec_spec(d), _vec_spec(d), _row_spec(tm, d)],
        out_specs=[_row_spec(tm, d), _vec_spec(d), _vec_spec(d)],
        out_shape=[jax.ShapeDtypeStruct((s, d), F32), jax.ShapeDtypeStruct((1, d), F32),
                   jax.ShapeDtypeStruct((1, d), F32)],
        compiler_params=_params(("arbitrary",)),
    )(x, dh, g, sc, dx_in)


def _gate_bwd(dx, y, gate, name):
    s, d = dx.shape
    tm = _pick(s, 512, 8)

    def body(dx_ref, y_ref, g_ref, dy_ref, dg_ref):
        i = pl.program_id(0)
        dxv = dx_ref[...]
        dy_ref[...] = (dxv * g_ref[...]).astype(BF)

        @pl.when(i == 0)
        def _():
            dg_ref[...] = jnp.zeros_like(dg_ref)

        dg_ref[...] += jnp.sum(dxv * y_ref[...].astype(F32), axis=0, keepdims=True)

    return pl.pallas_call(
        body, name=name, grid=(s // tm,),
        in_specs=[_row_spec(tm, d), _row_spec(tm, d), _vec_spec(d)],
        out_specs=[_row_spec(tm, d), _vec_spec(d)],
        out_shape=[jax.ShapeDtypeStruct((s, d), BF), jax.ShapeDtypeStruct((1, d), F32)],
        compiler_params=_params(("arbitrary",)),
    )(dx, y, gate)


def _final_loss(x, tgt, g, name):
    s, d = x.shape
    tm = _pick(s, 512, 8)

    def body(x_ref, t_ref, g_ref, dx_ref, loss_ref, dg_ref):
        i = pl.program_id(0)
        xv = x_ref[...]
        gv = g_ref[...]
        r = lax.rsqrt(jnp.mean(xv * xv, axis=-1, keepdims=True) + EPS)
        xn = xv * r
        err = xn * gv - t_ref[...]
        dy = err * (1.0 / d)
        dxn = dy * gv
        dx_ref[...] = r * (dxn - xn * jnp.mean(dxn * xn, axis=-1, keepdims=True))

        @pl.when(i == 0)
        def _():
            loss_ref[...] = jnp.zeros_like(loss_ref)
            dg_ref[...] = jnp.zeros_like(dg_ref)

        part = 0.5 * jnp.sum(jnp.sum(err * err, axis=-1, keepdims=True) * (1.0 / d), axis=0, keepdims=True)
        loss_ref[...] += jnp.broadcast_to(part, loss_ref.shape)
        dg_ref[...] += jnp.sum(dy * xn, axis=0, keepdims=True)

    return pl.pallas_call(
        body, name=name, grid=(s // tm,),
        in_specs=[_row_spec(tm, d), _row_spec(tm, d), _vec_spec(d)],
        out_specs=[_row_spec(tm, d), pl.BlockSpec((8, LANES), lambda i: (0, 0)), _vec_spec(d)],
        out_shape=[jax.ShapeDtypeStruct((s, d), F32), jax.ShapeDtypeStruct((8, LANES), F32),
                   jax.ShapeDtypeStruct((1, d), F32)],
        compiler_params=_params(("arbitrary",)),
    )(x, tgt, g)


def _pool_counts(tm, gd, row0, w):
    t = lax.broadcasted_iota(jnp.int32, (tm, gd), 0) + row0
    return jnp.minimum(t + 1, w).astype(F32)


def _pool_fwd(u, name):
    s, d = u.shape
    tm = _pick(s, 256, POOL_HALO)
    gd = d // len(POOL_WINDOWS)
    per = tm // POOL_HALO

    def body(prev_ref, cur_ref, o_ref, ext):
        i = pl.program_id(0)
        ext[0:POOL_HALO, :] = jnp.where(i > 0, prev_ref[...], 0.0)
        ext[POOL_HALO:, :] = cur_ref[...]
        for g, w in enumerate(POOL_WINDOWS):
            cols = slice(g * gd, (g + 1) * gd)
            acc = ext[POOL_HALO:POOL_HALO + tm, cols]
            own = acc
            for k in range(1, w):
                acc = acc + ext[POOL_HALO - k:POOL_HALO - k + tm, cols]
            o_ref[:, cols] = (acc / _pool_counts(tm, gd, i * tm, w) - own).astype(BF)

    return pl.pallas_call(
        body, name=name, grid=(s // tm,),
        in_specs=[pl.BlockSpec((POOL_HALO, d), lambda i: (jnp.maximum(i * per - 1, 0), 0)), _row_spec(tm, d)],
        out_specs=_row_spec(tm, d), out_shape=jax.ShapeDtypeStruct((s, d), BF),
        scratch_shapes=[pltpu.VMEM((tm + POOL_HALO, d), F32)],
        compiler_params=_params(("parallel",)),
    )(u, u)


def _pool_bwd(dp, name):
    s, d = dp.shape
    tm = _pick(s, 256, POOL_HALO)
    gd = d // len(POOL_WINDOWS)
    per = tm // POOL_HALO
    nt = s // tm
    last_halo = s // POOL_HALO - 1

    def body(cur_ref, nxt_ref, o_ref, ext):
        i = pl.program_id(0)
        for g, w in enumerate(POOL_WINDOWS):
            cols = slice(g * gd, (g + 1) * gd)
            ext[0:tm, cols] = cur_ref[:, cols].astype(F32) / _pool_counts(tm, gd, i * tm, w)
            nxt = nxt_ref[:, cols].astype(F32) / _pool_counts(POOL_HALO, gd, (i + 1) * tm, w)
            ext[tm:, cols] = jnp.where(i < nt - 1, nxt, 0.0)
        for g, w in enumerate(POOL_WINDOWS):
            cols = slice(g * gd, (g + 1) * gd)
            acc = ext[0:tm, cols]
            for k in range(1, w):
                acc = acc + ext[k:k + tm, cols]
            o_ref[:, cols] = (acc - cur_ref[:, cols].astype(F32)).astype(BF)

    return pl.pallas_call(
        body, name=name, grid=(nt,),
        in_specs=[_row_spec(tm, d), pl.BlockSpec((POOL_HALO, d), lambda i: (jnp.minimum((i + 1) * per, last_halo), 0))],
        out_specs=_row_spec(tm, d), out_shape=jax.ShapeDtypeStruct((s, d), BF),
        scratch_shapes=[pltpu.VMEM((tm + POOL_HALO, d), F32)],
        compiler_params=_params(("parallel",)),
    )(dp, dp)


def _grp_fwd(p, w, scale, name):
    s, d = p.shape
    ng, gd, _ = w.shape
    tm = _pick(s, 1024, 8)

    def body(p_ref, w_ref, s_ref, z_ref, y_ref):
        z = jnp.dot(p_ref[...], w_ref[...].astype(BF), preferred_element_type=F32)
        z_ref[...] = z.astype(BF)
        y_ref[...] = (z * s_ref[...]).astype(BF)

    blk = pl.BlockSpec((tm, gd), lambda i, g: (i, g))
    return pl.pallas_call(
        body, name=name, grid=(s // tm, ng),
        in_specs=[blk, pl.BlockSpec((None, gd, gd), lambda i, g: (g, 0, 0)), pl.BlockSpec((1, gd), lambda i, g: (0, g))],
        out_specs=[blk, blk],
        out_shape=[jax.ShapeDtypeStruct((s, d), BF), jax.ShapeDtypeStruct((s, d), BF)],
        compiler_params=_params(("parallel", "parallel")),
    )(p, w, scale)


def _grp_bwd(dy, z, w, scale, name):
    s, d = dy.shape
    ng, gd, _ = w.shape
    tm = _pick(s, 1024, 8)

    def body(dy_ref, z_ref, w_ref, s_ref, dz_ref, dp_ref, ds_ref):
        i = pl.program_id(1)
        dyv = dy_ref[...].astype(F32)
        dz = (dyv * s_ref[...]).astype(BF)
        dz_ref[...] = dz
        dp_ref[...] = lax.dot_general(dz, w_ref[...].astype(BF), (((1,), (1,)), ((), ())),
                                      preferred_element_type=F32).astype(BF)

        @pl.when(i == 0)
        def _():
            ds_ref[...] = jnp.zeros_like(ds_ref)

        ds_ref[...] += jnp.sum(dyv * z_ref[...].astype(F32), axis=0, keepdims=True)

    blk = pl.BlockSpec((tm, gd), lambda g, i: (i, g))
    vec = pl.BlockSpec((1, gd), lambda g, i: (0, g))
    return pl.pallas_call(
        body, name=name, grid=(ng, s // tm),
        in_specs=[blk, blk, pl.BlockSpec((None, gd, gd), lambda g, i: (g, 0, 0)), vec],
        out_specs=[blk, blk, vec],
        out_shape=[jax.ShapeDtypeStruct((s, d), BF), jax.ShapeDtypeStruct((s, d), BF),
                   jax.ShapeDtypeStruct((1, d), F32)],
        compiler_params=_params(("parallel", "arbitrary")),
    )(dy, z, w, scale)


def _grp_dw(p, dz, ng, name):
    s, d = p.shape
    gd = d // ng
    tk = _pick(s, 1024, 8)

    def body(p_ref, dz_ref, o_ref):
        k = pl.program_id(1)

        @pl.when(k == 0)
        def _():
            o_ref[...] = jnp.zeros_like(o_ref)

        o_ref[...] += lax.dot_general(p_ref[...], dz_ref[...], (((0,), (0,)), ((), ())), preferred_element_type=F32)

    blk = pl.BlockSpec((tk, gd), lambda g, k: (k, g))
    return pl.pallas_call(
        body, name=name, grid=(ng, s // tk),
        in_specs=[blk, blk], out_specs=pl.BlockSpec((None, gd, gd), lambda g, k: (g, 0, 0)),
        out_shape=jax.ShapeDtypeStruct((ng, gd, gd), F32),
        compiler_params=_params(("parallel", "arbitrary")),
    )(p, dz)


def _sigmoid(a):
    return 0.5 * jnp.tanh(0.5 * a) + 0.5


def _ffn_act(up, cw, cb, name):
    _, nq, s, fq = up.shape
    tm = _pick(s, 512, CONV_HALO)
    per = tm // CONV_HALO

    h = CONV_HALO
    rows = _pick(tm, FFN_ROWS, h)

    def body(prev_ref, a_ref, v_ref, w_ref, b_ref, o_ref, ext):
        i = pl.program_id(1)
        ext[0:h, :] = jnp.where(i > 0, prev_ref[...].astype(F32), 0.0)
        ext[h:, :] = a_ref[...].astype(F32)

        def step(c, carry):
            r0 = pl.multiple_of(c * rows, rows)
            e = ext[pl.ds(r0, rows + h), :]
            a2 = (b_ref[...] + e[h - 2:h - 2 + rows] * w_ref[0:1, :] + e[h - 1:h - 1 + rows] * w_ref[1:2, :]
                  + e[h:h + rows] * w_ref[2:3, :])
            vv = v_ref[pl.ds(r0, rows), :].astype(F32)
            o_ref[pl.ds(r0, rows), :] = (a2 * _sigmoid(a2) * vv).astype(BF)
            return carry

        lax.fori_loop(0, tm // rows, step, 0)

    return pl.pallas_call(
        body, name=name, grid=(nq, s // tm),
        in_specs=[pl.BlockSpec((None, None, CONV_HALO, fq), lambda q, i: (0, q, jnp.maximum(i * per - 1, 0), 0)),
                  pl.BlockSpec((None, None, tm, fq), lambda q, i: (0, q, i, 0)),
                  pl.BlockSpec((None, None, tm, fq), lambda q, i: (1, q, i, 0)),
                  pl.BlockSpec((None, CONV_WIDTH, fq), lambda q, i: (q, 0, 0)),
                  pl.BlockSpec((None, 1, fq), lambda q, i: (q, 0, 0))],
        out_specs=pl.BlockSpec((None, tm, fq), lambda q, i: (q, i, 0)),
        out_shape=jax.ShapeDtypeStruct((nq, s, fq), BF),
        scratch_shapes=[pltpu.VMEM((tm + CONV_HALO, fq), F32)],
        compiler_params=_params(("parallel", "parallel")),
    )(up, up, up, cw, cb)


def _ffn_act_bwd(up, dact, cw, cb, name):
    _, nq, s, fq = up.shape
    tm = _pick(s, 512, CONV_HALO)
    per = tm // CONV_HALO
    nt = s // tm
    last_halo = s // CONV_HALO - 1
    h = CONV_HALO
    te = tm + h

    rows = _pick(tm, FFN_ROWS, h)

    def body(ap_ref, a_ref, an_ref, v_ref, vn_ref, d_ref, dn_ref, w_ref, b_ref, dup_ref, dc_ref, ext_a, dap, sums):
        i = pl.program_id(1)
        ext_a[0:h, :] = jnp.where(i > 0, ap_ref[...].astype(F32), 0.0)
        ext_a[h:h + tm, :] = a_ref[...].astype(F32)
        ext_a[h + tm:, :] = an_ref[...].astype(F32)

        def pre_act(e, n):
            return (b_ref[...] + e[h - 2:h - 2 + n] * w_ref[0:1, :] + e[h - 1:h - 1 + n] * w_ref[1:2, :]
                    + e[h:h + n] * w_ref[2:3, :])

        def through_gate(a2, dd, vv):
            sig = _sigmoid(a2)
            return dd * vv * (sig * (1.0 + a2 * (1.0 - sig))), dd * (a2 * sig)

        def step1(c, carry):
            r0 = pl.multiple_of(c * rows, rows)
            a2 = pre_act(ext_a[pl.ds(r0, rows + h), :], rows)
            g, dgate = through_gate(a2, d_ref[pl.ds(r0, rows), :].astype(F32), v_ref[pl.ds(r0, rows), :].astype(F32))
            dap[pl.ds(r0, rows), :] = g
            dup_ref[1, pl.ds(r0, rows), :] = dgate.astype(BF)
            return carry

        lax.fori_loop(0, tm // rows, step1, 0)
        d_nxt = jnp.where(i < nt - 1, dn_ref[...].astype(F32), 0.0)
        g, _ = through_gate(pre_act(ext_a[tm:tm + 2 * h, :], h), d_nxt, vn_ref[...].astype(F32))
        dap[tm:, :] = g
        sums[...] = jnp.zeros_like(sums)

        def fold(t):
            acc = t[0:8]
            for k in range(8, rows, 8):
                acc = acc + t[k:k + 8]
            return acc

        def step2(c, carry):
            r0 = pl.multiple_of(c * rows, rows)
            gch = dap[pl.ds(r0, rows + h), :]
            g0 = gch[0:rows]
            dup_ref[0, pl.ds(r0, rows), :] = (gch[2:2 + rows] * w_ref[0:1, :] + gch[1:1 + rows] * w_ref[1:2, :]
                                              + g0 * w_ref[2:3, :]).astype(BF)
            e = ext_a[pl.ds(r0, rows + h), :]
            sums[0] += fold(g0 * e[h - 2:h - 2 + rows])
            sums[1] += fold(g0 * e[h - 1:h - 1 + rows])
            sums[2] += fold(g0 * e[h:h + rows])
            sums[3] += fold(g0)
            return carry

        lax.fori_loop(0, tm // rows, step2, 0)

        @pl.when(i == 0)
        def _():
            dc_ref[...] = jnp.zeros_like(dc_ref)

        for k in range(4):
            dc_ref[k:k + 1, :] += jnp.sum(sums[k], axis=0, keepdims=True)

    def cur(half):
        return pl.BlockSpec((None, None, tm, fq), lambda q, i: (half, q, i, 0))

    def nxt(half):
        return pl.BlockSpec((None, None, h, fq), lambda q, i: (half, q, jnp.minimum((i + 1) * per, last_halo), 0))

    return pl.pallas_call(
        body, name=name, grid=(nq, nt),
        in_specs=[pl.BlockSpec((None, None, h, fq), lambda q, i: (0, q, jnp.maximum(i * per - 1, 0), 0)),
                  cur(0), nxt(0), cur(1), nxt(1),
                  pl.BlockSpec((None, tm, fq), lambda q, i: (q, i, 0)),
                  pl.BlockSpec((None, h, fq), lambda q, i: (q, jnp.minimum((i + 1) * per, last_halo), 0)),
                  pl.BlockSpec((None, CONV_WIDTH, fq), lambda q, i: (q, 0, 0)),
                  pl.BlockSpec((None, 1, fq), lambda q, i: (q, 0, 0))],
        out_specs=[pl.BlockSpec((2, None, tm, fq), lambda q, i: (0, q, i, 0)),
                   pl.BlockSpec((None, 8, fq), lambda q, i: (q, 0, 0))],
        out_shape=[jax.ShapeDtypeStruct((2, nq, s, fq), BF), jax.ShapeDtypeStruct((nq, 8, fq), F32)],
        scratch_shapes=[pltpu.VMEM((tm + 2 * h, fq), F32), pltpu.VMEM((te, fq), F32), pltpu.VMEM((4, 8, fq), F32)],
        compiler_params=_params(("parallel", "arbitrary")),
    )(up, up, up, up, up, dact, dact, cw, cb)


def _band(blk, n_steps, dil, first):
    qi = lax.broadcasted_iota(jnp.int32, (blk, 2 * blk), 0) + blk
    ki = lax.broadcasted_iota(jnp.int32, (blk, 2 * blk), 1)
    delta = qi - ki
    valid = (delta >= 0) & (delta <= n_steps) & ((ki >= blk) | jnp.logical_not(first))
    return valid, (delta * dil).astype(F32)


def _attn_fwd(q_all, kv, g, slopes, d, name):
    window, dil = BRANCHES[g]
    n_steps = window // dil
    blk = max(ATTN_BLOCK, n_steps)
    s = q_all.shape[0]
    sub = s // dil
    nb = sub // blk
    assert nb * blk == sub
    nh = d // HEAD_DIM
    nbr = len(BRANCHES)
    qv = q_all.reshape(sub, dil * nbr * d)
    kvv = kv.reshape(sub, dil * 2 * nbr * d)
    scale = HEAD_DIM ** -0.5

    def body(q_ref, kp_ref, kc_ref, vp_ref, vc_ref, o_ref, l_ref):
        j = pl.program_id(1)
        valid, dist = _band(blk, n_steps, dil, j == 0)
        qb = q_ref[...]
        kb = jnp.concatenate([kp_ref[...], kc_ref[...]], axis=0)
        vb = jnp.concatenate([vp_ref[...], vc_ref[...]], axis=0)
        for h in range(nh):
            sl = slice(h * HEAD_DIM, (h + 1) * HEAD_DIM)
            sc = lax.dot_general(qb[:, sl], kb[:, sl], (((1,), (1,)), ((), ())), preferred_element_type=F32) * scale
            sc = jnp.where(valid, sc - float(slopes[h]) * dist, NEG)
            m = jnp.max(sc, axis=-1, keepdims=True)
            p = jnp.exp(sc - m)
            den = jnp.sum(p, axis=-1, keepdims=True)
            o = jnp.dot(p.astype(BF), vb[:, sl], preferred_element_type=F32) / den
            o_ref[:, sl] = o.astype(BF)
            l_ref[:, sl] = jnp.broadcast_to(m + jnp.log(den), (blk, HEAD_DIM))

    def spec(col, prev):
        if prev:
            return pl.BlockSpec((blk, d), lambda r, j: (jnp.maximum(j - 1, 0), r * col[0] + col[1]))
        return pl.BlockSpec((blk, d), lambda r, j: (j, r * col[0] + col[1]))

    qcol, kcol, vcol = (nbr, g), (2 * nbr, g), (2 * nbr, nbr + g)
    ospec = pl.BlockSpec((blk, d), lambda r, j: (j, r))
    o, lse = pl.pallas_call(
        body, name=name, grid=(dil, nb),
        in_specs=[spec(qcol, False), spec(kcol, True), spec(kcol, False), spec(vcol, True), spec(vcol, False)],
        out_specs=[ospec, ospec],
        out_shape=[jax.ShapeDtypeStruct((sub, dil * d), BF), jax.ShapeDtypeStruct((sub, dil * d), F32)],
        compiler_params=_params(("parallel", "parallel")),
    )(qv, kvv, kvv, kvv, kvv)
    return o.reshape(s, d), lse.reshape(s, d)


def _attn_combine(os, lses, name):
    s, d = os[0].shape
    tm = _pick(s, 512, 8)
    nbr = len(os)

    def body(*refs):
        o_refs, l_refs = refs[:nbr], refs[nbr:2 * nbr]
        o_ref, lt_ref = refs[2 * nbr], refs[2 * nbr + 1]
        ls = [r[...] for r in l_refs]
        m = ls[0]
        for v in ls[1:]:
            m = jnp.maximum(m, v)
        tot = jnp.exp(ls[0] - m)
        for v in ls[1:]:
            tot = tot + jnp.exp(v - m)
        lt = m + jnp.log(tot)
        acc = jnp.exp(ls[0] - lt) * o_refs[0][...].astype(F32)
        for v, r in zip(ls[1:], o_refs[1:]):
            acc = acc + jnp.exp(v - lt) * r[...].astype(F32)
        o_ref[...] = acc.astype(BF)
        lt_ref[...] = lt

    return pl.pallas_call(
        body, name=name, grid=(s // tm,),
        in_specs=[_row_spec(tm, d)] * (2 * nbr), out_specs=[_row_spec(tm, d), _row_spec(tm, d)],
        out_shape=[jax.ShapeDtypeStruct((s, d), BF), jax.ShapeDtypeStruct((s, d), F32)],
        compiler_params=_params(("parallel",)),
    )(*os, *lses)


def _attn_bwd(q_all, kv, do, o, lt, g, slopes, d, name, dk_in=None, dv_in=None):
    window, dil = BRANCHES[g]
    n_steps = window // dil
    blk = max(ATTN_BLOCK, n_steps)
    s = q_all.shape[0]
    sub = s // dil
    nb = sub // blk
    nh = d // HEAD_DIM
    nbr = len(BRANCHES)
    qv = q_all.reshape(sub, dil * nbr * d)
    kvv = kv.reshape(sub, dil * 2 * nbr * d)
    scale = HEAD_DIM ** -0.5
    acc_in = dk_in is not None

    def body(*refs):
        q_ref, do_ref, o_ref, lt_ref, kp_ref, kc_ref, vp_ref, vc_ref = refs[:8]
        n_in = 10 if acc_in else 8
        dkin_ref, dvin_ref = (refs[8], refs[9]) if acc_in else (None, None)
        dq_ref, dk_ref, dv_ref, keep_k, keep_v, part_k, part_v = refs[n_in:n_in + 7]
        t = pl.program_id(1)

        def emit(prev_k, prev_v):
            if acc_in:
                prev_k = prev_k + dkin_ref[...].astype(F32)
                prev_v = prev_v + dvin_ref[...].astype(F32)
            dk_ref[...] = prev_k.astype(BF)
            dv_ref[...] = prev_v.astype(BF)

        @pl.when(t < nb)
        def _():
            valid, dist = _band(blk, n_steps, dil, t == 0)
            qb = q_ref[...]
            dob = do_ref[...]
            kb = jnp.concatenate([kp_ref[...], kc_ref[...]], axis=0)
            vb = jnp.concatenate([vp_ref[...], vc_ref[...]], axis=0)
            for h in range(nh):
                sl = slice(h * HEAD_DIM, (h + 1) * HEAD_DIM)
                qh, kh, vh, doh = qb[:, sl], kb[:, sl], vb[:, sl], dob[:, sl]
                sc = lax.dot_general(qh, kh, (((1,), (1,)), ((), ())), preferred_element_type=F32) * scale
                sc = sc - float(slopes[h]) * dist
                p = jnp.where(valid, jnp.exp(jnp.minimum(sc - lt_ref[:, sl][:, 0:1], 30.0)), 0.0)
                dlt = jnp.sum(doh.astype(F32) * o_ref[:, sl].astype(F32), axis=-1, keepdims=True)
                dp = lax.dot_general(doh, vh, (((1,), (1,)), ((), ())), preferred_element_type=F32)
                ds = (p * (dp - dlt)).astype(BF)
                dq_ref[:, sl] = (jnp.dot(ds, kh, preferred_element_type=F32) * scale).astype(BF)
                part_k[:, sl] = lax.dot_general(ds, qh, (((0,), (0,)), ((), ())), preferred_element_type=F32) * scale
                part_v[:, sl] = lax.dot_general(p.astype(BF), doh, (((0,), (0,)), ((), ())),
                                                preferred_element_type=F32)

            @pl.when(t > 0)
            def _():
                emit(keep_k[...] + part_k[0:blk, :], keep_v[...] + part_v[0:blk, :])

            keep_k[...] = part_k[blk:, :]
            keep_v[...] = part_v[blk:, :]

        @pl.when(t == nb)
        def _():
            emit(keep_k[...], keep_v[...])

    def qspec(col):
        return pl.BlockSpec((blk, d), lambda r, t: (jnp.minimum(t, nb - 1), r * col[0] + col[1]))

    def kspec(col, prev):
        if prev:
            return pl.BlockSpec((blk, d), lambda r, t: (jnp.maximum(jnp.minimum(t, nb - 1) - 1, 0), r * col[0] + col[1]))
        return qspec(col)

    kout = pl.BlockSpec((blk, d), lambda r, t: (jnp.maximum(t - 1, 0), r))
    qcol, kcol, vcol, one = (nbr, g), (2 * nbr, g), (2 * nbr, nbr + g), (1, 0)
    in_specs = [qspec(qcol), qspec(one), qspec(one), qspec(one),
                kspec(kcol, True), kspec(kcol, False), kspec(vcol, True), kspec(vcol, False)]
    args = [qv, do.reshape(sub, dil * d), o.reshape(sub, dil * d), lt.reshape(sub, dil * d), kvv, kvv, kvv, kvv]
    if acc_in:
        in_specs += [kout, kout]
        args += [dk_in.reshape(sub, dil * d), dv_in.reshape(sub, dil * d)]
    shp = jax.ShapeDtypeStruct((sub, dil * d), BF)
    dq, dk, dv = pl.pallas_call(
        body, name=name, grid=(dil, nb + 1),
        in_specs=in_specs, out_specs=[qspec(one), kout, kout], out_shape=[shp, shp, shp],
        scratch_shapes=[pltpu.VMEM((blk, d), F32), pltpu.VMEM((blk, d), F32),
                        pltpu.VMEM((2 * blk, d), F32), pltpu.VMEM((2 * blk, d), F32)],
        compiler_params=_params(("parallel", "arbitrary")),
    )(*args)
    return dq.reshape(s, d), dk.reshape(s, d), dv.reshape(s, d)


def _adamw(parts_list, w, m, v, name):
    nl, r, c = w.shape
    assert len(parts_list) == nl
    npart = parts_list[0].shape[0]
    tr = _pick(r, 256, 16)
    c1 = 1.0 / (1.0 - ADAM_B1 ** ADAM_STEP)
    c2 = 1.0 / (1.0 - ADAM_B2 ** ADAM_STEP)

    def body(*refs):
        p_refs = refs[:nl]
        w_ref, m_ref, v_ref, g_ref, d_ref, nm_ref, nv_ref = refs[nl:]
        layer = pl.program_id(0)
        for idx in range(nl):
            @pl.when(layer == idx)
            def _(p_ref=p_refs[idx]):
                g = p_ref[0].astype(F32)
                for k in range(1, npart):
                    g = g + p_ref[k].astype(F32)
                nm = ADAM_B1 * m_ref[...] + (1.0 - ADAM_B1) * g
                nv = ADAM_B2 * v_ref[...] + (1.0 - ADAM_B2) * (g * g)
                g_ref[...] = g
                nm_ref[...] = nm
                nv_ref[...] = nv
                d_ref[...] = -ADAM_LR * ((nm * c1) / (jnp.sqrt(nv * c2) + ADAM_EPS) + ADAM_WD * w_ref[...])

    def part_spec(idx):
        return pl.BlockSpec((npart, tr, c), lambda l, i: (0, jnp.where(l == idx, i, 0), 0))

    blk = pl.BlockSpec((None, tr, c), lambda l, i: (l, i, 0))
    shp = jax.ShapeDtypeStruct((nl, r, c), F32)
    return pl.pallas_call(
        body, name=name, grid=(nl, r // tr),
        in_specs=[part_spec(idx) for idx in range(nl)] + [blk, blk, blk],
        out_specs=[blk, blk, blk, blk], out_shape=[shp, shp, shp, shp],
        compiler_params=_params(("parallel", "parallel")),
    )(*parts_list, w, m, v)


def _full_from_slots(slots, shard_shape, axis):
    a = slots.reshape((N_DEV,) + tuple(shard_shape))
    a = jnp.moveaxis(a, 0, axis)
    full = list(shard_shape)
    full[axis] *= N_DEV
    return a.reshape(full)


def kernel(x, c, ada_w, ada_b, norm1_g, norm2_g, pool_w_in, pool_w_grp, pool_scale, pool_w_out, kv_norm_g, kv_ada_w, kv_ada_b, w_kv, attn_w_q, attn_w_o, ffn_w_up, ffn_conv_w, ffn_conv_b, ffn_w_down, final_g, loss_target, m_ada_w, m_ada_b, m_norm1_g, m_norm2_g, m_pool_w_in, m_pool_w_grp, m_pool_scale, m_pool_w_out, m_kv_norm_g, m_kv_ada_w, m_kv_ada_b, m_w_kv, m_attn_w_q, m_attn_w_o, m_ffn_w_up, m_ffn_conv_w, m_ffn_conv_b, m_ffn_w_down, m_final_g, v_ada_w, v_ada_b, v_norm1_g, v_norm2_g, v_pool_w_in, v_pool_w_grp, v_pool_scale, v_pool_w_out, v_kv_norm_g, v_kv_ada_w, v_kv_ada_b, v_w_kv, v_attn_w_q, v_attn_w_o, v_ffn_w_up, v_ffn_conv_w, v_ffn_conv_b, v_ffn_w_down, v_final_g):
    weights = dict(ada_w=ada_w, ada_b=ada_b, norm1_g=norm1_g, norm2_g=norm2_g, pool_w_in=pool_w_in,
                   pool_w_grp=pool_w_grp, pool_scale=pool_scale, pool_w_out=pool_w_out, kv_norm_g=kv_norm_g,
                   kv_ada_w=kv_ada_w, kv_ada_b=kv_ada_b, w_kv=w_kv, attn_w_q=attn_w_q, attn_w_o=attn_w_o,
                   ffn_w_up=ffn_w_up, ffn_conv_w=ffn_conv_w, ffn_conv_b=ffn_conv_b, ffn_w_down=ffn_w_down,
                   final_g=final_g)
    mom1 = dict(ada_w=m_ada_w, ada_b=m_ada_b, norm1_g=m_norm1_g, norm2_g=m_norm2_g, pool_w_in=m_pool_w_in,
                pool_w_grp=m_pool_w_grp, pool_scale=m_pool_scale, pool_w_out=m_pool_w_out, kv_norm_g=m_kv_norm_g,
                kv_ada_w=m_kv_ada_w, kv_ada_b=m_kv_ada_b, w_kv=m_w_kv, attn_w_q=m_attn_w_q, attn_w_o=m_attn_w_o,
                ffn_w_up=m_ffn_w_up, ffn_conv_w=m_ffn_conv_w, ffn_conv_b=m_ffn_conv_b, ffn_w_down=m_ffn_w_down,
                final_g=m_final_g)
    mom2 = dict(ada_w=v_ada_w, ada_b=v_ada_b, norm1_g=v_norm1_g, norm2_g=v_norm2_g, pool_w_in=v_pool_w_in,
                pool_w_grp=v_pool_w_grp, pool_scale=v_pool_scale, pool_w_out=v_pool_w_out, kv_norm_g=v_kv_norm_g,
                kv_ada_w=v_kv_ada_w, kv_ada_b=v_kv_ada_b, w_kv=v_w_kv, attn_w_q=v_attn_w_q, attn_w_o=v_attn_w_o,
                ffn_w_up=v_ffn_w_up, ffn_conv_w=v_ffn_conv_w, ffn_conv_b=v_ffn_conv_b, ffn_w_down=v_ffn_w_down,
                final_g=v_final_g)
    order = list(weights)

    seq, d = x.shape[1], x.shape[2]
    depth = ada_w.shape[0]
    n_pool = pool_w_in.shape[0]
    f = ffn_conv_b.shape[1]
    nbr = len(BRANCHES)
    nh = d // HEAD_DIM
    slopes = _alibi_slopes(nbr * nh).reshape(nbr, nh)
    me = 4 * lax.axis_index("x") + 2 * lax.axis_index("y") + lax.axis_index("c")
    xs = x[0]
    tgt = loss_target[0]

    def start_gathers(after):
        keys, groups = [], []
        for l in range(depth):
            if l < n_pool:
                parts = [("mixer", [pool_w_in[l], pool_w_grp[l].reshape(-1, pool_w_grp.shape[-1]), pool_w_out[l]])]
            else:
                j = l - n_pool
                parts = [("mixer", [attn_w_q[j]] + ([w_kv] if j == 0 else [])), ("out", [attn_w_o[j]])]
            for part, srcs in parts + [("ffn", [ffn_w_up[l], ffn_w_down[l]])]:
                keys.append((l, part))
                groups.append([a.astype(BF) for a in srcs])
        sizes = [len(g) for g in groups]
        *flat, _ = lax.optimization_barrier((*[a for g in groups for a in g], after))
        bounds = np.cumsum([0] + sizes)
        groups = [flat[int(bounds[k]):int(bounds[k + 1])] for k in range(len(sizes))]
        handles, _ = _comm_start(groups, "gather_start", scatter=False)
        return dict(zip(keys, handles))

    cond = c * (1.0 / (1.0 + jnp.exp(-c)))
    small_in = jnp.concatenate([cond.reshape(-1), ffn_conv_w.reshape(-1), pool_scale.reshape(-1)])
    n_small_in = small_in.shape[0]
    gath = _all_gather(_to_rows(small_in), "gather_small").reshape(N_DEV, -1)[:, :n_small_in]
    cond_all = gath[:, :d]
    o1 = d + ffn_conv_w.size
    conv_w_full = _full_from_slots(gath[:, d:o1], ffn_conv_w.shape, 2)
    pool_scale_full = _full_from_slots(gath[:, o1:], pool_scale.shape, 1)
    cond16 = jnp.concatenate([cond_all, jnp.zeros_like(cond_all)], axis=0)

    mod_part = _ada_fwd(cond16, ada_w, "ada_fwd")[:, :N_DEV]
    kv_part = _ada_fwd(cond16, kv_ada_w[None], "kv_ada_fwd")[0, :N_DEV]
    n_mod = depth * mod_part.shape[2] + kv_part.shape[1]
    send = jnp.concatenate([jnp.moveaxis(mod_part, 1, 0).reshape(N_DEV, -1), kv_part], axis=1)
    send_rows = jax.vmap(_to_rows)(send)
    got = _all_to_all(send_rows, "exchange_mod").reshape(N_DEV, -1)[:, :n_mod]
    ncol = mod_part.shape[2]
    mods = []
    for l in range(depth):
        row = got[:, l * ncol:(l + 1) * ncol].reshape(1, -1) + ada_b[l][None]
        mods.append([row[:, k * d:(k + 1) * d] for k in range(6)])
    kv_row = got[:, depth * ncol:].reshape(1, -1) + kv_ada_b[None]
    kv_shift, kv_scale = kv_row[:, :d], kv_row[:, d:]
    gather_handles = start_gathers(got)

    def vec(a):
        return a.reshape(1, -1)

    nq = 4
    fq = f // nq
    ng = len(POOL_WINDOWS)
    cw_slots = jnp.moveaxis(conv_w_full.reshape(depth, CONV_WIDTH, nq, fq), 2, 1)
    cb_slots = ffn_conv_b.reshape(depth, nq, 1, fq)

    saved = []
    xcur = xs
    kvs = None
    hkv = None
    x_kv = None
    w_kv_slots = None
    for l in range(depth):
        sh1, sc1, g1, sh2, sc2, g2 = mods[l]
        st = dict(x=xcur)
        h = _norm_mod(xcur, vec(norm1_g[l]), sc1, sh1, f"norm1_{l}")
        st["h"] = h
        gw = _comm_wait(gather_handles[l, "mixer"], h, f"gather_wait_mixer_{l}")
        if l < n_pool:
            w_in = gw[0].reshape(d, d)
            w_grp = jnp.moveaxis(gw[1].reshape(N_DEV, ng, -1, d // ng), 0, 1).reshape(ng, d // ng, d // ng)
            w_out = gw[2].reshape(d, d)
            u = _mm(h, w_in, f"pool_in_{l}")
            pooled = _pool_fwd(u, f"pool_fwd_{l}")
            z, y = _grp_fwd(pooled, w_grp, vec(pool_scale_full[l]), f"grp_fwd_{l}")
            mix, x1 = _mm(y, w_out, f"pool_out_{l}", gate=g1, resid=xcur, pre_dtype=BF)
            st.update(pooled=pooled, z=z, y=y, w_in=w_in, w_grp=w_grp, w_out=w_out)
        else:
            j = l - n_pool
            w_q_slots = gw[0]
            if j == 0:
                w_kv_slots = gw[1]
                x_kv = xcur
                hkv = _norm_mod(xcur, vec(kv_norm_g), kv_scale, kv_shift, "norm_kv")
                kvs = _mm(hkv, w_kv_slots, "kv_proj", bf="qkn", out_dtype=BF)
            q = _mm(h, w_q_slots, f"q_proj_{l}", bf="qkn", out_dtype=BF)
            outs, lses = [], []
            for g in range(nbr):
                og, lg = _attn_fwd(q, kvs, g, slopes[g], d, f"attn_fwd_{l}_{g}")
                outs.append(og)
                lses.append(lg)
            o, lt = _attn_combine(outs, lses, f"attn_mix_{l}")
            w_o = _comm_wait(gather_handles[l, "out"], o, f"gather_wait_out_{l}")[0].reshape(d, d)
            mix, x1 = _mm(o, w_o, f"attn_out_{l}", gate=g1, resid=xcur, pre_dtype=BF)
            st.update(q=q, o=o, lt=lt, w_q=w_q_slots, w_o=w_o)
        h2 = _norm_mod(x1, vec(norm2_g[l]), sc2, sh2, f"norm2_{l}")
        w_up_slots, w_down = _comm_wait(gather_handles[l, "ffn"], h2, f"gather_wait_ffn_{l}")
        w_down = w_down.reshape(f, d)
        st.update(w_up=w_up_slots, w_down=w_down)
        up = _mm(h2, w_up_slots, f"ffn_up_{l}", bf="qkn", of="qmn", out_dtype=BF).reshape(2, nq, seq, fq)
        act = _ffn_act(up, cw_slots[l], cb_slots[l], f"ffn_act_{l}")
        ffo, x2 = _mm(act, w_down, f"ffn_down_{l}", af="qmk", gate=g2, resid=x1, pre_dtype=BF)
        st.update(mix=mix, x1=x1, h2=h2, up=up, act=act, ffo=ffo)
        saved.append(st)
        xcur = x2

    dx, loss_blk, d_final_g = _final_loss(xcur, tgt, vec(final_g), "final_loss")
    loss = lax.psum(loss_blk[0, 0], ("x", "y", "c"))

    d_mod = [None] * depth
    d_n1 = [None] * depth
    d_n2 = [None] * depth
    d_conv = [None] * depth
    d_pscale = [None] * n_pool
    dk_acc = [None] * nbr
    dv_acc = [None] * nbr
    ffn_handles = [None] * depth
    mixer_handles = [None] * depth
    tok = 0.0
    for l in reversed(range(depth)):
        sh1, sc1, g1, sh2, sc2, g2 = mods[l]
        st = saved[l]
        dffo, dg2 = _gate_bwd(dx, st["ffo"], g2 + tok, f"gate2_bwd_{l}")
        g_down = _mm(st["act"], dffo, f"ffn_down_dw_{l}", af="qkm", out_dtype=BF)
        dact = _mm(dffo, st["w_down"], f"ffn_down_dx_{l}", bf="nk", of="qmn", tn=fq, out_dtype=BF)
        dup, dc = _ffn_act_bwd(st["up"], dact, cw_slots[l], cb_slots[l], f"ffn_act_bwd_{l}")
        dup = dup.reshape(N_DEV, seq, -1)
        d_conv[l] = dc
        g_up = _mm(st["h2"], dup, f"ffn_up_dw_{l}", af="km", bf="qkn", of="qmn", out_dtype=BF)
        dh2 = _mm(dup, st["w_up"], f"ffn_up_dx_{l}", af="qmk", bf="qnk")
        (ffn_handles[l],), tok = _comm_start([[g_up, g_down.reshape(N_DEV, -1, d)]], f"exchange_start_ffn_{l}",
                                             scatter=True)
        dx1, dsh2, dw2 = _norm_mod_bwd(st["x1"], dh2, vec(norm2_g[l]), sc2 + tok, dx, f"norm2_bwd_{l}")
        d_n2[l] = dw2 * (1.0 + sc2)
        dsc2 = dw2 * vec(norm2_g[l])

        dmix, dg1 = _gate_bwd(dx1, st["mix"], g1, f"gate1_bwd_{l}")
        if l < n_pool:
            g_out = _mm(st["y"], dmix, f"pool_out_dw_{l}", af="km", out_dtype=BF)
            dy = _mm(dmix, st["w_out"], f"pool_out_dx_{l}", bf="nk", out_dtype=BF)
            dz, dpool, dps = _grp_bwd(dy, st["z"], st["w_grp"], vec(pool_scale_full[l]), f"grp_bwd_{l}")
            d_pscale[l] = dps
            g_grp = _grp_dw(st["pooled"], dz, ng, f"grp_dw_{l}")
            du = _pool_bwd(dpool, f"pool_bwd_{l}")
            g_in = _mm(st["h"], du, f"pool_in_dw_{l}", af="km", out_dtype=BF)
            dh = _mm(du, st["w_in"], f"pool_in_dx_{l}", bf="nk")
            g_grp_slots = jnp.moveaxis(g_grp.reshape(ng, N_DEV, -1, d // ng), 1, 0).reshape(N_DEV, -1, d // ng)
            send = [g_in.reshape(N_DEV, -1, d), g_grp_slots.astype(BF), g_out.reshape(N_DEV, -1, d)]
        else:
            j = l - n_pool
            g_o = _mm(st["o"], dmix, f"attn_out_dw_{l}", af="km", out_dtype=BF)
            do = _mm(dmix, st["w_o"], f"attn_out_dx_{l}", bf="nk", out_dtype=BF)
            dqs = []
            for g in range(nbr):
                dq_g, dk_g, dv_g = _attn_bwd(st["q"], kvs, do, st["o"], st["lt"], g, slopes[g], d,
                                             f"attn_bwd_{l}_{g}", dk_in=dk_acc[g], dv_in=dv_acc[g])
                dqs.append(dq_g)
                dk_acc[g], dv_acc[g] = dk_g, dv_g
            dq = jnp.concatenate(dqs, axis=1)
            nqc = st["w_q"].shape[2]
            g_q = _mm(st["h"], dq, f"q_proj_dw_{l}", af="km", of="qmn", tn=nqc, out_dtype=BF)
            dh = _mm(dq, st["w_q"], f"q_proj_dx_{l}", bf="qnk")
            send = [g_q, g_o.reshape(N_DEV, -1, d)]
        dx0, dsh1, dw1 = _norm_mod_bwd(st["x"], dh, vec(norm1_g[l]), sc1, dx1, f"norm1_bwd_{l}")
        d_n1[l] = dw1 * (1.0 + sc1)
        dsc1 = dw1 * vec(norm1_g[l])
        d_mod[l] = jnp.concatenate([dsh1, dsc1, dg1, dsh2, dsc2, dg2], axis=1)
        dx = dx0
        if l == n_pool:
            dkv = jnp.concatenate(dk_acc + dv_acc, axis=1)
            nkc = w_kv_slots.shape[2]
            g_kv = _mm(hkv, dkv, "kv_proj_dw", af="km", of="qmn", tn=nkc, out_dtype=BF)
            dhkv = _mm(dkv, w_kv_slots, "kv_proj_dx", bf="qnk")
            dx, dsh_kv, dw_kv = _norm_mod_bwd(x_kv, dhkv, vec(kv_norm_g), kv_scale, dx, "norm_kv_bwd")
            d_kv_norm = dw_kv * (1.0 + kv_scale)
            d_kv_mod = jnp.concatenate([dsh_kv, dw_kv * vec(kv_norm_g)], axis=1)
            send.append(g_kv)
        (mixer_handles[l],), tok = _comm_start([send], f"exchange_start_mixer_{l}", scatter=True)
    grad_x = dx[None]
    d_final_g = d_final_g + tok

    small = [jnp.concatenate(d_mod, axis=1).reshape(-1), d_kv_mod.reshape(-1),
             jnp.concatenate(d_n1, axis=0).reshape(-1), jnp.concatenate(d_n2, axis=0).reshape(-1),
             d_kv_norm.reshape(-1), d_final_g.reshape(-1),
             jnp.stack([dcl[:, 3, :] for dcl in d_conv]).reshape(-1),
             jnp.stack([jnp.moveaxis(dcl[:, 0:CONV_WIDTH, :], 0, 1) for dcl in d_conv]).reshape(-1),
             jnp.concatenate(d_pscale, axis=0).reshape(-1)]
    sizes = [a.shape[0] for a in small]
    small_rows = _to_rows(jnp.concatenate(small))
    small_all = _all_gather(small_rows, "gather_small_grads")
    dmod_all = small_all.reshape(N_DEV, -1)[:, :sizes[0] + sizes[1]]

    dmod16 = jnp.concatenate([dmod_all, jnp.zeros_like(dmod_all)], axis=0)
    dm = dmod16[:, :sizes[0]].reshape(16, depth, N_DEV, ncol)
    dm_mine = lax.dynamic_index_in_dim(dm, me, axis=2, keepdims=False)
    g_ada_w = _ada_bwd(cond16, jnp.moveaxis(dm_mine, 0, 1), "ada_bwd")
    nkv = kv_part.shape[1]
    dkm = dmod16[:, sizes[0]:].reshape(16, N_DEV, nkv)
    dkm_mine = lax.dynamic_index_in_dim(dkm, me, axis=1, keepdims=False)
    g_kv_ada_w = _ada_bwd(cond16, dkm_mine[None], "kv_ada_bwd")

    res = {}

    def update(n, parts_list, shape3):
        w3, m3, v3 = (a[n].reshape(shape3) for a in (weights, mom1, mom2))
        outs = _adamw(parts_list, w3, m3, v3, f"adamw_{n}")
        res[n] = [a.reshape(weights[n].shape) for a in outs]

    update("ada_w", [g_ada_w[l][None] for l in range(depth)], ada_w.shape)
    update("kv_ada_w", [g_kv_ada_w], (1,) + kv_ada_w.shape)
    after = res["ada_w"][0]
    parts_ffn = [_comm_wait(ffn_handles[l], after, f"exchange_wait_ffn_{l}") for l in reversed(range(depth))][::-1]
    parts = [_comm_wait(mixer_handles[l], after, f"exchange_wait_mixer_{l}") for l in reversed(range(depth))][::-1]
    pool_layers, attn_layers = range(n_pool), range(n_pool, depth)
    update("ffn_w_up", [parts_ffn[l][0] for l in range(depth)], ffn_w_up.shape)
    update("ffn_w_down", [parts_ffn[l][1] for l in range(depth)], ffn_w_down.shape)
    update("attn_w_q", [parts[l][0] for l in attn_layers], attn_w_q.shape)
    update("attn_w_o", [parts[l][1] for l in attn_layers], (depth - n_pool, -1, d))
    update("w_kv", [parts[n_pool][2]], (1,) + w_kv.shape)
    update("pool_w_in", [parts[l][0] for l in pool_layers], (n_pool, -1, d))
    update("pool_w_grp", [parts[l][1] for l in pool_layers], (n_pool, -1, d // ng))
    update("pool_w_out", [parts[l][2] for l in pool_layers], (n_pool, -1, d))

    tot = small_all.reshape(N_DEV, -1)
    offs = np.cumsum([0] + sizes)
    seg = {k: (int(offs[i]), int(offs[i + 1])) for i, k in enumerate(
        ["mod", "kv_mod", "n1", "n2", "kv_norm", "final", "conv_b", "conv_w", "pscale"])}

    def rows_of(a, b):
        return tot[:, a:b]

    nf8 = f // N_DEV
    conv_w_parts = lax.dynamic_slice_in_dim(
        rows_of(*seg["conv_w"]).reshape(N_DEV, depth, CONV_WIDTH, N_DEV, nf8), me, 1, axis=3).reshape(N_DEV, -1)
    nd8 = d // N_DEV
    pscale_parts = lax.dynamic_slice_in_dim(
        rows_of(*seg["pscale"]).reshape(N_DEV, n_pool, N_DEV, nd8), me, 1, axis=2).reshape(N_DEV, -1)
    small_names = ["ada_b", "norm1_g", "norm2_g", "pool_scale", "kv_norm_g", "kv_ada_b", "ffn_conv_w",
                   "ffn_conv_b", "final_g"]
    small_parts = [rows_of(*seg["mod"]), rows_of(*seg["n1"]), rows_of(*seg["n2"]), pscale_parts,
                   rows_of(*seg["kv_norm"]), rows_of(*seg["kv_mod"]), conv_w_parts, rows_of(*seg["conv_b"]),
                   rows_of(*seg["final"])]
    sp = jnp.concatenate(small_parts, axis=1)
    n_sp = sp.shape[1]
    sp_rows = jax.vmap(_to_rows)(sp)

    def packed(src):
        return _to_rows(jnp.concatenate([src[n].reshape(-1) for n in small_names]))[None]

    outs = _adamw([sp_rows], packed(weights), packed(mom1), packed(mom2), "adamw_small")
    outs = [a.reshape(-1)[:n_sp] for a in outs]
    off = 0
    for n in small_names:
        size = weights[n].size
        res[n] = [a[off:off + size].reshape(weights[n].shape) for a in outs]
        off += size

    grads = [res[n][0] for n in order]
    deltas = [res[n][1] for n in order]
    new_m = [res[n][2] for n in order]
    new_v = [res[n][3] for n in order]
    return (loss, grad_x, *grads, *deltas, *new_m, *new_v)
```

```python
import math

import numpy as np
import jax
import jax.numpy as jnp
from jax import lax
from jax.experimental import pallas as pl
from jax.experimental.pallas import tpu as pltpu

F32 = jnp.float32
BF = jnp.bfloat16

POOL_WINDOWS = (2, 4, 8, 16)
BRANCHES = ((128, 1), (512, 4), (2048, 16))
HEAD_DIM = 64
ATTN_BLOCK = 128
CONV_WIDTH = 3
EPS = 1e-6
ADAM_LR = 0.001
ADAM_B1 = 0.9
ADAM_B2 = 0.999
ADAM_EPS = 1e-08
ADAM_WD = 0.01
ADAM_STEP = 10

N_DEV = 8
LANES = 128
POOL_HALO = 16
CONV_HALO = 8
FFN_ROWS = 16
VMEM_LIMIT = 48 * 1024 * 1024
MM_TILE = 1024
NEG = -1e30

MESH = pl.DeviceIdType.MESH
ANY = pl.BlockSpec(memory_space=pl.ANY)


def _params(sem=None):
    if sem is None:
        return pltpu.CompilerParams(vmem_limit_bytes=VMEM_LIMIT)
    return pltpu.CompilerParams(dimension_semantics=sem, vmem_limit_bytes=VMEM_LIMIT)


def _pick(dim, pref, mult=LANES):
    if dim <= pref:
        return dim
    t = (pref // mult) * mult
    while t >= mult:
        if dim % t == 0:
            return t
        t -= mult
    return dim


def _alibi_slopes(n):
    def pow2(m):
        start = 2.0 ** (-(2.0 ** -(math.log2(m) - 3)))
        return [start ** (i + 1) for i in range(m)]
    if math.log2(n).is_integer():
        s = pow2(n)
    else:
        c = 2 ** math.floor(math.log2(n))
        s = pow2(c) + pow2(2 * c)[0::2][: n - c]
    s = np.asarray(s, dtype=np.float32)
    return -np.sort(-s)


def _my_place():
    return lax.axis_index("x"), lax.axis_index("y"), lax.axis_index("c")


def _all_gather_many(xs, name):
    n = len(xs)

    def body(*refs):
        x_refs, out_refs = refs[:n], refs[n:2 * n]
        send_sems, recv_sems, local_sems = refs[2 * n:]
        xi, yi, ci = _my_place()
        me, sibling = (xi, yi, ci), (xi, yi, 1 - ci)
        chips = [(1 - xi, yi), (xi, 1 - yi), (1 - xi, 1 - yi)]

        def slot(a, px, py, pc):
            return out_refs[a].at[4 * px + 2 * py + pc]

        def copy(a, k, block, to, src=None):
            return pltpu.make_async_remote_copy(
                src_ref=slot(a, *block) if src is None else src, dst_ref=slot(a, *block),
                send_sem=send_sems.at[7 * a + k], recv_sem=recv_sems.at[7 * a + k],
                device_id=to, device_id_type=MESH)

        mine = [pltpu.make_async_copy(x_refs[a], slot(a, *me), local_sems.at[a]) for a in range(n)]
        for cp in mine:
            cp.start()
        sent = []
        for a in range(n):
            first = [copy(a, 0, me, sibling, src=x_refs[a])]
            first += [copy(a, 1 + j, me, (*chip, ci), src=x_refs[a]) for j, chip in enumerate(chips)]
            for cp in first:
                cp.start()
            sent += first
        for a in range(n):
            for j, chip in enumerate(chips):
                copy(a, 1 + j, (*chip, ci), me).wait_recv()
                fwd = copy(a, 4 + j, (*chip, ci), sibling)
                fwd.start()
                sent.append(fwd)
        for a in range(n):
            copy(a, 0, sibling, me).wait_recv()
            for j, chip in enumerate(chips):
                copy(a, 4 + j, (*chip, 1 - ci), me).wait_recv()
        for cp in sent:
            cp.wait_send()
        for cp in mine:
            cp.wait()

    return pl.pallas_call(
        body, name=name,
        out_shape=[jax.ShapeDtypeStruct((N_DEV,) + x.shape, x.dtype) for x in xs],
        in_specs=[ANY] * n, out_specs=[ANY] * n,
        scratch_shapes=[pltpu.SemaphoreType.DMA((7 * n,)), pltpu.SemaphoreType.DMA((7 * n,)),
                        pltpu.SemaphoreType.DMA((n,))],
    )(*xs)


def _all_gather(x, name):
    return _all_gather_many([x], name)[0]


def _all_to_all_many(xs, name):
    n = len(xs)

    def body(*refs):
        x_refs, out_refs = refs[:n], refs[n:2 * n]
        send_sems, recv_sems, local_sems = refs[2 * n:]
        xi, yi, ci = _my_place()
        me = 4 * xi + 2 * yi + ci
        mine = [pltpu.make_async_copy(x_refs[a].at[me], out_refs[a].at[me], local_sems.at[a]) for a in range(n)]
        for cp in mine:
            cp.start()
        copies = []
        for a in range(n):
            for k in range(1, N_DEV):
                px = 1 - xi if k & 4 else xi
                py = 1 - yi if k & 2 else yi
                pc = 1 - ci if k & 1 else ci
                peer = 4 * px + 2 * py + pc
                cp = pltpu.make_async_remote_copy(
                    src_ref=x_refs[a].at[peer], dst_ref=out_refs[a].at[me],
                    send_sem=send_sems.at[7 * a + k - 1], recv_sem=recv_sems.at[7 * a + k - 1],
                    device_id=(px, py, pc), device_id_type=MESH)
                cp.start()
                copies.append(cp)
        for cp in copies:
            cp.wait()
        for cp in mine:
            cp.wait()

    return pl.pallas_call(
        body, name=name,
        out_shape=[jax.ShapeDtypeStruct(x.shape, x.dtype) for x in xs],
        in_specs=[ANY] * n, out_specs=[ANY] * n,
        scratch_shapes=[pltpu.SemaphoreType.DMA((7 * n,)), pltpu.SemaphoreType.DMA((7 * n,)),
                        pltpu.SemaphoreType.DMA((n,))],
    )(*xs)


def _all_to_all(x, name):
    return _all_to_all_many([x], name)[0]


HBM = pl.BlockSpec(memory_space=pltpu.HBM)
SEM = pl.BlockSpec(memory_space=pltpu.SEMAPHORE)
EFFECT = pltpu.SideEffectType.DATAFLOW_SIDE_EFFECTING


def _peer(k, xi, yi, ci):
    px = 1 - xi if k & 4 else xi
    py = 1 - yi if k & 2 else yi
    pc = 1 - ci if k & 1 else ci
    return (px, py, pc), 4 * px + 2 * py + pc


def _split_copies(x_refs, land_refs, send_sem, recv_sem, scatter):
    xi, yi, ci = _my_place()
    me = 4 * xi + 2 * yi + ci
    copies = []
    for p, (x_ref, land_ref) in enumerate(zip(x_refs, land_refs)):
        for k in range(1, N_DEV):
            place, peer = _peer(k, xi, yi, ci)
            copies.append(pltpu.make_async_remote_copy(
                src_ref=x_ref.at[peer] if scatter else x_ref, dst_ref=land_ref.at[me],
                send_sem=send_sem.at[7 * p + k - 1], recv_sem=recv_sem.at[7 * p + k - 1],
                device_id=place, device_id_type=MESH))
    return copies


def _comm_start(groups, name, scatter, after=None):
    sizes = [len(g) for g in groups]
    xs = [x for g in groups for x in g]
    n, ng = len(xs), len(groups)
    lands = [lax.empty(x.shape if scatter else (N_DEV,) + x.shape, x.dtype) for x in xs]
    starts = np.cumsum([0] + sizes)
    n_in = 2 * n + (after is not None)

    def body(*refs):
        x_refs, land_refs = refs[:n], refs[n:2 * n]
        send_sems, recv_sems = refs[n_in:n_in + ng], refs[n_in + ng:n_in + 2 * ng]
        token = refs[n_in + 2 * ng + 2 * n]
        for gi in range(ng):
            lo, hi = int(starts[gi]), int(starts[gi + 1])
            for cp in _split_copies(x_refs[lo:hi], land_refs[lo:hi], send_sems[gi], recv_sems[gi], scatter):
                cp.start()
        token[...] = jnp.zeros_like(token)

    sem_shapes = [pltpu.SemaphoreType.DMA((7 * m,)) for m in sizes]
    thru = [pltpu.HBM(a.shape, a.dtype) for a in xs + lands]
    res = pl.pallas_call(
        body, name=name,
        out_shape=sem_shapes + sem_shapes + thru + [jax.ShapeDtypeStruct((8, LANES), F32)],
        in_specs=[HBM] * n_in,
        out_specs=[SEM] * (2 * ng) + [HBM] * (2 * n) + [pl.BlockSpec(memory_space=pltpu.VMEM)],
        input_output_aliases={i: 2 * ng + i for i in range(2 * n)},
        compiler_params=pltpu.CompilerParams(has_side_effects=EFFECT),
    )(*[pltpu.with_memory_space_constraint(a, pltpu.HBM) for a in xs + lands + ([] if after is None else [after])])
    send_sems, recv_sems = res[:ng], res[ng:2 * ng]
    x_thru, land_thru = res[2 * ng:2 * ng + n], res[2 * ng + n:2 * ng + 2 * n]
    handles = []
    for gi in range(ng):
        lo, hi = int(starts[gi]), int(starts[gi + 1])
        handles.append((send_sems[gi], recv_sems[gi], list(x_thru[lo:hi]), list(land_thru[lo:hi]), scatter))
    return handles, res[-1][0, 0]


def _comm_wait(handle, after, name):
    send_sem, recv_sem, x_thru, land_thru, scatter = handle
    m = len(x_thru)

    blocks = [a.shape[1:] if scatter else a.shape for a in x_thru]

    def body(*refs):
        x_refs, land_refs = refs[:m], refs[m:2 * m]
        local_sems, stage = refs[4 * m + 3], refs[4 * m + 4:]
        xi, yi, ci = _my_place()
        me = 4 * xi + 2 * yi + ci
        load = [pltpu.make_async_copy(x_refs[p].at[me] if scatter else x_refs[p], stage[p], local_sems.at[2 * p])
                for p in range(m)]
        store = [pltpu.make_async_copy(stage[p], land_refs[p].at[me], local_sems.at[2 * p + 1]) for p in range(m)]
        for cp in load:
            cp.start()
        for p in range(m):
            load[p].wait()
            store[p].start()
        for cp in _split_copies(x_refs, land_refs, refs[2 * m], refs[2 * m + 1], scatter):
            cp.wait_send()
            cp.wait_recv()
        for cp in store:
            cp.wait()

    res = pl.pallas_call(
        body, name=name,
        out_shape=[pltpu.HBM(a.shape, a.dtype) for a in x_thru + land_thru],
        in_specs=[HBM] * (2 * m) + [SEM, SEM, ANY], out_specs=[HBM] * (2 * m),
        input_output_aliases={i: i for i in range(2 * m)},
        scratch_shapes=[pltpu.SemaphoreType.DMA((2 * m,))] + [pltpu.VMEM(b, a.dtype) for b, a in zip(blocks, x_thru)],
        compiler_params=pltpu.CompilerParams(has_side_effects=EFFECT),
    )(*x_thru, *land_thru, send_sem, recv_sem, after)
    return list(res[m:])


def _to_rows(vec):
    n = vec.shape[0]
    unit = 8 * LANES
    pad = (-n) % unit
    if pad:
        vec = jnp.concatenate([vec, jnp.zeros((pad,), vec.dtype)])
    return vec.reshape(-1, LANES)


def _mm(a, b, name, *, af="mk", bf="kn", of="mn", out_dtype=F32, tm=None, tn=None, tk=None,
        gate=None, resid=None, pre_dtype=None):
    if af == "mk":
        m, kk = a.shape
    elif af == "km":
        kk, m = a.shape
    elif af == "qmk":
        qa, m, tk = a.shape
        kk = qa * tk
    else:
        qa, kk, tm = a.shape
        m = qa * tm
    if bf == "kn":
        k2, n = b.shape
    elif bf == "nk":
        n, k2 = b.shape
    elif bf == "qkn":
        qb, k2, tn = b.shape
        n = qb * tn
    else:
        qb, n, tkb = b.shape
        k2 = qb * tkb
        assert af != "qmk" or tkb == tk
        tk = tkb
    assert kk == k2, (name, a.shape, b.shape, af, bf)
    tm = _pick(m, MM_TILE) if tm is None else tm
    tn = _pick(n, MM_TILE) if tn is None else tn
    tk = _pick(kk, MM_TILE) if tk is None else tk
    assert m % tm == 0 and n % tn == 0 and kk % tk == 0, (name, m, n, kk, tm, tn, tk)
    nk = kk // tk
    a_spec = {"mk": pl.BlockSpec((tm, tk), lambda i, j, k: (i, k)),
              "km": pl.BlockSpec((tk, tm), lambda i, j, k: (k, i)),
              "qmk": pl.BlockSpec((None, tm, tk), lambda i, j, k: (k, i, 0)),
              "qkm": pl.BlockSpec((None, tk, tm), lambda i, j, k: (i, k, 0))}[af]
    b_spec = {"kn": pl.BlockSpec((tk, tn), lambda i, j, k: (k, j)),
              "nk": pl.BlockSpec((tn, tk), lambda i, j, k: (j, k)),
              "qkn": pl.BlockSpec((None, tk, tn), lambda i, j, k: (j, k, 0)),
              "qnk": pl.BlockSpec((None, tn, tk), lambda i, j, k: (k, j, 0))}[bf]
    dims = (((1 if af in ("mk", "qmk") else 0,), (0 if bf in ("kn", "qkn") else 1,)), ((), ()))
    in_specs, args = [a_spec, b_spec], [a, b]
    if gate is not None:
        assert of == "mn"
        in_specs.append(pl.BlockSpec((1, tn), lambda i, j, k: (0, j)))
        args.append(gate)
    if resid is not None:
        assert of == "mn"
        in_specs.append(pl.BlockSpec((tm, tn), lambda i, j, k: (i, j)))
        args.append(resid)
    if of == "mn":
        o_spec, o_shape = pl.BlockSpec((tm, tn), lambda i, j, k: (i, j)), (m, n)
    else:
        o_spec, o_shape = pl.BlockSpec((None, tm, tn), lambda i, j, k: (j, i, 0)), (n // tn, m, tn)
    out_shape, out_specs = [jax.ShapeDtypeStruct(o_shape, out_dtype)], [o_spec]
    if pre_dtype is not None:
        out_shape.insert(0, jax.ShapeDtypeStruct(o_shape, pre_dtype))
        out_specs.insert(0, o_spec)
    n_in = len(args)
    n_out = len(out_shape)

    def body(*refs):
        a_ref, b_ref = refs[0], refs[1]
        extra = list(refs[2:n_in])
        outs = refs[n_in:n_in + n_out]
        gate_ref = extra.pop(0) if gate is not None else None
        resid_ref = extra.pop(0) if resid is not None else None

        def product():
            return lax.dot_general(a_ref[...].astype(BF), b_ref[...].astype(BF), dims, preferred_element_type=F32)

        def finish(r):
            if pre_dtype is not None:
                outs[0][...] = r.astype(pre_dtype)
            if gate_ref is not None:
                r = r * gate_ref[...]
            if resid_ref is not None:
                r = resid_ref[...] + r
            outs[-1][...] = r.astype(out_dtype)

        if nk == 1:
            finish(product())
        else:
            acc = refs[n_in + n_out]
            k = pl.program_id(2)

            @pl.when(k == 0)
            def _():
                acc[...] = product()

            @pl.when(k > 0)
            def _():
                acc[...] += product()

            @pl.when(k == nk - 1)
            def _():
                finish(acc[...])

    res = pl.pallas_call(
        body, name=name, grid=(m // tm, n // tn, nk),
        in_specs=in_specs, out_specs=out_specs, out_shape=out_shape,
        scratch_shapes=[pltpu.VMEM((tm, tn), F32)] if nk > 1 else [],
        compiler_params=_params(("parallel", "parallel", "arbitrary")),
    )(*args)
    return res if pre_dtype is not None else res[0]


def _ada_fwd(cond16, w, name):
    nl, d, n = w.shape

    def body(c_ref, w_ref, o_ref):
        o_ref[...] = jnp.dot(c_ref[...].astype(BF), w_ref[...].astype(BF), preferred_element_type=F32)

    return pl.pallas_call(
        body, name=name, grid=(nl,),
        in_specs=[pl.BlockSpec((16, d), lambda l: (0, 0)), pl.BlockSpec((None, d, n), lambda l: (l, 0, 0))],
        out_specs=pl.BlockSpec((None, 16, n), lambda l: (l, 0, 0)),
        out_shape=jax.ShapeDtypeStruct((nl, 16, n), F32),
        compiler_params=_params(("parallel",)),
    )(cond16, w)


def _ada_bwd(cond16, dmod, name):
    nl, _, n = dmod.shape
    d = cond16.shape[1]

    def body(c_ref, g_ref, o_ref):
        o_ref[...] = lax.dot_general(c_ref[...].astype(BF), g_ref[...].astype(BF), (((0,), (0,)), ((), ())),
                                     preferred_element_type=F32)

    return pl.pallas_call(
        body, name=name, grid=(nl,),
        in_specs=[pl.BlockSpec((16, d), lambda l: (0, 0)), pl.BlockSpec((None, 16, n), lambda l: (l, 0, 0))],
        out_specs=pl.BlockSpec((None, d, n), lambda l: (l, 0, 0)),
        out_shape=jax.ShapeDtypeStruct((nl, d, n), F32),
        compiler_params=_params(("parallel",)),
    )(cond16, dmod)


def _row_spec(tm, d):
    return pl.BlockSpec((tm, d), lambda i: (i, 0))


def _vec_spec(d):
    return pl.BlockSpec((1, d), lambda i: (0, 0))


def _norm_mod(x, g, sc, sh, name):
    s, d = x.shape
    tm = _pick(s, 512, 8)

    def body(x_ref, g_ref, sc_ref, sh_ref, o_ref):
        xv = x_ref[...]
        r = lax.rsqrt(jnp.mean(xv * xv, axis=-1, keepdims=True) + EPS)
        y = (xv * r) * g_ref[...]
        o_ref[...] = (y * (1.0 + sc_ref[...]) + sh_ref[...]).astype(BF)

    return pl.pallas_call(
        body, name=name, grid=(s // tm,),
        in_specs=[_row_spec(tm, d), _vec_spec(d), _vec_spec(d), _vec_spec(d)],
        out_specs=_row_spec(tm, d), out_shape=jax.ShapeDtypeStruct((s, d), BF),
        compiler_params=_params(("parallel",)),
    )(x, g, sc, sh)


def _norm_mod_bwd(x, dh, g, sc, dx_in, name):
    s, d = x.shape
    tm = _pick(s, 512, 8)

    def body(x_ref, dh_ref, g_ref, sc_ref, dxin_ref, dx_ref, dsh_ref, dw_ref):
        i = pl.program_id(0)
        xv = x_ref[...]
        dhv = dh_ref[...].astype(F32)
        r = lax.rsqrt(jnp.mean(xv * xv, axis=-1, keepdims=True) + EPS)
        xn = xv * r
        dxn = dhv * (g_ref[...] * (1.0 + sc_ref[...]))
        dx_ref[...] = dxin_ref[...] + r * (dxn - xn * jnp.mean(dxn * xn, axis=-1, keepdims=True))

        @pl.when(i == 0)
        def _():
            dsh_ref[...] = jnp.zeros_like(dsh_ref)
            dw_ref[...] = jnp.zeros_like(dw_ref)

        dsh_ref[...] += jnp.sum(dhv, axis=0, keepdims=True)
        dw_ref[...] += jnp.sum(dhv * xn, axis=0, keepdims=True)

    return pl.pallas_call(
        body, name=name, grid=(s // tm,),
        in_specs=[_row_spec(tm, d), _row_spec(tm, d), _vec_spec(d), _vec_spec(d), _row_spec(tm, d)],
        out_specs=[_row_spec(tm, d), _vec_spec(d), _vec_spec(d)],
        out_shape=[jax.ShapeDtypeStruct((s, d), F32), jax.ShapeDtypeStruct((1, d), F32),
                   jax.ShapeDtypeStruct((1, d), F32)],
        compiler_params=_params(("arbitrary",)),
    )(x, dh, g, sc, dx_in)


def _gate_bwd(dx, y, gate, name):
    s, d = dx.shape
    tm = _pick(s, 512, 8)

    def body(dx_ref, y_ref, g_ref, dy_ref, dg_ref):
        i = pl.program_id(0)
        dxv = dx_ref[...]
        dy_ref[...] = (dxv * g_ref[...]).astype(BF)

        @pl.when(i == 0)
        def _():
            dg_ref[...] = jnp.zeros_like(dg_ref)

        dg_ref[...] += jnp.sum(dxv * y_ref[...].astype(F32), axis=0, keepdims=True)

    return pl.pallas_call(
        body, name=name, grid=(s // tm,),
        in_specs=[_row_spec(tm, d), _row_spec(tm, d), _vec_spec(d)],
        out_specs=[_row_spec(tm, d), _vec_spec(d)],
        out_shape=[jax.ShapeDtypeStruct((s, d), BF), jax.ShapeDtypeStruct((1, d), F32)],
        compiler_params=_params(("arbitrary",)),
    )(dx, y, gate)


def _final_loss(x, tgt, g, name):
    s, d = x.shape
    tm = _pick(s, 512, 8)

    def body(x_ref, t_ref, g_ref, dx_ref, loss_ref, dg_ref):
        i = pl.program_id(0)
        xv = x_ref[...]
        gv = g_ref[...]
        r = lax.rsqrt(jnp.mean(xv * xv, axis=-1, keepdims=True) + EPS)
        xn = xv * r
        err = xn * gv - t_ref[...]
        dy = err * (1.0 / d)
        dxn = dy * gv
        dx_ref[...] = r * (dxn - xn * jnp.mean(dxn * xn, axis=-1, keepdims=True))

        @pl.when(i == 0)
        def _():
            loss_ref[...] = jnp.zeros_like(loss_ref)
            dg_ref[...] = jnp.zeros_like(dg_ref)

        part = 0.5 * jnp.sum(jnp.sum(err * err, axis=-1, keepdims=True) * (1.0 / d), axis=0, keepdims=True)
        loss_ref[...] += jnp.broadcast_to(part, loss_ref.shape)
        dg_ref[...] += jnp.sum(dy * xn, axis=0, keepdims=True)

    return pl.pallas_call(
        body, name=name, grid=(s // tm,),
        in_specs=[_row_spec(tm, d), _row_spec(tm, d), _vec_spec(d)],
        out_specs=[_row_spec(tm, d), pl.BlockSpec((8, LANES), lambda i: (0, 0)), _vec_spec(d)],
        out_shape=[jax.ShapeDtypeStruct((s, d), F32), jax.ShapeDtypeStruct((8, LANES), F32),
                   jax.ShapeDtypeStruct((1, d), F32)],
        compiler_params=_params(("arbitrary",)),
    )(x, tgt, g)


def _pool_counts(tm, gd, row0, w):
    t = lax.broadcasted_iota(jnp.int32, (tm, gd), 0) + row0
    return jnp.minimum(t + 1, w).astype(F32)


def _pool_fwd(u, name):
    s, d = u.shape
    tm = _pick(s, 256, POOL_HALO)
    gd = d // len(POOL_WINDOWS)
    per = tm // POOL_HALO

    def body(prev_ref, cur_ref, o_ref, ext):
        i = pl.program_id(0)
        ext[0:POOL_HALO, :] = jnp.where(i > 0, prev_ref[...], 0.0)
        ext[POOL_HALO:, :] = cur_ref[...]
        for g, w in enumerate(POOL_WINDOWS):
            cols = slice(g * gd, (g + 1) * gd)
            acc = ext[POOL_HALO:POOL_HALO + tm, cols]
            own = acc
            for k in range(1, w):
                acc = acc + ext[POOL_HALO - k:POOL_HALO - k + tm, cols]
            o_ref[:, cols] = (acc / _pool_counts(tm, gd, i * tm, w) - own).astype(BF)

    return pl.pallas_call(
        body, name=name, grid=(s // tm,),
        in_specs=[pl.BlockSpec((POOL_HALO, d), lambda i: (jnp.maximum(i * per - 1, 0), 0)), _row_spec(tm, d)],
        out_specs=_row_spec(tm, d), out_shape=jax.ShapeDtypeStruct((s, d), BF),
        scratch_shapes=[pltpu.VMEM((tm + POOL_HALO, d), F32)],
        compiler_params=_params(("parallel",)),
    )(u, u)


def _pool_bwd(dp, name):
    s, d = dp.shape
    tm = _pick(s, 256, POOL_HALO)
    gd = d // len(POOL_WINDOWS)
    per = tm // POOL_HALO
    nt = s // tm
    last_halo = s // POOL_HALO - 1

    def body(cur_ref, nxt_ref, o_ref, ext):
        i = pl.program_id(0)
        for g, w in enumerate(POOL_WINDOWS):
            cols = slice(g * gd, (g + 1) * gd)
            ext[0:tm, cols] = cur_ref[:, cols].astype(F32) / _pool_counts(tm, gd, i * tm, w)
            nxt = nxt_ref[:, cols].astype(F32) / _pool_counts(POOL_HALO, gd, (i + 1) * tm, w)
            ext[tm:, cols] = jnp.where(i < nt - 1, nxt, 0.0)
        for g, w in enumerate(POOL_WINDOWS):
            cols = slice(g * gd, (g + 1) * gd)
            acc = ext[0:tm, cols]
            for k in range(1, w):
                acc = acc + ext[k:k + tm, cols]
            o_ref[:, cols] = (acc - cur_ref[:, cols].astype(F32)).astype(BF)

    return pl.pallas_call(
        body, name=name, grid=(nt,),
        in_specs=[_row_spec(tm, d), pl.BlockSpec((POOL_HALO, d), lambda i: (jnp.minimum((i + 1) * per, last_halo), 0))],
        out_specs=_row_spec(tm, d), out_shape=jax.ShapeDtypeStruct((s, d), BF),
        scratch_shapes=[pltpu.VMEM((tm + POOL_HALO, d), F32)],
        compiler_params=_params(("parallel",)),
    )(dp, dp)


def _grp_fwd(p, w, scale, name):
    s, d = p.shape
    ng, gd, _ = w.shape
    tm = _pick(s, 1024, 8)

    def body(p_ref, w_ref, s_ref, z_ref, y_ref):
        z = jnp.dot(p_ref[...], w_ref[...].astype(BF), preferred_element_type=F32)
        z_ref[...] = z.astype(BF)
        y_ref[...] = (z * s_ref[...]).astype(BF)

    blk = pl.BlockSpec((tm, gd), lambda i, g: (i, g))
    return pl.pallas_call(
        body, name=name, grid=(s // tm, ng),
        in_specs=[blk, pl.BlockSpec((None, gd, gd), lambda i, g: (g, 0, 0)), pl.BlockSpec((1, gd), lambda i, g: (0, g))],
        out_specs=[blk, blk],
        out_shape=[jax.ShapeDtypeStruct((s, d), BF), jax.ShapeDtypeStruct((s, d), BF)],
        compiler_params=_params(("parallel", "parallel")),
    )(p, w, scale)


def _grp_bwd(dy, z, w, scale, name):
    s, d = dy.shape
    ng, gd, _ = w.shape
    tm = _pick(s, 1024, 8)

    def body(dy_ref, z_ref, w_ref, s_ref, dz_ref, dp_ref, ds_ref):
        i = pl.program_id(1)
        dyv = dy_ref[...].astype(F32)
        dz = (dyv * s_ref[...]).astype(BF)
        dz_ref[...] = dz
        dp_ref[...] = lax.dot_general(dz, w_ref[...].astype(BF), (((1,), (1,)), ((), ())),
                                      preferred_element_type=F32).astype(BF)

        @pl.when(i == 0)
        def _():
            ds_ref[...] = jnp.zeros_like(ds_ref)

        ds_ref[...] += jnp.sum(dyv * z_ref[...].astype(F32), axis=0, keepdims=True)

    blk = pl.BlockSpec((tm, gd), lambda g, i: (i, g))
    vec = pl.BlockSpec((1, gd), lambda g, i: (0, g))
    return pl.pallas_call(
        body, name=name, grid=(ng, s // tm),
        in_specs=[blk, blk, pl.BlockSpec((None, gd, gd), lambda g, i: (g, 0, 0)), vec],
        out_specs=[blk, blk, vec],
        out_shape=[jax.ShapeDtypeStruct((s, d), BF), jax.ShapeDtypeStruct((s, d), BF),
                   jax.ShapeDtypeStruct((1, d), F32)],
        compiler_params=_params(("parallel", "arbitrary")),
    )(dy, z, w, scale)


def _grp_dw(p, dz, ng, name):
    s, d = p.shape
    gd = d // ng
    tk = _pick(s, 1024, 8)

    def body(p_ref, dz_ref, o_ref):
        k = pl.program_id(1)

        @pl.when(k == 0)
        def _():
            o_ref[...] = jnp.zeros_like(o_ref)

        o_ref[...] += lax.dot_general(p_ref[...], dz_ref[...], (((0,), (0,)), ((), ())), preferred_element_type=F32)

    blk = pl.BlockSpec((tk, gd), lambda g, k: (k, g))
    return pl.pallas_call(
        body, name=name, grid=(ng, s // tk),
        in_specs=[blk, blk], out_specs=pl.BlockSpec((None, gd, gd), lambda g, k: (g, 0, 0)),
        out_shape=jax.ShapeDtypeStruct((ng, gd, gd), F32),
        compiler_params=_params(("parallel", "arbitrary")),
    )(p, dz)


def _sigmoid(a):
    return 0.5 * jnp.tanh(0.5 * a) + 0.5


def _ffn_act(up, cw, cb, name):
    _, nq, s, fq = up.shape
    tm = _pick(s, 512, CONV_HALO)
    per = tm // CONV_HALO

    h = CONV_HALO
    rows = _pick(tm, FFN_ROWS, h)

    def body(prev_ref, a_ref, v_ref, w_ref, b_ref, o_ref, ext):
        i = pl.program_id(1)
        ext[0:h, :] = jnp.where(i > 0, prev_ref[...].astype(F32), 0.0)
        ext[h:, :] = a_ref[...].astype(F32)

        def step(c, carry):
            r0 = pl.multiple_of(c * rows, rows)
            e = ext[pl.ds(r0, rows + h), :]
            a2 = (b_ref[...] + e[h - 2:h - 2 + rows] * w_ref[0:1, :] + e[h - 1:h - 1 + rows] * w_ref[1:2, :]
                  + e[h:h + rows] * w_ref[2:3, :])
            vv = v_ref[pl.ds(r0, rows), :].astype(F32)
            o_ref[pl.ds(r0, rows), :] = (a2 * _sigmoid(a2) * vv).astype(BF)
            return carry

        lax.fori_loop(0, tm // rows, step, 0)

    return pl.pallas_call(
        body, name=name, grid=(nq, s // tm),
        in_specs=[pl.BlockSpec((None, None, CONV_HALO, fq), lambda q, i: (0, q, jnp.maximum(i * per - 1, 0), 0)),
                  pl.BlockSpec((None, None, tm, fq), lambda q, i: (0, q, i, 0)),
                  pl.BlockSpec((None, None, tm, fq), lambda q, i: (1, q, i, 0)),
                  pl.BlockSpec((None, CONV_WIDTH, fq), lambda q, i: (q, 0, 0)),
                  pl.BlockSpec((None, 1, fq), lambda q, i: (q, 0, 0))],
        out_specs=pl.BlockSpec((None, tm, fq), lambda q, i: (q, i, 0)),
        out_shape=jax.ShapeDtypeStruct((nq, s, fq), BF),
        scratch_shapes=[pltpu.VMEM((tm + CONV_HALO, fq), F32)],
        compiler_params=_params(("parallel", "parallel")),
    )(up, up, up, cw, cb)


def _ffn_act_bwd(up, dact, cw, cb, name):
    _, nq, s, fq = up.shape
    tm = _pick(s, 512, CONV_HALO)
    per = tm // CONV_HALO
    nt = s // tm
    last_halo = s // CONV_HALO - 1
    h = CONV_HALO
    te = tm + h

    rows = _pick(tm, FFN_ROWS, h)

    def body(ap_ref, a_ref, an_ref, v_ref, vn_ref, d_ref, dn_ref, w_ref, b_ref, dup_ref, dc_ref, ext_a, dap, sums):
        i = pl.program_id(1)
        ext_a[0:h, :] = jnp.where(i > 0, ap_ref[...].astype(F32), 0.0)
        ext_a[h:h + tm, :] = a_ref[...].astype(F32)
        ext_a[h + tm:, :] = an_ref[...].astype(F32)

        def pre_act(e, n):
            return (b_ref[...] + e[h - 2:h - 2 + n] * w_ref[0:1, :] + e[h - 1:h - 1 + n] * w_ref[1:2, :]
                    + e[h:h + n] * w_ref[2:3, :])

        def through_gate(a2, dd, vv):
            sig = _sigmoid(a2)
            return dd * vv * (sig * (1.0 + a2 * (1.0 - sig))), dd * (a2 * sig)

        def step1(c, carry):
            r0 = pl.multiple_of(c * rows, rows)
            a2 = pre_act(ext_a[pl.ds(r0, rows + h), :], rows)
            g, dgate = through_gate(a2, d_ref[pl.ds(r0, rows), :].astype(F32), v_ref[pl.ds(r0, rows), :].astype(F32))
            dap[pl.ds(r0, rows), :] = g
            dup_ref[1, pl.ds(r0, rows), :] = dgate.astype(BF)
            return carry

        lax.fori_loop(0, tm // rows, step1, 0)
        d_nxt = jnp.where(i < nt - 1, dn_ref[...].astype(F32), 0.0)
        g, _ = through_gate(pre_act(ext_a[tm:tm + 2 * h, :], h), d_nxt, vn_ref[...].astype(F32))
        dap[tm:, :] = g
        sums[...] = jnp.zeros_like(sums)

        def fold(t):
            acc = t[0:8]
            for k in range(8, rows, 8):
                acc = acc + t[k:k + 8]
            return acc

        def step2(c, carry):
            r0 = pl.multiple_of(c * rows, rows)
            gch = dap[pl.ds(r0, rows + h), :]
            g0 = gch[0:rows]
            dup_ref[0, pl.ds(r0, rows), :] = (gch[2:2 + rows] * w_ref[0:1, :] + gch[1:1 + rows] * w_ref[1:2, :]
                                              + g0 * w_ref[2:3, :]).astype(BF)
            e = ext_a[pl.ds(r0, rows + h), :]
            sums[0] += fold(g0 * e[h - 2:h - 2 + rows])
            sums[1] += fold(g0 * e[h - 1:h - 1 + rows])
            sums[2] += fold(g0 * e[h:h + rows])
            sums[3] += fold(g0)
            return carry

        lax.fori_loop(0, tm // rows, step2, 0)

        @pl.when(i == 0)
        def _():
            dc_ref[...] = jnp.zeros_like(dc_ref)

        for k in range(4):
            dc_ref[k:k + 1, :] += jnp.sum(sums[k], axis=0, keepdims=True)

    def cur(half):
        return pl.BlockSpec((None, None, tm, fq), lambda q, i: (half, q, i, 0))

    def nxt(half):
        return pl.BlockSpec((None, None, h, fq), lambda q, i: (half, q, jnp.minimum((i + 1) * per, last_halo), 0))

    return pl.pallas_call(
        body, name=name, grid=(nq, nt),
        in_specs=[pl.BlockSpec((None, None, h, fq), lambda q, i: (0, q, jnp.maximum(i * per - 1, 0), 0)),
                  cur(0), nxt(0), cur(1), nxt(1),
                  pl.BlockSpec((None, tm, fq), lambda q, i: (q, i, 0)),
                  pl.BlockSpec((None, h, fq), lambda q, i: (q, jnp.minimum((i + 1) * per, last_halo), 0)),
                  pl.BlockSpec((None, CONV_WIDTH, fq), lambda q, i: (q, 0, 0)),
                  pl.BlockSpec((None, 1, fq), lambda q, i: (q, 0, 0))],
        out_specs=[pl.BlockSpec((2, None, tm, fq), lambda q, i: (0, q, i, 0)),
                   pl.BlockSpec((None, 8, fq), lambda q, i: (q, 0, 0))],
        out_shape=[jax.ShapeDtypeStruct((2, nq, s, fq), BF), jax.ShapeDtypeStruct((nq, 8, fq), F32)],
        scratch_shapes=[pltpu.VMEM((tm + 2 * h, fq), F32), pltpu.VMEM((te, fq), F32), pltpu.VMEM((4, 8, fq), F32)],
        compiler_params=_params(("parallel", "arbitrary")),
    )(up, up, up, up, up, dact, dact, cw, cb)


def _band(blk, n_steps, dil, first):
    qi = lax.broadcasted_iota(jnp.int32, (blk, 2 * blk), 0) + blk
    ki = lax.broadcasted_iota(jnp.int32, (blk, 2 * blk), 1)
    delta = qi - ki
    valid = (delta >= 0) & (delta <= n_steps) & ((ki >= blk) | jnp.logical_not(first))
    return valid, (delta * dil).astype(F32)


def _attn_fwd(q_all, kv, g, slopes, d, name):
    window, dil = BRANCHES[g]
    n_steps = window // dil
    blk = max(ATTN_BLOCK, n_steps)
    s = q_all.shape[0]
    sub = s // dil
    nb = sub // blk
    assert nb * blk == sub
    nh = d // HEAD_DIM
    nbr = len(BRANCHES)
    qv = q_all.reshape(sub, dil * nbr * d)
    kvv = kv.reshape(sub, dil * 2 * nbr * d)
    scale = HEAD_DIM ** -0.5

    def body(q_ref, kp_ref, kc_ref, vp_ref, vc_ref, o_ref, l_ref):
        j = pl.program_id(1)
        valid, dist = _band(blk, n_steps, dil, j == 0)
        qb = q_ref[...]
        kb = jnp.concatenate([kp_ref[...], kc_ref[...]], axis=0)
        vb = jnp.concatenate([vp_ref[...], vc_ref[...]], axis=0)
        for h in range(nh):
            sl = slice(h * HEAD_DIM, (h + 1) * HEAD_DIM)
            sc = lax.dot_general(qb[:, sl], kb[:, sl], (((1,), (1,)), ((), ())), preferred_element_type=F32) * scale
            sc = jnp.where(valid, sc - float(slopes[h]) * dist, NEG)
            m = jnp.max(sc, axis=-1, keepdims=True)
            p = jnp.exp(sc - m)
            den = jnp.sum(p, axis=-1, keepdims=True)
            o = jnp.dot(p.astype(BF), vb[:, sl], preferred_element_type=F32) / den
            o_ref[:, sl] = o.astype(BF)
            l_ref[:, sl] = jnp.broadcast_to(m + jnp.log(den), (blk, HEAD_DIM))

    def spec(col, prev):
        if prev:
            return pl.BlockSpec((blk, d), lambda r, j: (jnp.maximum(j - 1, 0), r * col[0] + col[1]))
        return pl.BlockSpec((blk, d), lambda r, j: (j, r * col[0] + col[1]))

    qcol, kcol, vcol = (nbr, g), (2 * nbr, g), (2 * nbr, nbr + g)
    ospec = pl.BlockSpec((blk, d), lambda r, j: (j, r))
    o, lse = pl.pallas_call(
        body, name=name, grid=(dil, nb),
        in_specs=[spec(qcol, False), spec(kcol, True), spec(kcol, False), spec(vcol, True), spec(vcol, False)],
        out_specs=[ospec, ospec],
        out_shape=[jax.ShapeDtypeStruct((sub, dil * d), BF), jax.ShapeDtypeStruct((sub, dil * d), F32)],
        compiler_params=_params(("parallel", "parallel")),
    )(qv, kvv, kvv, kvv, kvv)
    return o.reshape(s, d), lse.reshape(s, d)


def _attn_combine(os, lses, name):
    s, d = os[0].shape
    tm = _pick(s, 512, 8)
    nbr = len(os)

    def body(*refs):
        o_refs, l_refs = refs[:nbr], refs[nbr:2 * nbr]
        o_ref, lt_ref = refs[2 * nbr], refs[2 * nbr + 1]
        ls = [r[...] for r in l_refs]
        m = ls[0]
        for v in ls[1:]:
            m = jnp.maximum(m, v)
        tot = jnp.exp(ls[0] - m)
        for v in ls[1:]:
            tot = tot + jnp.exp(v - m)
        lt = m + jnp.log(tot)
        acc = jnp.exp(ls[0] - lt) * o_refs[0][...].astype(F32)
        for v, r in zip(ls[1:], o_refs[1:]):
            acc = acc + jnp.exp(v - lt) * r[...].astype(F32)
        o_ref[...] = acc.astype(BF)
        lt_ref[...] = lt

    return pl.pallas_call(
        body, name=name, grid=(s // tm,),
        in_specs=[_row_spec(tm, d)] * (2 * nbr), out_specs=[_row_spec(tm, d), _row_spec(tm, d)],
        out_shape=[jax.ShapeDtypeStruct((s, d), BF), jax.ShapeDtypeStruct((s, d), F32)],
        compiler_params=_params(("parallel",)),
    )(*os, *lses)


def _attn_bwd(q_all, kv, do, o, lt, g, slopes, d, name, dk_in=None, dv_in=None):
    window, dil = BRANCHES[g]
    n_steps = window // dil
    blk = max(ATTN_BLOCK, n_steps)
    s = q_all.shape[0]
    sub = s // dil
    nb = sub // blk
    nh = d // HEAD_DIM
    nbr = len(BRANCHES)
    qv = q_all.reshape(sub, dil * nbr * d)
    kvv = kv.reshape(sub, dil * 2 * nbr * d)
    scale = HEAD_DIM ** -0.5
    acc_in = dk_in is not None

    def body(*refs):
        q_ref, do_ref, o_ref, lt_ref, kp_ref, kc_ref, vp_ref, vc_ref = refs[:8]
        n_in = 10 if acc_in else 8
        dkin_ref, dvin_ref = (refs[8], refs[9]) if acc_in else (None, None)
        dq_ref, dk_ref, dv_ref, keep_k, keep_v, part_k, part_v = refs[n_in:n_in + 7]
        t = pl.program_id(1)

        def emit(prev_k, prev_v):
            if acc_in:
                prev_k = prev_k + dkin_ref[...].astype(F32)
                prev_v = prev_v + dvin_ref[...].astype(F32)
            dk_ref[...] = prev_k.astype(BF)
            dv_ref[...] = prev_v.astype(BF)

        @pl.when(t < nb)
        def _():
            valid, dist = _band(blk, n_steps, dil, t == 0)
            qb = q_ref[...]
            dob = do_ref[...]
            kb = jnp.concatenate([kp_ref[...], kc_ref[...]], axis=0)
            vb = jnp.concatenate([vp_ref[...], vc_ref[...]], axis=0)
            for h in range(nh):
                sl = slice(h * HEAD_DIM, (h + 1) * HEAD_DIM)
                qh, kh, vh, doh = qb[:, sl], kb[:, sl], vb[:, sl], dob[:, sl]
                sc = lax.dot_general(qh, kh, (((1,), (1,)), ((), ())), preferred_element_type=F32) * scale
                sc = sc - float(slopes[h]) * dist
                p = jnp.where(valid, jnp.exp(jnp.minimum(sc - lt_ref[:, sl][:, 0:1], 30.0)), 0.0)
                dlt = jnp.sum(doh.astype(F32) * o_ref[:, sl].astype(F32), axis=-1, keepdims=True)
                dp = lax.dot_general(doh, vh, (((1,), (1,)), ((), ())), preferred_element_type=F32)
                ds = (p * (dp - dlt)).astype(BF)
                dq_ref[:, sl] = (jnp.dot(ds, kh, preferred_element_type=F32) * scale).astype(BF)
                part_k[:, sl] = lax.dot_general(ds, qh, (((0,), (0,)), ((), ())), preferred_element_type=F32) * scale
                part_v[:, sl] = lax.dot_general(p.astype(BF), doh, (((0,), (0,)), ((), ())),
                                                preferred_element_type=F32)

            @pl.when(t > 0)
            def _():
                emit(keep_k[...] + part_k[0:blk, :], keep_v[...] + part_v[0:blk, :])

            keep_k[...] = part_k[blk:, :]
            keep_v[...] = part_v[blk:, :]

        @pl.when(t == nb)
        def _():
            emit(keep_k[...], keep_v[...])

    def qspec(col):
        return pl.BlockSpec((blk, d), lambda r, t: (jnp.minimum(t, nb - 1), r * col[0] + col[1]))

    def kspec(col, prev):
        if prev:
            return pl.BlockSpec((blk, d), lambda r, t: (jnp.maximum(jnp.minimum(t, nb - 1) - 1, 0), r * col[0] + col[1]))
        return qspec(col)

    kout = pl.BlockSpec((blk, d), lambda r, t: (jnp.maximum(t - 1, 0), r))
    qcol, kcol, vcol, one = (nbr, g), (2 * nbr, g), (2 * nbr, nbr + g), (1, 0)
    in_specs = [qspec(qcol), qspec(one), qspec(one), qspec(one),
                kspec(kcol, True), kspec(kcol, False), kspec(vcol, True), kspec(vcol, False)]
    args = [qv, do.reshape(sub, dil * d), o.reshape(sub, dil * d), lt.reshape(sub, dil * d), kvv, kvv, kvv, kvv]
    if acc_in:
        in_specs += [kout, kout]
        args += [dk_in.reshape(sub, dil * d), dv_in.reshape(sub, dil * d)]
    shp = jax.ShapeDtypeStruct((sub, dil * d), BF)
    dq, dk, dv = pl.pallas_call(
        body, name=name, grid=(dil, nb + 1),
        in_specs=in_specs, out_specs=[qspec(one), kout, kout], out_shape=[shp, shp, shp],
        scratch_shapes=[pltpu.VMEM((blk, d), F32), pltpu.VMEM((blk, d), F32),
                        pltpu.VMEM((2 * blk, d), F32), pltpu.VMEM((2 * blk, d), F32)],
        compiler_params=_params(("parallel", "arbitrary")),
    )(*args)
    return dq.reshape(s, d), dk.reshape(s, d), dv.reshape(s, d)


def _adamw(parts_list, w, m, v, name):
    nl, r, c = w.shape
    assert len(parts_list) == nl
    npart = parts_list[0].shape[0]
    tr = _pick(r, 256, 16)
    c1 = 1.0 / (1.0 - ADAM_B1 ** ADAM_STEP)
    c2 = 1.0 / (1.0 - ADAM_B2 ** ADAM_STEP)

    def body(*refs):
        p_refs = refs[:nl]
        w_ref, m_ref, v_ref, g_ref, d_ref, nm_ref, nv_ref = refs[nl:]
        layer = pl.program_id(0)
        for idx in range(nl):
            @pl.when(layer == idx)
            def _(p_ref=p_refs[idx]):
                g = p_ref[0].astype(F32)
                for k in range(1, npart):
                    g = g + p_ref[k].astype(F32)
                nm = ADAM_B1 * m_ref[...] + (1.0 - ADAM_B1) * g
                nv = ADAM_B2 * v_ref[...] + (1.0 - ADAM_B2) * (g * g)
                g_ref[...] = g
                nm_ref[...] = nm
                nv_ref[...] = nv
                d_ref[...] = -ADAM_LR * ((nm * c1) / (jnp.sqrt(nv * c2) + ADAM_EPS) + ADAM_WD * w_ref[...])

    def part_spec(idx):
        return pl.BlockSpec((npart, tr, c), lambda l, i: (0, jnp.where(l == idx, i, 0), 0))

    blk = pl.BlockSpec((None, tr, c), lambda l, i: (l, i, 0))
    shp = jax.ShapeDtypeStruct((nl, r, c), F32)
    return pl.pallas_call(
        body, name=name, grid=(nl, r // tr),
        in_specs=[part_spec(idx) for idx in range(nl)] + [blk, blk, blk],
        out_specs=[blk, blk, blk, blk], out_shape=[shp, shp, shp, shp],
        compiler_params=_params(("parallel", "parallel")),
    )(*parts_list, w, m, v)


def _full_from_slots(slots, shard_shape, axis):
    a = slots.reshape((N_DEV,) + tuple(shard_shape))
    a = jnp.moveaxis(a, 0, axis)
    full = list(shard_shape)
    full[axis] *= N_DEV
    return a.reshape(full)


def kernel(x, c, ada_w, ada_b, norm1_g, norm2_g, pool_w_in, pool_w_grp, pool_scale, pool_w_out, kv_norm_g, kv_ada_w, kv_ada_b, w_kv, attn_w_q, attn_w_o, ffn_w_up, ffn_conv_w, ffn_conv_b, ffn_w_down, final_g, loss_target, m_ada_w, m_ada_b, m_norm1_g, m_norm2_g, m_pool_w_in, m_pool_w_grp, m_pool_scale, m_pool_w_out, m_kv_norm_g, m_kv_ada_w, m_kv_ada_b, m_w_kv, m_attn_w_q, m_attn_w_o, m_ffn_w_up, m_ffn_conv_w, m_ffn_conv_b, m_ffn_w_down, m_final_g, v_ada_w, v_ada_b, v_norm1_g, v_norm2_g, v_pool_w_in, v_pool_w_grp, v_pool_scale, v_pool_w_out, v_kv_norm_g, v_kv_ada_w, v_kv_ada_b, v_w_kv, v_attn_w_q, v_attn_w_o, v_ffn_w_up, v_ffn_conv_w, v_ffn_conv_b, v_ffn_w_down, v_final_g):
    weights = dict(ada_w=ada_w, ada_b=ada_b, norm1_g=norm1_g, norm2_g=norm2_g, pool_w_in=pool_w_in,
                   pool_w_grp=pool_w_grp, pool_scale=pool_scale, pool_w_out=pool_w_out, kv_norm_g=kv_norm_g,
                   kv_ada_w=kv_ada_w, kv_ada_b=kv_ada_b, w_kv=w_kv, attn_w_q=attn_w_q, attn_w_o=attn_w_o,
                   ffn_w_up=ffn_w_up, ffn_conv_w=ffn_conv_w, ffn_conv_b=ffn_conv_b, ffn_w_down=ffn_w_down,
                   final_g=final_g)
    mom1 = dict(ada_w=m_ada_w, ada_b=m_ada_b, norm1_g=m_norm1_g, norm2_g=m_norm2_g, pool_w_in=m_pool_w_in,
                pool_w_grp=m_pool_w_grp, pool_scale=m_pool_scale, pool_w_out=m_pool_w_out, kv_norm_g=m_kv_norm_g,
                kv_ada_w=m_kv_ada_w, kv_ada_b=m_kv_ada_b, w_kv=m_w_kv, attn_w_q=m_attn_w_q, attn_w_o=m_attn_w_o,
                ffn_w_up=m_ffn_w_up, ffn_conv_w=m_ffn_conv_w, ffn_conv_b=m_ffn_conv_b, ffn_w_down=m_ffn_w_down,
                final_g=m_final_g)
    mom2 = dict(ada_w=v_ada_w, ada_b=v_ada_b, norm1_g=v_norm1_g, norm2_g=v_norm2_g, pool_w_in=v_pool_w_in,
                pool_w_grp=v_pool_w_grp, pool_scale=v_pool_scale, pool_w_out=v_pool_w_out, kv_norm_g=v_kv_norm_g,
                kv_ada_w=v_kv_ada_w, kv_ada_b=v_kv_ada_b, w_kv=v_w_kv, attn_w_q=v_attn_w_q, attn_w_o=v_attn_w_o,
                ffn_w_up=v_ffn_w_up, ffn_conv_w=v_ffn_conv_w, ffn_conv_b=v_ffn_conv_b, ffn_w_down=v_ffn_w_down,
                final_g=v_final_g)
    order = list(weights)

    seq, d = x.shape[1], x.shape[2]
    depth = ada_w.shape[0]
    n_pool = pool_w_in.shape[0]
    f = ffn_conv_b.shape[1]
    nbr = len(BRANCHES)
    nh = d // HEAD_DIM
    slopes = _alibi_slopes(nbr * nh).reshape(nbr, nh)
    me = 4 * lax.axis_index("x") + 2 * lax.axis_index("y") + lax.axis_index("c")
    xs = x[0]
    tgt = loss_target[0]

    def start_gathers(after):
        keys, groups = [], []
        for l in range(depth):
            if l < n_pool:
                parts = [("mixer", [pool_w_in[l], pool_w_grp[l].reshape(-1, pool_w_grp.shape[-1]), pool_w_out[l]])]
            else:
                j = l - n_pool
                parts = [("mixer", [attn_w_q[j]] + ([w_kv] if j == 0 else [])), ("out", [attn_w_o[j]])]
            for part, srcs in parts + [("ffn", [ffn_w_up[l], ffn_w_down[l]])]:
                keys.append((l, part))
                groups.append([a.astype(BF) for a in srcs])
        handles, _ = _comm_start(groups, "gather_start", scatter=False, after=after)
        return dict(zip(keys, handles))

    cond = c * (1.0 / (1.0 + jnp.exp(-c)))
    small_in = jnp.concatenate([cond.reshape(-1), ffn_conv_w.reshape(-1), pool_scale.reshape(-1)])
    n_small_in = small_in.shape[0]
    gath = _all_gather(_to_rows(small_in), "gather_small").reshape(N_DEV, -1)[:, :n_small_in]
    cond_all = gath[:, :d]
    o1 = d + ffn_conv_w.size
    conv_w_full = _full_from_slots(gath[:, d:o1], ffn_conv_w.shape, 2)
    pool_scale_full = _full_from_slots(gath[:, o1:], pool_scale.shape, 1)
    cond16 = jnp.concatenate([cond_all, jnp.zeros_like(cond_all)], axis=0)

    mod_part = _ada_fwd(cond16, ada_w, "ada_fwd")[:, :N_DEV]
    kv_part = _ada_fwd(cond16, kv_ada_w[None], "kv_ada_fwd")[0, :N_DEV]
    n_mod = depth * mod_part.shape[2] + kv_part.shape[1]
    send = jnp.concatenate([jnp.moveaxis(mod_part, 1, 0).reshape(N_DEV, -1), kv_part], axis=1)
    send_rows = jax.vmap(_to_rows)(send)
    got = _all_to_all(send_rows, "exchange_mod").reshape(N_DEV, -1)[:, :n_mod]
    ncol = mod_part.shape[2]
    mods = []
    for l in range(depth):
        row = got[:, l * ncol:(l + 1) * ncol].reshape(1, -1) + ada_b[l][None]
        mods.append([row[:, k * d:(k + 1) * d] for k in range(6)])
    kv_row = got[:, depth * ncol:].reshape(1, -1) + kv_ada_b[None]
    kv_shift, kv_scale = kv_row[:, :d], kv_row[:, d:]
    gather_handles = start_gathers(got)

    def vec(a):
        return a.reshape(1, -1)

    nq = 4
    fq = f // nq
    ng = len(POOL_WINDOWS)
    cw_slots = jnp.moveaxis(conv_w_full.reshape(depth, CONV_WIDTH, nq, fq), 2, 1)
    cb_slots = ffn_conv_b.reshape(depth, nq, 1, fq)

    saved = []
    xcur = xs
    kvs = None
    hkv = None
    x_kv = None
    w_kv_slots = None
    for l in range(depth):
        sh1, sc1, g1, sh2, sc2, g2 = mods[l]
        st = dict(x=xcur)
        h = _norm_mod(xcur, vec(norm1_g[l]), sc1, sh1, f"norm1_{l}")
        st["h"] = h
        gw = _comm_wait(gather_handles[l, "mixer"], h, f"gather_wait_mixer_{l}")
        if l < n_pool:
            w_in = gw[0].reshape(d, d)
            w_grp = jnp.moveaxis(gw[1].reshape(N_DEV, ng, -1, d // ng), 0, 1).reshape(ng, d // ng, d // ng)
            w_out = gw[2].reshape(d, d)
            u = _mm(h, w_in, f"pool_in_{l}")
            pooled = _pool_fwd(u, f"pool_fwd_{l}")
            z, y = _grp_fwd(pooled, w_grp, vec(pool_scale_full[l]), f"grp_fwd_{l}")
            mix, x1 = _mm(y, w_out, f"pool_out_{l}", gate=g1, resid=xcur, pre_dtype=BF)
            st.update(pooled=pooled, z=z, y=y, w_in=w_in, w_grp=w_grp, w_out=w_out)
        else:
            j = l - n_pool
            w_q_slots = gw[0]
            if j == 0:
                w_kv_slots = gw[1]
                x_kv = xcur
                hkv = _norm_mod(xcur, vec(kv_norm_g), kv_scale, kv_shift, "norm_kv")
                kvs = _mm(hkv, w_kv_slots, "kv_proj", bf="qkn", out_dtype=BF)
            q = _mm(h, w_q_slots, f"q_proj_{l}", bf="qkn", out_dtype=BF)
            outs, lses = [], []
            for g in range(nbr):
                og, lg = _attn_fwd(q, kvs, g, slopes[g], d, f"attn_fwd_{l}_{g}")
                outs.append(og)
                lses.append(lg)
            o, lt = _attn_combine(outs, lses, f"attn_mix_{l}")
            w_o = _comm_wait(gather_handles[l, "out"], o, f"gather_wait_out_{l}")[0].reshape(d, d)
            mix, x1 = _mm(o, w_o, f"attn_out_{l}", gate=g1, resid=xcur, pre_dtype=BF)
            st.update(q=q, o=o, lt=lt, w_q=w_q_slots, w_o=w_o)
        h2 = _norm_mod(x1, vec(norm2_g[l]), sc2, sh2, f"norm2_{l}")
        w_up_slots, w_down = _comm_wait(gather_handles[l, "ffn"], h2, f"gather_wait_ffn_{l}")
        w_down = w_down.reshape(f, d)
        st.update(w_up=w_up_slots, w_down=w_down)
        up = _mm(h2, w_up_slots, f"ffn_up_{l}", bf="qkn", of="qmn", out_dtype=BF).reshape(2, nq, seq, fq)
        act = _ffn_act(up, cw_slots[l], cb_slots[l], f"ffn_act_{l}")
        ffo, x2 = _mm(act, w_down, f"ffn_down_{l}", af="qmk", gate=g2, resid=x1, pre_dtype=BF)
        st.update(mix=mix, x1=x1, h2=h2, up=up, act=act, ffo=ffo)
        saved.append(st)
        xcur = x2

    dx, loss_blk, d_final_g = _final_loss(xcur, tgt, vec(final_g), "final_loss")
    loss = lax.psum(loss_blk[0, 0], ("x", "y", "c"))

    d_mod = [None] * depth
    d_n1 = [None] * depth
    d_n2 = [None] * depth
    d_conv = [None] * depth
    d_pscale = [None] * n_pool
    dk_acc = [None] * nbr
    dv_acc = [None] * nbr
    ffn_handles = [None] * depth
    mixer_handles = [None] * depth
    tok = 0.0
    for l in reversed(range(depth)):
        sh1, sc1, g1, sh2, sc2, g2 = mods[l]
        st = saved[l]
        dffo, dg2 = _gate_bwd(dx, st["ffo"], g2 + tok, f"gate2_bwd_{l}")
        g_down = _mm(st["act"], dffo, f"ffn_down_dw_{l}", af="qkm", out_dtype=BF)
        dact = _mm(dffo, st["w_down"], f"ffn_down_dx_{l}", bf="nk", of="qmn", tn=fq, out_dtype=BF)
        dup, dc = _ffn_act_bwd(st["up"], dact, cw_slots[l], cb_slots[l], f"ffn_act_bwd_{l}")
        dup = dup.reshape(N_DEV, seq, -1)
        d_conv[l] = dc
        g_up = _mm(st["h2"], dup, f"ffn_up_dw_{l}", af="km", bf="qkn", of="qmn", out_dtype=BF)
        dh2 = _mm(dup, st["w_up"], f"ffn_up_dx_{l}", af="qmk", bf="qnk")
        (ffn_handles[l],), tok = _comm_start([[g_up, g_down.reshape(N_DEV, -1, d)]], f"exchange_start_ffn_{l}",
                                             scatter=True)
        dx1, dsh2, dw2 = _norm_mod_bwd(st["x1"], dh2, vec(norm2_g[l]), sc2 + tok, dx, f"norm2_bwd_{l}")
        d_n2[l] = dw2 * (1.0 + sc2)
        dsc2 = dw2 * vec(norm2_g[l])

        dmix, dg1 = _gate_bwd(dx1, st["mix"], g1, f"gate1_bwd_{l}")
        if l < n_pool:
            g_out = _mm(st["y"], dmix, f"pool_out_dw_{l}", af="km", out_dtype=BF)
            dy = _mm(dmix, st["w_out"], f"pool_out_dx_{l}", bf="nk", out_dtype=BF)
            dz, dpool, dps = _grp_bwd(dy, st["z"], st["w_grp"], vec(pool_scale_full[l]), f"grp_bwd_{l}")
            d_pscale[l] = dps
            g_grp = _grp_dw(st["pooled"], dz, ng, f"grp_dw_{l}")
            du = _pool_bwd(dpool, f"pool_bwd_{l}")
            g_in = _mm(st["h"], du, f"pool_in_dw_{l}", af="km", out_dtype=BF)
            dh = _mm(du, st["w_in"], f"pool_in_dx_{l}", bf="nk")
            g_grp_slots = jnp.moveaxis(g_grp.reshape(ng, N_DEV, -1, d // ng), 1, 0).reshape(N_DEV, -1, d // ng)
            send = [g_in.reshape(N_DEV, -1, d), g_grp_slots.astype(BF), g_out.reshape(N_DEV, -1, d)]
        else:
            j = l - n_pool
            g_o = _mm(st["o"], dmix, f"attn_out_dw_{l}", af="km", out_dtype=BF)
            do = _mm(dmix, st["w_o"], f"attn_out_dx_{l}", bf="nk", out_dtype=BF)
            dqs = []
            for g in range(nbr):
                dq_g, dk_g, dv_g = _attn_bwd(st["q"], kvs, do, st["o"], st["lt"], g, slopes[g], d,
                                             f"attn_bwd_{l}_{g}", dk_in=dk_acc[g], dv_in=dv_acc[g])
                dqs.append(dq_g)
                dk_acc[g], dv_acc[g] = dk_g, dv_g
            dq = jnp.concatenate(dqs, axis=1)
            nqc = st["w_q"].shape[2]
            g_q = _mm(st["h"], dq, f"q_proj_dw_{l}", af="km", of="qmn", tn=nqc, out_dtype=BF)
            dh = _mm(dq, st["w_q"], f"q_proj_dx_{l}", bf="qnk")
            send = [g_q, g_o.reshape(N_DEV, -1, d)]
        dx0, dsh1, dw1 = _norm_mod_bwd(st["x"], dh, vec(norm1_g[l]), sc1, dx1, f"norm1_bwd_{l}")
        d_n1[l] = dw1 * (1.0 + sc1)
        dsc1 = dw1 * vec(norm1_g[l])
        d_mod[l] = jnp.concatenate([dsh1, dsc1, dg1, dsh2, dsc2, dg2], axis=1)
        dx = dx0
        if l == n_pool:
            dkv = jnp.concatenate(dk_acc + dv_acc, axis=1)
            nkc = w_kv_slots.shape[2]
            g_kv = _mm(hkv, dkv, "kv_proj_dw", af="km", of="qmn", tn=nkc, out_dtype=BF)
            dhkv = _mm(dkv, w_kv_slots, "kv_proj_dx", bf="qnk")
            dx, dsh_kv, dw_kv = _norm_mod_bwd(x_kv, dhkv, vec(kv_norm_g), kv_scale, dx, "norm_kv_bwd")
            d_kv_norm = dw_kv * (1.0 + kv_scale)
            d_kv_mod = jnp.concatenate([dsh_kv, dw_kv * vec(kv_norm_g)], axis=1)
            send.append(g_kv)
        (mixer_handles[l],), tok = _comm_start([send], f"exchange_start_mixer_{l}", scatter=True)
    grad_x = dx[None]
    d_final_g = d_final_g + tok

    small = [jnp.concatenate(d_mod, axis=1).reshape(-1), d_kv_mod.reshape(-1),
             jnp.concatenate(d_n1, axis=0).reshape(-1), jnp.concatenate(d_n2, axis=0).reshape(-1),
             d_kv_norm.reshape(-1), d_final_g.reshape(-1),
             jnp.stack([dcl[:, 3, :] for dcl in d_conv]).reshape(-1),
             jnp.stack([jnp.moveaxis(dcl[:, 0:CONV_WIDTH, :], 0, 1) for dcl in d_conv]).reshape(-1),
             jnp.concatenate(d_pscale, axis=0).reshape(-1)]
    sizes = [a.shape[0] for a in small]
    small_rows = _to_rows(jnp.concatenate(small))
    small_all = _all_gather(small_rows, "gather_small_grads")
    dmod_all = small_all.reshape(N_DEV, -1)[:, :sizes[0] + sizes[1]]

    dmod16 = jnp.concatenate([dmod_all, jnp.zeros_like(dmod_all)], axis=0)
    dm = dmod16[:, :sizes[0]].reshape(16, depth, N_DEV, ncol)
    dm_mine = lax.dynamic_index_in_dim(dm, me, axis=2, keepdims=False)
    g_ada_w = _ada_bwd(cond16, jnp.moveaxis(dm_mine, 0, 1), "ada_bwd")
    nkv = kv_part.shape[1]
    dkm = dmod16[:, sizes[0]:].reshape(16, N_DEV, nkv)
    dkm_mine = lax.dynamic_index_in_dim(dkm, me, axis=1, keepdims=False)
    g_kv_ada_w = _ada_bwd(cond16, dkm_mine[None], "kv_ada_bwd")

    res = {}

    def update(n, parts_list, shape3):
        w3, m3, v3 = (a[n].reshape(shape3) for a in (weights, mom1, mom2))
        outs = _adamw(parts_list, w3, m3, v3, f"adamw_{n}")
        res[n] = [a.reshape(weights[n].shape) for a in outs]

    update("ada_w", [g_ada_w[l][None] for l in range(depth)], ada_w.shape)
    update("kv_ada_w", [g_kv_ada_w], (1,) + kv_ada_w.shape)
    after = res["ada_w"][0]
    parts_ffn = [_comm_wait(ffn_handles[l], after, f"exchange_wait_ffn_{l}") for l in reversed(range(depth))][::-1]
    parts = [_comm_wait(mixer_handles[l], after, f"exchange_wait_mixer_{l}") for l in reversed(range(depth))][::-1]
    pool_layers, attn_layers = range(n_pool), range(n_pool, depth)
    update("ffn_w_up", [parts_ffn[l][0] for l in range(depth)], ffn_w_up.shape)
    update("ffn_w_down", [parts_ffn[l][1] for l in range(depth)], ffn_w_down.shape)
    update("attn_w_q", [parts[l][0] for l in attn_layers], attn_w_q.shape)
    update("attn_w_o", [parts[l][1] for l in attn_layers], (depth - n_pool, -1, d))
    update("w_kv", [parts[n_pool][2]], (1,) + w_kv.shape)
    update("pool_w_in", [parts[l][0] for l in pool_layers], (n_pool, -1, d))
    update("pool_w_grp", [parts[l][1] for l in pool_layers], (n_pool, -1, d // ng))
    update("pool_w_out", [parts[l][2] for l in pool_layers], (n_pool, -1, d))

    tot = small_all.reshape(N_DEV, -1)
    offs = np.cumsum([0] + sizes)
    seg = {k: (int(offs[i]), int(offs[i + 1])) for i, k in enumerate(
        ["mod", "kv_mod", "n1", "n2", "kv_norm", "final", "conv_b", "conv_w", "pscale"])}

    def rows_of(a, b):
        return tot[:, a:b]

    nf8 = f // N_DEV
    conv_w_parts = lax.dynamic_slice_in_dim(
        rows_of(*seg["conv_w"]).reshape(N_DEV, depth, CONV_WIDTH, N_DEV, nf8), me, 1, axis=3).reshape(N_DEV, -1)
    nd8 = d // N_DEV
    pscale_parts = lax.dynamic_slice_in_dim(
        rows_of(*seg["pscale"]).reshape(N_DEV, n_pool, N_DEV, nd8), me, 1, axis=2).reshape(N_DEV, -1)
    small_names = ["ada_b", "norm1_g", "norm2_g", "pool_scale", "kv_norm_g", "kv_ada_b", "ffn_conv_w",
                   "ffn_conv_b", "final_g"]
    small_parts = [rows_of(*seg["mod"]), rows_of(*seg["n1"]), rows_of(*seg["n2"]), pscale_parts,
                   rows_of(*seg["kv_norm"]), rows_of(*seg["kv_mod"]), conv_w_parts, rows_of(*seg["conv_b"]),
                   rows_of(*seg["final"])]
    sp = jnp.concatenate(small_parts, axis=1)
    n_sp = sp.shape[1]
    sp_rows = jax.vmap(_to_rows)(sp)

    def packed(src):
        return _to_rows(jnp.concatenate([src[n].reshape(-1) for n in small_names]))[None]

    outs = _adamw([sp_rows], packed(weights), packed(mom1), packed(mom2), "adamw_small")
    outs = [a.reshape(-1)[:n_sp] for a in outs]
    off = 0
    for n in small_names:
        size = weights[n].size
        res[n] = [a[off:off + size].reshape(weights[n].shape) for a in outs]
        off += size

    grads = [res[n][0] for n in order]
    deltas = [res[n][1] for n in order]
    new_m = [res[n][2] for n in order]
    new_v = [res[n][3] for n in order]
    return (loss, grad_x, *grads, *deltas, *new_m, *new_v)
```

```python
import math

import numpy as np
import jax
import jax.numpy as jnp
from jax import lax
from jax.experimental import pallas as pl
from jax.experimental.pallas import tpu as pltpu

F32 = jnp.float32
BF = jnp.bfloat16

POOL_WINDOWS = (2, 4, 8, 16)
BRANCHES = ((128, 1), (512, 4), (2048, 16))
HEAD_DIM = 64
ATTN_BLOCK = 128
CONV_WIDTH = 3
EPS = 1e-6
ADAM_LR = 0.001
ADAM_B1 = 0.9
ADAM_B2 = 0.999
ADAM_EPS = 1e-08
ADAM_WD = 0.01
ADAM_STEP = 10

N_DEV = 8
LANES = 128
POOL_HALO = 16
CONV_HALO = 8
FFN_ROWS = 16
VMEM_LIMIT = 48 * 1024 * 1024
MM_TILE = 1024
NEG = -1e30

MESH = pl.DeviceIdType.MESH
ANY = pl.BlockSpec(memory_space=pl.ANY)


def _params(sem=None):
    if sem is None:
        return pltpu.CompilerParams(vmem_limit_bytes=VMEM_LIMIT)
    return pltpu.CompilerParams(dimension_semantics=sem, vmem_limit_bytes=VMEM_LIMIT)


def _pick(dim, pref, mult=LANES):
    if dim <= pref:
        return dim
    t = (pref // mult) * mult
    while t >= mult:
        if dim % t == 0:
            return t
        t -= mult
    return dim


def _alibi_slopes(n):
    def pow2(m):
        start = 2.0 ** (-(2.0 ** -(math.log2(m) - 3)))
        return [start ** (i + 1) for i in range(m)]
    if math.log2(n).is_integer():
        s = pow2(n)
    else:
        c = 2 ** math.floor(math.log2(n))
        s = pow2(c) + pow2(2 * c)[0::2][: n - c]
    s = np.asarray(s, dtype=np.float32)
    return -np.sort(-s)


def _my_place():
    return lax.axis_index("x"), lax.axis_index("y"), lax.axis_index("c")


def _all_gather_many(xs, name):
    n = len(xs)

    def body(*refs):
        x_refs, out_refs = refs[:n], refs[n:2 * n]
        send_sems, recv_sems, local_sems = refs[2 * n:]
        xi, yi, ci = _my_place()
        me, sibling = (xi, yi, ci), (xi, yi, 1 - ci)
        chips = [(1 - xi, yi), (xi, 1 - yi), (1 - xi, 1 - yi)]

        def slot(a, px, py, pc):
            return out_refs[a].at[4 * px + 2 * py + pc]

        def copy(a, k, block, to, src=None):
            return pltpu.make_async_remote_copy(
                src_ref=slot(a, *block) if src is None else src, dst_ref=slot(a, *block),
                send_sem=send_sems.at[7 * a + k], recv_sem=recv_sems.at[7 * a + k],
                device_id=to, device_id_type=MESH)

        mine = [pltpu.make_async_copy(x_refs[a], slot(a, *me), local_sems.at[a]) for a in range(n)]
        for cp in mine:
            cp.start()
        sent = []
        for a in range(n):
            first = [copy(a, 0, me, sibling, src=x_refs[a])]
            first += [copy(a, 1 + j, me, (*chip, ci), src=x_refs[a]) for j, chip in enumerate(chips)]
            for cp in first:
                cp.start()
            sent += first
        for a in range(n):
            for j, chip in enumerate(chips):
                copy(a, 1 + j, (*chip, ci), me).wait_recv()
                fwd = copy(a, 4 + j, (*chip, ci), sibling)
                fwd.start()
                sent.append(fwd)
        for a in range(n):
            copy(a, 0, sibling, me).wait_recv()
            for j, chip in enumerate(chips):
                copy(a, 4 + j, (*chip, 1 - ci), me).wait_recv()
        for cp in sent:
            cp.wait_send()
        for cp in mine:
            cp.wait()

    return pl.pallas_call(
        body, name=name,
        out_shape=[jax.ShapeDtypeStruct((N_DEV,) + x.shape, x.dtype) for x in xs],
        in_specs=[ANY] * n, out_specs=[ANY] * n,
        scratch_shapes=[pltpu.SemaphoreType.DMA((7 * n,)), pltpu.SemaphoreType.DMA((7 * n,)),
                        pltpu.SemaphoreType.DMA((n,))],
    )(*xs)


def _all_gather(x, name):
    return _all_gather_many([x], name)[0]


def _all_to_all_many(xs, name):
    n = len(xs)

    def body(*refs):
        x_refs, out_refs = refs[:n], refs[n:2 * n]
        send_sems, recv_sems, local_sems = refs[2 * n:]
        xi, yi, ci = _my_place()
        me = 4 * xi + 2 * yi + ci
        mine = [pltpu.make_async_copy(x_refs[a].at[me], out_refs[a].at[me], local_sems.at[a]) for a in range(n)]
        for cp in mine:
            cp.start()
        copies = []
        for a in range(n):
            for k in range(1, N_DEV):
                px = 1 - xi if k & 4 else xi
                py = 1 - yi if k & 2 else yi
                pc = 1 - ci if k & 1 else ci
                peer = 4 * px + 2 * py + pc
                cp = pltpu.make_async_remote_copy(
                    src_ref=x_refs[a].at[peer], dst_ref=out_refs[a].at[me],
                    send_sem=send_sems.at[7 * a + k - 1], recv_sem=recv_sems.at[7 * a + k - 1],
                    device_id=(px, py, pc), device_id_type=MESH)
                cp.start()
                copies.append(cp)
        for cp in copies:
            cp.wait()
        for cp in mine:
            cp.wait()

    return pl.pallas_call(
        body, name=name,
        out_shape=[jax.ShapeDtypeStruct(x.shape, x.dtype) for x in xs],
        in_specs=[ANY] * n, out_specs=[ANY] * n,
        scratch_shapes=[pltpu.SemaphoreType.DMA((7 * n,)), pltpu.SemaphoreType.DMA((7 * n,)),
                        pltpu.SemaphoreType.DMA((n,))],
    )(*xs)


def _all_to_all(x, name):
    return _all_to_all_many([x], name)[0]


HBM = pl.BlockSpec(memory_space=pltpu.HBM)
SEM = pl.BlockSpec(memory_space=pltpu.SEMAPHORE)
EFFECT = pltpu.SideEffectType.DATAFLOW_SIDE_EFFECTING


def _peer(k, xi, yi, ci):
    px = 1 - xi if k & 4 else xi
    py = 1 - yi if k & 2 else yi
    pc = 1 - ci if k & 1 else ci
    return (px, py, pc), 4 * px + 2 * py + pc


def _split_copies(x_refs, land_refs, send_sem, recv_sem, scatter):
    xi, yi, ci = _my_place()
    me = 4 * xi + 2 * yi + ci
    copies = []
    for p, (x_ref, land_ref) in enumerate(zip(x_refs, land_refs)):
        for k in range(1, N_DEV):
            place, peer = _peer(k, xi, yi, ci)
            copies.append(pltpu.make_async_remote_copy(
                src_ref=x_ref.at[peer] if scatter else x_ref, dst_ref=land_ref.at[me],
                send_sem=send_sem.at[7 * p + k - 1], recv_sem=recv_sem.at[7 * p + k - 1],
                device_id=place, device_id_type=MESH))
    return copies


def _comm_start(groups, name, scatter, after=None):
    sizes = [len(g) for g in groups]
    xs = [x for g in groups for x in g]
    n, ng = len(xs), len(groups)
    lands = [lax.empty(x.shape if scatter else (N_DEV,) + x.shape, x.dtype) for x in xs]
    starts = np.cumsum([0] + sizes)
    n_in = 2 * n + (after is not None)

    def body(*refs):
        x_refs, land_refs = refs[:n], refs[n:2 * n]
        send_sems, recv_sems = refs[n_in:n_in + ng], refs[n_in + ng:n_in + 2 * ng]
        token = refs[n_in + 2 * ng + 2 * n]
        for gi in range(ng):
            lo, hi = int(starts[gi]), int(starts[gi + 1])
            for cp in _split_copies(x_refs[lo:hi], land_refs[lo:hi], send_sems[gi], recv_sems[gi], scatter):
                cp.start()
        token[...] = jnp.zeros_like(token)

    sem_shapes = [pltpu.SemaphoreType.DMA((7 * m,)) for m in sizes]
    thru = [pltpu.HBM(a.shape, a.dtype) for a in xs + lands]
    res = pl.pallas_call(
        body, name=name,
        out_shape=sem_shapes + sem_shapes + thru + [jax.ShapeDtypeStruct((8, LANES), F32)],
        in_specs=[HBM] * n_in,
        out_specs=[SEM] * (2 * ng) + [HBM] * (2 * n) + [pl.BlockSpec(memory_space=pltpu.VMEM)],
        input_output_aliases={i: 2 * ng + i for i in range(2 * n)},
        compiler_params=pltpu.CompilerParams(has_side_effects=EFFECT),
    )(*[pltpu.with_memory_space_constraint(a, pltpu.HBM) for a in xs + lands + ([] if after is None else [after])])
    send_sems, recv_sems = res[:ng], res[ng:2 * ng]
    x_thru, land_thru = res[2 * ng:2 * ng + n], res[2 * ng + n:2 * ng + 2 * n]
    handles = []
    for gi in range(ng):
        lo, hi = int(starts[gi]), int(starts[gi + 1])
        handles.append((send_sems[gi], recv_sems[gi], list(x_thru[lo:hi]), list(land_thru[lo:hi]), scatter))
    return handles, res[-1][0, 0]


def _comm_wait(handle, after, name):
    send_sem, recv_sem, x_thru, land_thru, scatter = handle
    m = len(x_thru)

    blocks = [a.shape[1:] if scatter else a.shape for a in x_thru]

    def body(*refs):
        x_refs, land_refs = refs[:m], refs[m:2 * m]
        local_sems, stage = refs[4 * m + 3], refs[4 * m + 4:]
        xi, yi, ci = _my_place()
        me = 4 * xi + 2 * yi + ci
        load = [pltpu.make_async_copy(x_refs[p].at[me] if scatter else x_refs[p], stage[p], local_sems.at[2 * p])
                for p in range(m)]
        store = [pltpu.make_async_copy(stage[p], land_refs[p].at[me], local_sems.at[2 * p + 1]) for p in range(m)]
        for cp in load:
            cp.start()
        for p in range(m):
            load[p].wait()
            store[p].start()
        for cp in _split_copies(x_refs, land_refs, refs[2 * m], refs[2 * m + 1], scatter):
            cp.wait_send()
            cp.wait_recv()
        for cp in store:
            cp.wait()

    res = pl.pallas_call(
        body, name=name,
        out_shape=[pltpu.HBM(a.shape, a.dtype) for a in x_thru + land_thru],
        in_specs=[HBM] * (2 * m) + [SEM, SEM, ANY], out_specs=[HBM] * (2 * m),
        input_output_aliases={i: i for i in range(2 * m)},
        scratch_shapes=[pltpu.SemaphoreType.DMA((2 * m,))] + [pltpu.VMEM(b, a.dtype) for b, a in zip(blocks, x_thru)],
        compiler_params=pltpu.CompilerParams(has_side_effects=EFFECT),
    )(*x_thru, *land_thru, send_sem, recv_sem, after)
    return list(res[m:])


def _to_rows(vec):
    n = vec.shape[0]
    unit = 8 * LANES
    pad = (-n) % unit
    if pad:
        vec = jnp.concatenate([vec, jnp.zeros((pad,), vec.dtype)])
    return vec.reshape(-1, LANES)


def _mm(a, b, name, *, af="mk", bf="kn", of="mn", out_dtype=F32, tm=None, tn=None, tk=None,
        gate=None, resid=None, pre_dtype=None):
    if af == "mk":
        m, kk = a.shape
    elif af == "km":
        kk, m = a.shape
    elif af == "qmk":
        qa, m, tk = a.shape
        kk = qa * tk
    else:
        qa, kk, tm = a.shape
        m = qa * tm
    if bf == "kn":
        k2, n = b.shape
    elif bf == "nk":
        n, k2 = b.shape
    elif bf == "qkn":
        qb, k2, tn = b.shape
        n = qb * tn
    else:
        qb, n, tkb = b.shape
        k2 = qb * tkb
        assert af != "qmk" or tkb == tk
        tk = tkb
    assert kk == k2, (name, a.shape, b.shape, af, bf)
    tm = _pick(m, MM_TILE) if tm is None else tm
    tn = _pick(n, MM_TILE) if tn is None else tn
    tk = _pick(kk, MM_TILE) if tk is None else tk
    assert m % tm == 0 and n % tn == 0 and kk % tk == 0, (name, m, n, kk, tm, tn, tk)
    nk = kk // tk
    a_spec = {"mk": pl.BlockSpec((tm, tk), lambda i, j, k: (i, k)),
              "km": pl.BlockSpec((tk, tm), lambda i, j, k: (k, i)),
              "qmk": pl.BlockSpec((None, tm, tk), lambda i, j, k: (k, i, 0)),
              "qkm": pl.BlockSpec((None, tk, tm), lambda i, j, k: (i, k, 0))}[af]
    b_spec = {"kn": pl.BlockSpec((tk, tn), lambda i, j, k: (k, j)),
              "nk": pl.BlockSpec((tn, tk), lambda i, j, k: (j, k)),
              "qkn": pl.BlockSpec((None, tk, tn), lambda i, j, k: (j, k, 0)),
              "qnk": pl.BlockSpec((None, tn, tk), lambda i, j, k: (k, j, 0))}[bf]
    dims = (((1 if af in ("mk", "qmk") else 0,), (0 if bf in ("kn", "qkn") else 1,)), ((), ()))
    in_specs, args = [a_spec, b_spec], [a, b]
    if gate is not None:
        assert of == "mn"
        in_specs.append(pl.BlockSpec((1, tn), lambda i, j, k: (0, j)))
        args.append(gate)
    if resid is not None:
        assert of == "mn"
        in_specs.append(pl.BlockSpec((tm, tn), lambda i, j, k: (i, j)))
        args.append(resid)
    if of == "mn":
        o_spec, o_shape = pl.BlockSpec((tm, tn), lambda i, j, k: (i, j)), (m, n)
    else:
        o_spec, o_shape = pl.BlockSpec((None, tm, tn), lambda i, j, k: (j, i, 0)), (n // tn, m, tn)
    out_shape, out_specs = [jax.ShapeDtypeStruct(o_shape, out_dtype)], [o_spec]
    if pre_dtype is not None:
        out_shape.insert(0, jax.ShapeDtypeStruct(o_shape, pre_dtype))
        out_specs.insert(0, o_spec)
    n_in = len(args)
    n_out = len(out_shape)

    def body(*refs):
        a_ref, b_ref = refs[0], refs[1]
        extra = list(refs[2:n_in])
        outs = refs[n_in:n_in + n_out]
        gate_ref = extra.pop(0) if gate is not None else None
        resid_ref = extra.pop(0) if resid is not None else None

        def product():
            return lax.dot_general(a_ref[...].astype(BF), b_ref[...].astype(BF), dims, preferred_element_type=F32)

        def finish(r):
            if pre_dtype is not None:
                outs[0][...] = r.astype(pre_dtype)
            if gate_ref is not None:
                r = r * gate_ref[...]
            if resid_ref is not None:
                r = resid_ref[...] + r
            outs[-1][...] = r.astype(out_dtype)

        if nk == 1:
            finish(product())
        else:
            acc = refs[n_in + n_out]
            k = pl.program_id(2)

            @pl.when(k == 0)
            def _():
                acc[...] = product()

            @pl.when(k > 0)
            def _():
                acc[...] += product()

            @pl.when(k == nk - 1)
            def _():
                finish(acc[...])

    res = pl.pallas_call(
        body, name=name, grid=(m // tm, n // tn, nk),
        in_specs=in_specs, out_specs=out_specs, out_shape=out_shape,
        scratch_shapes=[pltpu.VMEM((tm, tn), F32)] if nk > 1 else [],
        compiler_params=_params(("parallel", "parallel", "arbitrary")),
    )(*args)
    return res if pre_dtype is not None else res[0]


def _ada_fwd(cond16, w, name):
    nl, d, n = w.shape

    def body(c_ref, w_ref, o_ref):
        o_ref[...] = jnp.dot(c_ref[...].astype(BF), w_ref[...].astype(BF), preferred_element_type=F32)

    return pl.pallas_call(
        body, name=name, grid=(nl,),
        in_specs=[pl.BlockSpec((16, d), lambda l: (0, 0)), pl.BlockSpec((None, d, n), lambda l: (l, 0, 0))],
        out_specs=pl.BlockSpec((None, 16, n), lambda l: (l, 0, 0)),
        out_shape=jax.ShapeDtypeStruct((nl, 16, n), F32),
        compiler_params=_params(("parallel",)),
    )(cond16, w)


def _ada_bwd(cond16, dmod, name):
    nl, _, n = dmod.shape
    d = cond16.shape[1]

    def body(c_ref, g_ref, o_ref):
        o_ref[...] = lax.dot_general(c_ref[...].astype(BF), g_ref[...].astype(BF), (((0,), (0,)), ((), ())),
                                     preferred_element_type=F32)

    return pl.pallas_call(
        body, name=name, grid=(nl,),
        in_specs=[pl.BlockSpec((16, d), lambda l: (0, 0)), pl.BlockSpec((None, 16, n), lambda l: (l, 0, 0))],
        out_specs=pl.BlockSpec((None, d, n), lambda l: (l, 0, 0)),
        out_shape=jax.ShapeDtypeStruct((nl, d, n), F32),
        compiler_params=_params(("parallel",)),
    )(cond16, dmod)


def _row_spec(tm, d):
    return pl.BlockSpec((tm, d), lambda i: (i, 0))


def _vec_spec(d):
    return pl.BlockSpec((1, d), lambda i: (0, 0))


def _norm_mod(x, g, sc, sh, name):
    s, d = x.shape
    tm = _pick(s, 512, 8)

    def body(x_ref, g_ref, sc_ref, sh_ref, o_ref):
        xv = x_ref[...]
        r = lax.rsqrt(jnp.mean(xv * xv, axis=-1, keepdims=True) + EPS)
        y = (xv * r) * g_ref[...]
        o_ref[...] = (y * (1.0 + sc_ref[...]) + sh_ref[...]).astype(BF)

    return pl.pallas_call(
        body, name=name, grid=(s // tm,),
        in_specs=[_row_spec(tm, d), _vec_spec(d), _vec_spec(d), _vec_spec(d)],
        out_specs=_row_spec(tm, d), out_shape=jax.ShapeDtypeStruct((s, d), BF),
        compiler_params=_params(("parallel",)),
    )(x, g, sc, sh)


def _norm_mod_bwd(x, dh, g, sc, dx_in, name):
    s, d = x.shape
    tm = _pick(s, 512, 8)

    def body(x_ref, dh_ref, g_ref, sc_ref, dxin_ref, dx_ref, dsh_ref, dw_ref):
        i = pl.program_id(0)
        xv = x_ref[...]
        dhv = dh_ref[...].astype(F32)
        r = lax.rsqrt(jnp.mean(xv * xv, axis=-1, keepdims=True) + EPS)
        xn = xv * r
        dxn = dhv * (g_ref[...] * (1.0 + sc_ref[...]))
        dx_ref[...] = dxin_ref[...] + r * (dxn - xn * jnp.mean(dxn * xn, axis=-1, keepdims=True))

        @pl.when(i == 0)
        def _():
            dsh_ref[...] = jnp.zeros_like(dsh_ref)
            dw_ref[...] = jnp.zeros_like(dw_ref)

        dsh_ref[...] += jnp.sum(dhv, axis=0, keepdims=True)
        dw_ref[...] += jnp.sum(dhv * xn, axis=0, keepdims=True)

    return pl.pallas_call(
        body, name=name, grid=(s // tm,),
        in_specs=[_row_spec(tm, d), _row_spec(tm, d), _vec_spec(d), _vec_spec(d), _row_spec(tm, d)],
        out_specs=[_row_spec(tm, d), _vec_spec(d), _vec_spec(d)],
        out_shape=[jax.ShapeDtypeStruct((s, d), F32), jax.ShapeDtypeStruct((1, d), F32),
                   jax.ShapeDtypeStruct((1, d), F32)],
        compiler_params=_params(("arbitrary",)),
    )(x, dh, g, sc, dx_in)


def _gate_bwd(dx, y, gate, name):
    s, d = dx.shape
    tm = _pick(s, 512, 8)

    def body(dx_ref, y_ref, g_ref, dy_ref, dg_ref):
        i = pl.program_id(0)
        dxv = dx_ref[...]
        dy_ref[...] = (dxv * g_ref[...]).astype(BF)

        @pl.when(i == 0)
        def _():
            dg_ref[...] = jnp.zeros_like(dg_ref)

        dg_ref[...] += jnp.sum(dxv * y_ref[...].astype(F32), axis=0, keepdims=True)

    return pl.pallas_call(
        body, name=name, grid=(s // tm,),
        in_specs=[_row_spec(tm, d), _row_spec(tm, d), _vec_spec(d)],
        out_specs=[_row_spec(tm, d), _vec_spec(d)],
        out_shape=[jax.ShapeDtypeStruct((s, d), BF), jax.ShapeDtypeStruct((1, d), F32)],
        compiler_params=_params(("arbitrary",)),
    )(dx, y, gate)


def _final_loss(x, tgt, g, name):
    s, d = x.shape
    tm = _pick(s, 512, 8)

    def body(x_ref, t_ref, g_ref, dx_ref, loss_ref, dg_ref):
        i = pl.program_id(0)
        xv = x_ref[...]
        gv = g_ref[...]
        r = lax.rsqrt(jnp.mean(xv * xv, axis=-1, keepdims=True) + EPS)
        xn = xv * r
        err = xn * gv - t_ref[...]
        dy = err * (1.0 / d)
        dxn = dy * gv
        dx_ref[...] = r * (dxn - xn * jnp.mean(dxn * xn, axis=-1, keepdims=True))

        @pl.when(i == 0)
        def _():
            loss_ref[...] = jnp.zeros_like(loss_ref)
            dg_ref[...] = jnp.zeros_like(dg_ref)

        part = 0.5 * jnp.sum(jnp.sum(err * err, axis=-1, keepdims=True) * (1.0 / d), axis=0, keepdims=True)
        loss_ref[...] += jnp.broadcast_to(part, loss_ref.shape)
        dg_ref[...] += jnp.sum(dy * xn, axis=0, keepdims=True)

    return pl.pallas_call(
        body, name=name, grid=(s // tm,),
        in_specs=[_row_spec(tm, d), _row_spec(tm, d), _vec_spec(d)],
        out_specs=[_row_spec(tm, d), pl.BlockSpec((8, LANES), lambda i: (0, 0)), _vec_spec(d)],
        out_shape=[jax.ShapeDtypeStruct((s, d), F32), jax.ShapeDtypeStruct((8, LANES), F32),
                   jax.ShapeDtypeStruct((1, d), F32)],
        compiler_params=_params(("arbitrary",)),
    )(x, tgt, g)


def _pool_counts(tm, gd, row0, w):
    t = lax.broadcasted_iota(jnp.int32, (tm, gd), 0) + row0
    return jnp.minimum(t + 1, w).astype(F32)


def _pool_fwd(u, name):
    s, d = u.shape
    tm = _pick(s, 256, POOL_HALO)
    gd = d // len(POOL_WINDOWS)
    per = tm // POOL_HALO

    def body(prev_ref, cur_ref, o_ref, ext):
        i = pl.program_id(0)
        ext[0:POOL_HALO, :] = jnp.where(i > 0, prev_ref[...], 0.0)
        ext[POOL_HALO:, :] = cur_ref[...]
        for g, w in enumerate(POOL_WINDOWS):
            cols = slice(g * gd, (g + 1) * gd)
            acc = ext[POOL_HALO:POOL_HALO + tm, cols]
            own = acc
            for k in range(1, w):
                acc = acc + ext[POOL_HALO - k:POOL_HALO - k + tm, cols]
            o_ref[:, cols] = (acc / _pool_counts(tm, gd, i * tm, w) - own).astype(BF)

    return pl.pallas_call(
        body, name=name, grid=(s // tm,),
        in_specs=[pl.BlockSpec((POOL_HALO, d), lambda i: (jnp.maximum(i * per - 1, 0), 0)), _row_spec(tm, d)],
        out_specs=_row_spec(tm, d), out_shape=jax.ShapeDtypeStruct((s, d), BF),
        scratch_shapes=[pltpu.VMEM((tm + POOL_HALO, d), F32)],
        compiler_params=_params(("parallel",)),
    )(u, u)


def _pool_bwd(dp, name):
    s, d = dp.shape
    tm = _pick(s, 256, POOL_HALO)
    gd = d // len(POOL_WINDOWS)
    per = tm // POOL_HALO
    nt = s // tm
    last_halo = s // POOL_HALO - 1

    def body(cur_ref, nxt_ref, o_ref, ext):
        i = pl.program_id(0)
        for g, w in enumerate(POOL_WINDOWS):
            cols = slice(g * gd, (g + 1) * gd)
            ext[0:tm, cols] = cur_ref[:, cols].astype(F32) / _pool_counts(tm, gd, i * tm, w)
            nxt = nxt_ref[:, cols].astype(F32) / _pool_counts(POOL_HALO, gd, (i + 1) * tm, w)
            ext[tm:, cols] = jnp.where(i < nt - 1, nxt, 0.0)
        for g, w in enumerate(POOL_WINDOWS):
            cols = slice(g * gd, (g + 1) * gd)
            acc = ext[0:tm, cols]
            for k in range(1, w):
                acc = acc + ext[k:k + tm, cols]
            o_ref[:, cols] = (acc - cur_ref[:, cols].astype(F32)).astype(BF)

    return pl.pallas_call(
        body, name=name, grid=(nt,),
        in_specs=[_row_spec(tm, d), pl.BlockSpec((POOL_HALO, d), lambda i: (jnp.minimum((i + 1) * per, last_halo), 0))],
        out_specs=_row_spec(tm, d), out_shape=jax.ShapeDtypeStruct((s, d), BF),
        scratch_shapes=[pltpu.VMEM((tm + POOL_HALO, d), F32)],
        compiler_params=_params(("parallel",)),
    )(dp, dp)


def _grp_fwd(p, w, scale, name):
    s, d = p.shape
    ng, gd, _ = w.shape
    tm = _pick(s, 1024, 8)

    def body(p_ref, w_ref, s_ref, z_ref, y_ref):
        z = jnp.dot(p_ref[...], w_ref[...].astype(BF), preferred_element_type=F32)
        z_ref[...] = z.astype(BF)
        y_ref[...] = (z * s_ref[...]).astype(BF)

    blk = pl.BlockSpec((tm, gd), lambda i, g: (i, g))
    return pl.pallas_call(
        body, name=name, grid=(s // tm, ng),
        in_specs=[blk, pl.BlockSpec((None, gd, gd), lambda i, g: (g, 0, 0)), pl.BlockSpec((1, gd), lambda i, g: (0, g))],
        out_specs=[blk, blk],
        out_shape=[jax.ShapeDtypeStruct((s, d), BF), jax.ShapeDtypeStruct((s, d), BF)],
        compiler_params=_params(("parallel", "parallel")),
    )(p, w, scale)


def _grp_bwd(dy, z, w, scale, name):
    s, d = dy.shape
    ng, gd, _ = w.shape
    tm = _pick(s, 1024, 8)

    def body(dy_ref, z_ref, w_ref, s_ref, dz_ref, dp_ref, ds_ref):
        i = pl.program_id(1)
        dyv = dy_ref[...].astype(F32)
        dz = (dyv * s_ref[...]).astype(BF)
        dz_ref[...] = dz
        dp_ref[...] = lax.dot_general(dz, w_ref[...].astype(BF), (((1,), (1,)), ((), ())),
                                      preferred_element_type=F32).astype(BF)

        @pl.when(i == 0)
        def _():
            ds_ref[...] = jnp.zeros_like(ds_ref)

        ds_ref[...] += jnp.sum(dyv * z_ref[...].astype(F32), axis=0, keepdims=True)

    blk = pl.BlockSpec((tm, gd), lambda g, i: (i, g))
    vec = pl.BlockSpec((1, gd), lambda g, i: (0, g))
    return pl.pallas_call(
        body, name=name, grid=(ng, s // tm),
        in_specs=[blk, blk, pl.BlockSpec((None, gd, gd), lambda g, i: (g, 0, 0)), vec],
        out_specs=[blk, blk, vec],
        out_shape=[jax.ShapeDtypeStruct((s, d), BF), jax.ShapeDtypeStruct((s, d), BF),
                   jax.ShapeDtypeStruct((1, d), F32)],
        compiler_params=_params(("parallel", "arbitrary")),
    )(dy, z, w, scale)


def _grp_dw(p, dz, ng, name):
    s, d = p.shape
    gd = d // ng
    tk = _pick(s, 1024, 8)

    def body(p_ref, dz_ref, o_ref):
        k = pl.program_id(1)

        @pl.when(k == 0)
        def _():
            o_ref[...] = jnp.zeros_like(o_ref)

        o_ref[...] += lax.dot_general(p_ref[...], dz_ref[...], (((0,), (0,)), ((), ())), preferred_element_type=F32)

    blk = pl.BlockSpec((tk, gd), lambda g, k: (k, g))
    return pl.pallas_call(
        body, name=name, grid=(ng, s // tk),
        in_specs=[blk, blk], out_specs=pl.BlockSpec((None, gd, gd), lambda g, k: (g, 0, 0)),
        out_shape=jax.ShapeDtypeStruct((ng, gd, gd), F32),
        compiler_params=_params(("parallel", "arbitrary")),
    )(p, dz)


def _sigmoid(a):
    return 0.5 * jnp.tanh(0.5 * a) + 0.5


def _ffn_act(up, cw, cb, name):
    _, nq, s, fq = up.shape
    tm = _pick(s, 512, CONV_HALO)
    per = tm // CONV_HALO

    h = CONV_HALO
    rows = _pick(tm, FFN_ROWS, h)

    def body(prev_ref, a_ref, v_ref, w_ref, b_ref, o_ref, ext):
        i = pl.program_id(1)
        ext[0:h, :] = jnp.where(i > 0, prev_ref[...].astype(F32), 0.0)
        ext[h:, :] = a_ref[...].astype(F32)

        def step(c, carry):
            r0 = pl.multiple_of(c * rows, rows)
            e = ext[pl.ds(r0, rows + h), :]
            a2 = (b_ref[...] + e[h - 2:h - 2 + rows] * w_ref[0:1, :] + e[h - 1:h - 1 + rows] * w_ref[1:2, :]
                  + e[h:h + rows] * w_ref[2:3, :])
            vv = v_ref[pl.ds(r0, rows), :].astype(F32)
            o_ref[pl.ds(r0, rows), :] = (a2 * _sigmoid(a2) * vv).astype(BF)
            return carry

        lax.fori_loop(0, tm // rows, step, 0)

    return pl.pallas_call(
        body, name=name, grid=(nq, s // tm),
        in_specs=[pl.BlockSpec((None, None, CONV_HALO, fq), lambda q, i: (0, q, jnp.maximum(i * per - 1, 0), 0)),
                  pl.BlockSpec((None, None, tm, fq), lambda q, i: (0, q, i, 0)),
                  pl.BlockSpec((None, None, tm, fq), lambda q, i: (1, q, i, 0)),
                  pl.BlockSpec((None, CONV_WIDTH, fq), lambda q, i: (q, 0, 0)),
                  pl.BlockSpec((None, 1, fq), lambda q, i: (q, 0, 0))],
        out_specs=pl.BlockSpec((None, tm, fq), lambda q, i: (q, i, 0)),
        out_shape=jax.ShapeDtypeStruct((nq, s, fq), BF),
        scratch_shapes=[pltpu.VMEM((tm + CONV_HALO, fq), F32)],
        compiler_params=_params(("parallel", "parallel")),
    )(up, up, up, cw, cb)


def _ffn_act_bwd(up, dact, cw, cb, name):
    _, nq, s, fq = up.shape
    tm = _pick(s, 512, CONV_HALO)
    per = tm // CONV_HALO
    nt = s // tm
    last_halo = s // CONV_HALO - 1
    h = CONV_HALO
    te = tm + h

    rows = _pick(tm, FFN_ROWS, h)

    def body(ap_ref, a_ref, an_ref, v_ref, vn_ref, d_ref, dn_ref, w_ref, b_ref, dup_ref, dc_ref, ext_a, dap, sums):
        i = pl.program_id(1)
        ext_a[0:h, :] = jnp.where(i > 0, ap_ref[...].astype(F32), 0.0)
        ext_a[h:h + tm, :] = a_ref[...].astype(F32)
        ext_a[h + tm:, :] = an_ref[...].astype(F32)

        def pre_act(e, n):
            return (b_ref[...] + e[h - 2:h - 2 + n] * w_ref[0:1, :] + e[h - 1:h - 1 + n] * w_ref[1:2, :]
                    + e[h:h + n] * w_ref[2:3, :])

        def through_gate(a2, dd, vv):
            sig = _sigmoid(a2)
            return dd * vv * (sig * (1.0 + a2 * (1.0 - sig))), dd * (a2 * sig)

        def step1(c, carry):
            r0 = pl.multiple_of(c * rows, rows)
            a2 = pre_act(ext_a[pl.ds(r0, rows + h), :], rows)
            g, dgate = through_gate(a2, d_ref[pl.ds(r0, rows), :].astype(F32), v_ref[pl.ds(r0, rows), :].astype(F32))
            dap[pl.ds(r0, rows), :] = g
            dup_ref[1, pl.ds(r0, rows), :] = dgate.astype(BF)
            return carry

        lax.fori_loop(0, tm // rows, step1, 0)
        d_nxt = jnp.where(i < nt - 1, dn_ref[...].astype(F32), 0.0)
        g, _ = through_gate(pre_act(ext_a[tm:tm + 2 * h, :], h), d_nxt, vn_ref[...].astype(F32))
        dap[tm:, :] = g
        sums[...] = jnp.zeros_like(sums)

        def fold(t):
            acc = t[0:8]
            for k in range(8, rows, 8):
                acc = acc + t[k:k + 8]
            return acc

        def step2(c, carry):
            r0 = pl.multiple_of(c * rows, rows)
            gch = dap[pl.ds(r0, rows + h), :]
            g0 = gch[0:rows]
            dup_ref[0, pl.ds(r0, rows), :] = (gch[2:2 + rows] * w_ref[0:1, :] + gch[1:1 + rows] * w_ref[1:2, :]
                                              + g0 * w_ref[2:3, :]).astype(BF)
            e = ext_a[pl.ds(r0, rows + h), :]
            sums[0] += fold(g0 * e[h - 2:h - 2 + rows])
            sums[1] += fold(g0 * e[h - 1:h - 1 + rows])
            sums[2] += fold(g0 * e[h:h + rows])
            sums[3] += fold(g0)
            return carry

        lax.fori_loop(0, tm // rows, step2, 0)

        @pl.when(i == 0)
        def _():
            dc_ref[...] = jnp.zeros_like(dc_ref)

        for k in range(4):
            dc_ref[k:k + 1, :] += jnp.sum(sums[k], axis=0, keepdims=True)

    def cur(half):
        return pl.BlockSpec((None, None, tm, fq), lambda q, i: (half, q, i, 0))

    def nxt(half):
        return pl.BlockSpec((None, None, h, fq), lambda q, i: (half, q, jnp.minimum((i + 1) * per, last_halo), 0))

    return pl.pallas_call(
        body, name=name, grid=(nq, nt),
        in_specs=[pl.BlockSpec((None, None, h, fq), lambda q, i: (0, q, jnp.maximum(i * per - 1, 0), 0)),
                  cur(0), nxt(0), cur(1), nxt(1),
                  pl.BlockSpec((None, tm, fq), lambda q, i: (q, i, 0)),
                  pl.BlockSpec((None, h, fq), lambda q, i: (q, jnp.minimum((i + 1) * per, last_halo), 0)),
                  pl.BlockSpec((None, CONV_WIDTH, fq), lambda q, i: (q, 0, 0)),
                  pl.BlockSpec((None, 1, fq), lambda q, i: (q, 0, 0))],
        out_specs=[pl.BlockSpec((2, None, tm, fq), lambda q, i: (0, q, i, 0)),
                   pl.BlockSpec((None, 8, fq), lambda q, i: (q, 0, 0))],
        out_shape=[jax.ShapeDtypeStruct((2, nq, s, fq), BF), jax.ShapeDtypeStruct((nq, 8, fq), F32)],
        scratch_shapes=[pltpu.VMEM((tm + 2 * h, fq), F32), pltpu.VMEM((te, fq), F32), pltpu.VMEM((4, 8, fq), F32)],
        compiler_params=_params(("parallel", "arbitrary")),
    )(up, up, up, up, up, dact, dact, cw, cb)


def _band(blk, n_steps, dil, first):
    qi = lax.broadcasted_iota(jnp.int32, (blk, 2 * blk), 0) + blk
    ki = lax.broadcasted_iota(jnp.int32, (blk, 2 * blk), 1)
    delta = qi - ki
    valid = (delta >= 0) & (delta <= n_steps) & ((ki >= blk) | jnp.logical_not(first))
    return valid, (delta * dil).astype(F32)


def _branch_views(q_all, kv, g, d):
    _, dil = BRANCHES[g]
    nbr = len(BRANCHES)
    sub = q_all.shape[0] // dil
    if dil == 1:
        return (q_all, kv, kv), ((nbr, g), (2 * nbr, g), (2 * nbr, nbr + g))
    cols = [q_all[:, g * d:(g + 1) * d], kv[:, g * d:(g + 1) * d], kv[:, (nbr + g) * d:(nbr + g + 1) * d]]
    return tuple(a.reshape(sub, dil * d) for a in cols), ((1, 0), (1, 0), (1, 0))


def _attn_fwd(q_all, kv, g, slopes, d, name):
    window, dil = BRANCHES[g]
    n_steps = window // dil
    blk = max(ATTN_BLOCK, n_steps)
    s = q_all.shape[0]
    sub = s // dil
    nb = sub // blk
    assert nb * blk == sub
    nh = d // HEAD_DIM
    (qv, kview, vview), (qcol, kcol, vcol) = _branch_views(q_all, kv, g, d)
    scale = HEAD_DIM ** -0.5

    def body(q_ref, kp_ref, kc_ref, vp_ref, vc_ref, o_ref, l_ref):
        j = pl.program_id(1)
        valid, dist = _band(blk, n_steps, dil, j == 0)
        qb = q_ref[...]
        kb = jnp.concatenate([kp_ref[...], kc_ref[...]], axis=0)
        vb = jnp.concatenate([vp_ref[...], vc_ref[...]], axis=0)
        first = lax.broadcasted_iota(jnp.int32, (1, 2 * HEAD_DIM), 1) < HEAD_DIM
        for hp in range(nh // 2):
            sl = slice(2 * hp * HEAD_DIM, 2 * (hp + 1) * HEAD_DIM)
            qp, kp, vp = qb[:, sl], kb[:, sl], vb[:, sl]
            out, lse = None, None
            for half in range(2):
                sel = first if half == 0 else jnp.logical_not(first)
                sc = lax.dot_general(jnp.where(sel, qp, 0), kp, (((1,), (1,)), ((), ())),
                                     preferred_element_type=F32) * scale
                sc = jnp.where(valid, sc - float(slopes[2 * hp + half]) * dist, NEG)
                m = jnp.max(sc, axis=-1, keepdims=True)
                p = jnp.exp(sc - m)
                den = jnp.sum(p, axis=-1, keepdims=True)
                o = jnp.dot(p.astype(BF), jnp.where(sel, vp, 0), preferred_element_type=F32) / den
                lse_h = m + jnp.log(den)
                out = o if half == 0 else out + o
                lse = lse_h if half == 0 else jnp.where(first, lse, lse_h)
            o_ref[:, sl] = out.astype(BF)
            l_ref[:, sl] = jnp.broadcast_to(lse, (blk, 2 * HEAD_DIM))

    def spec(col, prev):
        if prev:
            return pl.BlockSpec((blk, d), lambda r, j: (jnp.maximum(j - 1, 0), r * col[0] + col[1]))
        return pl.BlockSpec((blk, d), lambda r, j: (j, r * col[0] + col[1]))

    ospec = pl.BlockSpec((blk, d), lambda r, j: (j, r))
    o, lse = pl.pallas_call(
        body, name=name, grid=(dil, nb),
        in_specs=[spec(qcol, False), spec(kcol, True), spec(kcol, False), spec(vcol, True), spec(vcol, False)],
        out_specs=[ospec, ospec],
        out_shape=[jax.ShapeDtypeStruct((sub, dil * d), BF), jax.ShapeDtypeStruct((sub, dil * d), F32)],
        compiler_params=_params(("parallel", "parallel")),
    )(qv, kview, kview, vview, vview)
    return o.reshape(s, d), lse.reshape(s, d)


def _attn_combine(os, lses, name):
    s, d = os[0].shape
    tm = _pick(s, 512, 8)
    nbr = len(os)

    def body(*refs):
        o_refs, l_refs = refs[:nbr], refs[nbr:2 * nbr]
        o_ref, lt_ref = refs[2 * nbr], refs[2 * nbr + 1]
        ls = [r[...] for r in l_refs]
        m = ls[0]
        for v in ls[1:]:
            m = jnp.maximum(m, v)
        tot = jnp.exp(ls[0] - m)
        for v in ls[1:]:
            tot = tot + jnp.exp(v - m)
        lt = m + jnp.log(tot)
        acc = jnp.exp(ls[0] - lt) * o_refs[0][...].astype(F32)
        for v, r in zip(ls[1:], o_refs[1:]):
            acc = acc + jnp.exp(v - lt) * r[...].astype(F32)
        o_ref[...] = acc.astype(BF)
        lt_ref[...] = lt

    return pl.pallas_call(
        body, name=name, grid=(s // tm,),
        in_specs=[_row_spec(tm, d)] * (2 * nbr), out_specs=[_row_spec(tm, d), _row_spec(tm, d)],
        out_shape=[jax.ShapeDtypeStruct((s, d), BF), jax.ShapeDtypeStruct((s, d), F32)],
        compiler_params=_params(("parallel",)),
    )(*os, *lses)


def _attn_bwd(q_all, kv, do, o, lt, g, slopes, d, name, dk_in=None, dv_in=None):
    window, dil = BRANCHES[g]
    n_steps = window // dil
    blk = max(ATTN_BLOCK, n_steps)
    s = q_all.shape[0]
    sub = s // dil
    nb = sub // blk
    nh = d // HEAD_DIM
    (qv, kview, vview), (qcol, kcol, vcol) = _branch_views(q_all, kv, g, d)
    scale = HEAD_DIM ** -0.5
    acc_in = dk_in is not None

    def body(*refs):
        q_ref, do_ref, o_ref, lt_ref, kp_ref, kc_ref, vp_ref, vc_ref = refs[:8]
        n_in = 10 if acc_in else 8
        dkin_ref, dvin_ref = (refs[8], refs[9]) if acc_in else (None, None)
        dq_ref, dk_ref, dv_ref, keep_k, keep_v, part_k, part_v = refs[n_in:n_in + 7]
        t = pl.program_id(1)

        def emit(prev_k, prev_v):
            if acc_in:
                prev_k = prev_k + dkin_ref[...].astype(F32)
                prev_v = prev_v + dvin_ref[...].astype(F32)
            dk_ref[...] = prev_k.astype(BF)
            dv_ref[...] = prev_v.astype(BF)

        @pl.when(t < nb)
        def _():
            valid, dist = _band(blk, n_steps, dil, t == 0)
            qb = q_ref[...]
            dob = do_ref[...]
            kb = jnp.concatenate([kp_ref[...], kc_ref[...]], axis=0)
            vb = jnp.concatenate([vp_ref[...], vc_ref[...]], axis=0)
            first = lax.broadcasted_iota(jnp.int32, (1, 2 * HEAD_DIM), 1) < HEAD_DIM
            for hp in range(nh // 2):
                sl = slice(2 * hp * HEAD_DIM, 2 * (hp + 1) * HEAD_DIM)
                qp, kp, vp, dop = qb[:, sl], kb[:, sl], vb[:, sl], dob[:, sl]
                op = o_ref[:, sl].astype(F32)
                ltp = lt_ref[:, sl]
                dq, dk, dv = None, None, None
                for half in range(2):
                    sel = first if half == 0 else jnp.logical_not(first)
                    qh, doh = jnp.where(sel, qp, 0), jnp.where(sel, dop, 0)
                    sc = lax.dot_general(qh, kp, (((1,), (1,)), ((), ())), preferred_element_type=F32) * scale
                    sc = sc - float(slopes[2 * hp + half]) * dist
                    lt_h = ltp[:, half * HEAD_DIM:half * HEAD_DIM + 1]
                    p = jnp.where(valid, jnp.exp(jnp.minimum(sc - lt_h, 30.0)), 0.0)
                    dlt = jnp.sum(doh.astype(F32) * op, axis=-1, keepdims=True)
                    dp = lax.dot_general(doh, vp, (((1,), (1,)), ((), ())), preferred_element_type=F32)
                    ds = (p * (dp - dlt)).astype(BF)
                    dq_h = jnp.dot(ds, jnp.where(sel, kp, 0), preferred_element_type=F32)
                    dk_h = lax.dot_general(ds, qh, (((0,), (0,)), ((), ())), preferred_element_type=F32)
                    dv_h = lax.dot_general(p.astype(BF), doh, (((0,), (0,)), ((), ())), preferred_element_type=F32)
                    dq = dq_h if half == 0 else dq + dq_h
                    dk = dk_h if half == 0 else dk + dk_h
                    dv = dv_h if half == 0 else dv + dv_h
                dq_ref[:, sl] = (dq * scale).astype(BF)
                part_k[:, sl] = dk * scale
                part_v[:, sl] = dv

            @pl.when(t > 0)
            def _():
                emit(keep_k[...] + part_k[0:blk, :], keep_v[...] + part_v[0:blk, :])

            keep_k[...] = part_k[blk:, :]
            keep_v[...] = part_v[blk:, :]

        @pl.when(t == nb)
        def _():
            emit(keep_k[...], keep_v[...])

    def qspec(col):
        return pl.BlockSpec((blk, d), lambda r, t: (jnp.minimum(t, nb - 1), r * col[0] + col[1]))

    def kspec(col, prev):
        if prev:
            return pl.BlockSpec((blk, d), lambda r, t: (jnp.maximum(jnp.minimum(t, nb - 1) - 1, 0), r * col[0] + col[1]))
        return qspec(col)

    kout = pl.BlockSpec((blk, d), lambda r, t: (jnp.maximum(t - 1, 0), r))
    one = (1, 0)
    in_specs = [qspec(qcol), qspec(one), qspec(one), qspec(one),
                kspec(kcol, True), kspec(kcol, False), kspec(vcol, True), kspec(vcol, False)]
    args = [qv, do.reshape(sub, dil * d), o.reshape(sub, dil * d), lt.reshape(sub, dil * d),
            kview, kview, vview, vview]
    if acc_in:
        in_specs += [kout, kout]
        args += [dk_in.reshape(sub, dil * d), dv_in.reshape(sub, dil * d)]
    shp = jax.ShapeDtypeStruct((sub, dil * d), BF)
    dq, dk, dv = pl.pallas_call(
        body, name=name, grid=(dil, nb + 1),
        in_specs=in_specs, out_specs=[qspec(one), kout, kout], out_shape=[shp, shp, shp],
        scratch_shapes=[pltpu.VMEM((blk, d), F32), pltpu.VMEM((blk, d), F32),
                        pltpu.VMEM((2 * blk, d), F32), pltpu.VMEM((2 * blk, d), F32)],
        compiler_params=_params(("parallel", "arbitrary")),
    )(*args)
    return dq.reshape(s, d), dk.reshape(s, d), dv.reshape(s, d)


def _adamw(parts_list, w, m, v, name):
    nl, r, c = w.shape
    assert len(parts_list) == nl
    npart = parts_list[0].shape[0]
    tr = _pick(r, 256, 16)
    c1 = 1.0 / (1.0 - ADAM_B1 ** ADAM_STEP)
    c2 = 1.0 / (1.0 - ADAM_B2 ** ADAM_STEP)

    def body(*refs):
        p_refs = refs[:nl]
        w_ref, m_ref, v_ref, g_ref, d_ref, nm_ref, nv_ref = refs[nl:]
        layer = pl.program_id(0)
        for idx in range(nl):
            @pl.when(layer == idx)
            def _(p_ref=p_refs[idx]):
                g = p_ref[0].astype(F32)
                for k in range(1, npart):
                    g = g + p_ref[k].astype(F32)
                nm = ADAM_B1 * m_ref[...] + (1.0 - ADAM_B1) * g
                nv = ADAM_B2 * v_ref[...] + (1.0 - ADAM_B2) * (g * g)
                g_ref[...] = g
                nm_ref[...] = nm
                nv_ref[...] = nv
                d_ref[...] = -ADAM_LR * ((nm * c1) / (jnp.sqrt(nv * c2) + ADAM_EPS) + ADAM_WD * w_ref[...])

    def part_spec(idx):
        return pl.BlockSpec((npart, tr, c), lambda l, i: (0, jnp.where(l == idx, i, 0), 0))

    blk = pl.BlockSpec((None, tr, c), lambda l, i: (l, i, 0))
    shp = jax.ShapeDtypeStruct((nl, r, c), F32)
    return pl.pallas_call(
        body, name=name, grid=(nl, r // tr),
        in_specs=[part_spec(idx) for idx in range(nl)] + [blk, blk, blk],
        out_specs=[blk, blk, blk, blk], out_shape=[shp, shp, shp, shp],
        compiler_params=_params(("parallel", "parallel")),
    )(*parts_list, w, m, v)


def _full_from_slots(slots, shard_shape, axis):
    a = slots.reshape((N_DEV,) + tuple(shard_shape))
    a = jnp.moveaxis(a, 0, axis)
    full = list(shard_shape)
    full[axis] *= N_DEV
    return a.reshape(full)


def kernel(x, c, ada_w, ada_b, norm1_g, norm2_g, pool_w_in, pool_w_grp, pool_scale, pool_w_out, kv_norm_g, kv_ada_w, kv_ada_b, w_kv, attn_w_q, attn_w_o, ffn_w_up, ffn_conv_w, ffn_conv_b, ffn_w_down, final_g, loss_target, m_ada_w, m_ada_b, m_norm1_g, m_norm2_g, m_pool_w_in, m_pool_w_grp, m_pool_scale, m_pool_w_out, m_kv_norm_g, m_kv_ada_w, m_kv_ada_b, m_w_kv, m_attn_w_q, m_attn_w_o, m_ffn_w_up, m_ffn_conv_w, m_ffn_conv_b, m_ffn_w_down, m_final_g, v_ada_w, v_ada_b, v_norm1_g, v_norm2_g, v_pool_w_in, v_pool_w_grp, v_pool_scale, v_pool_w_out, v_kv_norm_g, v_kv_ada_w, v_kv_ada_b, v_w_kv, v_attn_w_q, v_attn_w_o, v_ffn_w_up, v_ffn_conv_w, v_ffn_conv_b, v_ffn_w_down, v_final_g):
    weights = dict(ada_w=ada_w, ada_b=ada_b, norm1_g=norm1_g, norm2_g=norm2_g, pool_w_in=pool_w_in,
                   pool_w_grp=pool_w_grp, pool_scale=pool_scale, pool_w_out=pool_w_out, kv_norm_g=kv_norm_g,
                   kv_ada_w=kv_ada_w, kv_ada_b=kv_ada_b, w_kv=w_kv, attn_w_q=attn_w_q, attn_w_o=attn_w_o,
                   ffn_w_up=ffn_w_up, ffn_conv_w=ffn_conv_w, ffn_conv_b=ffn_conv_b, ffn_w_down=ffn_w_down,
                   final_g=final_g)
    mom1 = dict(ada_w=m_ada_w, ada_b=m_ada_b, norm1_g=m_norm1_g, norm2_g=m_norm2_g, pool_w_in=m_pool_w_in,
                pool_w_grp=m_pool_w_grp, pool_scale=m_pool_scale, pool_w_out=m_pool_w_out, kv_norm_g=m_kv_norm_g,
                kv_ada_w=m_kv_ada_w, kv_ada_b=m_kv_ada_b, w_kv=m_w_kv, attn_w_q=m_attn_w_q, attn_w_o=m_attn_w_o,
                ffn_w_up=m_ffn_w_up, ffn_conv_w=m_ffn_conv_w, ffn_conv_b=m_ffn_conv_b, ffn_w_down=m_ffn_w_down,
                final_g=m_final_g)
    mom2 = dict(ada_w=v_ada_w, ada_b=v_ada_b, norm1_g=v_norm1_g, norm2_g=v_norm2_g, pool_w_in=v_pool_w_in,
                pool_w_grp=v_pool_w_grp, pool_scale=v_pool_scale, pool_w_out=v_pool_w_out, kv_norm_g=v_kv_norm_g,
                kv_ada_w=v_kv_ada_w, kv_ada_b=v_kv_ada_b, w_kv=v_w_kv, attn_w_q=v_attn_w_q, attn_w_o=v_attn_w_o,
                ffn_w_up=v_ffn_w_up, ffn_conv_w=v_ffn_conv_w, ffn_conv_b=v_ffn_conv_b, ffn_w_down=v_ffn_w_down,
                final_g=v_final_g)
    order = list(weights)

    seq, d = x.shape[1], x.shape[2]
    depth = ada_w.shape[0]
    n_pool = pool_w_in.shape[0]
    f = ffn_conv_b.shape[1]
    nbr = len(BRANCHES)
    nh = d // HEAD_DIM
    slopes = _alibi_slopes(nbr * nh).reshape(nbr, nh)
    me = 4 * lax.axis_index("x") + 2 * lax.axis_index("y") + lax.axis_index("c")
    xs = x[0]
    tgt = loss_target[0]

    def start_gathers(after):
        keys, groups = [], []
        for l in range(depth):
            if l < n_pool:
                parts = [("mixer", [pool_w_in[l], pool_w_grp[l].reshape(-1, pool_w_grp.shape[-1]), pool_w_out[l]])]
            else:
                j = l - n_pool
                parts = [("mixer", [attn_w_q[j]] + ([w_kv] if j == 0 else [])), ("out", [attn_w_o[j]])]
            for part, srcs in parts + [("ffn", [ffn_w_up[l], ffn_w_down[l]])]:
                keys.append((l, part))
                groups.append([a.astype(BF) for a in srcs])
        handles, _ = _comm_start(groups, "gather_start", scatter=False, after=after)
        return dict(zip(keys, handles))

    cond = c * (1.0 / (1.0 + jnp.exp(-c)))
    small_in = jnp.concatenate([cond.reshape(-1), ffn_conv_w.reshape(-1), pool_scale.reshape(-1)])
    n_small_in = small_in.shape[0]
    gath = _all_gather(_to_rows(small_in), "gather_small").reshape(N_DEV, -1)[:, :n_small_in]
    cond_all = gath[:, :d]
    o1 = d + ffn_conv_w.size
    conv_w_full = _full_from_slots(gath[:, d:o1], ffn_conv_w.shape, 2)
    pool_scale_full = _full_from_slots(gath[:, o1:], pool_scale.shape, 1)
    cond16 = jnp.concatenate([cond_all, jnp.zeros_like(cond_all)], axis=0)

    mod_part = _ada_fwd(cond16, ada_w, "ada_fwd")[:, :N_DEV]
    kv_part = _ada_fwd(cond16, kv_ada_w[None], "kv_ada_fwd")[0, :N_DEV]
    n_mod = depth * mod_part.shape[2] + kv_part.shape[1]
    send = jnp.concatenate([jnp.moveaxis(mod_part, 1, 0).reshape(N_DEV, -1), kv_part], axis=1)
    send_rows = jax.vmap(_to_rows)(send)
    got = _all_to_all(send_rows, "exchange_mod").reshape(N_DEV, -1)[:, :n_mod]
    ncol = mod_part.shape[2]
    mods = []
    for l in range(depth):
        row = got[:, l * ncol:(l + 1) * ncol].reshape(1, -1) + ada_b[l][None]
        mods.append([row[:, k * d:(k + 1) * d] for k in range(6)])
    kv_row = got[:, depth * ncol:].reshape(1, -1) + kv_ada_b[None]
    kv_shift, kv_scale = kv_row[:, :d], kv_row[:, d:]
    gather_handles = start_gathers(got)

    def vec(a):
        return a.reshape(1, -1)

    nq = 4
    fq = f // nq
    ng = len(POOL_WINDOWS)
    cw_slots = jnp.moveaxis(conv_w_full.reshape(depth, CONV_WIDTH, nq, fq), 2, 1)
    cb_slots = ffn_conv_b.reshape(depth, nq, 1, fq)

    saved = []
    xcur = xs
    kvs = None
    hkv = None
    x_kv = None
    w_kv_slots = None
    for l in range(depth):
        sh1, sc1, g1, sh2, sc2, g2 = mods[l]
        st = dict(x=xcur)
        h = _norm_mod(xcur, vec(norm1_g[l]), sc1, sh1, f"norm1_{l}")
        st["h"] = h
        gw = _comm_wait(gather_handles[l, "mixer"], h, f"gather_wait_mixer_{l}")
        if l < n_pool:
            w_in = gw[0].reshape(d, d)
            w_grp = jnp.moveaxis(gw[1].reshape(N_DEV, ng, -1, d // ng), 0, 1).reshape(ng, d // ng, d // ng)
            w_out = gw[2].reshape(d, d)
            u = _mm(h, w_in, f"pool_in_{l}")
            pooled = _pool_fwd(u, f"pool_fwd_{l}")
            z, y = _grp_fwd(pooled, w_grp, vec(pool_scale_full[l]), f"grp_fwd_{l}")
            mix, x1 = _mm(y, w_out, f"pool_out_{l}", gate=g1, resid=xcur, pre_dtype=BF)
            st.update(pooled=pooled, z=z, y=y, w_in=w_in, w_grp=w_grp, w_out=w_out)
        else:
            j = l - n_pool
            w_q_slots = gw[0]
            if j == 0:
                w_kv_slots = gw[1]
                x_kv = xcur
                hkv = _norm_mod(xcur, vec(kv_norm_g), kv_scale, kv_shift, "norm_kv")
                kvs = _mm(hkv, w_kv_slots, "kv_proj", bf="qkn", out_dtype=BF)
            q = _mm(h, w_q_slots, f"q_proj_{l}", bf="qkn", out_dtype=BF)
            outs, lses = [], []
            for g in range(nbr):
                og, lg = _attn_fwd(q, kvs, g, slopes[g], d, f"attn_fwd_{l}_{g}")
                outs.append(og)
                lses.append(lg)
            o, lt = _attn_combine(outs, lses, f"attn_mix_{l}")
            w_o = _comm_wait(gather_handles[l, "out"], o, f"gather_wait_out_{l}")[0].reshape(d, d)
            mix, x1 = _mm(o, w_o, f"attn_out_{l}", gate=g1, resid=xcur, pre_dtype=BF)
            st.update(q=q, o=o, lt=lt, w_q=w_q_slots, w_o=w_o)
        h2 = _norm_mod(x1, vec(norm2_g[l]), sc2, sh2, f"norm2_{l}")
        w_up_slots, w_down = _comm_wait(gather_handles[l, "ffn"], h2, f"gather_wait_ffn_{l}")
        w_down = w_down.reshape(f, d)
        st.update(w_up=w_up_slots, w_down=w_down)
        up = _mm(h2, w_up_slots, f"ffn_up_{l}", bf="qkn", of="qmn", out_dtype=BF).reshape(2, nq, seq, fq)
        act = _ffn_act(up, cw_slots[l], cb_slots[l], f"ffn_act_{l}")
        ffo, x2 = _mm(act, w_down, f"ffn_down_{l}", af="qmk", gate=g2, resid=x1, pre_dtype=BF)
        st.update(mix=mix, x1=x1, h2=h2, up=up, act=act, ffo=ffo)
        saved.append(st)
        xcur = x2

    dx, loss_blk, d_final_g = _final_loss(xcur, tgt, vec(final_g), "final_loss")
    loss = lax.psum(loss_blk[0, 0], ("x", "y", "c"))

    d_mod = [None] * depth
    d_n1 = [None] * depth
    d_n2 = [None] * depth
    d_conv = [None] * depth
    d_pscale = [None] * n_pool
    dk_acc = [None] * nbr
    dv_acc = [None] * nbr
    ffn_handles = [None] * depth
    mixer_handles = [None] * depth
    tok = 0.0
    for l in reversed(range(depth)):
        sh1, sc1, g1, sh2, sc2, g2 = mods[l]
        st = saved[l]
        dffo, dg2 = _gate_bwd(dx, st["ffo"], g2 + tok, f"gate2_bwd_{l}")
        g_down = _mm(st["act"], dffo, f"ffn_down_dw_{l}", af="qkm", out_dtype=BF)
        dact = _mm(dffo, st["w_down"], f"ffn_down_dx_{l}", bf="nk", of="qmn", tn=fq, out_dtype=BF)
        dup, dc = _ffn_act_bwd(st["up"], dact, cw_slots[l], cb_slots[l], f"ffn_act_bwd_{l}")
        dup = dup.reshape(N_DEV, seq, -1)
        d_conv[l] = dc
        g_up = _mm(st["h2"], dup, f"ffn_up_dw_{l}", af="km", bf="qkn", of="qmn", out_dtype=BF)
        dh2 = _mm(dup, st["w_up"], f"ffn_up_dx_{l}", af="qmk", bf="qnk")
        (ffn_handles[l],), tok = _comm_start([[g_up, g_down.reshape(N_DEV, -1, d)]], f"exchange_start_ffn_{l}",
                                             scatter=True)
        dx1, dsh2, dw2 = _norm_mod_bwd(st["x1"], dh2, vec(norm2_g[l]), sc2 + tok, dx, f"norm2_bwd_{l}")
        d_n2[l] = dw2 * (1.0 + sc2)
        dsc2 = dw2 * vec(norm2_g[l])

        dmix, dg1 = _gate_bwd(dx1, st["mix"], g1, f"gate1_bwd_{l}")
        if l < n_pool:
            g_out = _mm(st["y"], dmix, f"pool_out_dw_{l}", af="km", out_dtype=BF)
            dy = _mm(dmix, st["w_out"], f"pool_out_dx_{l}", bf="nk", out_dtype=BF)
            dz, dpool, dps = _grp_bwd(dy, st["z"], st["w_grp"], vec(pool_scale_full[l]), f"grp_bwd_{l}")
            d_pscale[l] = dps
            g_grp = _grp_dw(st["pooled"], dz, ng, f"grp_dw_{l}")
            du = _pool_bwd(dpool, f"pool_bwd_{l}")
            g_in = _mm(st["h"], du, f"pool_in_dw_{l}", af="km", out_dtype=BF)
            dh = _mm(du, st["w_in"], f"pool_in_dx_{l}", bf="nk")
            g_grp_slots = jnp.moveaxis(g_grp.reshape(ng, N_DEV, -1, d // ng), 1, 0).reshape(N_DEV, -1, d // ng)
            send = [g_in.reshape(N_DEV, -1, d), g_grp_slots.astype(BF), g_out.reshape(N_DEV, -1, d)]
        else:
            j = l - n_pool
            g_o = _mm(st["o"], dmix, f"attn_out_dw_{l}", af="km", out_dtype=BF)
            do = _mm(dmix, st["w_o"], f"attn_out_dx_{l}", bf="nk", out_dtype=BF)
            dqs = []
            for g in range(nbr):
                dq_g, dk_g, dv_g = _attn_bwd(st["q"], kvs, do, st["o"], st["lt"], g, slopes[g], d,
                                             f"attn_bwd_{l}_{g}", dk_in=dk_acc[g], dv_in=dv_acc[g])
                dqs.append(dq_g)
                dk_acc[g], dv_acc[g] = dk_g, dv_g
            dq = jnp.concatenate(dqs, axis=1)
            nqc = st["w_q"].shape[2]
            g_q = _mm(st["h"], dq, f"q_proj_dw_{l}", af="km", of="qmn", tn=nqc, out_dtype=BF)
            dh = _mm(dq, st["w_q"], f"q_proj_dx_{l}", bf="qnk")
            send = [g_q, g_o.reshape(N_DEV, -1, d)]
        dx0, dsh1, dw1 = _norm_mod_bwd(st["x"], dh, vec(norm1_g[l]), sc1, dx1, f"norm1_bwd_{l}")
        d_n1[l] = dw1 * (1.0 + sc1)
        dsc1 = dw1 * vec(norm1_g[l])
        d_mod[l] = jnp.concatenate([dsh1, dsc1, dg1, dsh2, dsc2, dg2], axis=1)
        dx = dx0
        if l == n_pool:
            dkv = jnp.concatenate(dk_acc + dv_acc, axis=1)
            nkc = w_kv_slots.shape[2]
            g_kv = _mm(hkv, dkv, "kv_proj_dw", af="km", of="qmn", tn=nkc, out_dtype=BF)
            dhkv = _mm(dkv, w_kv_slots, "kv_proj_dx", bf="qnk")
            dx, dsh_kv, dw_kv = _norm_mod_bwd(x_kv, dhkv, vec(kv_norm_g), kv_scale, dx, "norm_kv_bwd")
            d_kv_norm = dw_kv * (1.0 + kv_scale)
            d_kv_mod = jnp.concatenate([dsh_kv, dw_kv * vec(kv_norm_g)], axis=1)
            send.append(g_kv)
        (mixer_handles[l],), tok = _comm_start([send], f"exchange_start_mixer_{l}", scatter=True)
    grad_x = dx[None]
    d_final_g = d_final_g + tok

    small = [jnp.concatenate(d_mod, axis=1).reshape(-1), d_kv_mod.reshape(-1),
             jnp.concatenate(d_n1, axis=0).reshape(-1), jnp.concatenate(d_n2, axis=0).reshape(-1),
             d_kv_norm.reshape(-1), d_final_g.reshape(-1),
             jnp.stack([dcl[:, 3, :] for dcl in d_conv]).reshape(-1),
             jnp.stack([jnp.moveaxis(dcl[:, 0:CONV_WIDTH, :], 0, 1) for dcl in d_conv]).reshape(-1),
             jnp.concatenate(d_pscale, axis=0).reshape(-1)]
    sizes = [a.shape[0] for a in small]
    small_rows = _to_rows(jnp.concatenate(small))
    small_all = _all_gather(small_rows, "gather_small_grads")
    dmod_all = small_all.reshape(N_DEV, -1)[:, :sizes[0] + sizes[1]]

    dmod16 = jnp.concatenate([dmod_all, jnp.zeros_like(dmod_all)], axis=0)
    dm = dmod16[:, :sizes[0]].reshape(16, depth, N_DEV, ncol)
    dm_mine = lax.dynamic_index_in_dim(dm, me, axis=2, keepdims=False)
    g_ada_w = _ada_bwd(cond16, jnp.moveaxis(dm_mine, 0, 1), "ada_bwd")
    nkv = kv_part.shape[1]
    dkm = dmod16[:, sizes[0]:].reshape(16, N_DEV, nkv)
    dkm_mine = lax.dynamic_index_in_dim(dkm, me, axis=1, keepdims=False)
    g_kv_ada_w = _ada_bwd(cond16, dkm_mine[None], "kv_ada_bwd")

    res = {}

    def update(n, parts_list, shape3):
        w3, m3, v3 = (a[n].reshape(shape3) for a in (weights, mom1, mom2))
        outs = _adamw(parts_list, w3, m3, v3, f"adamw_{n}")
        res[n] = [a.reshape(weights[n].shape) for a in outs]

    update("ada_w", [g_ada_w[l][None] for l in range(depth)], ada_w.shape)
    update("kv_ada_w", [g_kv_ada_w], (1,) + kv_ada_w.shape)
    after = res["ada_w"][0]
    parts_ffn = [_comm_wait(ffn_handles[l], after, f"exchange_wait_ffn_{l}") for l in reversed(range(depth))][::-1]
    parts = [_comm_wait(mixer_handles[l], after, f"exchange_wait_mixer_{l}") for l in reversed(range(depth))][::-1]
    pool_layers, attn_layers = range(n_pool), range(n_pool, depth)
    update("ffn_w_up", [parts_ffn[l][0] for l in range(depth)], ffn_w_up.shape)
    update("ffn_w_down", [parts_ffn[l][1] for l in range(depth)], ffn_w_down.shape)
    update("attn_w_q", [parts[l][0] for l in attn_layers], attn_w_q.shape)
    update("attn_w_o", [parts[l][1] for l in attn_layers], (depth - n_pool, -1, d))
    update("w_kv", [parts[n_pool][2]], (1,) + w_kv.shape)
    update("pool_w_in", [parts[l][0] for l in pool_layers], (n_pool, -1, d))
    update("pool_w_grp", [parts[l][1] for l in pool_layers], (n_pool, -1, d // ng))
    update("pool_w_out", [parts[l][2] for l in pool_layers], (n_pool, -1, d))

    tot = small_all.reshape(N_DEV, -1)
    offs = np.cumsum([0] + sizes)
    seg = {k: (int(offs[i]), int(offs[i + 1])) for i, k in enumerate(
        ["mod", "kv_mod", "n1", "n2", "kv_norm", "final", "conv_b", "conv_w", "pscale"])}

    def rows_of(a, b):
        return tot[:, a:b]

    nf8 = f // N_DEV
    conv_w_parts = lax.dynamic_slice_in_dim(
        rows_of(*seg["conv_w"]).reshape(N_DEV, depth, CONV_WIDTH, N_DEV, nf8), me, 1, axis=3).reshape(N_DEV, -1)
    nd8 = d // N_DEV
    pscale_parts = lax.dynamic_slice_in_dim(
        rows_of(*seg["pscale"]).reshape(N_DEV, n_pool, N_DEV, nd8), me, 1, axis=2).reshape(N_DEV, -1)
    small_names = ["ada_b", "norm1_g", "norm2_g", "pool_scale", "kv_norm_g", "kv_ada_b", "ffn_conv_w",
                   "ffn_conv_b", "final_g"]
    small_parts = [rows_of(*seg["mod"]), rows_of(*seg["n1"]), rows_of(*seg["n2"]), pscale_parts,
                   rows_of(*seg["kv_norm"]), rows_of(*seg["kv_mod"]), conv_w_parts, rows_of(*seg["conv_b"]),
                   rows_of(*seg["final"])]
    sp = jnp.concatenate(small_parts, axis=1)
    n_sp = sp.shape[1]
    sp_rows = jax.vmap(_to_rows)(sp)

    def packed(src):
        return _to_rows(jnp.concatenate([src[n].reshape(-1) for n in small_names]))[None]

    outs = _adamw([sp_rows], packed(weights), packed(mom1), packed(mom2), "adamw_small")
    outs = [a.reshape(-1)[:n_sp] for a in outs]
    off = 0
    for n in small_names:
        size = weights[n].size
        res[n] = [a[off:off + size].reshape(weights[n].shape) for a in outs]
        off += size

    grads = [res[n][0] for n in order]
    deltas = [res[n][1] for n in order]
    new_m = [res[n][2] for n in order]
    new_v = [res[n][3] for n in order]
    return (loss, grad_x, *grads, *deltas, *new_m, *new_v)
```

```python
import math

import numpy as np
import jax
import jax.numpy as jnp
from jax import lax
from jax.experimental import pallas as pl
from jax.experimental.pallas import tpu as pltpu

F32 = jnp.float32
BF = jnp.bfloat16

POOL_WINDOWS = (2, 4, 8, 16)
BRANCHES = ((128, 1), (512, 4), (2048, 16))
HEAD_DIM = 64
ATTN_BLOCK = 128
CONV_WIDTH = 3
EPS = 1e-6
ADAM_LR = 0.001
ADAM_B1 = 0.9
ADAM_B2 = 0.999
ADAM_EPS = 1e-08
ADAM_WD = 0.01
ADAM_STEP = 10

N_DEV = 8
LANES = 128
POOL_HALO = 16
CONV_HALO = 8
FFN_ROWS = 16
VMEM_LIMIT = 48 * 1024 * 1024
MM_TILE = 1024
NEG = -1e30

MESH = pl.DeviceIdType.MESH
ANY = pl.BlockSpec(memory_space=pl.ANY)


def _params(sem=None):
    if sem is None:
        return pltpu.CompilerParams(vmem_limit_bytes=VMEM_LIMIT)
    return pltpu.CompilerParams(dimension_semantics=sem, vmem_limit_bytes=VMEM_LIMIT)


def _pick(dim, pref, mult=LANES):
    if dim <= pref:
        return dim
    t = (pref // mult) * mult
    while t >= mult:
        if dim % t == 0:
            return t
        t -= mult
    return dim


def _alibi_slopes(n):
    def pow2(m):
        start = 2.0 ** (-(2.0 ** -(math.log2(m) - 3)))
        return [start ** (i + 1) for i in range(m)]
    if math.log2(n).is_integer():
        s = pow2(n)
    else:
        c = 2 ** math.floor(math.log2(n))
        s = pow2(c) + pow2(2 * c)[0::2][: n - c]
    s = np.asarray(s, dtype=np.float32)
    return -np.sort(-s)


def _my_place():
    return lax.axis_index("x"), lax.axis_index("y"), lax.axis_index("c")


def _all_gather_many(xs, name):
    n = len(xs)

    def body(*refs):
        x_refs, out_refs = refs[:n], refs[n:2 * n]
        send_sems, recv_sems, local_sems = refs[2 * n:]
        xi, yi, ci = _my_place()
        me, sibling = (xi, yi, ci), (xi, yi, 1 - ci)
        chips = [(1 - xi, yi), (xi, 1 - yi), (1 - xi, 1 - yi)]

        def slot(a, px, py, pc):
            return out_refs[a].at[4 * px + 2 * py + pc]

        def copy(a, k, block, to, src=None):
            return pltpu.make_async_remote_copy(
                src_ref=slot(a, *block) if src is None else src, dst_ref=slot(a, *block),
                send_sem=send_sems.at[7 * a + k], recv_sem=recv_sems.at[7 * a + k],
                device_id=to, device_id_type=MESH)

        mine = [pltpu.make_async_copy(x_refs[a], slot(a, *me), local_sems.at[a]) for a in range(n)]
        for cp in mine:
            cp.start()
        sent = []
        for a in range(n):
            first = [copy(a, 0, me, sibling, src=x_refs[a])]
            first += [copy(a, 1 + j, me, (*chip, ci), src=x_refs[a]) for j, chip in enumerate(chips)]
            for cp in first:
                cp.start()
            sent += first
        for a in range(n):
            for j, chip in enumerate(chips):
                copy(a, 1 + j, (*chip, ci), me).wait_recv()
                fwd = copy(a, 4 + j, (*chip, ci), sibling)
                fwd.start()
                sent.append(fwd)
        for a in range(n):
            copy(a, 0, sibling, me).wait_recv()
            for j, chip in enumerate(chips):
                copy(a, 4 + j, (*chip, 1 - ci), me).wait_recv()
        for cp in sent:
            cp.wait_send()
        for cp in mine:
            cp.wait()

    return pl.pallas_call(
        body, name=name,
        out_shape=[jax.ShapeDtypeStruct((N_DEV,) + x.shape, x.dtype) for x in xs],
        in_specs=[ANY] * n, out_specs=[ANY] * n,
        scratch_shapes=[pltpu.SemaphoreType.DMA((7 * n,)), pltpu.SemaphoreType.DMA((7 * n,)),
                        pltpu.SemaphoreType.DMA((n,))],
    )(*xs)


def _all_gather(x, name):
    return _all_gather_many([x], name)[0]


def _all_to_all_many(xs, name):
    n = len(xs)

    def body(*refs):
        x_refs, out_refs = refs[:n], refs[n:2 * n]
        send_sems, recv_sems, local_sems = refs[2 * n:]
        xi, yi, ci = _my_place()
        me = 4 * xi + 2 * yi + ci
        mine = [pltpu.make_async_copy(x_refs[a].at[me], out_refs[a].at[me], local_sems.at[a]) for a in range(n)]
        for cp in mine:
            cp.start()
        copies = []
        for a in range(n):
            for k in range(1, N_DEV):
                px = 1 - xi if k & 4 else xi
                py = 1 - yi if k & 2 else yi
                pc = 1 - ci if k & 1 else ci
                peer = 4 * px + 2 * py + pc
                cp = pltpu.make_async_remote_copy(
                    src_ref=x_refs[a].at[peer], dst_ref=out_refs[a].at[me],
                    send_sem=send_sems.at[7 * a + k - 1], recv_sem=recv_sems.at[7 * a + k - 1],
                    device_id=(px, py, pc), device_id_type=MESH)
                cp.start()
                copies.append(cp)
        for cp in copies:
            cp.wait()
        for cp in mine:
            cp.wait()

    return pl.pallas_call(
        body, name=name,
        out_shape=[jax.ShapeDtypeStruct(x.shape, x.dtype) for x in xs],
        in_specs=[ANY] * n, out_specs=[ANY] * n,
        scratch_shapes=[pltpu.SemaphoreType.DMA((7 * n,)), pltpu.SemaphoreType.DMA((7 * n,)),
                        pltpu.SemaphoreType.DMA((n,))],
    )(*xs)


def _all_to_all(x, name):
    return _all_to_all_many([x], name)[0]


HBM = pl.BlockSpec(memory_space=pltpu.HBM)
SEM = pl.BlockSpec(memory_space=pltpu.SEMAPHORE)
EFFECT = pltpu.SideEffectType.DATAFLOW_SIDE_EFFECTING


def _peer(k, xi, yi, ci):
    px = 1 - xi if k & 4 else xi
    py = 1 - yi if k & 2 else yi
    pc = 1 - ci if k & 1 else ci
    return (px, py, pc), 4 * px + 2 * py + pc


def _split_copies(x_refs, land_refs, send_sem, recv_sem, scatter):
    xi, yi, ci = _my_place()
    me = 4 * xi + 2 * yi + ci
    copies = []
    for p, (x_ref, land_ref) in enumerate(zip(x_refs, land_refs)):
        for k in range(1, N_DEV):
            place, peer = _peer(k, xi, yi, ci)
            copies.append(pltpu.make_async_remote_copy(
                src_ref=x_ref.at[peer] if scatter else x_ref, dst_ref=land_ref.at[me],
                send_sem=send_sem.at[7 * p + k - 1], recv_sem=recv_sem.at[7 * p + k - 1],
                device_id=place, device_id_type=MESH))
    return copies


def _comm_start(groups, name, scatter, after=None):
    sizes = [len(g) for g in groups]
    xs = [x for g in groups for x in g]
    n, ng = len(xs), len(groups)
    lands = [lax.empty(x.shape if scatter else (N_DEV,) + x.shape, x.dtype) for x in xs]
    starts = np.cumsum([0] + sizes)
    n_in = 2 * n + (after is not None)

    def body(*refs):
        x_refs, land_refs = refs[:n], refs[n:2 * n]
        send_sems, recv_sems = refs[n_in:n_in + ng], refs[n_in + ng:n_in + 2 * ng]
        token = refs[n_in + 2 * ng + 2 * n]
        for gi in range(ng):
            lo, hi = int(starts[gi]), int(starts[gi + 1])
            for cp in _split_copies(x_refs[lo:hi], land_refs[lo:hi], send_sems[gi], recv_sems[gi], scatter):
                cp.start()
        token[...] = jnp.zeros_like(token)

    sem_shapes = [pltpu.SemaphoreType.DMA((7 * m,)) for m in sizes]
    thru = [pltpu.HBM(a.shape, a.dtype) for a in xs + lands]
    res = pl.pallas_call(
        body, name=name,
        out_shape=sem_shapes + sem_shapes + thru + [jax.ShapeDtypeStruct((8, LANES), F32)],
        in_specs=[HBM] * n_in,
        out_specs=[SEM] * (2 * ng) + [HBM] * (2 * n) + [pl.BlockSpec(memory_space=pltpu.VMEM)],
        input_output_aliases={i: 2 * ng + i for i in range(2 * n)},
        compiler_params=pltpu.CompilerParams(has_side_effects=EFFECT),
    )(*[pltpu.with_memory_space_constraint(a, pltpu.HBM) for a in xs + lands + ([] if after is None else [after])])
    send_sems, recv_sems = res[:ng], res[ng:2 * ng]
    x_thru, land_thru = res[2 * ng:2 * ng + n], res[2 * ng + n:2 * ng + 2 * n]
    handles = []
    for gi in range(ng):
        lo, hi = int(starts[gi]), int(starts[gi + 1])
        handles.append((send_sems[gi], recv_sems[gi], list(x_thru[lo:hi]), list(land_thru[lo:hi]), scatter))
    return handles, res[-1][0, 0]


def _comm_wait(handle, after, name):
    send_sem, recv_sem, x_thru, land_thru, scatter = handle
    m = len(x_thru)

    blocks = [a.shape[1:] if scatter else a.shape for a in x_thru]

    def body(*refs):
        x_refs, land_refs = refs[:m], refs[m:2 * m]
        local_sems, stage = refs[4 * m + 3], refs[4 * m + 4:]
        xi, yi, ci = _my_place()
        me = 4 * xi + 2 * yi + ci
        load = [pltpu.make_async_copy(x_refs[p].at[me] if scatter else x_refs[p], stage[p], local_sems.at[2 * p])
                for p in range(m)]
        store = [pltpu.make_async_copy(stage[p], land_refs[p].at[me], local_sems.at[2 * p + 1]) for p in range(m)]
        for cp in load:
            cp.start()
        for p in range(m):
            load[p].wait()
            store[p].start()
        for cp in _split_copies(x_refs, land_refs, refs[2 * m], refs[2 * m + 1], scatter):
            cp.wait_send()
            cp.wait_recv()
        for cp in store:
            cp.wait()

    res = pl.pallas_call(
        body, name=name,
        out_shape=[pltpu.HBM(a.shape, a.dtype) for a in x_thru + land_thru],
        in_specs=[HBM] * (2 * m) + [SEM, SEM, ANY], out_specs=[HBM] * (2 * m),
        input_output_aliases={i: i for i in range(2 * m)},
        scratch_shapes=[pltpu.SemaphoreType.DMA((2 * m,))] + [pltpu.VMEM(b, a.dtype) for b, a in zip(blocks, x_thru)],
        compiler_params=pltpu.CompilerParams(has_side_effects=EFFECT),
    )(*x_thru, *land_thru, send_sem, recv_sem, after)
    return list(res[m:])


def _to_rows(vec):
    n = vec.shape[0]
    unit = 8 * LANES
    pad = (-n) % unit
    if pad:
        vec = jnp.concatenate([vec, jnp.zeros((pad,), vec.dtype)])
    return vec.reshape(-1, LANES)


def _mm(a, b, name, *, af="mk", bf="kn", of="mn", out_dtype=F32, tm=None, tn=None, tk=None,
        gate=None, resid=None, pre_dtype=None):
    if af == "mk":
        m, kk = a.shape
    elif af == "km":
        kk, m = a.shape
    elif af == "qmk":
        qa, m, tk = a.shape
        kk = qa * tk
    else:
        qa, kk, tm = a.shape
        m = qa * tm
    if bf == "kn":
        k2, n = b.shape
    elif bf == "nk":
        n, k2 = b.shape
    elif bf == "qkn":
        qb, k2, tn = b.shape
        n = qb * tn
    else:
        qb, n, tkb = b.shape
        k2 = qb * tkb
        assert af != "qmk" or tkb == tk
        tk = tkb
    assert kk == k2, (name, a.shape, b.shape, af, bf)
    tm = _pick(m, MM_TILE) if tm is None else tm
    tn = _pick(n, MM_TILE) if tn is None else tn
    tk = _pick(kk, MM_TILE) if tk is None else tk
    assert m % tm == 0 and n % tn == 0 and kk % tk == 0, (name, m, n, kk, tm, tn, tk)
    nk = kk // tk
    a_spec = {"mk": pl.BlockSpec((tm, tk), lambda i, j, k: (i, k)),
              "km": pl.BlockSpec((tk, tm), lambda i, j, k: (k, i)),
              "qmk": pl.BlockSpec((None, tm, tk), lambda i, j, k: (k, i, 0)),
              "qkm": pl.BlockSpec((None, tk, tm), lambda i, j, k: (i, k, 0))}[af]
    b_spec = {"kn": pl.BlockSpec((tk, tn), lambda i, j, k: (k, j)),
              "nk": pl.BlockSpec((tn, tk), lambda i, j, k: (j, k)),
              "qkn": pl.BlockSpec((None, tk, tn), lambda i, j, k: (j, k, 0)),
              "qnk": pl.BlockSpec((None, tn, tk), lambda i, j, k: (k, j, 0))}[bf]
    dims = (((1 if af in ("mk", "qmk") else 0,), (0 if bf in ("kn", "qkn") else 1,)), ((), ()))
    in_specs, args = [a_spec, b_spec], [a, b]
    if gate is not None:
        assert of == "mn"
        in_specs.append(pl.BlockSpec((1, tn), lambda i, j, k: (0, j)))
        args.append(gate)
    if resid is not None:
        assert of == "mn"
        in_specs.append(pl.BlockSpec((tm, tn), lambda i, j, k: (i, j)))
        args.append(resid)
    if of == "mn":
        o_spec, o_shape = pl.BlockSpec((tm, tn), lambda i, j, k: (i, j)), (m, n)
    else:
        o_spec, o_shape = pl.BlockSpec((None, tm, tn), lambda i, j, k: (j, i, 0)), (n // tn, m, tn)
    out_shape, out_specs = [jax.ShapeDtypeStruct(o_shape, out_dtype)], [o_spec]
    if pre_dtype is not None:
        out_shape.insert(0, jax.ShapeDtypeStruct(o_shape, pre_dtype))
        out_specs.insert(0, o_spec)
    n_in = len(args)
    n_out = len(out_shape)

    def body(*refs):
        a_ref, b_ref = refs[0], refs[1]
        extra = list(refs[2:n_in])
        outs = refs[n_in:n_in + n_out]
        gate_ref = extra.pop(0) if gate is not None else None
        resid_ref = extra.pop(0) if resid is not None else None

        def product():
            return lax.dot_general(a_ref[...].astype(BF), b_ref[...].astype(BF), dims, preferred_element_type=F32)

        def finish(r):
            if pre_dtype is not None:
                outs[0][...] = r.astype(pre_dtype)
            if gate_ref is not None:
                r = r * gate_ref[...]
            if resid_ref is not None:
                r = resid_ref[...] + r
            outs[-1][...] = r.astype(out_dtype)

        if nk == 1:
            finish(product())
        else:
            acc = refs[n_in + n_out]
            k = pl.program_id(2)

            @pl.when(k == 0)
            def _():
                acc[...] = product()

            @pl.when(k > 0)
            def _():
                acc[...] += product()

            @pl.when(k == nk - 1)
            def _():
                finish(acc[...])

    res = pl.pallas_call(
        body, name=name, grid=(m // tm, n // tn, nk),
        in_specs=in_specs, out_specs=out_specs, out_shape=out_shape,
        scratch_shapes=[pltpu.VMEM((tm, tn), F32)] if nk > 1 else [],
        compiler_params=_params(("parallel", "parallel", "arbitrary")),
    )(*args)
    return res if pre_dtype is not None else res[0]


def _ada_fwd(cond16, w, name):
    nl, d, n = w.shape

    def body(c_ref, w_ref, o_ref):
        o_ref[...] = jnp.dot(c_ref[...].astype(BF), w_ref[...].astype(BF), preferred_element_type=F32)

    return pl.pallas_call(
        body, name=name, grid=(nl,),
        in_specs=[pl.BlockSpec((16, d), lambda l: (0, 0)), pl.BlockSpec((None, d, n), lambda l: (l, 0, 0))],
        out_specs=pl.BlockSpec((None, 16, n), lambda l: (l, 0, 0)),
        out_shape=jax.ShapeDtypeStruct((nl, 16, n), F32),
        compiler_params=_params(("parallel",)),
    )(cond16, w)


def _ada_bwd(cond16, dmod, name):
    nl, _, n = dmod.shape
    d = cond16.shape[1]

    def body(c_ref, g_ref, o_ref):
        o_ref[...] = lax.dot_general(c_ref[...].astype(BF), g_ref[...].astype(BF), (((0,), (0,)), ((), ())),
                                     preferred_element_type=F32)

    return pl.pallas_call(
        body, name=name, grid=(nl,),
        in_specs=[pl.BlockSpec((16, d), lambda l: (0, 0)), pl.BlockSpec((None, 16, n), lambda l: (l, 0, 0))],
        out_specs=pl.BlockSpec((None, d, n), lambda l: (l, 0, 0)),
        out_shape=jax.ShapeDtypeStruct((nl, d, n), F32),
        compiler_params=_params(("parallel",)),
    )(cond16, dmod)


def _row_spec(tm, d):
    return pl.BlockSpec((tm, d), lambda i: (i, 0))


def _vec_spec(d):
    return pl.BlockSpec((1, d), lambda i: (0, 0))


def _norm_mod(x, g, sc, sh, name):
    s, d = x.shape
    tm = _pick(s, 512, 8)

    def body(x_ref, g_ref, sc_ref, sh_ref, o_ref):
        xv = x_ref[...]
        r = lax.rsqrt(jnp.mean(xv * xv, axis=-1, keepdims=True) + EPS)
        y = (xv * r) * g_ref[...]
        o_ref[...] = (y * (1.0 + sc_ref[...]) + sh_ref[...]).astype(BF)

    return pl.pallas_call(
        body, name=name, grid=(s // tm,),
        in_specs=[_row_spec(tm, d), _vec_spec(d), _vec_spec(d), _vec_spec(d)],
        out_specs=_row_spec(tm, d), out_shape=jax.ShapeDtypeStruct((s, d), BF),
        compiler_params=_params(("parallel",)),
    )(x, g, sc, sh)


def _norm_mod_bwd(x, dh, g, sc, dx_in, name):
    s, d = x.shape
    tm = _pick(s, 512, 8)

    def body(x_ref, dh_ref, g_ref, sc_ref, dxin_ref, dx_ref, dsh_ref, dw_ref):
        i = pl.program_id(0)
        xv = x_ref[...]
        dhv = dh_ref[...].astype(F32)
        r = lax.rsqrt(jnp.mean(xv * xv, axis=-1, keepdims=True) + EPS)
        xn = xv * r
        dxn = dhv * (g_ref[...] * (1.0 + sc_ref[...]))
        dx_ref[...] = dxin_ref[...] + r * (dxn - xn * jnp.mean(dxn * xn, axis=-1, keepdims=True))

        @pl.when(i == 0)
        def _():
            dsh_ref[...] = jnp.zeros_like(dsh_ref)
            dw_ref[...] = jnp.zeros_like(dw_ref)

        dsh_ref[...] += jnp.sum(dhv, axis=0, keepdims=True)
        dw_ref[...] += jnp.sum(dhv * xn, axis=0, keepdims=True)

    return pl.pallas_call(
        body, name=name, grid=(s // tm,),
        in_specs=[_row_spec(tm, d), _row_spec(tm, d), _vec_spec(d), _vec_spec(d), _row_spec(tm, d)],
        out_specs=[_row_spec(tm, d), _vec_spec(d), _vec_spec(d)],
        out_shape=[jax.ShapeDtypeStruct((s, d), F32), jax.ShapeDtypeStruct((1, d), F32),
                   jax.ShapeDtypeStruct((1, d), F32)],
        compiler_params=_params(("arbitrary",)),
    )(x, dh, g, sc, dx_in)


def _gate_bwd(dx, y, gate, name):
    s, d = dx.shape
    tm = _pick(s, 512, 8)

    def body(dx_ref, y_ref, g_ref, dy_ref, dg_ref):
        i = pl.program_id(0)
        dxv = dx_ref[...]
        dy_ref[...] = (dxv * g_ref[...]).astype(BF)

        @pl.when(i == 0)
        def _():
            dg_ref[...] = jnp.zeros_like(dg_ref)

        dg_ref[...] += jnp.sum(dxv * y_ref[...].astype(F32), axis=0, keepdims=True)

    return pl.pallas_call(
        body, name=name, grid=(s // tm,),
        in_specs=[_row_spec(tm, d), _row_spec(tm, d), _vec_spec(d)],
        out_specs=[_row_spec(tm, d), _vec_spec(d)],
        out_shape=[jax.ShapeDtypeStruct((s, d), BF), jax.ShapeDtypeStruct((1, d), F32)],
        compiler_params=_params(("arbitrary",)),
    )(dx, y, gate)


def _final_loss(x, tgt, g, name):
    s, d = x.shape
    tm = _pick(s, 512, 8)

    def body(x_ref, t_ref, g_ref, dx_ref, loss_ref, dg_ref):
        i = pl.program_id(0)
        xv = x_ref[...]
        gv = g_ref[...]
        r = lax.rsqrt(jnp.mean(xv * xv, axis=-1, keepdims=True) + EPS)
        xn = xv * r
        err = xn * gv - t_ref[...]
        dy = err * (1.0 / d)
        dxn = dy * gv
        dx_ref[...] = r * (dxn - xn * jnp.mean(dxn * xn, axis=-1, keepdims=True))

        @pl.when(i == 0)
        def _():
            loss_ref[...] = jnp.zeros_like(loss_ref)
            dg_ref[...] = jnp.zeros_like(dg_ref)

        part = 0.5 * jnp.sum(jnp.sum(err * err, axis=-1, keepdims=True) * (1.0 / d), axis=0, keepdims=True)
        loss_ref[...] += jnp.broadcast_to(part, loss_ref.shape)
        dg_ref[...] += jnp.sum(dy * xn, axis=0, keepdims=True)

    return pl.pallas_call(
        body, name=name, grid=(s // tm,),
        in_specs=[_row_spec(tm, d), _row_spec(tm, d), _vec_spec(d)],
        out_specs=[_row_spec(tm, d), pl.BlockSpec((8, LANES), lambda i: (0, 0)), _vec_spec(d)],
        out_shape=[jax.ShapeDtypeStruct((s, d), F32), jax.ShapeDtypeStruct((8, LANES), F32),
                   jax.ShapeDtypeStruct((1, d), F32)],
        compiler_params=_params(("arbitrary",)),
    )(x, tgt, g)


def _pool_counts(tm, gd, row0, w):
    t = lax.broadcasted_iota(jnp.int32, (tm, gd), 0) + row0
    return jnp.minimum(t + 1, w).astype(F32)


def _pool_fwd(u, name):
    s, d = u.shape
    tm = _pick(s, 256, POOL_HALO)
    gd = d // len(POOL_WINDOWS)
    per = tm // POOL_HALO

    def body(prev_ref, cur_ref, o_ref, ext):
        i = pl.program_id(0)
        ext[0:POOL_HALO, :] = jnp.where(i > 0, prev_ref[...], 0.0)
        ext[POOL_HALO:, :] = cur_ref[...]
        for g, w in enumerate(POOL_WINDOWS):
            cols = slice(g * gd, (g + 1) * gd)
            acc = ext[POOL_HALO:POOL_HALO + tm, cols]
            own = acc
            for k in range(1, w):
                acc = acc + ext[POOL_HALO - k:POOL_HALO - k + tm, cols]
            o_ref[:, cols] = (acc / _pool_counts(tm, gd, i * tm, w) - own).astype(BF)

    return pl.pallas_call(
        body, name=name, grid=(s // tm,),
        in_specs=[pl.BlockSpec((POOL_HALO, d), lambda i: (jnp.maximum(i * per - 1, 0), 0)), _row_spec(tm, d)],
        out_specs=_row_spec(tm, d), out_shape=jax.ShapeDtypeStruct((s, d), BF),
        scratch_shapes=[pltpu.VMEM((tm + POOL_HALO, d), F32)],
        compiler_params=_params(("parallel",)),
    )(u, u)


def _pool_bwd(dp, name):
    s, d = dp.shape
    tm = _pick(s, 256, POOL_HALO)
    gd = d // len(POOL_WINDOWS)
    per = tm // POOL_HALO
    nt = s // tm
    last_halo = s // POOL_HALO - 1

    def body(cur_ref, nxt_ref, o_ref, ext):
        i = pl.program_id(0)
        for g, w in enumerate(POOL_WINDOWS):
            cols = slice(g * gd, (g + 1) * gd)
            ext[0:tm, cols] = cur_ref[:, cols].astype(F32) / _pool_counts(tm, gd, i * tm, w)
            nxt = nxt_ref[:, cols].astype(F32) / _pool_counts(POOL_HALO, gd, (i + 1) * tm, w)
            ext[tm:, cols] = jnp.where(i < nt - 1, nxt, 0.0)
        for g, w in enumerate(POOL_WINDOWS):
            cols = slice(g * gd, (g + 1) * gd)
            acc = ext[0:tm, cols]
            for k in range(1, w):
                acc = acc + ext[k:k + tm, cols]
            o_ref[:, cols] = (acc - cur_ref[:, cols].astype(F32)).astype(BF)

    return pl.pallas_call(
        body, name=name, grid=(nt,),
        in_specs=[_row_spec(tm, d), pl.BlockSpec((POOL_HALO, d), lambda i: (jnp.minimum((i + 1) * per, last_halo), 0))],
        out_specs=_row_spec(tm, d), out_shape=jax.ShapeDtypeStruct((s, d), BF),
        scratch_shapes=[pltpu.VMEM((tm + POOL_HALO, d), F32)],
        compiler_params=_params(("parallel",)),
    )(dp, dp)


def _grp_fwd(p, w, scale, name):
    s, d = p.shape
    ng, gd, _ = w.shape
    tm = _pick(s, 1024, 8)

    def body(p_ref, w_ref, s_ref, z_ref, y_ref):
        z = jnp.dot(p_ref[...], w_ref[...].astype(BF), preferred_element_type=F32)
        z_ref[...] = z.astype(BF)
        y_ref[...] = (z * s_ref[...]).astype(BF)

    blk = pl.BlockSpec((tm, gd), lambda i, g: (i, g))
    return pl.pallas_call(
        body, name=name, grid=(s // tm, ng),
        in_specs=[blk, pl.BlockSpec((None, gd, gd), lambda i, g: (g, 0, 0)), pl.BlockSpec((1, gd), lambda i, g: (0, g))],
        out_specs=[blk, blk],
        out_shape=[jax.ShapeDtypeStruct((s, d), BF), jax.ShapeDtypeStruct((s, d), BF)],
        compiler_params=_params(("parallel", "parallel")),
    )(p, w, scale)


def _grp_bwd(dy, z, w, scale, name):
    s, d = dy.shape
    ng, gd, _ = w.shape
    tm = _pick(s, 1024, 8)

    def body(dy_ref, z_ref, w_ref, s_ref, dz_ref, dp_ref, ds_ref):
        i = pl.program_id(1)
        dyv = dy_ref[...].astype(F32)
        dz = (dyv * s_ref[...]).astype(BF)
        dz_ref[...] = dz
        dp_ref[...] = lax.dot_general(dz, w_ref[...].astype(BF), (((1,), (1,)), ((), ())),
                                      preferred_element_type=F32).astype(BF)

        @pl.when(i == 0)
        def _():
            ds_ref[...] = jnp.zeros_like(ds_ref)

        ds_ref[...] += jnp.sum(dyv * z_ref[...].astype(F32), axis=0, keepdims=True)

    blk = pl.BlockSpec((tm, gd), lambda g, i: (i, g))
    vec = pl.BlockSpec((1, gd), lambda g, i: (0, g))
    return pl.pallas_call(
        body, name=name, grid=(ng, s // tm),
        in_specs=[blk, blk, pl.BlockSpec((None, gd, gd), lambda g, i: (g, 0, 0)), vec],
        out_specs=[blk, blk, vec],
        out_shape=[jax.ShapeDtypeStruct((s, d), BF), jax.ShapeDtypeStruct((s, d), BF),
                   jax.ShapeDtypeStruct((1, d), F32)],
        compiler_params=_params(("parallel", "arbitrary")),
    )(dy, z, w, scale)


def _grp_dw(p, dz, ng, name):
    s, d = p.shape
    gd = d // ng
    tk = _pick(s, 1024, 8)

    def body(p_ref, dz_ref, o_ref):
        k = pl.program_id(1)

        @pl.when(k == 0)
        def _():
            o_ref[...] = jnp.zeros_like(o_ref)

        o_ref[...] += lax.dot_general(p_ref[...], dz_ref[...], (((0,), (0,)), ((), ())), preferred_element_type=F32)

    blk = pl.BlockSpec((tk, gd), lambda g, k: (k, g))
    return pl.pallas_call(
        body, name=name, grid=(ng, s // tk),
        in_specs=[blk, blk], out_specs=pl.BlockSpec((None, gd, gd), lambda g, k: (g, 0, 0)),
        out_shape=jax.ShapeDtypeStruct((ng, gd, gd), F32),
        compiler_params=_params(("parallel", "arbitrary")),
    )(p, dz)


def _sigmoid(a):
    return 0.5 * jnp.tanh(0.5 * a) + 0.5


def _ffn_act_down(up, cw, cb, w_down, gate, resid, name):
    _, nq, s, fq = up.shape
    d = w_down.shape[1]
    tm = _pick(s, 512, CONV_HALO)
    per = tm // CONV_HALO
    h = CONV_HALO
    rows = _pick(tm, FFN_ROWS, h)

    def body(prev_ref, a_ref, v_ref, w_ref, b_ref, wd_ref, g_ref, r_ref, act_ref, ffo_ref, x_ref, ext, acc):
        i, q = pl.program_id(0), pl.program_id(1)
        ext[0:h, :] = jnp.where(i > 0, prev_ref[...].astype(F32), 0.0)
        ext[h:, :] = a_ref[...].astype(F32)

        def step(c, carry):
            r0 = pl.multiple_of(c * rows, rows)
            e = ext[pl.ds(r0, rows + h), :]
            a2 = (b_ref[...] + e[h - 2:h - 2 + rows] * w_ref[0:1, :] + e[h - 1:h - 1 + rows] * w_ref[1:2, :]
                  + e[h:h + rows] * w_ref[2:3, :])
            vv = v_ref[pl.ds(r0, rows), :].astype(F32)
            act_ref[pl.ds(r0, rows), :] = (a2 * _sigmoid(a2) * vv).astype(BF)
            return carry

        lax.fori_loop(0, tm // rows, step, 0)
        part = jnp.dot(act_ref[...], wd_ref[...], preferred_element_type=F32)

        @pl.when(q == 0)
        def _():
            acc[...] = part

        @pl.when(q > 0)
        def _():
            acc[...] += part

        @pl.when(q == nq - 1)
        def _():
            r = acc[...]
            ffo_ref[...] = r.astype(BF)
            x_ref[...] = r_ref[...] + g_ref[...] * r

    row = pl.BlockSpec((tm, d), lambda i, q: (i, 0))
    return pl.pallas_call(
        body, name=name, grid=(s // tm, nq),
        in_specs=[pl.BlockSpec((None, None, h, fq), lambda i, q: (0, q, jnp.maximum(i * per - 1, 0), 0)),
                  pl.BlockSpec((None, None, tm, fq), lambda i, q: (0, q, i, 0)),
                  pl.BlockSpec((None, None, tm, fq), lambda i, q: (1, q, i, 0)),
                  pl.BlockSpec((None, CONV_WIDTH, fq), lambda i, q: (q, 0, 0)),
                  pl.BlockSpec((None, 1, fq), lambda i, q: (q, 0, 0)),
                  pl.BlockSpec((fq, d), lambda i, q: (q, 0)),
                  pl.BlockSpec((1, d), lambda i, q: (0, 0)), row],
        out_specs=[pl.BlockSpec((None, tm, fq), lambda i, q: (q, i, 0)), row, row],
        out_shape=[jax.ShapeDtypeStruct((nq, s, fq), BF), jax.ShapeDtypeStruct((s, d), BF),
                   jax.ShapeDtypeStruct((s, d), F32)],
        scratch_shapes=[pltpu.VMEM((tm + h, fq), F32), pltpu.VMEM((tm, d), F32)],
        compiler_params=_params(("parallel", "arbitrary")),
    )(up, up, up, cw, cb, w_down, gate, resid)


def _ffn_act_bwd_up(up, dact, cw, cb, w_up, name):
    _, nq, s, fq = up.shape
    d = w_up.shape[1]
    tm = _pick(s, 512, CONV_HALO)
    per = tm // CONV_HALO
    nt = s // tm
    last_halo = s // CONV_HALO - 1
    h = CONV_HALO
    te = tm + h
    rows = _pick(tm, FFN_ROWS, h)
    lanes = (((1,), (1,)), ((), ()))

    def body(ap_ref, a_ref, an_ref, v_ref, vn_ref, d_ref, dn_ref, w_ref, b_ref, wa_ref, wv_ref,
             dup_ref, dc_ref, dx_ref, ext_a, dap, sums, acc):
        i, q = pl.program_id(0), pl.program_id(1)
        ext_a[0:h, :] = jnp.where(i > 0, ap_ref[...].astype(F32), 0.0)
        ext_a[h:h + tm, :] = a_ref[...].astype(F32)
        ext_a[h + tm:, :] = an_ref[...].astype(F32)

        def pre_act(e, n):
            return (b_ref[...] + e[h - 2:h - 2 + n] * w_ref[0:1, :] + e[h - 1:h - 1 + n] * w_ref[1:2, :]
                    + e[h:h + n] * w_ref[2:3, :])

        def through_gate(a2, dd, vv):
            sig = _sigmoid(a2)
            return dd * vv * (sig * (1.0 + a2 * (1.0 - sig))), dd * (a2 * sig)

        def step1(c, carry):
            r0 = pl.multiple_of(c * rows, rows)
            a2 = pre_act(ext_a[pl.ds(r0, rows + h), :], rows)
            g, dgate = through_gate(a2, d_ref[pl.ds(r0, rows), :].astype(F32), v_ref[pl.ds(r0, rows), :].astype(F32))
            dap[pl.ds(r0, rows), :] = g
            dup_ref[1, pl.ds(r0, rows), :] = dgate.astype(BF)
            return carry

        lax.fori_loop(0, tm // rows, step1, 0)
        d_nxt = jnp.where(i < nt - 1, dn_ref[...].astype(F32), 0.0)
        g, _ = through_gate(pre_act(ext_a[tm:tm + 2 * h, :], h), d_nxt, vn_ref[...].astype(F32))
        dap[tm:, :] = g
        sums[...] = jnp.zeros_like(sums)

        def fold(t):
            part = t[0:8]
            for k in range(8, rows, 8):
                part = part + t[k:k + 8]
            return part

        def step2(c, carry):
            r0 = pl.multiple_of(c * rows, rows)
            gch = dap[pl.ds(r0, rows + h), :]
            g0 = gch[0:rows]
            dup_ref[0, pl.ds(r0, rows), :] = (gch[2:2 + rows] * w_ref[0:1, :] + gch[1:1 + rows] * w_ref[1:2, :]
                                              + g0 * w_ref[2:3, :]).astype(BF)
            e = ext_a[pl.ds(r0, rows + h), :]
            sums[0] += fold(g0 * e[h - 2:h - 2 + rows])
            sums[1] += fold(g0 * e[h - 1:h - 1 + rows])
            sums[2] += fold(g0 * e[h:h + rows])
            sums[3] += fold(g0)
            return carry

        lax.fori_loop(0, tm // rows, step2, 0)

        @pl.when(i == 0)
        def _():
            dc_ref[q] = jnp.zeros((8, fq), F32)

        for k in range(4):
            dc_ref[q, k:k + 1, :] += jnp.sum(sums[k], axis=0, keepdims=True)

        part = (lax.dot_general(dup_ref[0], wa_ref[...], lanes, preferred_element_type=F32)
                + lax.dot_general(dup_ref[1], wv_ref[...], lanes, preferred_element_type=F32))

        @pl.when(q == 0)
        def _():
            acc[...] = part

        @pl.when(q > 0)
        def _():
            acc[...] += part

        @pl.when(q == nq - 1)
        def _():
            dx_ref[...] = acc[...]

    def cur(half):
        return pl.BlockSpec((None, None, tm, fq), lambda i, q: (half, q, i, 0))

    def nxt(half):
        return pl.BlockSpec((None, None, h, fq), lambda i, q: (half, q, jnp.minimum((i + 1) * per, last_halo), 0))

    return pl.pallas_call(
        body, name=name, grid=(nt, nq),
        in_specs=[pl.BlockSpec((None, None, h, fq), lambda i, q: (0, q, jnp.maximum(i * per - 1, 0), 0)),
                  cur(0), nxt(0), cur(1), nxt(1),
                  pl.BlockSpec((None, tm, fq), lambda i, q: (q, i, 0)),
                  pl.BlockSpec((None, h, fq), lambda i, q: (q, jnp.minimum((i + 1) * per, last_halo), 0)),
                  pl.BlockSpec((None, CONV_WIDTH, fq), lambda i, q: (q, 0, 0)),
                  pl.BlockSpec((None, 1, fq), lambda i, q: (q, 0, 0)),
                  pl.BlockSpec((None, d, fq), lambda i, q: (q, 0, 0)),
                  pl.BlockSpec((None, d, fq), lambda i, q: (q + nq, 0, 0))],
        out_specs=[pl.BlockSpec((2, None, tm, fq), lambda i, q: (0, q, i, 0)),
                   pl.BlockSpec((nq, 8, fq), lambda i, q: (0, 0, 0)),
                   pl.BlockSpec((tm, d), lambda i, q: (i, 0))],
        out_shape=[jax.ShapeDtypeStruct((2, nq, s, fq), BF), jax.ShapeDtypeStruct((nq, 8, fq), F32),
                   jax.ShapeDtypeStruct((s, d), F32)],
        scratch_shapes=[pltpu.VMEM((tm + 2 * h, fq), F32), pltpu.VMEM((te, fq), F32), pltpu.VMEM((4, 8, fq), F32),
                        pltpu.VMEM((tm, d), F32)],
        compiler_params=_params(("arbitrary", "arbitrary")),
    )(up, up, up, up, up, dact, dact, cw, cb, w_up, w_up)


def _band(blk, n_steps, dil, first):
    qi = lax.broadcasted_iota(jnp.int32, (blk, 2 * blk), 0) + blk
    ki = lax.broadcasted_iota(jnp.int32, (blk, 2 * blk), 1)
    delta = qi - ki
    valid = (delta >= 0) & (delta <= n_steps) & ((ki >= blk) | jnp.logical_not(first))
    return valid, (delta * dil).astype(F32)


def _branch_views(q_all, kv, g, d):
    _, dil = BRANCHES[g]
    nbr = len(BRANCHES)
    sub = q_all.shape[0] // dil
    if dil == 1:
        return (q_all, kv, kv), ((nbr, g), (2 * nbr, g), (2 * nbr, nbr + g))
    cols = [q_all[:, g * d:(g + 1) * d], kv[:, g * d:(g + 1) * d], kv[:, (nbr + g) * d:(nbr + g + 1) * d]]
    return tuple(a.reshape(sub, dil * d) for a in cols), ((1, 0), (1, 0), (1, 0))


def _attn_fwd(q_all, kv, g, slopes, d, name):
    window, dil = BRANCHES[g]
    n_steps = window // dil
    blk = max(ATTN_BLOCK, n_steps)
    s = q_all.shape[0]
    sub = s // dil
    nb = sub // blk
    assert nb * blk == sub
    nh = d // HEAD_DIM
    (qv, kview, vview), (qcol, kcol, vcol) = _branch_views(q_all, kv, g, d)
    scale = HEAD_DIM ** -0.5

    def body(q_ref, kp_ref, kc_ref, vp_ref, vc_ref, o_ref, l_ref):
        j = pl.program_id(1)
        valid, dist = _band(blk, n_steps, dil, j == 0)
        qb = q_ref[...]
        kb = jnp.concatenate([kp_ref[...], kc_ref[...]], axis=0)
        vb = jnp.concatenate([vp_ref[...], vc_ref[...]], axis=0)
        first = lax.broadcasted_iota(jnp.int32, (1, 2 * HEAD_DIM), 1) < HEAD_DIM
        for hp in range(nh // 2):
            sl = slice(2 * hp * HEAD_DIM, 2 * (hp + 1) * HEAD_DIM)
            qp, kp, vp = qb[:, sl], kb[:, sl], vb[:, sl]
            out, lse = None, None
            for half in range(2):
                sel = first if half == 0 else jnp.logical_not(first)
                sc = lax.dot_general(jnp.where(sel, qp, 0), kp, (((1,), (1,)), ((), ())),
                                     preferred_element_type=F32) * scale
                sc = jnp.where(valid, sc - float(slopes[2 * hp + half]) * dist, NEG)
                m = jnp.max(sc, axis=-1, keepdims=True)
                p = jnp.exp(sc - m)
                den = jnp.sum(p, axis=-1, keepdims=True)
                o = jnp.dot(p.astype(BF), jnp.where(sel, vp, 0), preferred_element_type=F32) / den
                lse_h = m + jnp.log(den)
                out = o if half == 0 else out + o
                lse = lse_h if half == 0 else jnp.where(first, lse, lse_h)
            o_ref[:, sl] = out.astype(BF)
            l_ref[:, sl] = jnp.broadcast_to(lse, (blk, 2 * HEAD_DIM))

    def spec(col, prev):
        if prev:
            return pl.BlockSpec((blk, d), lambda r, j: (jnp.maximum(j - 1, 0), r * col[0] + col[1]))
        return pl.BlockSpec((blk, d), lambda r, j: (j, r * col[0] + col[1]))

    ospec = pl.BlockSpec((blk, d), lambda r, j: (j, r))
    o, lse = pl.pallas_call(
        body, name=name, grid=(dil, nb),
        in_specs=[spec(qcol, False), spec(kcol, True), spec(kcol, False), spec(vcol, True), spec(vcol, False)],
        out_specs=[ospec, ospec],
        out_shape=[jax.ShapeDtypeStruct((sub, dil * d), BF), jax.ShapeDtypeStruct((sub, dil * d), F32)],
        compiler_params=_params(("parallel", "parallel")),
    )(qv, kview, kview, vview, vview)
    return o.reshape(s, d), lse.reshape(s, d)


def _attn_combine(os, lses, name):
    s, d = os[0].shape
    tm = _pick(s, 512, 8)
    nbr = len(os)

    def body(*refs):
        o_refs, l_refs = refs[:nbr], refs[nbr:2 * nbr]
        o_ref, lt_ref = refs[2 * nbr], refs[2 * nbr + 1]
        ls = [r[...] for r in l_refs]
        m = ls[0]
        for v in ls[1:]:
            m = jnp.maximum(m, v)
        tot = jnp.exp(ls[0] - m)
        for v in ls[1:]:
            tot = tot + jnp.exp(v - m)
        lt = m + jnp.log(tot)
        acc = jnp.exp(ls[0] - lt) * o_refs[0][...].astype(F32)
        for v, r in zip(ls[1:], o_refs[1:]):
            acc = acc + jnp.exp(v - lt) * r[...].astype(F32)
        o_ref[...] = acc.astype(BF)
        lt_ref[...] = lt

    return pl.pallas_call(
        body, name=name, grid=(s // tm,),
        in_specs=[_row_spec(tm, d)] * (2 * nbr), out_specs=[_row_spec(tm, d), _row_spec(tm, d)],
        out_shape=[jax.ShapeDtypeStruct((s, d), BF), jax.ShapeDtypeStruct((s, d), F32)],
        compiler_params=_params(("parallel",)),
    )(*os, *lses)


def _attn_bwd(q_all, kv, do, o, lt, g, slopes, d, name, dk_in=None, dv_in=None):
    window, dil = BRANCHES[g]
    n_steps = window // dil
    blk = max(ATTN_BLOCK, n_steps)
    s = q_all.shape[0]
    sub = s // dil
    nb = sub // blk
    nh = d // HEAD_DIM
    (qv, kview, vview), (qcol, kcol, vcol) = _branch_views(q_all, kv, g, d)
    scale = HEAD_DIM ** -0.5
    acc_in = dk_in is not None

    def body(*refs):
        q_ref, do_ref, o_ref, lt_ref, kp_ref, kc_ref, vp_ref, vc_ref = refs[:8]
        n_in = 10 if acc_in else 8
        dkin_ref, dvin_ref = (refs[8], refs[9]) if acc_in else (None, None)
        dq_ref, dk_ref, dv_ref, keep_k, keep_v, part_k, part_v = refs[n_in:n_in + 7]
        t = pl.program_id(1)

        def emit(prev_k, prev_v):
            if acc_in:
                prev_k = prev_k + dkin_ref[...].astype(F32)
                prev_v = prev_v + dvin_ref[...].astype(F32)
            dk_ref[...] = prev_k.astype(BF)
            dv_ref[...] = prev_v.astype(BF)

        @pl.when(t < nb)
        def _():
            valid, dist = _band(blk, n_steps, dil, t == 0)
            qb = q_ref[...]
            dob = do_ref[...]
            kb = jnp.concatenate([kp_ref[...], kc_ref[...]], axis=0)
            vb = jnp.concatenate([vp_ref[...], vc_ref[...]], axis=0)
            first = lax.broadcasted_iota(jnp.int32, (1, 2 * HEAD_DIM), 1) < HEAD_DIM
            for hp in range(nh // 2):
                sl = slice(2 * hp * HEAD_DIM, 2 * (hp + 1) * HEAD_DIM)
                qp, kp, vp, dop = qb[:, sl], kb[:, sl], vb[:, sl], dob[:, sl]
                op = o_ref[:, sl].astype(F32)
                ltp = lt_ref[:, sl]
                dq, dk, dv = None, None, None
                for half in range(2):
                    sel = first if half == 0 else jnp.logical_not(first)
                    qh, doh = jnp.where(sel, qp, 0), jnp.where(sel, dop, 0)
                    sc = lax.dot_general(qh, kp, (((1,), (1,)), ((), ())), preferred_element_type=F32) * scale
                    sc = sc - float(slopes[2 * hp + half]) * dist
                    lt_h = ltp[:, half * HEAD_DIM:half * HEAD_DIM + 1]
                    p = jnp.where(valid, jnp.exp(jnp.minimum(sc - lt_h, 30.0)), 0.0)
                    dlt = jnp.sum(doh.astype(F32) * op, axis=-1, keepdims=True)
                    dp = lax.dot_general(doh, vp, (((1,), (1,)), ((), ())), preferred_element_type=F32)
                    ds = (p * (dp - dlt)).astype(BF)
                    dq_h = jnp.dot(ds, jnp.where(sel, kp, 0), preferred_element_type=F32)
                    dk_h = lax.dot_general(ds, qh, (((0,), (0,)), ((), ())), preferred_element_type=F32)
                    dv_h = lax.dot_general(p.astype(BF), doh, (((0,), (0,)), ((), ())), preferred_element_type=F32)
                    dq = dq_h if half == 0 else dq + dq_h
                    dk = dk_h if half == 0 else dk + dk_h
                    dv = dv_h if half == 0 else dv + dv_h
                dq_ref[:, sl] = (dq * scale).astype(BF)
                part_k[:, sl] = dk * scale
                part_v[:, sl] = dv

            @pl.when(t > 0)
            def _():
                emit(keep_k[...] + part_k[0:blk, :], keep_v[...] + part_v[0:blk, :])

            keep_k[...] = part_k[blk:, :]
            keep_v[...] = part_v[blk:, :]

        @pl.when(t == nb)
        def _():
            emit(keep_k[...], keep_v[...])

    def qspec(col):
        return pl.BlockSpec((blk, d), lambda r, t: (jnp.minimum(t, nb - 1), r * col[0] + col[1]))

    def kspec(col, prev):
        if prev:
            return pl.BlockSpec((blk, d), lambda r, t: (jnp.maximum(jnp.minimum(t, nb - 1) - 1, 0), r * col[0] + col[1]))
        return qspec(col)

    kout = pl.BlockSpec((blk, d), lambda r, t: (jnp.maximum(t - 1, 0), r))
    one = (1, 0)
    in_specs = [qspec(qcol), qspec(one), qspec(one), qspec(one),
                kspec(kcol, True), kspec(kcol, False), kspec(vcol, True), kspec(vcol, False)]
    args = [qv, do.reshape(sub, dil * d), o.reshape(sub, dil * d), lt.reshape(sub, dil * d),
            kview, kview, vview, vview]
    if acc_in:
        in_specs += [kout, kout]
        args += [dk_in.reshape(sub, dil * d), dv_in.reshape(sub, dil * d)]
    shp = jax.ShapeDtypeStruct((sub, dil * d), BF)
    dq, dk, dv = pl.pallas_call(
        body, name=name, grid=(dil, nb + 1),
        in_specs=in_specs, out_specs=[qspec(one), kout, kout], out_shape=[shp, shp, shp],
        scratch_shapes=[pltpu.VMEM((blk, d), F32), pltpu.VMEM((blk, d), F32),
                        pltpu.VMEM((2 * blk, d), F32), pltpu.VMEM((2 * blk, d), F32)],
        compiler_params=_params(("parallel", "arbitrary")),
    )(*args)
    return dq.reshape(s, d), dk.reshape(s, d), dv.reshape(s, d)


def _adamw(parts_list, w, m, v, name):
    nl, r, c = w.shape
    assert len(parts_list) == nl
    npart = parts_list[0].shape[0]
    tr = _pick(r, 256, 16)
    c1 = 1.0 / (1.0 - ADAM_B1 ** ADAM_STEP)
    c2 = 1.0 / (1.0 - ADAM_B2 ** ADAM_STEP)

    def body(*refs):
        p_refs = refs[:nl]
        w_ref, m_ref, v_ref, g_ref, d_ref, nm_ref, nv_ref = refs[nl:]
        layer = pl.program_id(0)
        for idx in range(nl):
            @pl.when(layer == idx)
            def _(p_ref=p_refs[idx]):
                g = p_ref[0].astype(F32)
                for k in range(1, npart):
                    g = g + p_ref[k].astype(F32)
                nm = ADAM_B1 * m_ref[...] + (1.0 - ADAM_B1) * g
                nv = ADAM_B2 * v_ref[...] + (1.0 - ADAM_B2) * (g * g)
                g_ref[...] = g
                nm_ref[...] = nm
                nv_ref[...] = nv
                d_ref[...] = -ADAM_LR * ((nm * c1) / (jnp.sqrt(nv * c2) + ADAM_EPS) + ADAM_WD * w_ref[...])

    def part_spec(idx):
        return pl.BlockSpec((npart, tr, c), lambda l, i: (0, jnp.where(l == idx, i, 0), 0))

    blk = pl.BlockSpec((None, tr, c), lambda l, i: (l, i, 0))
    shp = jax.ShapeDtypeStruct((nl, r, c), F32)
    return pl.pallas_call(
        body, name=name, grid=(nl, r // tr),
        in_specs=[part_spec(idx) for idx in range(nl)] + [blk, blk, blk],
        out_specs=[blk, blk, blk, blk], out_shape=[shp, shp, shp, shp],
        compiler_params=_params(("parallel", "parallel")),
    )(*parts_list, w, m, v)


def _full_from_slots(slots, shard_shape, axis):
    a = slots.reshape((N_DEV,) + tuple(shard_shape))
    a = jnp.moveaxis(a, 0, axis)
    full = list(shard_shape)
    full[axis] *= N_DEV
    return a.reshape(full)


def kernel(x, c, ada_w, ada_b, norm1_g, norm2_g, pool_w_in, pool_w_grp, pool_scale, pool_w_out, kv_norm_g, kv_ada_w, kv_ada_b, w_kv, attn_w_q, attn_w_o, ffn_w_up, ffn_conv_w, ffn_conv_b, ffn_w_down, final_g, loss_target, m_ada_w, m_ada_b, m_norm1_g, m_norm2_g, m_pool_w_in, m_pool_w_grp, m_pool_scale, m_pool_w_out, m_kv_norm_g, m_kv_ada_w, m_kv_ada_b, m_w_kv, m_attn_w_q, m_attn_w_o, m_ffn_w_up, m_ffn_conv_w, m_ffn_conv_b, m_ffn_w_down, m_final_g, v_ada_w, v_ada_b, v_norm1_g, v_norm2_g, v_pool_w_in, v_pool_w_grp, v_pool_scale, v_pool_w_out, v_kv_norm_g, v_kv_ada_w, v_kv_ada_b, v_w_kv, v_attn_w_q, v_attn_w_o, v_ffn_w_up, v_ffn_conv_w, v_ffn_conv_b, v_ffn_w_down, v_final_g):
    weights = dict(ada_w=ada_w, ada_b=ada_b, norm1_g=norm1_g, norm2_g=norm2_g, pool_w_in=pool_w_in,
                   pool_w_grp=pool_w_grp, pool_scale=pool_scale, pool_w_out=pool_w_out, kv_norm_g=kv_norm_g,
                   kv_ada_w=kv_ada_w, kv_ada_b=kv_ada_b, w_kv=w_kv, attn_w_q=attn_w_q, attn_w_o=attn_w_o,
                   ffn_w_up=ffn_w_up, ffn_conv_w=ffn_conv_w, ffn_conv_b=ffn_conv_b, ffn_w_down=ffn_w_down,
                   final_g=final_g)
    mom1 = dict(ada_w=m_ada_w, ada_b=m_ada_b, norm1_g=m_norm1_g, norm2_g=m_norm2_g, pool_w_in=m_pool_w_in,
                pool_w_grp=m_pool_w_grp, pool_scale=m_pool_scale, pool_w_out=m_pool_w_out, kv_norm_g=m_kv_norm_g,
                kv_ada_w=m_kv_ada_w, kv_ada_b=m_kv_ada_b, w_kv=m_w_kv, attn_w_q=m_attn_w_q, attn_w_o=m_attn_w_o,
                ffn_w_up=m_ffn_w_up, ffn_conv_w=m_ffn_conv_w, ffn_conv_b=m_ffn_conv_b, ffn_w_down=m_ffn_w_down,
                final_g=m_final_g)
    mom2 = dict(ada_w=v_ada_w, ada_b=v_ada_b, norm1_g=v_norm1_g, norm2_g=v_norm2_g, pool_w_in=v_pool_w_in,
                pool_w_grp=v_pool_w_grp, pool_scale=v_pool_scale, pool_w_out=v_pool_w_out, kv_norm_g=v_kv_norm_g,
                kv_ada_w=v_kv_ada_w, kv_ada_b=v_kv_ada_b, w_kv=v_w_kv, attn_w_q=v_attn_w_q, attn_w_o=v_attn_w_o,
                ffn_w_up=v_ffn_w_up, ffn_conv_w=v_ffn_conv_w, ffn_conv_b=v_ffn_conv_b, ffn_w_down=v_ffn_w_down,
                final_g=v_final_g)
    order = list(weights)

    seq, d = x.shape[1], x.shape[2]
    depth = ada_w.shape[0]
    n_pool = pool_w_in.shape[0]
    f = ffn_conv_b.shape[1]
    nbr = len(BRANCHES)
    nh = d // HEAD_DIM
    slopes = _alibi_slopes(nbr * nh).reshape(nbr, nh)
    me = 4 * lax.axis_index("x") + 2 * lax.axis_index("y") + lax.axis_index("c")
    xs = x[0]
    tgt = loss_target[0]

    def start_gathers(after):
        keys, groups = [], []
        for l in range(depth):
            if l < n_pool:
                parts = [("mixer", [pool_w_in[l], pool_w_grp[l].reshape(-1, pool_w_grp.shape[-1]), pool_w_out[l]])]
            else:
                j = l - n_pool
                parts = [("mixer", [attn_w_q[j]] + ([w_kv] if j == 0 else [])), ("out", [attn_w_o[j]])]
            for part, srcs in parts + [("ffn", [ffn_w_up[l], ffn_w_down[l]])]:
                keys.append((l, part))
                groups.append([a.astype(BF) for a in srcs])
        handles, _ = _comm_start(groups, "gather_start", scatter=False, after=after)
        return dict(zip(keys, handles))

    cond = c * (1.0 / (1.0 + jnp.exp(-c)))
    small_in = jnp.concatenate([cond.reshape(-1), ffn_conv_w.reshape(-1), pool_scale.reshape(-1)])
    n_small_in = small_in.shape[0]
    gath = _all_gather(_to_rows(small_in), "gather_small").reshape(N_DEV, -1)[:, :n_small_in]
    cond_all = gath[:, :d]
    o1 = d + ffn_conv_w.size
    conv_w_full = _full_from_slots(gath[:, d:o1], ffn_conv_w.shape, 2)
    pool_scale_full = _full_from_slots(gath[:, o1:], pool_scale.shape, 1)
    cond16 = jnp.concatenate([cond_all, jnp.zeros_like(cond_all)], axis=0)

    mod_part = _ada_fwd(cond16, ada_w, "ada_fwd")[:, :N_DEV]
    kv_part = _ada_fwd(cond16, kv_ada_w[None], "kv_ada_fwd")[0, :N_DEV]
    n_mod = depth * mod_part.shape[2] + kv_part.shape[1]
    send = jnp.concatenate([jnp.moveaxis(mod_part, 1, 0).reshape(N_DEV, -1), kv_part], axis=1)
    send_rows = jax.vmap(_to_rows)(send)
    got = _all_to_all(send_rows, "exchange_mod").reshape(N_DEV, -1)[:, :n_mod]
    ncol = mod_part.shape[2]
    mods = []
    for l in range(depth):
        row = got[:, l * ncol:(l + 1) * ncol].reshape(1, -1) + ada_b[l][None]
        mods.append([row[:, k * d:(k + 1) * d] for k in range(6)])
    kv_row = got[:, depth * ncol:].reshape(1, -1) + kv_ada_b[None]
    kv_shift, kv_scale = kv_row[:, :d], kv_row[:, d:]
    gather_handles = start_gathers(got)

    def vec(a):
        return a.reshape(1, -1)

    nq = 4
    fq = f // nq
    ng = len(POOL_WINDOWS)
    cw_slots = jnp.moveaxis(conv_w_full.reshape(depth, CONV_WIDTH, nq, fq), 2, 1)
    cb_slots = ffn_conv_b.reshape(depth, nq, 1, fq)

    saved = []
    xcur = xs
    kvs = None
    hkv = None
    x_kv = None
    w_kv_slots = None
    for l in range(depth):
        sh1, sc1, g1, sh2, sc2, g2 = mods[l]
        st = dict(x=xcur)
        h = _norm_mod(xcur, vec(norm1_g[l]), sc1, sh1, f"norm1_{l}")
        st["h"] = h
        gw = _comm_wait(gather_handles[l, "mixer"], h, f"gather_wait_mixer_{l}")
        if l < n_pool:
            w_in = gw[0].reshape(d, d)
            w_grp = jnp.moveaxis(gw[1].reshape(N_DEV, ng, -1, d // ng), 0, 1).reshape(ng, d // ng, d // ng)
            w_out = gw[2].reshape(d, d)
            u = _mm(h, w_in, f"pool_in_{l}")
            pooled = _pool_fwd(u, f"pool_fwd_{l}")
            z, y = _grp_fwd(pooled, w_grp, vec(pool_scale_full[l]), f"grp_fwd_{l}")
            mix, x1 = _mm(y, w_out, f"pool_out_{l}", gate=g1, resid=xcur, pre_dtype=BF)
            st.update(pooled=pooled, z=z, y=y, w_in=w_in, w_grp=w_grp, w_out=w_out)
        else:
            j = l - n_pool
            w_q_slots = gw[0]
            if j == 0:
                w_kv_slots = gw[1]
                x_kv = xcur
                hkv = _norm_mod(xcur, vec(kv_norm_g), kv_scale, kv_shift, "norm_kv")
                kvs = _mm(hkv, w_kv_slots, "kv_proj", bf="qkn", out_dtype=BF)
            q = _mm(h, w_q_slots, f"q_proj_{l}", bf="qkn", out_dtype=BF)
            outs, lses = [], []
            for g in range(nbr):
                og, lg = _attn_fwd(q, kvs, g, slopes[g], d, f"attn_fwd_{l}_{g}")
                outs.append(og)
                lses.append(lg)
            o, lt = _attn_combine(outs, lses, f"attn_mix_{l}")
            w_o = _comm_wait(gather_handles[l, "out"], o, f"gather_wait_out_{l}")[0].reshape(d, d)
            mix, x1 = _mm(o, w_o, f"attn_out_{l}", gate=g1, resid=xcur, pre_dtype=BF)
            st.update(q=q, o=o, lt=lt, w_q=w_q_slots, w_o=w_o)
        h2 = _norm_mod(x1, vec(norm2_g[l]), sc2, sh2, f"norm2_{l}")
        w_up_slots, w_down = _comm_wait(gather_handles[l, "ffn"], h2, f"gather_wait_ffn_{l}")
        w_down = w_down.reshape(f, d)
        st.update(w_up=w_up_slots, w_down=w_down)
        up = _mm(h2, w_up_slots, f"ffn_up_{l}", bf="qkn", of="qmn", out_dtype=BF).reshape(2, nq, seq, fq)
        act, ffo, x2 = _ffn_act_down(up, cw_slots[l], cb_slots[l], w_down, g2, x1, f"ffn_act_down_{l}")
        st.update(mix=mix, x1=x1, h2=h2, up=up, act=act, ffo=ffo)
        saved.append(st)
        xcur = x2

    dx, loss_blk, d_final_g = _final_loss(xcur, tgt, vec(final_g), "final_loss")
    loss = lax.psum(loss_blk[0, 0], ("x", "y", "c"))

    d_mod = [None] * depth
    d_n1 = [None] * depth
    d_n2 = [None] * depth
    d_conv = [None] * depth
    d_pscale = [None] * n_pool
    dk_acc = [None] * nbr
    dv_acc = [None] * nbr
    ffn_handles = [None] * depth
    mixer_handles = [None] * depth
    tok = 0.0
    for l in reversed(range(depth)):
        sh1, sc1, g1, sh2, sc2, g2 = mods[l]
        st = saved[l]
        dffo, dg2 = _gate_bwd(dx, st["ffo"], g2 + tok, f"gate2_bwd_{l}")
        g_down = _mm(st["act"], dffo, f"ffn_down_dw_{l}", af="qkm", out_dtype=BF)
        dact = _mm(dffo, st["w_down"], f"ffn_down_dx_{l}", bf="nk", of="qmn", tn=fq, out_dtype=BF)
        dup, dc, dh2 = _ffn_act_bwd_up(st["up"], dact, cw_slots[l], cb_slots[l], st["w_up"], f"ffn_act_bwd_up_{l}")
        dup = dup.reshape(N_DEV, seq, -1)
        d_conv[l] = dc
        g_up = _mm(st["h2"], dup, f"ffn_up_dw_{l}", af="km", bf="qkn", of="qmn", out_dtype=BF)
        (ffn_handles[l],), tok = _comm_start([[g_up, g_down.reshape(N_DEV, -1, d)]], f"exchange_start_ffn_{l}",
                                             scatter=True)
        dx1, dsh2, dw2 = _norm_mod_bwd(st["x1"], dh2, vec(norm2_g[l]), sc2 + tok, dx, f"norm2_bwd_{l}")
        d_n2[l] = dw2 * (1.0 + sc2)
        dsc2 = dw2 * vec(norm2_g[l])

        dmix, dg1 = _gate_bwd(dx1, st["mix"], g1, f"gate1_bwd_{l}")
        if l < n_pool:
            g_out = _mm(st["y"], dmix, f"pool_out_dw_{l}", af="km", out_dtype=BF)
            dy = _mm(dmix, st["w_out"], f"pool_out_dx_{l}", bf="nk", out_dtype=BF)
            dz, dpool, dps = _grp_bwd(dy, st["z"], st["w_grp"], vec(pool_scale_full[l]), f"grp_bwd_{l}")
            d_pscale[l] = dps
            g_grp = _grp_dw(st["pooled"], dz, ng, f"grp_dw_{l}")
            du = _pool_bwd(dpool, f"pool_bwd_{l}")
            g_in = _mm(st["h"], du, f"pool_in_dw_{l}", af="km", out_dtype=BF)
            dh = _mm(du, st["w_in"], f"pool_in_dx_{l}", bf="nk")
            g_grp_slots = jnp.moveaxis(g_grp.reshape(ng, N_DEV, -1, d // ng), 1, 0).reshape(N_DEV, -1, d // ng)
            send = [g_in.reshape(N_DEV, -1, d), g_grp_slots.astype(BF), g_out.reshape(N_DEV, -1, d)]
        else:
            j = l - n_pool
            g_o = _mm(st["o"], dmix, f"attn_out_dw_{l}", af="km", out_dtype=BF)
            do = _mm(dmix, st["w_o"], f"attn_out_dx_{l}", bf="nk", out_dtype=BF)
            dqs = []
            for g in range(nbr):
                dq_g, dk_g, dv_g = _attn_bwd(st["q"], kvs, do, st["o"], st["lt"], g, slopes[g], d,
                                             f"attn_bwd_{l}_{g}", dk_in=dk_acc[g], dv_in=dv_acc[g])
                dqs.append(dq_g)
                dk_acc[g], dv_acc[g] = dk_g, dv_g
            dq = jnp.concatenate(dqs, axis=1)
            nqc = st["w_q"].shape[2]
            g_q = _mm(st["h"], dq, f"q_proj_dw_{l}", af="km", of="qmn", tn=nqc, out_dtype=BF)
            dh = _mm(dq, st["w_q"], f"q_proj_dx_{l}", bf="qnk")
            send = [g_q, g_o.reshape(N_DEV, -1, d)]
        dx0, dsh1, dw1 = _norm_mod_bwd(st["x"], dh, vec(norm1_g[l]), sc1, dx1, f"norm1_bwd_{l}")
        d_n1[l] = dw1 * (1.0 + sc1)
        dsc1 = dw1 * vec(norm1_g[l])
        d_mod[l] = jnp.concatenate([dsh1, dsc1, dg1, dsh2, dsc2, dg2], axis=1)
        dx = dx0
        if l == n_pool:
            dkv = jnp.concatenate(dk_acc + dv_acc, axis=1)
            nkc = w_kv_slots.shape[2]
            g_kv = _mm(hkv, dkv, "kv_proj_dw", af="km", of="qmn", tn=nkc, out_dtype=BF)
            dhkv = _mm(dkv, w_kv_slots, "kv_proj_dx", bf="qnk")
            dx, dsh_kv, dw_kv = _norm_mod_bwd(x_kv, dhkv, vec(kv_norm_g), kv_scale, dx, "norm_kv_bwd")
            d_kv_norm = dw_kv * (1.0 + kv_scale)
            d_kv_mod = jnp.concatenate([dsh_kv, dw_kv * vec(kv_norm_g)], axis=1)
            send.append(g_kv)
        (mixer_handles[l],), tok = _comm_start([send], f"exchange_start_mixer_{l}", scatter=True)
    grad_x = dx[None]
    d_final_g = d_final_g + tok

    small = [jnp.concatenate(d_mod, axis=1).reshape(-1), d_kv_mod.reshape(-1),
             jnp.concatenate(d_n1, axis=0).reshape(-1), jnp.concatenate(d_n2, axis=0).reshape(-1),
             d_kv_norm.reshape(-1), d_final_g.reshape(-1),
             jnp.stack([dcl[:, 3, :] for dcl in d_conv]).reshape(-1),
             jnp.stack([jnp.moveaxis(dcl[:, 0:CONV_WIDTH, :], 0, 1) for dcl in d_conv]).reshape(-1),
             jnp.concatenate(d_pscale, axis=0).reshape(-1)]
    sizes = [a.shape[0] for a in small]
    small_rows = _to_rows(jnp.concatenate(small))
    small_all = _all_gather(small_rows, "gather_small_grads")
    dmod_all = small_all.reshape(N_DEV, -1)[:, :sizes[0] + sizes[1]]

    dmod16 = jnp.concatenate([dmod_all, jnp.zeros_like(dmod_all)], axis=0)
    dm = dmod16[:, :sizes[0]].reshape(16, depth, N_DEV, ncol)
    dm_mine = lax.dynamic_index_in_dim(dm, me, axis=2, keepdims=False)
    g_ada_w = _ada_bwd(cond16, jnp.moveaxis(dm_mine, 0, 1), "ada_bwd")
    nkv = kv_part.shape[1]
    dkm = dmod16[:, sizes[0]:].reshape(16, N_DEV, nkv)
    dkm_mine = lax.dynamic_index_in_dim(dkm, me, axis=1, keepdims=False)
    g_kv_ada_w = _ada_bwd(cond16, dkm_mine[None], "kv_ada_bwd")

    res = {}

    def update(n, parts_list, shape3):
        w3, m3, v3 = (a[n].reshape(shape3) for a in (weights, mom1, mom2))
        outs = _adamw(parts_list, w3, m3, v3, f"adamw_{n}")
        res[n] = [a.reshape(weights[n].shape) for a in outs]

    update("ada_w", [g_ada_w[l][None] for l in range(depth)], ada_w.shape)
    update("kv_ada_w", [g_kv_ada_w], (1,) + kv_ada_w.shape)
    after = res["ada_w"][0]
    parts_ffn = [_comm_wait(ffn_handles[l], after, f"exchange_wait_ffn_{l}") for l in reversed(range(depth))][::-1]
    parts = [_comm_wait(mixer_handles[l], after, f"exchange_wait_mixer_{l}") for l in reversed(range(depth))][::-1]
    pool_layers, attn_layers = range(n_pool), range(n_pool, depth)
    update("ffn_w_up", [parts_ffn[l][0] for l in range(depth)], ffn_w_up.shape)
    update("ffn_w_down", [parts_ffn[l][1] for l in range(depth)], ffn_w_down.shape)
    update("attn_w_q", [parts[l][0] for l in attn_layers], attn_w_q.shape)
    update("attn_w_o", [parts[l][1] for l in attn_layers], (depth - n_pool, -1, d))
    update("w_kv", [parts[n_pool][2]], (1,) + w_kv.shape)
    update("pool_w_in", [parts[l][0] for l in pool_layers], (n_pool, -1, d))
    update("pool_w_grp", [parts[l][1] for l in pool_layers], (n_pool, -1, d // ng))
    update("pool_w_out", [parts[l][2] for l in pool_layers], (n_pool, -1, d))

    tot = small_all.reshape(N_DEV, -1)
    offs = np.cumsum([0] + sizes)
    seg = {k: (int(offs[i]), int(offs[i + 1])) for i, k in enumerate(
        ["mod", "kv_mod", "n1", "n2", "kv_norm", "final", "conv_b", "conv_w", "pscale"])}

    def rows_of(a, b):
        return tot[:, a:b]

    nf8 = f // N_DEV
    conv_w_parts = lax.dynamic_slice_in_dim(
        rows_of(*seg["conv_w"]).reshape(N_DEV, depth, CONV_WIDTH, N_DEV, nf8), me, 1, axis=3).reshape(N_DEV, -1)
    nd8 = d // N_DEV
    pscale_parts = lax.dynamic_slice_in_dim(
        rows_of(*seg["pscale"]).reshape(N_DEV, n_pool, N_DEV, nd8), me, 1, axis=2).reshape(N_DEV, -1)
    small_names = ["ada_b", "norm1_g", "norm2_g", "pool_scale", "kv_norm_g", "kv_ada_b", "ffn_conv_w",
                   "ffn_conv_b", "final_g"]
    small_parts = [rows_of(*seg["mod"]), rows_of(*seg["n1"]), rows_of(*seg["n2"]), pscale_parts,
                   rows_of(*seg["kv_norm"]), rows_of(*seg["kv_mod"]), conv_w_parts, rows_of(*seg["conv_b"]),
                   rows_of(*seg["final"])]
    sp = jnp.concatenate(small_parts, axis=1)
    n_sp = sp.shape[1]
    sp_rows = jax.vmap(_to_rows)(sp)

    def packed(src):
        return _to_rows(jnp.concatenate([src[n].reshape(-1) for n in small_names]))[None]

    outs = _adamw([sp_rows], packed(weights), packed(mom1), packed(mom2), "adamw_small")
    outs = [a.reshape(-1)[:n_sp] for a in outs]
    off = 0
    for n in small_names:
        size = weights[n].size
        res[n] = [a[off:off + size].reshape(weights[n].shape) for a in outs]
        off += size

    grads = [res[n][0] for n in order]
    deltas = [res[n][1] for n in order]
    new_m = [res[n][2] for n in order]
    new_v = [res[n][3] for n in order]
    return (loss, grad_x, *grads, *deltas, *new_m, *new_v)
```

```python
import math

import numpy as np
import jax
import jax.numpy as jnp
from jax import lax
from jax.experimental import pallas as pl
from jax.experimental.pallas import tpu as pltpu

F32 = jnp.float32
BF = jnp.bfloat16

POOL_WINDOWS = (2, 4, 8, 16)
BRANCHES = ((128, 1), (512, 4), (2048, 16))
HEAD_DIM = 64
ATTN_BLOCK = 128
CONV_WIDTH = 3
EPS = 1e-6
ADAM_LR = 0.001
ADAM_B1 = 0.9
ADAM_B2 = 0.999
ADAM_EPS = 1e-08
ADAM_WD = 0.01
ADAM_STEP = 10

N_DEV = 8
LANES = 128
POOL_HALO = 16
CONV_HALO = 8
FFN_ROWS = 16
VMEM_LIMIT = 48 * 1024 * 1024
MM_TILE = 1024
MM_ROWS = 2048
NEG = -1e30

MESH = pl.DeviceIdType.MESH
ANY = pl.BlockSpec(memory_space=pl.ANY)


def _params(sem=None):
    if sem is None:
        return pltpu.CompilerParams(vmem_limit_bytes=VMEM_LIMIT)
    return pltpu.CompilerParams(dimension_semantics=sem, vmem_limit_bytes=VMEM_LIMIT)


def _pick(dim, pref, mult=LANES):
    if dim <= pref:
        return dim
    t = (pref // mult) * mult
    while t >= mult:
        if dim % t == 0:
            return t
        t -= mult
    return dim


def _alibi_slopes(n):
    def pow2(m):
        start = 2.0 ** (-(2.0 ** -(math.log2(m) - 3)))
        return [start ** (i + 1) for i in range(m)]
    if math.log2(n).is_integer():
        s = pow2(n)
    else:
        c = 2 ** math.floor(math.log2(n))
        s = pow2(c) + pow2(2 * c)[0::2][: n - c]
    s = np.asarray(s, dtype=np.float32)
    return -np.sort(-s)


def _my_place():
    return lax.axis_index("x"), lax.axis_index("y"), lax.axis_index("c")


def _all_gather_many(xs, name):
    n = len(xs)

    def body(*refs):
        x_refs, out_refs = refs[:n], refs[n:2 * n]
        send_sems, recv_sems, local_sems = refs[2 * n:]
        xi, yi, ci = _my_place()
        me, sibling = (xi, yi, ci), (xi, yi, 1 - ci)
        chips = [(1 - xi, yi), (xi, 1 - yi), (1 - xi, 1 - yi)]

        def slot(a, px, py, pc):
            return out_refs[a].at[4 * px + 2 * py + pc]

        def copy(a, k, block, to, src=None):
            return pltpu.make_async_remote_copy(
                src_ref=slot(a, *block) if src is None else src, dst_ref=slot(a, *block),
                send_sem=send_sems.at[7 * a + k], recv_sem=recv_sems.at[7 * a + k],
                device_id=to, device_id_type=MESH)

        mine = [pltpu.make_async_copy(x_refs[a], slot(a, *me), local_sems.at[a]) for a in range(n)]
        for cp in mine:
            cp.start()
        sent = []
        for a in range(n):
            first = [copy(a, 0, me, sibling, src=x_refs[a])]
            first += [copy(a, 1 + j, me, (*chip, ci), src=x_refs[a]) for j, chip in enumerate(chips)]
            for cp in first:
                cp.start()
            sent += first
        for a in range(n):
            for j, chip in enumerate(chips):
                copy(a, 1 + j, (*chip, ci), me).wait_recv()
                fwd = copy(a, 4 + j, (*chip, ci), sibling)
                fwd.start()
                sent.append(fwd)
        for a in range(n):
            copy(a, 0, sibling, me).wait_recv()
            for j, chip in enumerate(chips):
                copy(a, 4 + j, (*chip, 1 - ci), me).wait_recv()
        for cp in sent:
            cp.wait_send()
        for cp in mine:
            cp.wait()

    return pl.pallas_call(
        body, name=name,
        out_shape=[jax.ShapeDtypeStruct((N_DEV,) + x.shape, x.dtype) for x in xs],
        in_specs=[ANY] * n, out_specs=[ANY] * n,
        scratch_shapes=[pltpu.SemaphoreType.DMA((7 * n,)), pltpu.SemaphoreType.DMA((7 * n,)),
                        pltpu.SemaphoreType.DMA((n,))],
    )(*xs)


def _all_gather(x, name):
    return _all_gather_many([x], name)[0]


def _all_to_all_many(xs, name):
    n = len(xs)

    def body(*refs):
        x_refs, out_refs = refs[:n], refs[n:2 * n]
        send_sems, recv_sems, local_sems = refs[2 * n:]
        xi, yi, ci = _my_place()
        me = 4 * xi + 2 * yi + ci
        mine = [pltpu.make_async_copy(x_refs[a].at[me], out_refs[a].at[me], local_sems.at[a]) for a in range(n)]
        for cp in mine:
            cp.start()
        copies = []
        for a in range(n):
            for k in range(1, N_DEV):
                px = 1 - xi if k & 4 else xi
                py = 1 - yi if k & 2 else yi
                pc = 1 - ci if k & 1 else ci
                peer = 4 * px + 2 * py + pc
                cp = pltpu.make_async_remote_copy(
                    src_ref=x_refs[a].at[peer], dst_ref=out_refs[a].at[me],
                    send_sem=send_sems.at[7 * a + k - 1], recv_sem=recv_sems.at[7 * a + k - 1],
                    device_id=(px, py, pc), device_id_type=MESH)
                cp.start()
                copies.append(cp)
        for cp in copies:
            cp.wait()
        for cp in mine:
            cp.wait()

    return pl.pallas_call(
        body, name=name,
        out_shape=[jax.ShapeDtypeStruct(x.shape, x.dtype) for x in xs],
        in_specs=[ANY] * n, out_specs=[ANY] * n,
        scratch_shapes=[pltpu.SemaphoreType.DMA((7 * n,)), pltpu.SemaphoreType.DMA((7 * n,)),
                        pltpu.SemaphoreType.DMA((n,))],
    )(*xs)


def _all_to_all(x, name):
    return _all_to_all_many([x], name)[0]


HBM = pl.BlockSpec(memory_space=pltpu.HBM)
SEM = pl.BlockSpec(memory_space=pltpu.SEMAPHORE)
EFFECT = pltpu.SideEffectType.DATAFLOW_SIDE_EFFECTING


def _peer(k, xi, yi, ci):
    px = 1 - xi if k & 4 else xi
    py = 1 - yi if k & 2 else yi
    pc = 1 - ci if k & 1 else ci
    return (px, py, pc), 4 * px + 2 * py + pc


def _split_copies(x_refs, land_refs, send_sem, recv_sem, scatter):
    xi, yi, ci = _my_place()
    me = 4 * xi + 2 * yi + ci
    copies = []
    for p, (x_ref, land_ref) in enumerate(zip(x_refs, land_refs)):
        for k in range(1, N_DEV):
            place, peer = _peer(k, xi, yi, ci)
            copies.append(pltpu.make_async_remote_copy(
                src_ref=x_ref.at[peer] if scatter else x_ref, dst_ref=land_ref.at[me],
                send_sem=send_sem.at[7 * p + k - 1], recv_sem=recv_sem.at[7 * p + k - 1],
                device_id=place, device_id_type=MESH))
    return copies


def _comm_start(groups, name, scatter, after=None):
    sizes = [len(g) for g in groups]
    xs = [x for g in groups for x in g]
    n, ng = len(xs), len(groups)
    lands = [lax.empty(x.shape if scatter else (N_DEV,) + x.shape, x.dtype) for x in xs]
    starts = np.cumsum([0] + sizes)
    n_in = 2 * n + (after is not None)

    def body(*refs):
        x_refs, land_refs = refs[:n], refs[n:2 * n]
        send_sems, recv_sems = refs[n_in:n_in + ng], refs[n_in + ng:n_in + 2 * ng]
        token = refs[n_in + 2 * ng + 2 * n]
        for gi in range(ng):
            lo, hi = int(starts[gi]), int(starts[gi + 1])
            for cp in _split_copies(x_refs[lo:hi], land_refs[lo:hi], send_sems[gi], recv_sems[gi], scatter):
                cp.start()
        token[...] = jnp.zeros_like(token)

    sem_shapes = [pltpu.SemaphoreType.DMA((7 * m,)) for m in sizes]
    thru = [pltpu.HBM(a.shape, a.dtype) for a in xs + lands]
    res = pl.pallas_call(
        body, name=name,
        out_shape=sem_shapes + sem_shapes + thru + [jax.ShapeDtypeStruct((8, LANES), F32)],
        in_specs=[HBM] * n_in,
        out_specs=[SEM] * (2 * ng) + [HBM] * (2 * n) + [pl.BlockSpec(memory_space=pltpu.VMEM)],
        input_output_aliases={i: 2 * ng + i for i in range(2 * n)},
        compiler_params=pltpu.CompilerParams(has_side_effects=EFFECT),
    )(*[pltpu.with_memory_space_constraint(a, pltpu.HBM) for a in xs + lands + ([] if after is None else [after])])
    send_sems, recv_sems = res[:ng], res[ng:2 * ng]
    x_thru, land_thru = res[2 * ng:2 * ng + n], res[2 * ng + n:2 * ng + 2 * n]
    handles = []
    for gi in range(ng):
        lo, hi = int(starts[gi]), int(starts[gi + 1])
        handles.append((send_sems[gi], recv_sems[gi], list(x_thru[lo:hi]), list(land_thru[lo:hi]), scatter))
    return handles, res[-1][0, 0]


def _comm_wait(handle, after, name):
    send_sem, recv_sem, x_thru, land_thru, scatter = handle
    m = len(x_thru)

    blocks = [a.shape[1:] if scatter else a.shape for a in x_thru]

    def body(*refs):
        x_refs, land_refs = refs[:m], refs[m:2 * m]
        local_sems, stage = refs[4 * m + 3], refs[4 * m + 4:]
        xi, yi, ci = _my_place()
        me = 4 * xi + 2 * yi + ci
        load = [pltpu.make_async_copy(x_refs[p].at[me] if scatter else x_refs[p], stage[p], local_sems.at[2 * p])
                for p in range(m)]
        store = [pltpu.make_async_copy(stage[p], land_refs[p].at[me], local_sems.at[2 * p + 1]) for p in range(m)]
        for cp in load:
            cp.start()
        for p in range(m):
            load[p].wait()
            store[p].start()
        for cp in _split_copies(x_refs, land_refs, refs[2 * m], refs[2 * m + 1], scatter):
            cp.wait_send()
            cp.wait_recv()
        for cp in store:
            cp.wait()

    res = pl.pallas_call(
        body, name=name,
        out_shape=[pltpu.HBM(a.shape, a.dtype) for a in x_thru + land_thru],
        in_specs=[HBM] * (2 * m) + [SEM, SEM, ANY], out_specs=[HBM] * (2 * m),
        input_output_aliases={i: i for i in range(2 * m)},
        scratch_shapes=[pltpu.SemaphoreType.DMA((2 * m,))] + [pltpu.VMEM(b, a.dtype) for b, a in zip(blocks, x_thru)],
        compiler_params=pltpu.CompilerParams(has_side_effects=EFFECT),
    )(*x_thru, *land_thru, send_sem, recv_sem, after)
    return list(res[m:])


def _to_rows(vec):
    n = vec.shape[0]
    unit = 8 * LANES
    pad = (-n) % unit
    if pad:
        vec = jnp.concatenate([vec, jnp.zeros((pad,), vec.dtype)])
    return vec.reshape(-1, LANES)


def _mm(a, b, name, *, af="mk", bf="kn", of="mn", out_dtype=F32, tm=None, tn=None, tk=None,
        gate=None, resid=None, pre_dtype=None):
    if af == "mk":
        m, kk = a.shape
    elif af == "km":
        kk, m = a.shape
    elif af == "qmk":
        qa, m, tk = a.shape
        kk = qa * tk
    else:
        qa, kk, tm = a.shape
        m = qa * tm
    if bf == "kn":
        k2, n = b.shape
    elif bf == "nk":
        n, k2 = b.shape
    elif bf == "qkn":
        qb, k2, tn = b.shape
        n = qb * tn
    else:
        qb, n, tkb = b.shape
        k2 = qb * tkb
        assert af != "qmk" or tkb == tk
        tk = tkb
    assert kk == k2, (name, a.shape, b.shape, af, bf)
    tm = _pick(m, MM_TILE) if tm is None else tm
    tn = _pick(n, MM_TILE) if tn is None else tn
    tk = _pick(kk, MM_TILE) if tk is None else tk
    assert m % tm == 0 and n % tn == 0 and kk % tk == 0, (name, m, n, kk, tm, tn, tk)
    nk = kk // tk
    a_spec = {"mk": pl.BlockSpec((tm, tk), lambda i, j, k: (i, k)),
              "km": pl.BlockSpec((tk, tm), lambda i, j, k: (k, i)),
              "qmk": pl.BlockSpec((None, tm, tk), lambda i, j, k: (k, i, 0)),
              "qkm": pl.BlockSpec((None, tk, tm), lambda i, j, k: (i, k, 0))}[af]
    b_spec = {"kn": pl.BlockSpec((tk, tn), lambda i, j, k: (k, j)),
              "nk": pl.BlockSpec((tn, tk), lambda i, j, k: (j, k)),
              "qkn": pl.BlockSpec((None, tk, tn), lambda i, j, k: (j, k, 0)),
              "qnk": pl.BlockSpec((None, tn, tk), lambda i, j, k: (k, j, 0))}[bf]
    dims = (((1 if af in ("mk", "qmk") else 0,), (0 if bf in ("kn", "qkn") else 1,)), ((), ()))
    in_specs, args = [a_spec, b_spec], [a, b]
    if gate is not None:
        assert of == "mn"
        in_specs.append(pl.BlockSpec((1, tn), lambda i, j, k: (0, j)))
        args.append(gate)
    if resid is not None:
        assert of == "mn"
        in_specs.append(pl.BlockSpec((tm, tn), lambda i, j, k: (i, j)))
        args.append(resid)
    if of == "mn":
        o_spec, o_shape = pl.BlockSpec((tm, tn), lambda i, j, k: (i, j)), (m, n)
    else:
        o_spec, o_shape = pl.BlockSpec((None, tm, tn), lambda i, j, k: (j, i, 0)), (n // tn, m, tn)
    out_shape, out_specs = [jax.ShapeDtypeStruct(o_shape, out_dtype)], [o_spec]
    if pre_dtype is not None:
        out_shape.insert(0, jax.ShapeDtypeStruct(o_shape, pre_dtype))
        out_specs.insert(0, o_spec)
    n_in = len(args)
    n_out = len(out_shape)

    def body(*refs):
        a_ref, b_ref = refs[0], refs[1]
        extra = list(refs[2:n_in])
        outs = refs[n_in:n_in + n_out]
        gate_ref = extra.pop(0) if gate is not None else None
        resid_ref = extra.pop(0) if resid is not None else None

        def product():
            return lax.dot_general(a_ref[...].astype(BF), b_ref[...].astype(BF), dims, preferred_element_type=F32)

        def finish(r):
            if pre_dtype is not None:
                outs[0][...] = r.astype(pre_dtype)
            if gate_ref is not None:
                r = r * gate_ref[...]
            if resid_ref is not None:
                r = resid_ref[...] + r
            outs[-1][...] = r.astype(out_dtype)

        if nk == 1:
            finish(product())
        else:
            acc = refs[n_in + n_out]
            k = pl.program_id(2)

            @pl.when(k == 0)
            def _():
                acc[...] = product()

            @pl.when(k > 0)
            def _():
                acc[...] += product()

            @pl.when(k == nk - 1)
            def _():
                finish(acc[...])

    res = pl.pallas_call(
        body, name=name, grid=(m // tm, n // tn, nk),
        in_specs=in_specs, out_specs=out_specs, out_shape=out_shape,
        scratch_shapes=[pltpu.VMEM((tm, tn), F32)] if nk > 1 else [],
        compiler_params=_params(("parallel", "parallel", "arbitrary")),
    )(*args)
    return res if pre_dtype is not None else res[0]


def _ada_fwd(cond16, w, name):
    nl, d, n = w.shape

    def body(c_ref, w_ref, o_ref):
        o_ref[...] = jnp.dot(c_ref[...].astype(BF), w_ref[...].astype(BF), preferred_element_type=F32)

    return pl.pallas_call(
        body, name=name, grid=(nl,),
        in_specs=[pl.BlockSpec((16, d), lambda l: (0, 0)), pl.BlockSpec((None, d, n), lambda l: (l, 0, 0))],
        out_specs=pl.BlockSpec((None, 16, n), lambda l: (l, 0, 0)),
        out_shape=jax.ShapeDtypeStruct((nl, 16, n), F32),
        compiler_params=_params(("parallel",)),
    )(cond16, w)


def _ada_bwd(cond16, dmod, name):
    nl, _, n = dmod.shape
    d = cond16.shape[1]

    def body(c_ref, g_ref, o_ref):
        o_ref[...] = lax.dot_general(c_ref[...].astype(BF), g_ref[...].astype(BF), (((0,), (0,)), ((), ())),
                                     preferred_element_type=F32)

    return pl.pallas_call(
        body, name=name, grid=(nl,),
        in_specs=[pl.BlockSpec((16, d), lambda l: (0, 0)), pl.BlockSpec((None, 16, n), lambda l: (l, 0, 0))],
        out_specs=pl.BlockSpec((None, d, n), lambda l: (l, 0, 0)),
        out_shape=jax.ShapeDtypeStruct((nl, d, n), F32),
        compiler_params=_params(("parallel",)),
    )(cond16, dmod)


def _row_spec(tm, d):
    return pl.BlockSpec((tm, d), lambda i: (i, 0))


def _vec_spec(d):
    return pl.BlockSpec((1, d), lambda i: (0, 0))


def _norm_mod(x, g, sc, sh, name):
    s, d = x.shape
    tm = _pick(s, 512, 8)

    def body(x_ref, g_ref, sc_ref, sh_ref, o_ref):
        xv = x_ref[...]
        r = lax.rsqrt(jnp.mean(xv * xv, axis=-1, keepdims=True) + EPS)
        y = (xv * r) * g_ref[...]
        o_ref[...] = (y * (1.0 + sc_ref[...]) + sh_ref[...]).astype(BF)

    return pl.pallas_call(
        body, name=name, grid=(s // tm,),
        in_specs=[_row_spec(tm, d), _vec_spec(d), _vec_spec(d), _vec_spec(d)],
        out_specs=_row_spec(tm, d), out_shape=jax.ShapeDtypeStruct((s, d), BF),
        compiler_params=_params(("parallel",)),
    )(x, g, sc, sh)


def _norm_mod_bwd(x, dh, g, sc, dx_in, name):
    s, d = x.shape
    tm = _pick(s, 512, 8)

    def body(x_ref, dh_ref, g_ref, sc_ref, dxin_ref, dx_ref, dsh_ref, dw_ref):
        i = pl.program_id(0)
        xv = x_ref[...]
        dhv = dh_ref[...].astype(F32)
        r = lax.rsqrt(jnp.mean(xv * xv, axis=-1, keepdims=True) + EPS)
        xn = xv * r
        dxn = dhv * (g_ref[...] * (1.0 + sc_ref[...]))
        dx_ref[...] = dxin_ref[...] + r * (dxn - xn * jnp.mean(dxn * xn, axis=-1, keepdims=True))

        @pl.when(i == 0)
        def _():
            dsh_ref[...] = jnp.zeros_like(dsh_ref)
            dw_ref[...] = jnp.zeros_like(dw_ref)

        dsh_ref[...] += jnp.sum(dhv, axis=0, keepdims=True)
        dw_ref[...] += jnp.sum(dhv * xn, axis=0, keepdims=True)

    return pl.pallas_call(
        body, name=name, grid=(s // tm,),
        in_specs=[_row_spec(tm, d), _row_spec(tm, d), _vec_spec(d), _vec_spec(d), _row_spec(tm, d)],
        out_specs=[_row_spec(tm, d), _vec_spec(d), _vec_spec(d)],
        out_shape=[jax.ShapeDtypeStruct((s, d), F32), jax.ShapeDtypeStruct((1, d), F32),
                   jax.ShapeDtypeStruct((1, d), F32)],
        compiler_params=_params(("arbitrary",)),
    )(x, dh, g, sc, dx_in)


def _gate_bwd(dx, y, gate, name):
    s, d = dx.shape
    tm = _pick(s, 512, 8)

    def body(dx_ref, y_ref, g_ref, dy_ref, dg_ref):
        i = pl.program_id(0)
        dxv = dx_ref[...]
        dy_ref[...] = (dxv * g_ref[...]).astype(BF)

        @pl.when(i == 0)
        def _():
            dg_ref[...] = jnp.zeros_like(dg_ref)

        dg_ref[...] += jnp.sum(dxv * y_ref[...].astype(F32), axis=0, keepdims=True)

    return pl.pallas_call(
        body, name=name, grid=(s // tm,),
        in_specs=[_row_spec(tm, d), _row_spec(tm, d), _vec_spec(d)],
        out_specs=[_row_spec(tm, d), _vec_spec(d)],
        out_shape=[jax.ShapeDtypeStruct((s, d), BF), jax.ShapeDtypeStruct((1, d), F32)],
        compiler_params=_params(("arbitrary",)),
    )(dx, y, gate)


def _final_loss(x, tgt, g, name):
    s, d = x.shape
    tm = _pick(s, 512, 8)

    def body(x_ref, t_ref, g_ref, dx_ref, loss_ref, dg_ref):
        i = pl.program_id(0)
        xv = x_ref[...]
        gv = g_ref[...]
        r = lax.rsqrt(jnp.mean(xv * xv, axis=-1, keepdims=True) + EPS)
        xn = xv * r
        err = xn * gv - t_ref[...]
        dy = err * (1.0 / d)
        dxn = dy * gv
        dx_ref[...] = r * (dxn - xn * jnp.mean(dxn * xn, axis=-1, keepdims=True))

        @pl.when(i == 0)
        def _():
            loss_ref[...] = jnp.zeros_like(loss_ref)
            dg_ref[...] = jnp.zeros_like(dg_ref)

        part = 0.5 * jnp.sum(jnp.sum(err * err, axis=-1, keepdims=True) * (1.0 / d), axis=0, keepdims=True)
        loss_ref[...] += jnp.broadcast_to(part, loss_ref.shape)
        dg_ref[...] += jnp.sum(dy * xn, axis=0, keepdims=True)

    return pl.pallas_call(
        body, name=name, grid=(s // tm,),
        in_specs=[_row_spec(tm, d), _row_spec(tm, d), _vec_spec(d)],
        out_specs=[_row_spec(tm, d), pl.BlockSpec((8, LANES), lambda i: (0, 0)), _vec_spec(d)],
        out_shape=[jax.ShapeDtypeStruct((s, d), F32), jax.ShapeDtypeStruct((8, LANES), F32),
                   jax.ShapeDtypeStruct((1, d), F32)],
        compiler_params=_params(("arbitrary",)),
    )(x, tgt, g)


def _pool_counts(tm, gd, row0, w):
    t = lax.broadcasted_iota(jnp.int32, (tm, gd), 0) + row0
    return jnp.minimum(t + 1, w).astype(F32)


def _pool_fwd(u, name):
    s, d = u.shape
    tm = _pick(s, 256, POOL_HALO)
    gd = d // len(POOL_WINDOWS)
    per = tm // POOL_HALO

    def body(prev_ref, cur_ref, o_ref, ext):
        i = pl.program_id(0)
        ext[0:POOL_HALO, :] = jnp.where(i > 0, prev_ref[...], 0.0)
        ext[POOL_HALO:, :] = cur_ref[...]
        for g, w in enumerate(POOL_WINDOWS):
            cols = slice(g * gd, (g + 1) * gd)
            acc = ext[POOL_HALO:POOL_HALO + tm, cols]
            own = acc
            for k in range(1, w):
                acc = acc + ext[POOL_HALO - k:POOL_HALO - k + tm, cols]
            o_ref[:, cols] = (acc / _pool_counts(tm, gd, i * tm, w) - own).astype(BF)

    return pl.pallas_call(
        body, name=name, grid=(s // tm,),
        in_specs=[pl.BlockSpec((POOL_HALO, d), lambda i: (jnp.maximum(i * per - 1, 0), 0)), _row_spec(tm, d)],
        out_specs=_row_spec(tm, d), out_shape=jax.ShapeDtypeStruct((s, d), BF),
        scratch_shapes=[pltpu.VMEM((tm + POOL_HALO, d), F32)],
        compiler_params=_params(("parallel",)),
    )(u, u)


def _pool_bwd(dp, name):
    s, d = dp.shape
    tm = _pick(s, 256, POOL_HALO)
    gd = d // len(POOL_WINDOWS)
    per = tm // POOL_HALO
    nt = s // tm
    last_halo = s // POOL_HALO - 1

    def body(cur_ref, nxt_ref, o_ref, ext):
        i = pl.program_id(0)
        for g, w in enumerate(POOL_WINDOWS):
            cols = slice(g * gd, (g + 1) * gd)
            ext[0:tm, cols] = cur_ref[:, cols].astype(F32) / _pool_counts(tm, gd, i * tm, w)
            nxt = nxt_ref[:, cols].astype(F32) / _pool_counts(POOL_HALO, gd, (i + 1) * tm, w)
            ext[tm:, cols] = jnp.where(i < nt - 1, nxt, 0.0)
        for g, w in enumerate(POOL_WINDOWS):
            cols = slice(g * gd, (g + 1) * gd)
            acc = ext[0:tm, cols]
            for k in range(1, w):
                acc = acc + ext[k:k + tm, cols]
            o_ref[:, cols] = (acc - cur_ref[:, cols].astype(F32)).astype(BF)

    return pl.pallas_call(
        body, name=name, grid=(nt,),
        in_specs=[_row_spec(tm, d), pl.BlockSpec((POOL_HALO, d), lambda i: (jnp.minimum((i + 1) * per, last_halo), 0))],
        out_specs=_row_spec(tm, d), out_shape=jax.ShapeDtypeStruct((s, d), BF),
        scratch_shapes=[pltpu.VMEM((tm + POOL_HALO, d), F32)],
        compiler_params=_params(("parallel",)),
    )(dp, dp)


def _grp_fwd(p, w, scale, name):
    s, d = p.shape
    ng, gd, _ = w.shape
    tm = _pick(s, 1024, 8)

    def body(p_ref, w_ref, s_ref, z_ref, y_ref):
        z = jnp.dot(p_ref[...], w_ref[...].astype(BF), preferred_element_type=F32)
        z_ref[...] = z.astype(BF)
        y_ref[...] = (z * s_ref[...]).astype(BF)

    blk = pl.BlockSpec((tm, gd), lambda i, g: (i, g))
    return pl.pallas_call(
        body, name=name, grid=(s // tm, ng),
        in_specs=[blk, pl.BlockSpec((None, gd, gd), lambda i, g: (g, 0, 0)), pl.BlockSpec((1, gd), lambda i, g: (0, g))],
        out_specs=[blk, blk],
        out_shape=[jax.ShapeDtypeStruct((s, d), BF), jax.ShapeDtypeStruct((s, d), BF)],
        compiler_params=_params(("parallel", "parallel")),
    )(p, w, scale)


def _grp_bwd(dy, z, w, scale, name):
    s, d = dy.shape
    ng, gd, _ = w.shape
    tm = _pick(s, 1024, 8)

    def body(dy_ref, z_ref, w_ref, s_ref, dz_ref, dp_ref, ds_ref):
        i = pl.program_id(1)
        dyv = dy_ref[...].astype(F32)
        dz = (dyv * s_ref[...]).astype(BF)
        dz_ref[...] = dz
        dp_ref[...] = lax.dot_general(dz, w_ref[...].astype(BF), (((1,), (1,)), ((), ())),
                                      preferred_element_type=F32).astype(BF)

        @pl.when(i == 0)
        def _():
            ds_ref[...] = jnp.zeros_like(ds_ref)

        ds_ref[...] += jnp.sum(dyv * z_ref[...].astype(F32), axis=0, keepdims=True)

    blk = pl.BlockSpec((tm, gd), lambda g, i: (i, g))
    vec = pl.BlockSpec((1, gd), lambda g, i: (0, g))
    return pl.pallas_call(
        body, name=name, grid=(ng, s // tm),
        in_specs=[blk, blk, pl.BlockSpec((None, gd, gd), lambda g, i: (g, 0, 0)), vec],
        out_specs=[blk, blk, vec],
        out_shape=[jax.ShapeDtypeStruct((s, d), BF), jax.ShapeDtypeStruct((s, d), BF),
                   jax.ShapeDtypeStruct((1, d), F32)],
        compiler_params=_params(("parallel", "arbitrary")),
    )(dy, z, w, scale)


def _grp_dw(p, dz, ng, name):
    s, d = p.shape
    gd = d // ng
    tk = _pick(s, 1024, 8)

    def body(p_ref, dz_ref, o_ref):
        k = pl.program_id(1)

        @pl.when(k == 0)
        def _():
            o_ref[...] = jnp.zeros_like(o_ref)

        o_ref[...] += lax.dot_general(p_ref[...], dz_ref[...], (((0,), (0,)), ((), ())), preferred_element_type=F32)

    blk = pl.BlockSpec((tk, gd), lambda g, k: (k, g))
    return pl.pallas_call(
        body, name=name, grid=(ng, s // tk),
        in_specs=[blk, blk], out_specs=pl.BlockSpec((None, gd, gd), lambda g, k: (g, 0, 0)),
        out_shape=jax.ShapeDtypeStruct((ng, gd, gd), F32),
        compiler_params=_params(("parallel", "arbitrary")),
    )(p, dz)


def _sigmoid(a):
    return 0.5 * jnp.tanh(0.5 * a) + 0.5


def _ffn_act_down(up, cw, cb, w_down, gate, resid, name):
    _, nq, s, fq = up.shape
    d = w_down.shape[1]
    tm = _pick(s, 512, CONV_HALO)
    per = tm // CONV_HALO
    h = CONV_HALO
    rows = _pick(tm, FFN_ROWS, h)

    def body(prev_ref, a_ref, v_ref, w_ref, b_ref, wd_ref, g_ref, r_ref, act_ref, ffo_ref, x_ref, ext, acc):
        i, q = pl.program_id(0), pl.program_id(1)
        ext[0:h, :] = jnp.where(i > 0, prev_ref[...].astype(F32), 0.0)
        ext[h:, :] = a_ref[...].astype(F32)

        def step(c, carry):
            r0 = pl.multiple_of(c * rows, rows)
            e = ext[pl.ds(r0, rows + h), :]
            a2 = (b_ref[...] + e[h - 2:h - 2 + rows] * w_ref[0:1, :] + e[h - 1:h - 1 + rows] * w_ref[1:2, :]
                  + e[h:h + rows] * w_ref[2:3, :])
            vv = v_ref[pl.ds(r0, rows), :].astype(F32)
            act_ref[pl.ds(r0, rows), :] = (a2 * _sigmoid(a2) * vv).astype(BF)
            return carry

        lax.fori_loop(0, tm // rows, step, 0)
        part = jnp.dot(act_ref[...], wd_ref[...], preferred_element_type=F32)

        @pl.when(q == 0)
        def _():
            acc[...] = part

        @pl.when(q > 0)
        def _():
            acc[...] += part

        @pl.when(q == nq - 1)
        def _():
            r = acc[...]
            ffo_ref[...] = r.astype(BF)
            x_ref[...] = r_ref[...] + g_ref[...] * r

    row = pl.BlockSpec((tm, d), lambda i, q: (i, 0))
    return pl.pallas_call(
        body, name=name, grid=(s // tm, nq),
        in_specs=[pl.BlockSpec((None, None, h, fq), lambda i, q: (0, q, jnp.maximum(i * per - 1, 0), 0)),
                  pl.BlockSpec((None, None, tm, fq), lambda i, q: (0, q, i, 0)),
                  pl.BlockSpec((None, None, tm, fq), lambda i, q: (1, q, i, 0)),
                  pl.BlockSpec((None, CONV_WIDTH, fq), lambda i, q: (q, 0, 0)),
                  pl.BlockSpec((None, 1, fq), lambda i, q: (q, 0, 0)),
                  pl.BlockSpec((fq, d), lambda i, q: (q, 0)),
                  pl.BlockSpec((1, d), lambda i, q: (0, 0)), row],
        out_specs=[pl.BlockSpec((None, tm, fq), lambda i, q: (q, i, 0)), row, row],
        out_shape=[jax.ShapeDtypeStruct((nq, s, fq), BF), jax.ShapeDtypeStruct((s, d), BF),
                   jax.ShapeDtypeStruct((s, d), F32)],
        scratch_shapes=[pltpu.VMEM((tm + h, fq), F32), pltpu.VMEM((tm, d), F32)],
        compiler_params=_params(("parallel", "arbitrary")),
    )(up, up, up, cw, cb, w_down, gate, resid)


def _ffn_act_bwd_up(up, dact, cw, cb, w_up, name):
    _, nq, s, fq = up.shape
    d = w_up.shape[1]
    tm = _pick(s, 512, CONV_HALO)
    per = tm // CONV_HALO
    nt = s // tm
    last_halo = s // CONV_HALO - 1
    h = CONV_HALO
    te = tm + h
    rows = _pick(tm, FFN_ROWS, h)
    lanes = (((1,), (1,)), ((), ()))

    def body(ap_ref, a_ref, an_ref, v_ref, vn_ref, d_ref, dn_ref, w_ref, b_ref, wa_ref, wv_ref,
             dup_ref, dc_ref, dx_ref, ext_a, dap, sums, acc):
        i, q = pl.program_id(0), pl.program_id(1)
        ext_a[0:h, :] = jnp.where(i > 0, ap_ref[...].astype(F32), 0.0)
        ext_a[h:h + tm, :] = a_ref[...].astype(F32)
        ext_a[h + tm:, :] = an_ref[...].astype(F32)

        def pre_act(e, n):
            return (b_ref[...] + e[h - 2:h - 2 + n] * w_ref[0:1, :] + e[h - 1:h - 1 + n] * w_ref[1:2, :]
                    + e[h:h + n] * w_ref[2:3, :])

        def through_gate(a2, dd, vv):
            sig = _sigmoid(a2)
            return dd * vv * (sig * (1.0 + a2 * (1.0 - sig))), dd * (a2 * sig)

        def step1(c, carry):
            r0 = pl.multiple_of(c * rows, rows)
            a2 = pre_act(ext_a[pl.ds(r0, rows + h), :], rows)
            g, dgate = through_gate(a2, d_ref[pl.ds(r0, rows), :].astype(F32), v_ref[pl.ds(r0, rows), :].astype(F32))
            dap[pl.ds(r0, rows), :] = g
            dup_ref[1, pl.ds(r0, rows), :] = dgate.astype(BF)
            return carry

        lax.fori_loop(0, tm // rows, step1, 0)
        d_nxt = jnp.where(i < nt - 1, dn_ref[...].astype(F32), 0.0)
        g, _ = through_gate(pre_act(ext_a[tm:tm + 2 * h, :], h), d_nxt, vn_ref[...].astype(F32))
        dap[tm:, :] = g
        sums[...] = jnp.zeros_like(sums)

        def fold(t):
            part = t[0:8]
            for k in range(8, rows, 8):
                part = part + t[k:k + 8]
            return part

        def step2(c, carry):
            r0 = pl.multiple_of(c * rows, rows)
            gch = dap[pl.ds(r0, rows + h), :]
            g0 = gch[0:rows]
            dup_ref[0, pl.ds(r0, rows), :] = (gch[2:2 + rows] * w_ref[0:1, :] + gch[1:1 + rows] * w_ref[1:2, :]
                                              + g0 * w_ref[2:3, :]).astype(BF)
            e = ext_a[pl.ds(r0, rows + h), :]
            sums[0] += fold(g0 * e[h - 2:h - 2 + rows])
            sums[1] += fold(g0 * e[h - 1:h - 1 + rows])
            sums[2] += fold(g0 * e[h:h + rows])
            sums[3] += fold(g0)
            return carry

        lax.fori_loop(0, tm // rows, step2, 0)

        @pl.when(i == 0)
        def _():
            dc_ref[q] = jnp.zeros((8, fq), F32)

        for k in range(4):
            dc_ref[q, k:k + 1, :] += jnp.sum(sums[k], axis=0, keepdims=True)

        part = (lax.dot_general(dup_ref[0], wa_ref[...], lanes, preferred_element_type=F32)
                + lax.dot_general(dup_ref[1], wv_ref[...], lanes, preferred_element_type=F32))

        @pl.when(q == 0)
        def _():
            acc[...] = part

        @pl.when(q > 0)
        def _():
            acc[...] += part

        @pl.when(q == nq - 1)
        def _():
            dx_ref[...] = acc[...]

    def cur(half):
        return pl.BlockSpec((None, None, tm, fq), lambda i, q: (half, q, i, 0))

    def nxt(half):
        return pl.BlockSpec((None, None, h, fq), lambda i, q: (half, q, jnp.minimum((i + 1) * per, last_halo), 0))

    return pl.pallas_call(
        body, name=name, grid=(nt, nq),
        in_specs=[pl.BlockSpec((None, None, h, fq), lambda i, q: (0, q, jnp.maximum(i * per - 1, 0), 0)),
                  cur(0), nxt(0), cur(1), nxt(1),
                  pl.BlockSpec((None, tm, fq), lambda i, q: (q, i, 0)),
                  pl.BlockSpec((None, h, fq), lambda i, q: (q, jnp.minimum((i + 1) * per, last_halo), 0)),
                  pl.BlockSpec((None, CONV_WIDTH, fq), lambda i, q: (q, 0, 0)),
                  pl.BlockSpec((None, 1, fq), lambda i, q: (q, 0, 0)),
                  pl.BlockSpec((None, d, fq), lambda i, q: (q, 0, 0)),
                  pl.BlockSpec((None, d, fq), lambda i, q: (q + nq, 0, 0))],
        out_specs=[pl.BlockSpec((2, None, tm, fq), lambda i, q: (0, q, i, 0)),
                   pl.BlockSpec((nq, 8, fq), lambda i, q: (0, 0, 0)),
                   pl.BlockSpec((tm, d), lambda i, q: (i, 0))],
        out_shape=[jax.ShapeDtypeStruct((2, nq, s, fq), BF), jax.ShapeDtypeStruct((nq, 8, fq), F32),
                   jax.ShapeDtypeStruct((s, d), F32)],
        scratch_shapes=[pltpu.VMEM((tm + 2 * h, fq), F32), pltpu.VMEM((te, fq), F32), pltpu.VMEM((4, 8, fq), F32),
                        pltpu.VMEM((tm, d), F32)],
        compiler_params=_params(("arbitrary", "arbitrary")),
    )(up, up, up, up, up, dact, dact, cw, cb, w_up, w_up)


def _band(blk, n_steps, dil, first):
    qi = lax.broadcasted_iota(jnp.int32, (blk, 2 * blk), 0) + blk
    ki = lax.broadcasted_iota(jnp.int32, (blk, 2 * blk), 1)
    delta = qi - ki
    valid = (delta >= 0) & (delta <= n_steps) & ((ki >= blk) | jnp.logical_not(first))
    return valid, (delta * dil).astype(F32)


def _branch_views(q_all, kv, g, d):
    _, dil = BRANCHES[g]
    nbr = len(BRANCHES)
    sub = q_all.shape[0] // dil
    if dil == 1:
        return (q_all, kv, kv), ((nbr, g), (2 * nbr, g), (2 * nbr, nbr + g))
    cols = [q_all[:, g * d:(g + 1) * d], kv[:, g * d:(g + 1) * d], kv[:, (nbr + g) * d:(nbr + g + 1) * d]]
    return tuple(a.reshape(sub, dil * d) for a in cols), ((1, 0), (1, 0), (1, 0))


def _attn_fwd(q_all, kv, g, slopes, d, name):
    window, dil = BRANCHES[g]
    n_steps = window // dil
    blk = max(ATTN_BLOCK, n_steps)
    s = q_all.shape[0]
    sub = s // dil
    nb = sub // blk
    assert nb * blk == sub
    nh = d // HEAD_DIM
    (qv, kview, vview), (qcol, kcol, vcol) = _branch_views(q_all, kv, g, d)
    scale = HEAD_DIM ** -0.5

    def body(q_ref, kp_ref, kc_ref, vp_ref, vc_ref, o_ref, l_ref):
        j = pl.program_id(1)
        valid, dist = _band(blk, n_steps, dil, j == 0)
        qb = q_ref[...]
        kb = jnp.concatenate([kp_ref[...], kc_ref[...]], axis=0)
        vb = jnp.concatenate([vp_ref[...], vc_ref[...]], axis=0)
        first = lax.broadcasted_iota(jnp.int32, (1, 2 * HEAD_DIM), 1) < HEAD_DIM
        head_of_lane = lax.broadcasted_iota(jnp.int32, (1, LANES), 1) // (LANES // nh)
        lse_all = jnp.zeros((blk, LANES), F32)
        for hp in range(nh // 2):
            sl = slice(2 * hp * HEAD_DIM, 2 * (hp + 1) * HEAD_DIM)
            qp, kp, vp = qb[:, sl], kb[:, sl], vb[:, sl]
            out = None
            for half in range(2):
                sel = first if half == 0 else jnp.logical_not(first)
                sc = lax.dot_general(jnp.where(sel, qp, 0), kp, (((1,), (1,)), ((), ())),
                                     preferred_element_type=F32) * scale
                sc = jnp.where(valid, sc - float(slopes[2 * hp + half]) * dist, NEG)
                m = jnp.max(sc, axis=-1, keepdims=True)
                p = jnp.exp(sc - m)
                den = jnp.sum(p, axis=-1, keepdims=True)
                o = jnp.dot(p.astype(BF), jnp.where(sel, vp, 0), preferred_element_type=F32) / den
                out = o if half == 0 else out + o
                lse_all = jnp.where(head_of_lane == 2 * hp + half, m + jnp.log(den), lse_all)
            o_ref[:, sl] = out.astype(BF)
        l_ref[...] = lse_all

    def spec(col, prev):
        if prev:
            return pl.BlockSpec((blk, d), lambda r, j: (jnp.maximum(j - 1, 0), r * col[0] + col[1]))
        return pl.BlockSpec((blk, d), lambda r, j: (j, r * col[0] + col[1]))

    ospec = pl.BlockSpec((blk, d), lambda r, j: (j, r))
    o, lse = pl.pallas_call(
        body, name=name, grid=(dil, nb),
        in_specs=[spec(qcol, False), spec(kcol, True), spec(kcol, False), spec(vcol, True), spec(vcol, False)],
        out_specs=[ospec, pl.BlockSpec((blk, LANES), lambda r, j: (j, r))],
        out_shape=[jax.ShapeDtypeStruct((sub, dil * d), BF), jax.ShapeDtypeStruct((sub, dil * LANES), F32)],
        compiler_params=_params(("parallel", "parallel")),
    )(qv, kview, kview, vview, vview)
    return o.reshape(s, d), lse.reshape(s, LANES)


def _attn_combine(os, lses, name):
    s, d = os[0].shape
    tm = _pick(s, 512, 8)
    nbr = len(os)
    nh = d // HEAD_DIM
    per_head = LANES // nh

    def body(*refs):
        o_refs, l_refs = refs[:nbr], refs[nbr:2 * nbr]
        o_ref, lt_ref = refs[2 * nbr], refs[2 * nbr + 1]
        ls = [r[...] for r in l_refs]
        m = ls[0]
        for v in ls[1:]:
            m = jnp.maximum(m, v)
        tot = jnp.exp(ls[0] - m)
        for v in ls[1:]:
            tot = tot + jnp.exp(v - m)
        lt = m + jnp.log(tot)
        lt_ref[...] = lt
        ws = [jnp.exp(v - lt) for v in ls]
        first = lax.broadcasted_iota(jnp.int32, (1, 2 * HEAD_DIM), 1) < HEAD_DIM
        for hp in range(nh // 2):
            sl = slice(2 * hp * HEAD_DIM, 2 * (hp + 1) * HEAD_DIM)
            la, lb = 2 * hp * per_head, (2 * hp + 1) * per_head
            acc = None
            for w, r in zip(ws, o_refs):
                term = jnp.where(first, w[:, la:la + 1], w[:, lb:lb + 1]) * r[:, sl].astype(F32)
                acc = term if acc is None else acc + term
            o_ref[:, sl] = acc.astype(BF)

    lspec = _row_spec(tm, LANES)
    return pl.pallas_call(
        body, name=name, grid=(s // tm,),
        in_specs=[_row_spec(tm, d)] * nbr + [lspec] * nbr, out_specs=[_row_spec(tm, d), lspec],
        out_shape=[jax.ShapeDtypeStruct((s, d), BF), jax.ShapeDtypeStruct((s, LANES), F32)],
        compiler_params=_params(("parallel",)),
    )(*os, *lses)


def _attn_bwd(q_all, kv, do, o, lt, g, slopes, d, name, dk_in=None, dv_in=None):
    window, dil = BRANCHES[g]
    n_steps = window // dil
    blk = max(ATTN_BLOCK, n_steps)
    s = q_all.shape[0]
    sub = s // dil
    nb = sub // blk
    nh = d // HEAD_DIM
    (qv, kview, vview), (qcol, kcol, vcol) = _branch_views(q_all, kv, g, d)
    scale = HEAD_DIM ** -0.5
    acc_in = dk_in is not None

    def body(*refs):
        q_ref, do_ref, o_ref, lt_ref, kp_ref, kc_ref, vp_ref, vc_ref = refs[:8]
        n_in = 10 if acc_in else 8
        dkin_ref, dvin_ref = (refs[8], refs[9]) if acc_in else (None, None)
        dq_ref, dk_ref, dv_ref, keep_k, keep_v, part_k, part_v = refs[n_in:n_in + 7]
        t = pl.program_id(1)

        def emit(prev_k, prev_v):
            if acc_in:
                prev_k = prev_k + dkin_ref[...].astype(F32)
                prev_v = prev_v + dvin_ref[...].astype(F32)
            dk_ref[...] = prev_k.astype(BF)
            dv_ref[...] = prev_v.astype(BF)

        @pl.when(t < nb)
        def _():
            valid, dist = _band(blk, n_steps, dil, t == 0)
            qb = q_ref[...]
            dob = do_ref[...]
            ltb = lt_ref[...]
            kb = jnp.concatenate([kp_ref[...], kc_ref[...]], axis=0)
            vb = jnp.concatenate([vp_ref[...], vc_ref[...]], axis=0)
            first = lax.broadcasted_iota(jnp.int32, (1, 2 * HEAD_DIM), 1) < HEAD_DIM
            for hp in range(nh // 2):
                sl = slice(2 * hp * HEAD_DIM, 2 * (hp + 1) * HEAD_DIM)
                qp, kp, vp, dop = qb[:, sl], kb[:, sl], vb[:, sl], dob[:, sl]
                op = o_ref[:, sl].astype(F32)
                dq, dk, dv = None, None, None
                for half in range(2):
                    sel = first if half == 0 else jnp.logical_not(first)
                    qh, doh = jnp.where(sel, qp, 0), jnp.where(sel, dop, 0)
                    sc = lax.dot_general(qh, kp, (((1,), (1,)), ((), ())), preferred_element_type=F32) * scale
                    sc = sc - float(slopes[2 * hp + half]) * dist
                    lane = (2 * hp + half) * (LANES // nh)
                    lt_h = ltb[:, lane:lane + 1]
                    p = jnp.where(valid, jnp.exp(jnp.minimum(sc - lt_h, 30.0)), 0.0)
                    dlt = jnp.sum(doh.astype(F32) * op, axis=-1, keepdims=True)
                    dp = lax.dot_general(doh, vp, (((1,), (1,)), ((), ())), preferred_element_type=F32)
                    ds = (p * (dp - dlt)).astype(BF)
                    dq_h = jnp.dot(ds, jnp.where(sel, kp, 0), preferred_element_type=F32)
                    dk_h = lax.dot_general(ds, qh, (((0,), (0,)), ((), ())), preferred_element_type=F32)
                    dv_h = lax.dot_general(p.astype(BF), doh, (((0,), (0,)), ((), ())), preferred_element_type=F32)
                    dq = dq_h if half == 0 else dq + dq_h
                    dk = dk_h if half == 0 else dk + dk_h
                    dv = dv_h if half == 0 else dv + dv_h
                dq_ref[:, sl] = (dq * scale).astype(BF)
                part_k[:, sl] = dk * scale
                part_v[:, sl] = dv

            @pl.when(t > 0)
            def _():
                emit(keep_k[...] + part_k[0:blk, :], keep_v[...] + part_v[0:blk, :])

            keep_k[...] = part_k[blk:, :]
            keep_v[...] = part_v[blk:, :]

        @pl.when(t == nb)
        def _():
            emit(keep_k[...], keep_v[...])

    def qspec(col):
        return pl.BlockSpec((blk, d), lambda r, t: (jnp.minimum(t, nb - 1), r * col[0] + col[1]))

    def kspec(col, prev):
        if prev:
            return pl.BlockSpec((blk, d), lambda r, t: (jnp.maximum(jnp.minimum(t, nb - 1) - 1, 0), r * col[0] + col[1]))
        return qspec(col)

    kout = pl.BlockSpec((blk, d), lambda r, t: (jnp.maximum(t - 1, 0), r))
    one = (1, 0)
    ltspec = pl.BlockSpec((blk, LANES), lambda r, t: (jnp.minimum(t, nb - 1), r))
    in_specs = [qspec(qcol), qspec(one), qspec(one), ltspec,
                kspec(kcol, True), kspec(kcol, False), kspec(vcol, True), kspec(vcol, False)]
    args = [qv, do.reshape(sub, dil * d), o.reshape(sub, dil * d), lt.reshape(sub, dil * LANES),
            kview, kview, vview, vview]
    if acc_in:
        in_specs += [kout, kout]
        args += [dk_in.reshape(sub, dil * d), dv_in.reshape(sub, dil * d)]
    shp = jax.ShapeDtypeStruct((sub, dil * d), BF)
    dq, dk, dv = pl.pallas_call(
        body, name=name, grid=(dil, nb + 1),
        in_specs=in_specs, out_specs=[qspec(one), kout, kout], out_shape=[shp, shp, shp],
        scratch_shapes=[pltpu.VMEM((blk, d), F32), pltpu.VMEM((blk, d), F32),
                        pltpu.VMEM((2 * blk, d), F32), pltpu.VMEM((2 * blk, d), F32)],
        compiler_params=_params(("parallel", "arbitrary")),
    )(*args)
    return dq.reshape(s, d), dk.reshape(s, d), dv.reshape(s, d)


def _adamw(parts_list, w, m, v, name):
    nl, r, c = w.shape
    assert len(parts_list) == nl
    npart = parts_list[0].shape[0]
    tr = _pick(r, 256, 16)
    c1 = 1.0 / (1.0 - ADAM_B1 ** ADAM_STEP)
    c2 = 1.0 / (1.0 - ADAM_B2 ** ADAM_STEP)

    def body(*refs):
        p_refs = refs[:nl]
        w_ref, m_ref, v_ref, g_ref, d_ref, nm_ref, nv_ref = refs[nl:]
        layer = pl.program_id(0)
        for idx in range(nl):
            @pl.when(layer == idx)
            def _(p_ref=p_refs[idx]):
                g = p_ref[0].astype(F32)
                for k in range(1, npart):
                    g = g + p_ref[k].astype(F32)
                nm = ADAM_B1 * m_ref[...] + (1.0 - ADAM_B1) * g
                nv = ADAM_B2 * v_ref[...] + (1.0 - ADAM_B2) * (g * g)
                g_ref[...] = g
                nm_ref[...] = nm
                nv_ref[...] = nv
                d_ref[...] = -ADAM_LR * ((nm * c1) / (jnp.sqrt(nv * c2) + ADAM_EPS) + ADAM_WD * w_ref[...])

    def part_spec(idx):
        return pl.BlockSpec((npart, tr, c), lambda l, i: (0, jnp.where(l == idx, i, 0), 0))

    blk = pl.BlockSpec((None, tr, c), lambda l, i: (l, i, 0))
    shp = jax.ShapeDtypeStruct((nl, r, c), F32)
    return pl.pallas_call(
        body, name=name, grid=(nl, r // tr),
        in_specs=[part_spec(idx) for idx in range(nl)] + [blk, blk, blk],
        out_specs=[blk, blk, blk, blk], out_shape=[shp, shp, shp, shp],
        compiler_params=_params(("parallel", "parallel")),
    )(*parts_list, w, m, v)


def _full_from_slots(slots, shard_shape, axis):
    a = slots.reshape((N_DEV,) + tuple(shard_shape))
    a = jnp.moveaxis(a, 0, axis)
    full = list(shard_shape)
    full[axis] *= N_DEV
    return a.reshape(full)


def kernel(x, c, ada_w, ada_b, norm1_g, norm2_g, pool_w_in, pool_w_grp, pool_scale, pool_w_out, kv_norm_g, kv_ada_w, kv_ada_b, w_kv, attn_w_q, attn_w_o, ffn_w_up, ffn_conv_w, ffn_conv_b, ffn_w_down, final_g, loss_target, m_ada_w, m_ada_b, m_norm1_g, m_norm2_g, m_pool_w_in, m_pool_w_grp, m_pool_scale, m_pool_w_out, m_kv_norm_g, m_kv_ada_w, m_kv_ada_b, m_w_kv, m_attn_w_q, m_attn_w_o, m_ffn_w_up, m_ffn_conv_w, m_ffn_conv_b, m_ffn_w_down, m_final_g, v_ada_w, v_ada_b, v_norm1_g, v_norm2_g, v_pool_w_in, v_pool_w_grp, v_pool_scale, v_pool_w_out, v_kv_norm_g, v_kv_ada_w, v_kv_ada_b, v_w_kv, v_attn_w_q, v_attn_w_o, v_ffn_w_up, v_ffn_conv_w, v_ffn_conv_b, v_ffn_w_down, v_final_g):
    weights = dict(ada_w=ada_w, ada_b=ada_b, norm1_g=norm1_g, norm2_g=norm2_g, pool_w_in=pool_w_in,
                   pool_w_grp=pool_w_grp, pool_scale=pool_scale, pool_w_out=pool_w_out, kv_norm_g=kv_norm_g,
                   kv_ada_w=kv_ada_w, kv_ada_b=kv_ada_b, w_kv=w_kv, attn_w_q=attn_w_q, attn_w_o=attn_w_o,
                   ffn_w_up=ffn_w_up, ffn_conv_w=ffn_conv_w, ffn_conv_b=ffn_conv_b, ffn_w_down=ffn_w_down,
                   final_g=final_g)
    mom1 = dict(ada_w=m_ada_w, ada_b=m_ada_b, norm1_g=m_norm1_g, norm2_g=m_norm2_g, pool_w_in=m_pool_w_in,
                pool_w_grp=m_pool_w_grp, pool_scale=m_pool_scale, pool_w_out=m_pool_w_out, kv_norm_g=m_kv_norm_g,
                kv_ada_w=m_kv_ada_w, kv_ada_b=m_kv_ada_b, w_kv=m_w_kv, attn_w_q=m_attn_w_q, attn_w_o=m_attn_w_o,
                ffn_w_up=m_ffn_w_up, ffn_conv_w=m_ffn_conv_w, ffn_conv_b=m_ffn_conv_b, ffn_w_down=m_ffn_w_down,
                final_g=m_final_g)
    mom2 = dict(ada_w=v_ada_w, ada_b=v_ada_b, norm1_g=v_norm1_g, norm2_g=v_norm2_g, pool_w_in=v_pool_w_in,
                pool_w_grp=v_pool_w_grp, pool_scale=v_pool_scale, pool_w_out=v_pool_w_out, kv_norm_g=v_kv_norm_g,
                kv_ada_w=v_kv_ada_w, kv_ada_b=v_kv_ada_b, w_kv=v_w_kv, attn_w_q=v_attn_w_q, attn_w_o=v_attn_w_o,
                ffn_w_up=v_ffn_w_up, ffn_conv_w=v_ffn_conv_w, ffn_conv_b=v_ffn_conv_b, ffn_w_down=v_ffn_w_down,
                final_g=v_final_g)
    order = list(weights)

    seq, d = x.shape[1], x.shape[2]
    depth = ada_w.shape[0]
    n_pool = pool_w_in.shape[0]
    f = ffn_conv_b.shape[1]
    nbr = len(BRANCHES)
    nh = d // HEAD_DIM
    slopes = _alibi_slopes(nbr * nh).reshape(nbr, nh)
    me = 4 * lax.axis_index("x") + 2 * lax.axis_index("y") + lax.axis_index("c")
    xs = x[0]
    tgt = loss_target[0]

    def start_gathers(after):
        keys, groups = [], []
        for l in range(depth):
            if l < n_pool:
                parts = [("mixer", [pool_w_in[l], pool_w_grp[l].reshape(-1, pool_w_grp.shape[-1]), pool_w_out[l]])]
            else:
                j = l - n_pool
                parts = [("mixer", [attn_w_q[j]] + ([w_kv] if j == 0 else [])), ("out", [attn_w_o[j]])]
            for part, srcs in parts + [("ffn", [ffn_w_up[l], ffn_w_down[l]])]:
                keys.append((l, part))
                groups.append([a.astype(BF) for a in srcs])
        handles, _ = _comm_start(groups, "gather_start", scatter=False, after=after)
        return dict(zip(keys, handles))

    cond = c * (1.0 / (1.0 + jnp.exp(-c)))
    small_in = jnp.concatenate([cond.reshape(-1), ffn_conv_w.reshape(-1), pool_scale.reshape(-1)])
    n_small_in = small_in.shape[0]
    gath = _all_gather(_to_rows(small_in), "gather_small").reshape(N_DEV, -1)[:, :n_small_in]
    cond_all = gath[:, :d]
    o1 = d + ffn_conv_w.size
    conv_w_full = _full_from_slots(gath[:, d:o1], ffn_conv_w.shape, 2)
    pool_scale_full = _full_from_slots(gath[:, o1:], pool_scale.shape, 1)
    cond16 = jnp.concatenate([cond_all, jnp.zeros_like(cond_all)], axis=0)

    mod_part = _ada_fwd(cond16, ada_w, "ada_fwd")[:, :N_DEV]
    kv_part = _ada_fwd(cond16, kv_ada_w[None], "kv_ada_fwd")[0, :N_DEV]
    n_mod = depth * mod_part.shape[2] + kv_part.shape[1]
    send = jnp.concatenate([jnp.moveaxis(mod_part, 1, 0).reshape(N_DEV, -1), kv_part], axis=1)
    send_rows = jax.vmap(_to_rows)(send)
    got = _all_to_all(send_rows, "exchange_mod").reshape(N_DEV, -1)[:, :n_mod]
    ncol = mod_part.shape[2]
    mods = []
    for l in range(depth):
        row = got[:, l * ncol:(l + 1) * ncol].reshape(1, -1) + ada_b[l][None]
        mods.append([row[:, k * d:(k + 1) * d] for k in range(6)])
    kv_row = got[:, depth * ncol:].reshape(1, -1) + kv_ada_b[None]
    kv_shift, kv_scale = kv_row[:, :d], kv_row[:, d:]
    gather_handles = start_gathers(got)

    def vec(a):
        return a.reshape(1, -1)

    tall = _pick(seq, MM_ROWS)
    nq = 4
    fq = f // nq
    ng = len(POOL_WINDOWS)
    cw_slots = jnp.moveaxis(conv_w_full.reshape(depth, CONV_WIDTH, nq, fq), 2, 1)
    cb_slots = ffn_conv_b.reshape(depth, nq, 1, fq)

    saved = []
    xcur = xs
    kvs = None
    hkv = None
    x_kv = None
    w_kv_slots = None
    for l in range(depth):
        sh1, sc1, g1, sh2, sc2, g2 = mods[l]
        st = dict(x=xcur)
        h = _norm_mod(xcur, vec(norm1_g[l]), sc1, sh1, f"norm1_{l}")
        st["h"] = h
        gw = _comm_wait(gather_handles[l, "mixer"], h, f"gather_wait_mixer_{l}")
        if l < n_pool:
            w_in = gw[0].reshape(d, d)
            w_grp = jnp.moveaxis(gw[1].reshape(N_DEV, ng, -1, d // ng), 0, 1).reshape(ng, d // ng, d // ng)
            w_out = gw[2].reshape(d, d)
            u = _mm(h, w_in, f"pool_in_{l}")
            pooled = _pool_fwd(u, f"pool_fwd_{l}")
            z, y = _grp_fwd(pooled, w_grp, vec(pool_scale_full[l]), f"grp_fwd_{l}")
            mix, x1 = _mm(y, w_out, f"pool_out_{l}", gate=g1, resid=xcur, pre_dtype=BF)
            st.update(pooled=pooled, z=z, y=y, w_in=w_in, w_grp=w_grp, w_out=w_out)
        else:
            j = l - n_pool
            w_q_slots = gw[0]
            if j == 0:
                w_kv_slots = gw[1]
                x_kv = xcur
                hkv = _norm_mod(xcur, vec(kv_norm_g), kv_scale, kv_shift, "norm_kv")
                kvs = _mm(hkv, w_kv_slots, "kv_proj", bf="qkn", out_dtype=BF, tm=tall)
            q = _mm(h, w_q_slots, f"q_proj_{l}", bf="qkn", out_dtype=BF, tm=tall)
            outs, lses = [], []
            for g in range(nbr):
                og, lg = _attn_fwd(q, kvs, g, slopes[g], d, f"attn_fwd_{l}_{g}")
                outs.append(og)
                lses.append(lg)
            o, lt = _attn_combine(outs, lses, f"attn_mix_{l}")
            w_o = _comm_wait(gather_handles[l, "out"], o, f"gather_wait_out_{l}")[0].reshape(d, d)
            mix, x1 = _mm(o, w_o, f"attn_out_{l}", gate=g1, resid=xcur, pre_dtype=BF)
            st.update(q=q, o=o, lt=lt, w_q=w_q_slots, w_o=w_o)
        h2 = _norm_mod(x1, vec(norm2_g[l]), sc2, sh2, f"norm2_{l}")
        w_up_slots, w_down = _comm_wait(gather_handles[l, "ffn"], h2, f"gather_wait_ffn_{l}")
        w_down = w_down.reshape(f, d)
        st.update(w_up=w_up_slots, w_down=w_down)
        up = _mm(h2, w_up_slots, f"ffn_up_{l}", bf="qkn", of="qmn", out_dtype=BF, tm=tall).reshape(2, nq, seq, fq)
        act, ffo, x2 = _ffn_act_down(up, cw_slots[l], cb_slots[l], w_down, g2, x1, f"ffn_act_down_{l}")
        st.update(mix=mix, x1=x1, h2=h2, up=up, act=act, ffo=ffo)
        saved.append(st)
        xcur = x2

    dx, loss_blk, d_final_g = _final_loss(xcur, tgt, vec(final_g), "final_loss")
    loss = lax.psum(loss_blk[0, 0], ("x", "y", "c"))

    d_mod = [None] * depth
    d_n1 = [None] * depth
    d_n2 = [None] * depth
    d_conv = [None] * depth
    d_pscale = [None] * n_pool
    dk_acc = [None] * nbr
    dv_acc = [None] * nbr
    ffn_handles = [None] * depth
    mixer_handles = [None] * depth
    tok = 0.0
    for l in reversed(range(depth)):
        sh1, sc1, g1, sh2, sc2, g2 = mods[l]
        st = saved[l]
        dffo, dg2 = _gate_bwd(dx, st["ffo"], g2 + tok, f"gate2_bwd_{l}")
        g_down = _mm(st["act"], dffo, f"ffn_down_dw_{l}", af="qkm", out_dtype=BF)
        dact = _mm(dffo, st["w_down"], f"ffn_down_dx_{l}", bf="nk", of="qmn", tn=fq, out_dtype=BF, tm=tall)
        dup, dc, dh2 = _ffn_act_bwd_up(st["up"], dact, cw_slots[l], cb_slots[l], st["w_up"], f"ffn_act_bwd_up_{l}")
        dup = dup.reshape(N_DEV, seq, -1)
        d_conv[l] = dc
        g_up = _mm(st["h2"], dup, f"ffn_up_dw_{l}", af="km", bf="qkn", of="qmn", out_dtype=BF)
        (ffn_handles[l],), tok = _comm_start([[g_up, g_down.reshape(N_DEV, -1, d)]], f"exchange_start_ffn_{l}",
                                             scatter=True)
        dx1, dsh2, dw2 = _norm_mod_bwd(st["x1"], dh2, vec(norm2_g[l]), sc2 + tok, dx, f"norm2_bwd_{l}")
        d_n2[l] = dw2 * (1.0 + sc2)
        dsc2 = dw2 * vec(norm2_g[l])

        dmix, dg1 = _gate_bwd(dx1, st["mix"], g1, f"gate1_bwd_{l}")
        if l < n_pool:
            g_out = _mm(st["y"], dmix, f"pool_out_dw_{l}", af="km", out_dtype=BF)
            dy = _mm(dmix, st["w_out"], f"pool_out_dx_{l}", bf="nk", out_dtype=BF)
            dz, dpool, dps = _grp_bwd(dy, st["z"], st["w_grp"], vec(pool_scale_full[l]), f"grp_bwd_{l}")
            d_pscale[l] = dps
            g_grp = _grp_dw(st["pooled"], dz, ng, f"grp_dw_{l}")
            du = _pool_bwd(dpool, f"pool_bwd_{l}")
            g_in = _mm(st["h"], du, f"pool_in_dw_{l}", af="km", out_dtype=BF)
            dh = _mm(du, st["w_in"], f"pool_in_dx_{l}", bf="nk")
            g_grp_slots = jnp.moveaxis(g_grp.reshape(ng, N_DEV, -1, d // ng), 1, 0).reshape(N_DEV, -1, d // ng)
            send = [g_in.reshape(N_DEV, -1, d), g_grp_slots.astype(BF), g_out.reshape(N_DEV, -1, d)]
        else:
            j = l - n_pool
            g_o = _mm(st["o"], dmix, f"attn_out_dw_{l}", af="km", out_dtype=BF)
            do = _mm(dmix, st["w_o"], f"attn_out_dx_{l}", bf="nk", out_dtype=BF)
            dqs = []
            for g in range(nbr):
                dq_g, dk_g, dv_g = _attn_bwd(st["q"], kvs, do, st["o"], st["lt"], g, slopes[g], d,
                                             f"attn_bwd_{l}_{g}", dk_in=dk_acc[g], dv_in=dv_acc[g])
                dqs.append(dq_g)
                dk_acc[g], dv_acc[g] = dk_g, dv_g
            dq = jnp.concatenate(dqs, axis=1)
            nqc = st["w_q"].shape[2]
            g_q = _mm(st["h"], dq, f"q_proj_dw_{l}", af="km", of="qmn", tn=nqc, out_dtype=BF)
            dh = _mm(dq, st["w_q"], f"q_proj_dx_{l}", bf="qnk")
            send = [g_q, g_o.reshape(N_DEV, -1, d)]
        dx0, dsh1, dw1 = _norm_mod_bwd(st["x"], dh, vec(norm1_g[l]), sc1, dx1, f"norm1_bwd_{l}")
        d_n1[l] = dw1 * (1.0 + sc1)
        dsc1 = dw1 * vec(norm1_g[l])
        d_mod[l] = jnp.concatenate([dsh1, dsc1, dg1, dsh2, dsc2, dg2], axis=1)
        dx = dx0
        if l == n_pool:
            dkv = jnp.concatenate(dk_acc + dv_acc, axis=1)
            nkc = w_kv_slots.shape[2]
            g_kv = _mm(hkv, dkv, "kv_proj_dw", af="km", of="qmn", tn=nkc, out_dtype=BF)
            dhkv = _mm(dkv, w_kv_slots, "kv_proj_dx", bf="qnk")
            dx, dsh_kv, dw_kv = _norm_mod_bwd(x_kv, dhkv, vec(kv_norm_g), kv_scale, dx, "norm_kv_bwd")
            d_kv_norm = dw_kv * (1.0 + kv_scale)
            d_kv_mod = jnp.concatenate([dsh_kv, dw_kv * vec(kv_norm_g)], axis=1)
            send.append(g_kv)
        (mixer_handles[l],), tok = _comm_start([send], f"exchange_start_mixer_{l}", scatter=True)
    grad_x = dx[None]
    d_final_g = d_final_g + tok

    small = [jnp.concatenate(d_mod, axis=1).reshape(-1), d_kv_mod.reshape(-1),
             jnp.concatenate(d_n1, axis=0).reshape(-1), jnp.concatenate(d_n2, axis=0).reshape(-1),
             d_kv_norm.reshape(-1), d_final_g.reshape(-1),
             jnp.stack([dcl[:, 3, :] for dcl in d_conv]).reshape(-1),
             jnp.stack([jnp.moveaxis(dcl[:, 0:CONV_WIDTH, :], 0, 1) for dcl in d_conv]).reshape(-1),
             jnp.concatenate(d_pscale, axis=0).reshape(-1)]
    sizes = [a.shape[0] for a in small]
    small_rows = _to_rows(jnp.concatenate(small))
    small_all = _all_gather(small_rows, "gather_small_grads")
    dmod_all = small_all.reshape(N_DEV, -1)[:, :sizes[0] + sizes[1]]

    dmod16 = jnp.concatenate([dmod_all, jnp.zeros_like(dmod_all)], axis=0)
    dm = dmod16[:, :sizes[0]].reshape(16, depth, N_DEV, ncol)
    dm_mine = lax.dynamic_index_in_dim(dm, me, axis=2, keepdims=False)
    g_ada_w = _ada_bwd(cond16, jnp.moveaxis(dm_mine, 0, 1), "ada_bwd")
    nkv = kv_part.shape[1]
    dkm = dmod16[:, sizes[0]:].reshape(16, N_DEV, nkv)
    dkm_mine = lax.dynamic_index_in_dim(dkm, me, axis=1, keepdims=False)
    g_kv_ada_w = _ada_bwd(cond16, dkm_mine[None], "kv_ada_bwd")

    res = {}

    def update(n, parts_list, shape3):
        w3, m3, v3 = (a[n].reshape(shape3) for a in (weights, mom1, mom2))
        outs = _adamw(parts_list, w3, m3, v3, f"adamw_{n}")
        res[n] = [a.reshape(weights[n].shape) for a in outs]

    update("ada_w", [g_ada_w[l][None] for l in range(depth)], ada_w.shape)
    update("kv_ada_w", [g_kv_ada_w], (1,) + kv_ada_w.shape)
    after = res["ada_w"][0]
    parts_ffn = [_comm_wait(ffn_handles[l], after, f"exchange_wait_ffn_{l}") for l in reversed(range(depth))][::-1]
    parts = [_comm_wait(mixer_handles[l], after, f"exchange_wait_mixer_{l}") for l in reversed(range(depth))][::-1]
    pool_layers, attn_layers = range(n_pool), range(n_pool, depth)
    update("ffn_w_up", [parts_ffn[l][0] for l in range(depth)], ffn_w_up.shape)
    update("ffn_w_down", [parts_ffn[l][1] for l in range(depth)], ffn_w_down.shape)
    update("attn_w_q", [parts[l][0] for l in attn_layers], attn_w_q.shape)
    update("attn_w_o", [parts[l][1] for l in attn_layers], (depth - n_pool, -1, d))
    update("w_kv", [parts[n_pool][2]], (1,) + w_kv.shape)
    update("pool_w_in", [parts[l][0] for l in pool_layers], (n_pool, -1, d))
    update("pool_w_grp", [parts[l][1] for l in pool_layers], (n_pool, -1, d // ng))
    update("pool_w_out", [parts[l][2] for l in pool_layers], (n_pool, -1, d))

    tot = small_all.reshape(N_DEV, -1)
    offs = np.cumsum([0] + sizes)
    seg = {k: (int(offs[i]), int(offs[i + 1])) for i, k in enumerate(
        ["mod", "kv_mod", "n1", "n2", "kv_norm", "final", "conv_b", "conv_w", "pscale"])}

    def rows_of(a, b):
        return tot[:, a:b]

    nf8 = f // N_DEV
    conv_w_parts = lax.dynamic_slice_in_dim(
        rows_of(*seg["conv_w"]).reshape(N_DEV, depth, CONV_WIDTH, N_DEV, nf8), me, 1, axis=3).reshape(N_DEV, -1)
    nd8 = d // N_DEV
    pscale_parts = lax.dynamic_slice_in_dim(
        rows_of(*seg["pscale"]).reshape(N_DEV, n_pool, N_DEV, nd8), me, 1, axis=2).reshape(N_DEV, -1)
    small_names = ["ada_b", "norm1_g", "norm2_g", "pool_scale", "kv_norm_g", "kv_ada_b", "ffn_conv_w",
                   "ffn_conv_b", "final_g"]
    small_parts = [rows_of(*seg["mod"]), rows_of(*seg["n1"]), rows_of(*seg["n2"]), pscale_parts,
                   rows_of(*seg["kv_norm"]), rows_of(*seg["kv_mod"]), conv_w_parts, rows_of(*seg["conv_b"]),
                   rows_of(*seg["final"])]
    sp = jnp.concatenate(small_parts, axis=1)
    n_sp = sp.shape[1]
    sp_rows = jax.vmap(_to_rows)(sp)

    def packed(src):
        return _to_rows(jnp.concatenate([src[n].reshape(-1) for n in small_names]))[None]

    outs = _adamw([sp_rows], packed(weights), packed(mom1), packed(mom2), "adamw_small")
    outs = [a.reshape(-1)[:n_sp] for a in outs]
    off = 0
    for n in small_names:
        size = weights[n].size
        res[n] = [a[off:off + size].reshape(weights[n].shape) for a in outs]
        off += size

    grads = [res[n][0] for n in order]
    deltas = [res[n][1] for n in order]
    new_m = [res[n][2] for n in order]
    new_v = [res[n][3] for n in order]
    return (loss, grad_x, *grads, *deltas, *new_m, *new_v)
```

```python
import math

import numpy as np
import jax
import jax.numpy as jnp
from jax import lax
from jax.experimental import pallas as pl
from jax.experimental.pallas import tpu as pltpu

F32 = jnp.float32
BF = jnp.bfloat16

POOL_WINDOWS = (2, 4, 8, 16)
BRANCHES = ((128, 1), (512, 4), (2048, 16))
HEAD_DIM = 64
ATTN_BLOCK = 128
CONV_WIDTH = 3
EPS = 1e-6
ADAM_LR = 0.001
ADAM_B1 = 0.9
ADAM_B2 = 0.999
ADAM_EPS = 1e-08
ADAM_WD = 0.01
ADAM_STEP = 10

N_DEV = 8
LANES = 128
POOL_HALO = 16
CONV_HALO = 8
FFN_ROWS = 16
FFN_GROUP = 128
VMEM_LIMIT = 48 * 1024 * 1024
MM_TILE = 1024
MM_ROWS = 2048
NEG = -1e30

MESH = pl.DeviceIdType.MESH
ANY = pl.BlockSpec(memory_space=pl.ANY)


def _params(sem=None):
    if sem is None:
        return pltpu.CompilerParams(vmem_limit_bytes=VMEM_LIMIT)
    return pltpu.CompilerParams(dimension_semantics=sem, vmem_limit_bytes=VMEM_LIMIT)


def _pick(dim, pref, mult=LANES):
    if dim <= pref:
        return dim
    t = (pref // mult) * mult
    while t >= mult:
        if dim % t == 0:
            return t
        t -= mult
    return dim


def _alibi_slopes(n):
    def pow2(m):
        start = 2.0 ** (-(2.0 ** -(math.log2(m) - 3)))
        return [start ** (i + 1) for i in range(m)]
    if math.log2(n).is_integer():
        s = pow2(n)
    else:
        c = 2 ** math.floor(math.log2(n))
        s = pow2(c) + pow2(2 * c)[0::2][: n - c]
    s = np.asarray(s, dtype=np.float32)
    return -np.sort(-s)


def _my_place():
    return lax.axis_index("x"), lax.axis_index("y"), lax.axis_index("c")


def _all_gather_many(xs, name):
    n = len(xs)

    def body(*refs):
        x_refs, out_refs = refs[:n], refs[n:2 * n]
        send_sems, recv_sems, local_sems = refs[2 * n:]
        xi, yi, ci = _my_place()
        me, sibling = (xi, yi, ci), (xi, yi, 1 - ci)
        chips = [(1 - xi, yi), (xi, 1 - yi), (1 - xi, 1 - yi)]

        def slot(a, px, py, pc):
            return out_refs[a].at[4 * px + 2 * py + pc]

        def copy(a, k, block, to, src=None):
            return pltpu.make_async_remote_copy(
                src_ref=slot(a, *block) if src is None else src, dst_ref=slot(a, *block),
                send_sem=send_sems.at[7 * a + k], recv_sem=recv_sems.at[7 * a + k],
                device_id=to, device_id_type=MESH)

        mine = [pltpu.make_async_copy(x_refs[a], slot(a, *me), local_sems.at[a]) for a in range(n)]
        for cp in mine:
            cp.start()
        sent = []
        for a in range(n):
            first = [copy(a, 0, me, sibling, src=x_refs[a])]
            first += [copy(a, 1 + j, me, (*chip, ci), src=x_refs[a]) for j, chip in enumerate(chips)]
            for cp in first:
                cp.start()
            sent += first
        for a in range(n):
            for j, chip in enumerate(chips):
                copy(a, 1 + j, (*chip, ci), me).wait_recv()
                fwd = copy(a, 4 + j, (*chip, ci), sibling)
                fwd.start()
                sent.append(fwd)
        for a in range(n):
            copy(a, 0, sibling, me).wait_recv()
            for j, chip in enumerate(chips):
                copy(a, 4 + j, (*chip, 1 - ci), me).wait_recv()
        for cp in sent:
            cp.wait_send()
        for cp in mine:
            cp.wait()

    return pl.pallas_call(
        body, name=name,
        out_shape=[jax.ShapeDtypeStruct((N_DEV,) + x.shape, x.dtype) for x in xs],
        in_specs=[ANY] * n, out_specs=[ANY] * n,
        scratch_shapes=[pltpu.SemaphoreType.DMA((7 * n,)), pltpu.SemaphoreType.DMA((7 * n,)),
                        pltpu.SemaphoreType.DMA((n,))],
    )(*xs)


def _all_gather(x, name):
    return _all_gather_many([x], name)[0]


def _all_to_all_many(xs, name):
    n = len(xs)

    def body(*refs):
        x_refs, out_refs = refs[:n], refs[n:2 * n]
        send_sems, recv_sems, local_sems = refs[2 * n:]
        xi, yi, ci = _my_place()
        me = 4 * xi + 2 * yi + ci
        mine = [pltpu.make_async_copy(x_refs[a].at[me], out_refs[a].at[me], local_sems.at[a]) for a in range(n)]
        for cp in mine:
            cp.start()
        copies = []
        for a in range(n):
            for k in range(1, N_DEV):
                px = 1 - xi if k & 4 else xi
                py = 1 - yi if k & 2 else yi
                pc = 1 - ci if k & 1 else ci
                peer = 4 * px + 2 * py + pc
                cp = pltpu.make_async_remote_copy(
                    src_ref=x_refs[a].at[peer], dst_ref=out_refs[a].at[me],
                    send_sem=send_sems.at[7 * a + k - 1], recv_sem=recv_sems.at[7 * a + k - 1],
                    device_id=(px, py, pc), device_id_type=MESH)
                cp.start()
                copies.append(cp)
        for cp in copies:
            cp.wait()
        for cp in mine:
            cp.wait()

    return pl.pallas_call(
        body, name=name,
        out_shape=[jax.ShapeDtypeStruct(x.shape, x.dtype) for x in xs],
        in_specs=[ANY] * n, out_specs=[ANY] * n,
        scratch_shapes=[pltpu.SemaphoreType.DMA((7 * n,)), pltpu.SemaphoreType.DMA((7 * n,)),
                        pltpu.SemaphoreType.DMA((n,))],
    )(*xs)


def _all_to_all(x, name):
    return _all_to_all_many([x], name)[0]


HBM = pl.BlockSpec(memory_space=pltpu.HBM)
SEM = pl.BlockSpec(memory_space=pltpu.SEMAPHORE)
EFFECT = pltpu.SideEffectType.DATAFLOW_SIDE_EFFECTING


def _peer(k, xi, yi, ci):
    px = 1 - xi if k & 4 else xi
    py = 1 - yi if k & 2 else yi
    pc = 1 - ci if k & 1 else ci
    return (px, py, pc), 4 * px + 2 * py + pc


def _split_copies(x_refs, land_refs, send_sem, recv_sem, scatter):
    xi, yi, ci = _my_place()
    me = 4 * xi + 2 * yi + ci
    copies = []
    for p, (x_ref, land_ref) in enumerate(zip(x_refs, land_refs)):
        for k in range(1, N_DEV):
            place, peer = _peer(k, xi, yi, ci)
            copies.append(pltpu.make_async_remote_copy(
                src_ref=x_ref.at[peer] if scatter else x_ref, dst_ref=land_ref.at[me],
                send_sem=send_sem.at[7 * p + k - 1], recv_sem=recv_sem.at[7 * p + k - 1],
                device_id=place, device_id_type=MESH))
    return copies


def _comm_start(groups, name, scatter, after=None):
    sizes = [len(g) for g in groups]
    xs = [x for g in groups for x in g]
    n, ng = len(xs), len(groups)
    lands = [lax.empty(x.shape if scatter else (N_DEV,) + x.shape, x.dtype) for x in xs]
    starts = np.cumsum([0] + sizes)
    n_in = 2 * n + (after is not None)

    def body(*refs):
        x_refs, land_refs = refs[:n], refs[n:2 * n]
        send_sems, recv_sems = refs[n_in:n_in + ng], refs[n_in + ng:n_in + 2 * ng]
        token = refs[n_in + 2 * ng + 2 * n]
        for gi in range(ng):
            lo, hi = int(starts[gi]), int(starts[gi + 1])
            for cp in _split_copies(x_refs[lo:hi], land_refs[lo:hi], send_sems[gi], recv_sems[gi], scatter):
                cp.start()
        token[...] = jnp.zeros_like(token)

    sem_shapes = [pltpu.SemaphoreType.DMA((7 * m,)) for m in sizes]
    thru = [pltpu.HBM(a.shape, a.dtype) for a in xs + lands]
    res = pl.pallas_call(
        body, name=name,
        out_shape=sem_shapes + sem_shapes + thru + [jax.ShapeDtypeStruct((8, LANES), F32)],
        in_specs=[HBM] * n_in,
        out_specs=[SEM] * (2 * ng) + [HBM] * (2 * n) + [pl.BlockSpec(memory_space=pltpu.VMEM)],
        input_output_aliases={i: 2 * ng + i for i in range(2 * n)},
        compiler_params=pltpu.CompilerParams(has_side_effects=EFFECT),
    )(*[pltpu.with_memory_space_constraint(a, pltpu.HBM) for a in xs + lands + ([] if after is None else [after])])
    send_sems, recv_sems = res[:ng], res[ng:2 * ng]
    x_thru, land_thru = res[2 * ng:2 * ng + n], res[2 * ng + n:2 * ng + 2 * n]
    handles = []
    for gi in range(ng):
        lo, hi = int(starts[gi]), int(starts[gi + 1])
        handles.append((send_sems[gi], recv_sems[gi], list(x_thru[lo:hi]), list(land_thru[lo:hi]), scatter))
    return handles, res[-1][0, 0]


def _comm_wait(handle, after, name):
    send_sem, recv_sem, x_thru, land_thru, scatter = handle
    m = len(x_thru)

    blocks = [a.shape[1:] if scatter else a.shape for a in x_thru]

    def body(*refs):
        x_refs, land_refs = refs[:m], refs[m:2 * m]
        local_sems, stage = refs[4 * m + 3], refs[4 * m + 4:]
        xi, yi, ci = _my_place()
        me = 4 * xi + 2 * yi + ci
        load = [pltpu.make_async_copy(x_refs[p].at[me] if scatter else x_refs[p], stage[p], local_sems.at[2 * p])
                for p in range(m)]
        store = [pltpu.make_async_copy(stage[p], land_refs[p].at[me], local_sems.at[2 * p + 1]) for p in range(m)]
        for cp in load:
            cp.start()
        for p in range(m):
            load[p].wait()
            store[p].start()
        for cp in _split_copies(x_refs, land_refs, refs[2 * m], refs[2 * m + 1], scatter):
            cp.wait_send()
            cp.wait_recv()
        for cp in store:
            cp.wait()

    res = pl.pallas_call(
        body, name=name,
        out_shape=[pltpu.HBM(a.shape, a.dtype) for a in x_thru + land_thru],
        in_specs=[HBM] * (2 * m) + [SEM, SEM, ANY], out_specs=[HBM] * (2 * m),
        input_output_aliases={i: i for i in range(2 * m)},
        scratch_shapes=[pltpu.SemaphoreType.DMA((2 * m,))] + [pltpu.VMEM(b, a.dtype) for b, a in zip(blocks, x_thru)],
        compiler_params=pltpu.CompilerParams(has_side_effects=EFFECT),
    )(*x_thru, *land_thru, send_sem, recv_sem, after)
    return list(res[m:])


def _to_rows(vec):
    n = vec.shape[0]
    unit = 8 * LANES
    pad = (-n) % unit
    if pad:
        vec = jnp.concatenate([vec, jnp.zeros((pad,), vec.dtype)])
    return vec.reshape(-1, LANES)


def _mm(a, b, name, *, af="mk", bf="kn", of="mn", out_dtype=F32, tm=None, tn=None, tk=None,
        gate=None, resid=None, pre_dtype=None):
    if af == "mk":
        m, kk = a.shape
    elif af == "km":
        kk, m = a.shape
    elif af == "qmk":
        qa, m, tk = a.shape
        kk = qa * tk
    else:
        qa, kk, tm = a.shape
        m = qa * tm
    if bf == "kn":
        k2, n = b.shape
    elif bf == "nk":
        n, k2 = b.shape
    elif bf == "qkn":
        qb, k2, tn = b.shape
        n = qb * tn
    else:
        qb, n, tkb = b.shape
        k2 = qb * tkb
        assert af != "qmk" or tkb == tk
        tk = tkb
    assert kk == k2, (name, a.shape, b.shape, af, bf)
    tm = _pick(m, MM_TILE) if tm is None else tm
    tn = _pick(n, MM_TILE) if tn is None else tn
    tk = _pick(kk, MM_TILE) if tk is None else tk
    assert m % tm == 0 and n % tn == 0 and kk % tk == 0, (name, m, n, kk, tm, tn, tk)
    nk = kk // tk
    a_spec = {"mk": pl.BlockSpec((tm, tk), lambda i, j, k: (i, k)),
              "km": pl.BlockSpec((tk, tm), lambda i, j, k: (k, i)),
              "qmk": pl.BlockSpec((None, tm, tk), lambda i, j, k: (k, i, 0)),
              "qkm": pl.BlockSpec((None, tk, tm), lambda i, j, k: (i, k, 0))}[af]
    b_spec = {"kn": pl.BlockSpec((tk, tn), lambda i, j, k: (k, j)),
              "nk": pl.BlockSpec((tn, tk), lambda i, j, k: (j, k)),
              "qkn": pl.BlockSpec((None, tk, tn), lambda i, j, k: (j, k, 0)),
              "qnk": pl.BlockSpec((None, tn, tk), lambda i, j, k: (k, j, 0))}[bf]
    dims = (((1 if af in ("mk", "qmk") else 0,), (0 if bf in ("kn", "qkn") else 1,)), ((), ()))
    in_specs, args = [a_spec, b_spec], [a, b]
    if gate is not None:
        assert of == "mn"
        in_specs.append(pl.BlockSpec((1, tn), lambda i, j, k: (0, j)))
        args.append(gate)
    if resid is not None:
        assert of == "mn"
        in_specs.append(pl.BlockSpec((tm, tn), lambda i, j, k: (i, j)))
        args.append(resid)
    if of == "mn":
        o_spec, o_shape = pl.BlockSpec((tm, tn), lambda i, j, k: (i, j)), (m, n)
    else:
        o_spec, o_shape = pl.BlockSpec((None, tm, tn), lambda i, j, k: (j, i, 0)), (n // tn, m, tn)
    out_shape, out_specs = [jax.ShapeDtypeStruct(o_shape, out_dtype)], [o_spec]
    if pre_dtype is not None:
        out_shape.insert(0, jax.ShapeDtypeStruct(o_shape, pre_dtype))
        out_specs.insert(0, o_spec)
    n_in = len(args)
    n_out = len(out_shape)

    def body(*refs):
        a_ref, b_ref = refs[0], refs[1]
        extra = list(refs[2:n_in])
        outs = refs[n_in:n_in + n_out]
        gate_ref = extra.pop(0) if gate is not None else None
        resid_ref = extra.pop(0) if resid is not None else None

        def product():
            return lax.dot_general(a_ref[...].astype(BF), b_ref[...].astype(BF), dims, preferred_element_type=F32)

        def finish(r):
            if pre_dtype is not None:
                outs[0][...] = r.astype(pre_dtype)
            if gate_ref is not None:
                r = r * gate_ref[...]
            if resid_ref is not None:
                r = resid_ref[...] + r
            outs[-1][...] = r.astype(out_dtype)

        if nk == 1:
            finish(product())
        else:
            acc = refs[n_in + n_out]
            k = pl.program_id(2)

            @pl.when(k == 0)
            def _():
                acc[...] = product()

            @pl.when(k > 0)
            def _():
                acc[...] += product()

            @pl.when(k == nk - 1)
            def _():
                finish(acc[...])

    res = pl.pallas_call(
        body, name=name, grid=(m // tm, n // tn, nk),
        in_specs=in_specs, out_specs=out_specs, out_shape=out_shape,
        scratch_shapes=[pltpu.VMEM((tm, tn), F32)] if nk > 1 else [],
        compiler_params=_params(("parallel", "parallel", "arbitrary")),
    )(*args)
    return res if pre_dtype is not None else res[0]


def _ada_fwd(cond16, w, name):
    nl, d, n = w.shape

    def body(c_ref, w_ref, o_ref):
        o_ref[...] = jnp.dot(c_ref[...].astype(BF), w_ref[...].astype(BF), preferred_element_type=F32)

    return pl.pallas_call(
        body, name=name, grid=(nl,),
        in_specs=[pl.BlockSpec((16, d), lambda l: (0, 0)), pl.BlockSpec((None, d, n), lambda l: (l, 0, 0))],
        out_specs=pl.BlockSpec((None, 16, n), lambda l: (l, 0, 0)),
        out_shape=jax.ShapeDtypeStruct((nl, 16, n), F32),
        compiler_params=_params(("parallel",)),
    )(cond16, w)


def _ada_bwd(cond16, dmod, name):
    nl, _, n = dmod.shape
    d = cond16.shape[1]

    def body(c_ref, g_ref, o_ref):
        o_ref[...] = lax.dot_general(c_ref[...].astype(BF), g_ref[...].astype(BF), (((0,), (0,)), ((), ())),
                                     preferred_element_type=F32)

    return pl.pallas_call(
        body, name=name, grid=(nl,),
        in_specs=[pl.BlockSpec((16, d), lambda l: (0, 0)), pl.BlockSpec((None, 16, n), lambda l: (l, 0, 0))],
        out_specs=pl.BlockSpec((None, d, n), lambda l: (l, 0, 0)),
        out_shape=jax.ShapeDtypeStruct((nl, d, n), F32),
        compiler_params=_params(("parallel",)),
    )(cond16, dmod)


def _row_spec(tm, d):
    return pl.BlockSpec((tm, d), lambda i: (i, 0))


def _vec_spec(d):
    return pl.BlockSpec((1, d), lambda i: (0, 0))


def _norm_mod(x, g, sc, sh, name):
    s, d = x.shape
    tm = _pick(s, 512, 8)

    def body(x_ref, g_ref, sc_ref, sh_ref, o_ref):
        xv = x_ref[...]
        r = lax.rsqrt(jnp.mean(xv * xv, axis=-1, keepdims=True) + EPS)
        y = (xv * r) * g_ref[...]
        o_ref[...] = (y * (1.0 + sc_ref[...]) + sh_ref[...]).astype(BF)

    return pl.pallas_call(
        body, name=name, grid=(s // tm,),
        in_specs=[_row_spec(tm, d), _vec_spec(d), _vec_spec(d), _vec_spec(d)],
        out_specs=_row_spec(tm, d), out_shape=jax.ShapeDtypeStruct((s, d), BF),
        compiler_params=_params(("parallel",)),
    )(x, g, sc, sh)


def _norm_mod_bwd(x, dh, g, sc, dx_in, name):
    s, d = x.shape
    tm = _pick(s, 512, 8)

    def body(x_ref, dh_ref, g_ref, sc_ref, dxin_ref, dx_ref, dsh_ref, dw_ref):
        i = pl.program_id(0)
        xv = x_ref[...]
        dhv = dh_ref[...].astype(F32)
        r = lax.rsqrt(jnp.mean(xv * xv, axis=-1, keepdims=True) + EPS)
        xn = xv * r
        dxn = dhv * (g_ref[...] * (1.0 + sc_ref[...]))
        dx_ref[...] = dxin_ref[...] + r * (dxn - xn * jnp.mean(dxn * xn, axis=-1, keepdims=True))

        @pl.when(i == 0)
        def _():
            dsh_ref[...] = jnp.zeros_like(dsh_ref)
            dw_ref[...] = jnp.zeros_like(dw_ref)

        dsh_ref[...] += jnp.sum(dhv, axis=0, keepdims=True)
        dw_ref[...] += jnp.sum(dhv * xn, axis=0, keepdims=True)

    return pl.pallas_call(
        body, name=name, grid=(s // tm,),
        in_specs=[_row_spec(tm, d), _row_spec(tm, d), _vec_spec(d), _vec_spec(d), _row_spec(tm, d)],
        out_specs=[_row_spec(tm, d), _vec_spec(d), _vec_spec(d)],
        out_shape=[jax.ShapeDtypeStruct((s, d), F32), jax.ShapeDtypeStruct((1, d), F32),
                   jax.ShapeDtypeStruct((1, d), F32)],
        compiler_params=_params(("arbitrary",)),
    )(x, dh, g, sc, dx_in)


def _gate_bwd(dx, y, gate, name):
    s, d = dx.shape
    tm = _pick(s, 512, 8)

    def body(dx_ref, y_ref, g_ref, dy_ref, dg_ref):
        i = pl.program_id(0)
        dxv = dx_ref[...]
        dy_ref[...] = (dxv * g_ref[...]).astype(BF)

        @pl.when(i == 0)
        def _():
            dg_ref[...] = jnp.zeros_like(dg_ref)

        dg_ref[...] += jnp.sum(dxv * y_ref[...].astype(F32), axis=0, keepdims=True)

    return pl.pallas_call(
        body, name=name, grid=(s // tm,),
        in_specs=[_row_spec(tm, d), _row_spec(tm, d), _vec_spec(d)],
        out_specs=[_row_spec(tm, d), _vec_spec(d)],
        out_shape=[jax.ShapeDtypeStruct((s, d), BF), jax.ShapeDtypeStruct((1, d), F32)],
        compiler_params=_params(("arbitrary",)),
    )(dx, y, gate)


def _final_loss(x, tgt, g, name):
    s, d = x.shape
    tm = _pick(s, 512, 8)

    def body(x_ref, t_ref, g_ref, dx_ref, loss_ref, dg_ref):
        i = pl.program_id(0)
        xv = x_ref[...]
        gv = g_ref[...]
        r = lax.rsqrt(jnp.mean(xv * xv, axis=-1, keepdims=True) + EPS)
        xn = xv * r
        err = xn * gv - t_ref[...]
        dy = err * (1.0 / d)
        dxn = dy * gv
        dx_ref[...] = r * (dxn - xn * jnp.mean(dxn * xn, axis=-1, keepdims=True))

        @pl.when(i == 0)
        def _():
            loss_ref[...] = jnp.zeros_like(loss_ref)
            dg_ref[...] = jnp.zeros_like(dg_ref)

        part = 0.5 * jnp.sum(jnp.sum(err * err, axis=-1, keepdims=True) * (1.0 / d), axis=0, keepdims=True)
        loss_ref[...] += jnp.broadcast_to(part, loss_ref.shape)
        dg_ref[...] += jnp.sum(dy * xn, axis=0, keepdims=True)

    return pl.pallas_call(
        body, name=name, grid=(s // tm,),
        in_specs=[_row_spec(tm, d), _row_spec(tm, d), _vec_spec(d)],
        out_specs=[_row_spec(tm, d), pl.BlockSpec((8, LANES), lambda i: (0, 0)), _vec_spec(d)],
        out_shape=[jax.ShapeDtypeStruct((s, d), F32), jax.ShapeDtypeStruct((8, LANES), F32),
                   jax.ShapeDtypeStruct((1, d), F32)],
        compiler_params=_params(("arbitrary",)),
    )(x, tgt, g)


def _pool_counts(tm, gd, row0, w):
    t = lax.broadcasted_iota(jnp.int32, (tm, gd), 0) + row0
    return jnp.minimum(t + 1, w).astype(F32)


def _pool_fwd(u, name):
    s, d = u.shape
    tm = _pick(s, 256, POOL_HALO)
    gd = d // len(POOL_WINDOWS)
    per = tm // POOL_HALO

    def body(prev_ref, cur_ref, o_ref, ext):
        i = pl.program_id(0)
        ext[0:POOL_HALO, :] = jnp.where(i > 0, prev_ref[...], 0.0)
        ext[POOL_HALO:, :] = cur_ref[...]
        for g, w in enumerate(POOL_WINDOWS):
            cols = slice(g * gd, (g + 1) * gd)
            acc = ext[POOL_HALO:POOL_HALO + tm, cols]
            own = acc
            for k in range(1, w):
                acc = acc + ext[POOL_HALO - k:POOL_HALO - k + tm, cols]
            o_ref[:, cols] = (acc / _pool_counts(tm, gd, i * tm, w) - own).astype(BF)

    return pl.pallas_call(
        body, name=name, grid=(s // tm,),
        in_specs=[pl.BlockSpec((POOL_HALO, d), lambda i: (jnp.maximum(i * per - 1, 0), 0)), _row_spec(tm, d)],
        out_specs=_row_spec(tm, d), out_shape=jax.ShapeDtypeStruct((s, d), BF),
        scratch_shapes=[pltpu.VMEM((tm + POOL_HALO, d), F32)],
        compiler_params=_params(("parallel",)),
    )(u, u)


def _pool_bwd(dp, name):
    s, d = dp.shape
    tm = _pick(s, 256, POOL_HALO)
    gd = d // len(POOL_WINDOWS)
    per = tm // POOL_HALO
    nt = s // tm
    last_halo = s // POOL_HALO - 1

    def body(cur_ref, nxt_ref, o_ref, ext):
        i = pl.program_id(0)
        for g, w in enumerate(POOL_WINDOWS):
            cols = slice(g * gd, (g + 1) * gd)
            ext[0:tm, cols] = cur_ref[:, cols].astype(F32) / _pool_counts(tm, gd, i * tm, w)
            nxt = nxt_ref[:, cols].astype(F32) / _pool_counts(POOL_HALO, gd, (i + 1) * tm, w)
            ext[tm:, cols] = jnp.where(i < nt - 1, nxt, 0.0)
        for g, w in enumerate(POOL_WINDOWS):
            cols = slice(g * gd, (g + 1) * gd)
            acc = ext[0:tm, cols]
            for k in range(1, w):
                acc = acc + ext[k:k + tm, cols]
            o_ref[:, cols] = (acc - cur_ref[:, cols].astype(F32)).astype(BF)

    return pl.pallas_call(
        body, name=name, grid=(nt,),
        in_specs=[_row_spec(tm, d), pl.BlockSpec((POOL_HALO, d), lambda i: (jnp.minimum((i + 1) * per, last_halo), 0))],
        out_specs=_row_spec(tm, d), out_shape=jax.ShapeDtypeStruct((s, d), BF),
        scratch_shapes=[pltpu.VMEM((tm + POOL_HALO, d), F32)],
        compiler_params=_params(("parallel",)),
    )(dp, dp)


def _grp_fwd(p, w, scale, name):
    s, d = p.shape
    ng, gd, _ = w.shape
    tm = _pick(s, 1024, 8)

    def body(p_ref, w_ref, s_ref, z_ref, y_ref):
        z = jnp.dot(p_ref[...], w_ref[...].astype(BF), preferred_element_type=F32)
        z_ref[...] = z.astype(BF)
        y_ref[...] = (z * s_ref[...]).astype(BF)

    blk = pl.BlockSpec((tm, gd), lambda i, g: (i, g))
    return pl.pallas_call(
        body, name=name, grid=(s // tm, ng),
        in_specs=[blk, pl.BlockSpec((None, gd, gd), lambda i, g: (g, 0, 0)), pl.BlockSpec((1, gd), lambda i, g: (0, g))],
        out_specs=[blk, blk],
        out_shape=[jax.ShapeDtypeStruct((s, d), BF), jax.ShapeDtypeStruct((s, d), BF)],
        compiler_params=_params(("parallel", "parallel")),
    )(p, w, scale)


def _grp_bwd(dy, z, w, scale, name):
    s, d = dy.shape
    ng, gd, _ = w.shape
    tm = _pick(s, 1024, 8)

    def body(dy_ref, z_ref, w_ref, s_ref, dz_ref, dp_ref, ds_ref):
        i = pl.program_id(1)
        dyv = dy_ref[...].astype(F32)
        dz = (dyv * s_ref[...]).astype(BF)
        dz_ref[...] = dz
        dp_ref[...] = lax.dot_general(dz, w_ref[...].astype(BF), (((1,), (1,)), ((), ())),
                                      preferred_element_type=F32).astype(BF)

        @pl.when(i == 0)
        def _():
            ds_ref[...] = jnp.zeros_like(ds_ref)

        ds_ref[...] += jnp.sum(dyv * z_ref[...].astype(F32), axis=0, keepdims=True)

    blk = pl.BlockSpec((tm, gd), lambda g, i: (i, g))
    vec = pl.BlockSpec((1, gd), lambda g, i: (0, g))
    return pl.pallas_call(
        body, name=name, grid=(ng, s // tm),
        in_specs=[blk, blk, pl.BlockSpec((None, gd, gd), lambda g, i: (g, 0, 0)), vec],
        out_specs=[blk, blk, vec],
        out_shape=[jax.ShapeDtypeStruct((s, d), BF), jax.ShapeDtypeStruct((s, d), BF),
                   jax.ShapeDtypeStruct((1, d), F32)],
        compiler_params=_params(("parallel", "arbitrary")),
    )(dy, z, w, scale)


def _grp_dw(p, dz, ng, name):
    s, d = p.shape
    gd = d // ng
    tk = _pick(s, 1024, 8)

    def body(p_ref, dz_ref, o_ref):
        k = pl.program_id(1)

        @pl.when(k == 0)
        def _():
            o_ref[...] = jnp.zeros_like(o_ref)

        o_ref[...] += lax.dot_general(p_ref[...], dz_ref[...], (((0,), (0,)), ((), ())), preferred_element_type=F32)

    blk = pl.BlockSpec((tk, gd), lambda g, k: (k, g))
    return pl.pallas_call(
        body, name=name, grid=(ng, s // tk),
        in_specs=[blk, blk], out_specs=pl.BlockSpec((None, gd, gd), lambda g, k: (g, 0, 0)),
        out_shape=jax.ShapeDtypeStruct((ng, gd, gd), F32),
        compiler_params=_params(("parallel", "arbitrary")),
    )(p, dz)


def _sigmoid(a):
    return 0.5 * jnp.tanh(0.5 * a) + 0.5


def _ffn_act_down(up, cw, cb, w_down, gate, resid, name):
    _, nq, s, fq = up.shape
    d = w_down.shape[1]
    tm = _pick(s, 512, CONV_HALO)
    per = tm // CONV_HALO
    h = CONV_HALO
    rows = _pick(tm, FFN_ROWS, h)
    group = _pick(tm, FFN_GROUP, rows)

    def body(prev_ref, a_ref, v_ref, w_ref, b_ref, wd_ref, g_ref, r_ref, act_ref, ffo_ref, x_ref, ext, acc):
        i, q = pl.program_id(0), pl.program_id(1)
        ext[0:h, :] = jnp.where(i > 0, prev_ref[...].astype(F32), 0.0)
        ext[h:, :] = a_ref[...].astype(F32)

        @pl.when(q == 0)
        def _():
            acc[...] = jnp.zeros_like(acc)

        for g0 in range(0, tm, group):
            for r0 in range(g0, g0 + group, rows):
                e = ext[r0:r0 + rows + h, :]
                a2 = (b_ref[...] + e[h - 2:h - 2 + rows] * w_ref[0:1, :] + e[h - 1:h - 1 + rows] * w_ref[1:2, :]
                      + e[h:h + rows] * w_ref[2:3, :])
                vv = v_ref[r0:r0 + rows, :].astype(F32)
                act_ref[r0:r0 + rows, :] = (a2 * _sigmoid(a2) * vv).astype(BF)
            acc[g0:g0 + group, :] += jnp.dot(act_ref[g0:g0 + group, :], wd_ref[...], preferred_element_type=F32)

        @pl.when(q == nq - 1)
        def _():
            r = acc[...]
            ffo_ref[...] = r.astype(BF)
            x_ref[...] = r_ref[...] + g_ref[...] * r

    row = pl.BlockSpec((tm, d), lambda i, q: (i, 0))
    return pl.pallas_call(
        body, name=name, grid=(s // tm, nq),
        in_specs=[pl.BlockSpec((None, None, h, fq), lambda i, q: (0, q, jnp.maximum(i * per - 1, 0), 0)),
                  pl.BlockSpec((None, None, tm, fq), lambda i, q: (0, q, i, 0)),
                  pl.BlockSpec((None, None, tm, fq), lambda i, q: (1, q, i, 0)),
                  pl.BlockSpec((None, CONV_WIDTH, fq), lambda i, q: (q, 0, 0)),
                  pl.BlockSpec((None, 1, fq), lambda i, q: (q, 0, 0)),
                  pl.BlockSpec((fq, d), lambda i, q: (q, 0)),
                  pl.BlockSpec((1, d), lambda i, q: (0, 0)), row],
        out_specs=[pl.BlockSpec((None, tm, fq), lambda i, q: (q, i, 0)), row, row],
        out_shape=[jax.ShapeDtypeStruct((nq, s, fq), BF), jax.ShapeDtypeStruct((s, d), BF),
                   jax.ShapeDtypeStruct((s, d), F32)],
        scratch_shapes=[pltpu.VMEM((tm + h, fq), F32), pltpu.VMEM((tm, d), F32)],
        compiler_params=_params(("parallel", "arbitrary")),
    )(up, up, up, cw, cb, w_down, gate, resid)


def _ffn_act_bwd_up(up, dact, cw, cb, w_up, name):
    _, nq, s, fq = up.shape
    d = w_up.shape[1]
    tm = _pick(s, 512, CONV_HALO)
    per = tm // CONV_HALO
    nt = s // tm
    last_halo = s // CONV_HALO - 1
    h = CONV_HALO
    te = tm + h
    rows = _pick(tm, FFN_ROWS, h)
    group = _pick(tm, FFN_GROUP, rows)
    lanes = (((1,), (1,)), ((), ()))

    def body(ap_ref, a_ref, an_ref, v_ref, vn_ref, d_ref, dn_ref, w_ref, b_ref, wa_ref, wv_ref,
             dup_ref, dc_ref, dx_ref, ext_a, dap, sums, acc):
        i, q = pl.program_id(0), pl.program_id(1)
        ext_a[0:h, :] = jnp.where(i > 0, ap_ref[...].astype(F32), 0.0)
        ext_a[h:h + tm, :] = a_ref[...].astype(F32)
        ext_a[h + tm:, :] = an_ref[...].astype(F32)

        def pre_act(e, n):
            return (b_ref[...] + e[h - 2:h - 2 + n] * w_ref[0:1, :] + e[h - 1:h - 1 + n] * w_ref[1:2, :]
                    + e[h:h + n] * w_ref[2:3, :])

        def through_gate(a2, dd, vv):
            sig = _sigmoid(a2)
            return dd * vv * (sig * (1.0 + a2 * (1.0 - sig))), dd * (a2 * sig)

        def phase1(g0):
            for r0 in range(g0, g0 + group, rows):
                a2 = pre_act(ext_a[r0:r0 + rows + h, :], rows)
                g, dgate = through_gate(a2, d_ref[r0:r0 + rows, :].astype(F32), v_ref[r0:r0 + rows, :].astype(F32))
                dap[r0:r0 + rows, :] = g
                dup_ref[1, r0:r0 + rows, :] = dgate.astype(BF)

        def fold(t):
            part = t[0:8]
            for k in range(8, rows, 8):
                part = part + t[k:k + 8]
            return part

        def phase2(g0):
            for r0 in range(g0, g0 + group, rows):
                gch = dap[r0:r0 + rows + h, :]
                g0_ = gch[0:rows]
                dup_ref[0, r0:r0 + rows, :] = (gch[2:2 + rows] * w_ref[0:1, :] + gch[1:1 + rows] * w_ref[1:2, :]
                                               + g0_ * w_ref[2:3, :]).astype(BF)
                e = ext_a[r0:r0 + rows + h, :]
                sums[0] += fold(g0_ * e[h - 2:h - 2 + rows])
                sums[1] += fold(g0_ * e[h - 1:h - 1 + rows])
                sums[2] += fold(g0_ * e[h:h + rows])
                sums[3] += fold(g0_)

        @pl.when(q == 0)
        def _():
            acc[...] = jnp.zeros_like(acc)

        sums[...] = jnp.zeros_like(sums)
        phase1(0)
        for g0 in range(0, tm, group):
            if g0 + group < tm:
                phase1(g0 + group)
            else:
                d_nxt = jnp.where(i < nt - 1, dn_ref[...].astype(F32), 0.0)
                g, _ = through_gate(pre_act(ext_a[tm:tm + 2 * h, :], h), d_nxt, vn_ref[...].astype(F32))
                dap[tm:, :] = g
            phase2(g0)
            acc[g0:g0 + group, :] += (
                lax.dot_general(dup_ref[0, g0:g0 + group, :], wa_ref[...], lanes, preferred_element_type=F32)
                + lax.dot_general(dup_ref[1, g0:g0 + group, :], wv_ref[...], lanes, preferred_element_type=F32))

        @pl.when(i == 0)
        def _():
            dc_ref[q] = jnp.zeros((8, fq), F32)

        for k in range(4):
            dc_ref[q, k:k + 1, :] += jnp.sum(sums[k], axis=0, keepdims=True)

        @pl.when(q == nq - 1)
        def _():
            dx_ref[...] = acc[...]

    def cur(half):
        return pl.BlockSpec((None, None, tm, fq), lambda i, q: (half, q, i, 0))

    def nxt(half):
        return pl.BlockSpec((None, None, h, fq), lambda i, q: (half, q, jnp.minimum((i + 1) * per, last_halo), 0))

    return pl.pallas_call(
        body, name=name, grid=(nt, nq),
        in_specs=[pl.BlockSpec((None, None, h, fq), lambda i, q: (0, q, jnp.maximum(i * per - 1, 0), 0)),
                  cur(0), nxt(0), cur(1), nxt(1),
                  pl.BlockSpec((None, tm, fq), lambda i, q: (q, i, 0)),
                  pl.BlockSpec((None, h, fq), lambda i, q: (q, jnp.minimum((i + 1) * per, last_halo), 0)),
                  pl.BlockSpec((None, CONV_WIDTH, fq), lambda i, q: (q, 0, 0)),
                  pl.BlockSpec((None, 1, fq), lambda i, q: (q, 0, 0)),
                  pl.BlockSpec((None, d, fq), lambda i, q: (q, 0, 0)),
                  pl.BlockSpec((None, d, fq), lambda i, q: (q + nq, 0, 0))],
        out_specs=[pl.BlockSpec((2, None, tm, fq), lambda i, q: (0, q, i, 0)),
                   pl.BlockSpec((nq, 8, fq), lambda i, q: (0, 0, 0)),
                   pl.BlockSpec((tm, d), lambda i, q: (i, 0))],
        out_shape=[jax.ShapeDtypeStruct((2, nq, s, fq), BF), jax.ShapeDtypeStruct((nq, 8, fq), F32),
                   jax.ShapeDtypeStruct((s, d), F32)],
        scratch_shapes=[pltpu.VMEM((tm + 2 * h, fq), F32), pltpu.VMEM((te, fq), F32), pltpu.VMEM((4, 8, fq), F32),
                        pltpu.VMEM((tm, d), F32)],
        compiler_params=_params(("arbitrary", "arbitrary")),
    )(up, up, up, up, up, dact, dact, cw, cb, w_up, w_up)


def _band(blk, n_steps, dil, first):
    qi = lax.broadcasted_iota(jnp.int32, (blk, 2 * blk), 0) + blk
    ki = lax.broadcasted_iota(jnp.int32, (blk, 2 * blk), 1)
    delta = qi - ki
    valid = (delta >= 0) & (delta <= n_steps) & ((ki >= blk) | jnp.logical_not(first))
    return valid, (delta * dil).astype(F32)


def _branch_views(q_all, kv, g, d):
    _, dil = BRANCHES[g]
    nbr = len(BRANCHES)
    sub = q_all.shape[0] // dil
    if dil == 1:
        return (q_all, kv, kv), ((nbr, g), (2 * nbr, g), (2 * nbr, nbr + g))
    cols = [q_all[:, g * d:(g + 1) * d], kv[:, g * d:(g + 1) * d], kv[:, (nbr + g) * d:(nbr + g + 1) * d]]
    return tuple(a.reshape(sub, dil * d) for a in cols), ((1, 0), (1, 0), (1, 0))


def _attn_fwd(q_all, kv, g, slopes, d, name):
    window, dil = BRANCHES[g]
    n_steps = window // dil
    blk = max(ATTN_BLOCK, n_steps)
    s = q_all.shape[0]
    sub = s // dil
    nb = sub // blk
    assert nb * blk == sub
    nh = d // HEAD_DIM
    (qv, kview, vview), (qcol, kcol, vcol) = _branch_views(q_all, kv, g, d)
    scale = HEAD_DIM ** -0.5

    def body(q_ref, kp_ref, kc_ref, vp_ref, vc_ref, o_ref, l_ref):
        j = pl.program_id(1)
        valid, dist = _band(blk, n_steps, dil, j == 0)
        qb = q_ref[...]
        kb = jnp.concatenate([kp_ref[...], kc_ref[...]], axis=0)
        vb = jnp.concatenate([vp_ref[...], vc_ref[...]], axis=0)
        first = lax.broadcasted_iota(jnp.int32, (1, 2 * HEAD_DIM), 1) < HEAD_DIM
        head_of_lane = lax.broadcasted_iota(jnp.int32, (1, LANES), 1) // (LANES // nh)
        lse_all = jnp.zeros((blk, LANES), F32)
        for hp in range(nh // 2):
            sl = slice(2 * hp * HEAD_DIM, 2 * (hp + 1) * HEAD_DIM)
            qp, kp, vp = qb[:, sl], kb[:, sl], vb[:, sl]
            out = None
            for half in range(2):
                sel = first if half == 0 else jnp.logical_not(first)
                sc = lax.dot_general(jnp.where(sel, qp, 0), kp, (((1,), (1,)), ((), ())),
                                     preferred_element_type=F32) * scale
                sc = jnp.where(valid, sc - float(slopes[2 * hp + half]) * dist, NEG)
                m = jnp.max(sc, axis=-1, keepdims=True)
                p = jnp.exp(sc - m)
                den = jnp.sum(p, axis=-1, keepdims=True)
                o = jnp.dot(p.astype(BF), jnp.where(sel, vp, 0), preferred_element_type=F32) / den
                out = o if half == 0 else out + o
                lse_all = jnp.where(head_of_lane == 2 * hp + half, m + jnp.log(den), lse_all)
            o_ref[:, sl] = out.astype(BF)
        l_ref[...] = lse_all

    def spec(col, prev):
        if prev:
            return pl.BlockSpec((blk, d), lambda r, j: (jnp.maximum(j - 1, 0), r * col[0] + col[1]))
        return pl.BlockSpec((blk, d), lambda r, j: (j, r * col[0] + col[1]))

    ospec = pl.BlockSpec((blk, d), lambda r, j: (j, r))
    o, lse = pl.pallas_call(
        body, name=name, grid=(dil, nb),
        in_specs=[spec(qcol, False), spec(kcol, True), spec(kcol, False), spec(vcol, True), spec(vcol, False)],
        out_specs=[ospec, pl.BlockSpec((blk, LANES), lambda r, j: (j, r))],
        out_shape=[jax.ShapeDtypeStruct((sub, dil * d), BF), jax.ShapeDtypeStruct((sub, dil * LANES), F32)],
        compiler_params=_params(("parallel", "parallel")),
    )(qv, kview, kview, vview, vview)
    return o.reshape(s, d), lse.reshape(s, LANES)


def _attn_combine(os, lses, name):
    s, d = os[0].shape
    tm = _pick(s, 512, 8)
    nbr = len(os)
    nh = d // HEAD_DIM
    per_head = LANES // nh

    def body(*refs):
        o_refs, l_refs = refs[:nbr], refs[nbr:2 * nbr]
        o_ref, lt_ref = refs[2 * nbr], refs[2 * nbr + 1]
        ls = [r[...] for r in l_refs]
        m = ls[0]
        for v in ls[1:]:
            m = jnp.maximum(m, v)
        tot = jnp.exp(ls[0] - m)
        for v in ls[1:]:
            tot = tot + jnp.exp(v - m)
        lt = m + jnp.log(tot)
        lt_ref[...] = lt
        ws = [jnp.exp(v - lt) for v in ls]
        first = lax.broadcasted_iota(jnp.int32, (1, 2 * HEAD_DIM), 1) < HEAD_DIM
        for hp in range(nh // 2):
            sl = slice(2 * hp * HEAD_DIM, 2 * (hp + 1) * HEAD_DIM)
            la, lb = 2 * hp * per_head, (2 * hp + 1) * per_head
            acc = None
            for w, r in zip(ws, o_refs):
                term = jnp.where(first, w[:, la:la + 1], w[:, lb:lb + 1]) * r[:, sl].astype(F32)
                acc = term if acc is None else acc + term
            o_ref[:, sl] = acc.astype(BF)

    lspec = _row_spec(tm, LANES)
    return pl.pallas_call(
        body, name=name, grid=(s // tm,),
        in_specs=[_row_spec(tm, d)] * nbr + [lspec] * nbr, out_specs=[_row_spec(tm, d), lspec],
        out_shape=[jax.ShapeDtypeStruct((s, d), BF), jax.ShapeDtypeStruct((s, LANES), F32)],
        compiler_params=_params(("parallel",)),
    )(*os, *lses)


def _attn_bwd(q_all, kv, do, o, lt, g, slopes, d, name, dk_in=None, dv_in=None):
    window, dil = BRANCHES[g]
    n_steps = window // dil
    blk = max(ATTN_BLOCK, n_steps)
    s = q_all.shape[0]
    sub = s // dil
    nb = sub // blk
    nh = d // HEAD_DIM
    (qv, kview, vview), (qcol, kcol, vcol) = _branch_views(q_all, kv, g, d)
    scale = HEAD_DIM ** -0.5
    acc_in = dk_in is not None

    def body(*refs):
        q_ref, do_ref, o_ref, lt_ref, kp_ref, kc_ref, vp_ref, vc_ref = refs[:8]
        n_in = 10 if acc_in else 8
        dkin_ref, dvin_ref = (refs[8], refs[9]) if acc_in else (None, None)
        dq_ref, dk_ref, dv_ref, keep_k, keep_v, part_k, part_v = refs[n_in:n_in + 7]
        t = pl.program_id(1)

        def emit(prev_k, prev_v):
            if acc_in:
                prev_k = prev_k + dkin_ref[...].astype(F32)
                prev_v = prev_v + dvin_ref[...].astype(F32)
            dk_ref[...] = prev_k.astype(BF)
            dv_ref[...] = prev_v.astype(BF)

        @pl.when(t < nb)
        def _():
            valid, dist = _band(blk, n_steps, dil, t == 0)
            qb = q_ref[...]
            dob = do_ref[...]
            ltb = lt_ref[...]
            kb = jnp.concatenate([kp_ref[...], kc_ref[...]], axis=0)
            vb = jnp.concatenate([vp_ref[...], vc_ref[...]], axis=0)
            first = lax.broadcasted_iota(jnp.int32, (1, 2 * HEAD_DIM), 1) < HEAD_DIM
            for hp in range(nh // 2):
                sl = slice(2 * hp * HEAD_DIM, 2 * (hp + 1) * HEAD_DIM)
                qp, kp, vp, dop = qb[:, sl], kb[:, sl], vb[:, sl], dob[:, sl]
                op = o_ref[:, sl].astype(F32)
                dq, dk, dv = None, None, None
                for half in range(2):
                    sel = first if half == 0 else jnp.logical_not(first)
                    qh, doh = jnp.where(sel, qp, 0), jnp.where(sel, dop, 0)
                    sc = lax.dot_general(qh, kp, (((1,), (1,)), ((), ())), preferred_element_type=F32) * scale
                    sc = sc - float(slopes[2 * hp + half]) * dist
                    lane = (2 * hp + half) * (LANES // nh)
                    lt_h = ltb[:, lane:lane + 1]
                    p = jnp.where(valid, jnp.exp(jnp.minimum(sc - lt_h, 30.0)), 0.0)
                    dlt = jnp.sum(doh.astype(F32) * op, axis=-1, keepdims=True)
                    dp = lax.dot_general(doh, vp, (((1,), (1,)), ((), ())), preferred_element_type=F32)
                    ds = (p * (dp - dlt)).astype(BF)
                    dq_h = jnp.dot(ds, jnp.where(sel, kp, 0), preferred_element_type=F32)
                    dk_h = lax.dot_general(ds, qh, (((0,), (0,)), ((), ())), preferred_element_type=F32)
                    dv_h = lax.dot_general(p.astype(BF), doh, (((0,), (0,)), ((), ())), preferred_element_type=F32)
                    dq = dq_h if half == 0 else dq + dq_h
                    dk = dk_h if half == 0 else dk + dk_h
                    dv = dv_h if half == 0 else dv + dv_h
                dq_ref[:, sl] = (dq * scale).astype(BF)
                part_k[:, sl] = dk * scale
                part_v[:, sl] = dv

            @pl.when(t > 0)
            def _():
                emit(keep_k[...] + part_k[0:blk, :], keep_v[...] + part_v[0:blk, :])

            keep_k[...] = part_k[blk:, :]
            keep_v[...] = part_v[blk:, :]

        @pl.when(t == nb)
        def _():
            emit(keep_k[...], keep_v[...])

    def qspec(col):
        return pl.BlockSpec((blk, d), lambda r, t: (jnp.minimum(t, nb - 1), r * col[0] + col[1]))

    def kspec(col, prev):
        if prev:
            return pl.BlockSpec((blk, d), lambda r, t: (jnp.maximum(jnp.minimum(t, nb - 1) - 1, 0), r * col[0] + col[1]))
        return qspec(col)

    kout = pl.BlockSpec((blk, d), lambda r, t: (jnp.maximum(t - 1, 0), r))
    one = (1, 0)
    ltspec = pl.BlockSpec((blk, LANES), lambda r, t: (jnp.minimum(t, nb - 1), r))
    in_specs = [qspec(qcol), qspec(one), qspec(one), ltspec,
                kspec(kcol, True), kspec(kcol, False), kspec(vcol, True), kspec(vcol, False)]
    args = [qv, do.reshape(sub, dil * d), o.reshape(sub, dil * d), lt.reshape(sub, dil * LANES),
            kview, kview, vview, vview]
    if acc_in:
        in_specs += [kout, kout]
        args += [dk_in.reshape(sub, dil * d), dv_in.reshape(sub, dil * d)]
    shp = jax.ShapeDtypeStruct((sub, dil * d), BF)
    dq, dk, dv = pl.pallas_call(
        body, name=name, grid=(dil, nb + 1),
        in_specs=in_specs, out_specs=[qspec(one), kout, kout], out_shape=[shp, shp, shp],
        scratch_shapes=[pltpu.VMEM((blk, d), F32), pltpu.VMEM((blk, d), F32),
                        pltpu.VMEM((2 * blk, d), F32), pltpu.VMEM((2 * blk, d), F32)],
        compiler_params=_params(("parallel", "arbitrary")),
    )(*args)
    return dq.reshape(s, d), dk.reshape(s, d), dv.reshape(s, d)


def _adamw(parts_list, w, m, v, name):
    nl, r, c = w.shape
    assert len(parts_list) == nl
    npart = parts_list[0].shape[0]
    tr = _pick(r, 256, 16)
    c1 = 1.0 / (1.0 - ADAM_B1 ** ADAM_STEP)
    c2 = 1.0 / (1.0 - ADAM_B2 ** ADAM_STEP)

    def body(*refs):
        p_refs = refs[:nl]
        w_ref, m_ref, v_ref, g_ref, d_ref, nm_ref, nv_ref = refs[nl:]
        layer = pl.program_id(0)
        for idx in range(nl):
            @pl.when(layer == idx)
            def _(p_ref=p_refs[idx]):
                g = p_ref[0].astype(F32)
                for k in range(1, npart):
                    g = g + p_ref[k].astype(F32)
                nm = ADAM_B1 * m_ref[...] + (1.0 - ADAM_B1) * g
                nv = ADAM_B2 * v_ref[...] + (1.0 - ADAM_B2) * (g * g)
                g_ref[...] = g
                nm_ref[...] = nm
                nv_ref[...] = nv
                d_ref[...] = -ADAM_LR * ((nm * c1) / (jnp.sqrt(nv * c2) + ADAM_EPS) + ADAM_WD * w_ref[...])

    def part_spec(idx):
        return pl.BlockSpec((npart, tr, c), lambda l, i: (0, jnp.where(l == idx, i, 0), 0))

    blk = pl.BlockSpec((None, tr, c), lambda l, i: (l, i, 0))
    shp = jax.ShapeDtypeStruct((nl, r, c), F32)
    return pl.pallas_call(
        body, name=name, grid=(nl, r // tr),
        in_specs=[part_spec(idx) for idx in range(nl)] + [blk, blk, blk],
        out_specs=[blk, blk, blk, blk], out_shape=[shp, shp, shp, shp],
        compiler_params=_params(("parallel", "parallel")),
    )(*parts_list, w, m, v)


def _full_from_slots(slots, shard_shape, axis):
    a = slots.reshape((N_DEV,) + tuple(shard_shape))
    a = jnp.moveaxis(a, 0, axis)
    full = list(shard_shape)
    full[axis] *= N_DEV
    return a.reshape(full)


def kernel(x, c, ada_w, ada_b, norm1_g, norm2_g, pool_w_in, pool_w_grp, pool_scale, pool_w_out, kv_norm_g, kv_ada_w, kv_ada_b, w_kv, attn_w_q, attn_w_o, ffn_w_up, ffn_conv_w, ffn_conv_b, ffn_w_down, final_g, loss_target, m_ada_w, m_ada_b, m_norm1_g, m_norm2_g, m_pool_w_in, m_pool_w_grp, m_pool_scale, m_pool_w_out, m_kv_norm_g, m_kv_ada_w, m_kv_ada_b, m_w_kv, m_attn_w_q, m_attn_w_o, m_ffn_w_up, m_ffn_conv_w, m_ffn_conv_b, m_ffn_w_down, m_final_g, v_ada_w, v_ada_b, v_norm1_g, v_norm2_g, v_pool_w_in, v_pool_w_grp, v_pool_scale, v_pool_w_out, v_kv_norm_g, v_kv_ada_w, v_kv_ada_b, v_w_kv, v_attn_w_q, v_attn_w_o, v_ffn_w_up, v_ffn_conv_w, v_ffn_conv_b, v_ffn_w_down, v_final_g):
    weights = dict(ada_w=ada_w, ada_b=ada_b, norm1_g=norm1_g, norm2_g=norm2_g, pool_w_in=pool_w_in,
                   pool_w_grp=pool_w_grp, pool_scale=pool_scale, pool_w_out=pool_w_out, kv_norm_g=kv_norm_g,
                   kv_ada_w=kv_ada_w, kv_ada_b=kv_ada_b, w_kv=w_kv, attn_w_q=attn_w_q, attn_w_o=attn_w_o,
                   ffn_w_up=ffn_w_up, ffn_conv_w=ffn_conv_w, ffn_conv_b=ffn_conv_b, ffn_w_down=ffn_w_down,
                   final_g=final_g)
    mom1 = dict(ada_w=m_ada_w, ada_b=m_ada_b, norm1_g=m_norm1_g, norm2_g=m_norm2_g, pool_w_in=m_pool_w_in,
                pool_w_grp=m_pool_w_grp, pool_scale=m_pool_scale, pool_w_out=m_pool_w_out, kv_norm_g=m_kv_norm_g,
                kv_ada_w=m_kv_ada_w, kv_ada_b=m_kv_ada_b, w_kv=m_w_kv, attn_w_q=m_attn_w_q, attn_w_o=m_attn_w_o,
                ffn_w_up=m_ffn_w_up, ffn_conv_w=m_ffn_conv_w, ffn_conv_b=m_ffn_conv_b, ffn_w_down=m_ffn_w_down,
                final_g=m_final_g)
    mom2 = dict(ada_w=v_ada_w, ada_b=v_ada_b, norm1_g=v_norm1_g, norm2_g=v_norm2_g, pool_w_in=v_pool_w_in,
                pool_w_grp=v_pool_w_grp, pool_scale=v_pool_scale, pool_w_out=v_pool_w_out, kv_norm_g=v_kv_norm_g,
                kv_ada_w=v_kv_ada_w, kv_ada_b=v_kv_ada_b, w_kv=v_w_kv, attn_w_q=v_attn_w_q, attn_w_o=v_attn_w_o,
                ffn_w_up=v_ffn_w_up, ffn_conv_w=v_ffn_conv_w, ffn_conv_b=v_ffn_conv_b, ffn_w_down=v_ffn_w_down,
                final_g=v_final_g)
    order = list(weights)

    seq, d = x.shape[1], x.shape[2]
    depth = ada_w.shape[0]
    n_pool = pool_w_in.shape[0]
    f = ffn_conv_b.shape[1]
    nbr = len(BRANCHES)
    nh = d // HEAD_DIM
    slopes = _alibi_slopes(nbr * nh).reshape(nbr, nh)
    me = 4 * lax.axis_index("x") + 2 * lax.axis_index("y") + lax.axis_index("c")
    xs = x[0]
    tgt = loss_target[0]

    def start_gathers(after):
        keys, groups = [], []
        for l in range(depth):
            if l < n_pool:
                parts = [("mixer", [pool_w_in[l], pool_w_grp[l].reshape(-1, pool_w_grp.shape[-1]), pool_w_out[l]])]
            else:
                j = l - n_pool
                parts = [("mixer", [attn_w_q[j]] + ([w_kv] if j == 0 else [])), ("out", [attn_w_o[j]])]
            for part, srcs in parts + [("ffn", [ffn_w_up[l], ffn_w_down[l]])]:
                keys.append((l, part))
                groups.append([a.astype(BF) for a in srcs])
        handles, _ = _comm_start(groups, "gather_start", scatter=False, after=after)
        return dict(zip(keys, handles))

    cond = c * (1.0 / (1.0 + jnp.exp(-c)))
    small_in = jnp.concatenate([cond.reshape(-1), ffn_conv_w.reshape(-1), pool_scale.reshape(-1)])
    n_small_in = small_in.shape[0]
    gath = _all_gather(_to_rows(small_in), "gather_small").reshape(N_DEV, -1)[:, :n_small_in]
    cond_all = gath[:, :d]
    o1 = d + ffn_conv_w.size
    conv_w_full = _full_from_slots(gath[:, d:o1], ffn_conv_w.shape, 2)
    pool_scale_full = _full_from_slots(gath[:, o1:], pool_scale.shape, 1)
    cond16 = jnp.concatenate([cond_all, jnp.zeros_like(cond_all)], axis=0)

    mod_part = _ada_fwd(cond16, ada_w, "ada_fwd")[:, :N_DEV]
    kv_part = _ada_fwd(cond16, kv_ada_w[None], "kv_ada_fwd")[0, :N_DEV]
    n_mod = depth * mod_part.shape[2] + kv_part.shape[1]
    send = jnp.concatenate([jnp.moveaxis(mod_part, 1, 0).reshape(N_DEV, -1), kv_part], axis=1)
    send_rows = jax.vmap(_to_rows)(send)
    got = _all_to_all(send_rows, "exchange_mod").reshape(N_DEV, -1)[:, :n_mod]
    ncol = mod_part.shape[2]
    mods = []
    for l in range(depth):
        row = got[:, l * ncol:(l + 1) * ncol].reshape(1, -1) + ada_b[l][None]
        mods.append([row[:, k * d:(k + 1) * d] for k in range(6)])
    kv_row = got[:, depth * ncol:].reshape(1, -1) + kv_ada_b[None]
    kv_shift, kv_scale = kv_row[:, :d], kv_row[:, d:]
    gather_handles = start_gathers(got)

    def vec(a):
        return a.reshape(1, -1)

    tall = _pick(seq, MM_ROWS)
    nq = 4
    fq = f // nq
    ng = len(POOL_WINDOWS)
    cw_slots = jnp.moveaxis(conv_w_full.reshape(depth, CONV_WIDTH, nq, fq), 2, 1)
    cb_slots = ffn_conv_b.reshape(depth, nq, 1, fq)

    saved = []
    xcur = xs
    kvs = None
    hkv = None
    x_kv = None
    w_kv_slots = None
    for l in range(depth):
        sh1, sc1, g1, sh2, sc2, g2 = mods[l]
        st = dict(x=xcur)
        h = _norm_mod(xcur, vec(norm1_g[l]), sc1, sh1, f"norm1_{l}")
        st["h"] = h
        gw = _comm_wait(gather_handles[l, "mixer"], h, f"gather_wait_mixer_{l}")
        if l < n_pool:
            w_in = gw[0].reshape(d, d)
            w_grp = jnp.moveaxis(gw[1].reshape(N_DEV, ng, -1, d // ng), 0, 1).reshape(ng, d // ng, d // ng)
            w_out = gw[2].reshape(d, d)
            u = _mm(h, w_in, f"pool_in_{l}")
            pooled = _pool_fwd(u, f"pool_fwd_{l}")
            z, y = _grp_fwd(pooled, w_grp, vec(pool_scale_full[l]), f"grp_fwd_{l}")
            mix, x1 = _mm(y, w_out, f"pool_out_{l}", gate=g1, resid=xcur, pre_dtype=BF)
            st.update(pooled=pooled, z=z, y=y, w_in=w_in, w_grp=w_grp, w_out=w_out)
        else:
            j = l - n_pool
            w_q_slots = gw[0]
            if j == 0:
                w_kv_slots = gw[1]
                x_kv = xcur
                hkv = _norm_mod(xcur, vec(kv_norm_g), kv_scale, kv_shift, "norm_kv")
                kvs = _mm(hkv, w_kv_slots, "kv_proj", bf="qkn", out_dtype=BF, tm=tall)
            q = _mm(h, w_q_slots, f"q_proj_{l}", bf="qkn", out_dtype=BF, tm=tall)
            outs, lses = [], []
            for g in range(nbr):
                og, lg = _attn_fwd(q, kvs, g, slopes[g], d, f"attn_fwd_{l}_{g}")
                outs.append(og)
                lses.append(lg)
            o, lt = _attn_combine(outs, lses, f"attn_mix_{l}")
            w_o = _comm_wait(gather_handles[l, "out"], o, f"gather_wait_out_{l}")[0].reshape(d, d)
            mix, x1 = _mm(o, w_o, f"attn_out_{l}", gate=g1, resid=xcur, pre_dtype=BF)
            st.update(q=q, o=o, lt=lt, w_q=w_q_slots, w_o=w_o)
        h2 = _norm_mod(x1, vec(norm2_g[l]), sc2, sh2, f"norm2_{l}")
        w_up_slots, w_down = _comm_wait(gather_handles[l, "ffn"], h2, f"gather_wait_ffn_{l}")
        w_down = w_down.reshape(f, d)
        st.update(w_up=w_up_slots, w_down=w_down)
        up = _mm(h2, w_up_slots, f"ffn_up_{l}", bf="qkn", of="qmn", out_dtype=BF, tm=tall).reshape(2, nq, seq, fq)
        act, ffo, x2 = _ffn_act_down(up, cw_slots[l], cb_slots[l], w_down, g2, x1, f"ffn_act_down_{l}")
        st.update(mix=mix, x1=x1, h2=h2, up=up, act=act, ffo=ffo)
        saved.append(st)
        xcur = x2

    dx, loss_blk, d_final_g = _final_loss(xcur, tgt, vec(final_g), "final_loss")
    loss = lax.psum(loss_blk[0, 0], ("x", "y", "c"))

    d_mod = [None] * depth
    d_n1 = [None] * depth
    d_n2 = [None] * depth
    d_conv = [None] * depth
    d_pscale = [None] * n_pool
    dk_acc = [None] * nbr
    dv_acc = [None] * nbr
    ffn_handles = [None] * depth
    mixer_handles = [None] * depth
    tok = 0.0
    for l in reversed(range(depth)):
        sh1, sc1, g1, sh2, sc2, g2 = mods[l]
        st = saved[l]
        dffo, dg2 = _gate_bwd(dx, st["ffo"], g2 + tok, f"gate2_bwd_{l}")
        g_down = _mm(st["act"], dffo, f"ffn_down_dw_{l}", af="qkm", out_dtype=BF)
        dact = _mm(dffo, st["w_down"], f"ffn_down_dx_{l}", bf="nk", of="qmn", tn=fq, out_dtype=BF, tm=tall)
        dup, dc, dh2 = _ffn_act_bwd_up(st["up"], dact, cw_slots[l], cb_slots[l], st["w_up"], f"ffn_act_bwd_up_{l}")
        dup = dup.reshape(N_DEV, seq, -1)
        d_conv[l] = dc
        g_up = _mm(st["h2"], dup, f"ffn_up_dw_{l}", af="km", bf="qkn", of="qmn", out_dtype=BF)
        (ffn_handles[l],), tok = _comm_start([[g_up, g_down.reshape(N_DEV, -1, d)]], f"exchange_start_ffn_{l}",
                                             scatter=True)
        dx1, dsh2, dw2 = _norm_mod_bwd(st["x1"], dh2, vec(norm2_g[l]), sc2 + tok, dx, f"norm2_bwd_{l}")
        d_n2[l] = dw2 * (1.0 + sc2)
        dsc2 = dw2 * vec(norm2_g[l])

        dmix, dg1 = _gate_bwd(dx1, st["mix"], g1, f"gate1_bwd_{l}")
        if l < n_pool:
            g_out = _mm(st["y"], dmix, f"pool_out_dw_{l}", af="km", out_dtype=BF)
            dy = _mm(dmix, st["w_out"], f"pool_out_dx_{l}", bf="nk", out_dtype=BF)
            dz, dpool, dps = _grp_bwd(dy, st["z"], st["w_grp"], vec(pool_scale_full[l]), f"grp_bwd_{l}")
            d_pscale[l] = dps
            g_grp = _grp_dw(st["pooled"], dz, ng, f"grp_dw_{l}")
            du = _pool_bwd(dpool, f"pool_bwd_{l}")
            g_in = _mm(st["h"], du, f"pool_in_dw_{l}", af="km", out_dtype=BF)
            dh = _mm(du, st["w_in"], f"pool_in_dx_{l}", bf="nk")
            g_grp_slots = jnp.moveaxis(g_grp.reshape(ng, N_DEV, -1, d // ng), 1, 0).reshape(N_DEV, -1, d // ng)
            send = [g_in.reshape(N_DEV, -1, d), g_grp_slots.astype(BF), g_out.reshape(N_DEV, -1, d)]
        else:
            j = l - n_pool
            g_o = _mm(st["o"], dmix, f"attn_out_dw_{l}", af="km", out_dtype=BF)
            do = _mm(dmix, st["w_o"], f"attn_out_dx_{l}", bf="nk", out_dtype=BF)
            dqs = []
            for g in range(nbr):
                dq_g, dk_g, dv_g = _attn_bwd(st["q"], kvs, do, st["o"], st["lt"], g, slopes[g], d,
                                             f"attn_bwd_{l}_{g}", dk_in=dk_acc[g], dv_in=dv_acc[g])
                dqs.append(dq_g)
                dk_acc[g], dv_acc[g] = dk_g, dv_g
            dq = jnp.concatenate(dqs, axis=1)
            nqc = st["w_q"].shape[2]
            g_q = _mm(st["h"], dq, f"q_proj_dw_{l}", af="km", of="qmn", tn=nqc, out_dtype=BF)
            dh = _mm(dq, st["w_q"], f"q_proj_dx_{l}", bf="qnk")
            send = [g_q, g_o.reshape(N_DEV, -1, d)]
        dx0, dsh1, dw1 = _norm_mod_bwd(st["x"], dh, vec(norm1_g[l]), sc1, dx1, f"norm1_bwd_{l}")
        d_n1[l] = dw1 * (1.0 + sc1)
        dsc1 = dw1 * vec(norm1_g[l])
        d_mod[l] = jnp.concatenate([dsh1, dsc1, dg1, dsh2, dsc2, dg2], axis=1)
        dx = dx0
        if l == n_pool:
            dkv = jnp.concatenate(dk_acc + dv_acc, axis=1)
            nkc = w_kv_slots.shape[2]
            g_kv = _mm(hkv, dkv, "kv_proj_dw", af="km", of="qmn", tn=nkc, out_dtype=BF)
            dhkv = _mm(dkv, w_kv_slots, "kv_proj_dx", bf="qnk")
            dx, dsh_kv, dw_kv = _norm_mod_bwd(x_kv, dhkv, vec(kv_norm_g), kv_scale, dx, "norm_kv_bwd")
            d_kv_norm = dw_kv * (1.0 + kv_scale)
            d_kv_mod = jnp.concatenate([dsh_kv, dw_kv * vec(kv_norm_g)], axis=1)
            send.append(g_kv)
        if l > 0:
            (mixer_handles[l],), tok = _comm_start([send], f"exchange_start_mixer_{l}", scatter=True)
    grad_x = dx[None]

    small = [jnp.concatenate(d_mod, axis=1).reshape(-1), d_kv_mod.reshape(-1),
             jnp.concatenate(d_n1, axis=0).reshape(-1), jnp.concatenate(d_n2, axis=0).reshape(-1),
             d_kv_norm.reshape(-1), d_final_g.reshape(-1),
             jnp.stack([dcl[:, 3, :] for dcl in d_conv]).reshape(-1),
             jnp.stack([jnp.moveaxis(dcl[:, 0:CONV_WIDTH, :], 0, 1) for dcl in d_conv]).reshape(-1),
             jnp.concatenate(d_pscale, axis=0).reshape(-1)]
    sizes = [a.shape[0] for a in small]
    small_rows = _to_rows(jnp.concatenate(small))
    small_all = _all_gather(small_rows, "gather_small_grads")
    (mixer_handles[0],), _ = _comm_start([send], "exchange_start_mixer_0", scatter=True, after=small_all)
    dmod_all = small_all.reshape(N_DEV, -1)[:, :sizes[0] + sizes[1]]

    dmod16 = jnp.concatenate([dmod_all, jnp.zeros_like(dmod_all)], axis=0)
    dm = dmod16[:, :sizes[0]].reshape(16, depth, N_DEV, ncol)
    dm_mine = lax.dynamic_index_in_dim(dm, me, axis=2, keepdims=False)
    g_ada_w = _ada_bwd(cond16, jnp.moveaxis(dm_mine, 0, 1), "ada_bwd")
    nkv = kv_part.shape[1]
    dkm = dmod16[:, sizes[0]:].reshape(16, N_DEV, nkv)
    dkm_mine = lax.dynamic_index_in_dim(dkm, me, axis=1, keepdims=False)
    g_kv_ada_w = _ada_bwd(cond16, dkm_mine[None], "kv_ada_bwd")

    res = {}

    def update(n, parts_list, shape3):
        w3, m3, v3 = (a[n].reshape(shape3) for a in (weights, mom1, mom2))
        outs = _adamw(parts_list, w3, m3, v3, f"adamw_{n}")
        res[n] = [a.reshape(weights[n].shape) for a in outs]

    update("ada_w", [g_ada_w[l][None] for l in range(depth)], ada_w.shape)
    update("kv_ada_w", [g_kv_ada_w], (1,) + kv_ada_w.shape)
    after = res["ada_w"][0]
    parts_ffn = [_comm_wait(ffn_handles[l], after, f"exchange_wait_ffn_{l}") for l in reversed(range(depth))][::-1]
    parts = [_comm_wait(mixer_handles[l], after, f"exchange_wait_mixer_{l}") for l in reversed(range(depth))][::-1]
    pool_layers, attn_layers = range(n_pool), range(n_pool, depth)
    update("ffn_w_up", [parts_ffn[l][0] for l in range(depth)], ffn_w_up.shape)
    update("ffn_w_down", [parts_ffn[l][1] for l in range(depth)], ffn_w_down.shape)
    update("attn_w_q", [parts[l][0] for l in attn_layers], attn_w_q.shape)
    update("attn_w_o", [parts[l][1] for l in attn_layers], (depth - n_pool, -1, d))
    update("w_kv", [parts[n_pool][2]], (1,) + w_kv.shape)
    update("pool_w_in", [parts[l][0] for l in pool_layers], (n_pool, -1, d))
    update("pool_w_grp", [parts[l][1] for l in pool_layers], (n_pool, -1, d // ng))
    update("pool_w_out", [parts[l][2] for l in pool_layers], (n_pool, -1, d))

    tot = small_all.reshape(N_DEV, -1)
    offs = np.cumsum([0] + sizes)
    seg = {k: (int(offs[i]), int(offs[i + 1])) for i, k in enumerate(
        ["mod", "kv_mod", "n1", "n2", "kv_norm", "final", "conv_b", "conv_w", "pscale"])}

    def rows_of(a, b):
        return tot[:, a:b]

    nf8 = f // N_DEV
    conv_w_parts = lax.dynamic_slice_in_dim(
        rows_of(*seg["conv_w"]).reshape(N_DEV, depth, CONV_WIDTH, N_DEV, nf8), me, 1, axis=3).reshape(N_DEV, -1)
    nd8 = d // N_DEV
    pscale_parts = lax.dynamic_slice_in_dim(
        rows_of(*seg["pscale"]).reshape(N_DEV, n_pool, N_DEV, nd8), me, 1, axis=2).reshape(N_DEV, -1)
    small_names = ["ada_b", "norm1_g", "norm2_g", "pool_scale", "kv_norm_g", "kv_ada_b", "ffn_conv_w",
                   "ffn_conv_b", "final_g"]
    small_parts = [rows_of(*seg["mod"]), rows_of(*seg["n1"]), rows_of(*seg["n2"]), pscale_parts,
                   rows_of(*seg["kv_norm"]), rows_of(*seg["kv_mod"]), conv_w_parts, rows_of(*seg["conv_b"]),
                   rows_of(*seg["final"])]
    sp = jnp.concatenate(small_parts, axis=1)
    n_sp = sp.shape[1]
    sp_rows = jax.vmap(_to_rows)(sp)

    def packed(src):
        return _to_rows(jnp.concatenate([src[n].reshape(-1) for n in small_names]))[None]

    outs = _adamw([sp_rows], packed(weights), packed(mom1), packed(mom2), "adamw_small")
    outs = [a.reshape(-1)[:n_sp] for a in outs]
    off = 0
    for n in small_names:
        size = weights[n].size
        res[n] = [a[off:off + size].reshape(weights[n].shape) for a in outs]
        off += size

    grads = [res[n][0] for n in order]
    deltas = [res[n][1] for n in order]
    new_m = [res[n][2] for n in order]
    new_v = [res[n][3] for n in order]
    return (loss, grad_x, *grads, *deltas, *new_m, *new_v)
```

```python
import math

import numpy as np
import jax
import jax.numpy as jnp
from jax import lax
from jax.experimental import pallas as pl
from jax.experimental.pallas import tpu as pltpu

F32 = jnp.float32
BF = jnp.bfloat16

POOL_WINDOWS = (2, 4, 8, 16)
BRANCHES = ((128, 1), (512, 4), (2048, 16))
HEAD_DIM = 64
ATTN_BLOCK = 128
CONV_WIDTH = 3
EPS = 1e-6
ADAM_LR = 0.001
ADAM_B1 = 0.9
ADAM_B2 = 0.999
ADAM_EPS = 1e-08
ADAM_WD = 0.01
ADAM_STEP = 10

N_DEV = 8
LANES = 128
POOL_HALO = 16
CONV_HALO = 8
FFN_ROWS = 16
FFN_GROUP = 128
VMEM_LIMIT = 48 * 1024 * 1024
MM_TILE = 1024
MM_ROWS = 2048
NEG = -1e30

MESH = pl.DeviceIdType.MESH
ANY = pl.BlockSpec(memory_space=pl.ANY)


def _params(sem=None):
    if sem is None:
        return pltpu.CompilerParams(vmem_limit_bytes=VMEM_LIMIT)
    return pltpu.CompilerParams(dimension_semantics=sem, vmem_limit_bytes=VMEM_LIMIT)


def _pick(dim, pref, mult=LANES):
    if dim <= pref:
        return dim
    t = (pref // mult) * mult
    while t >= mult:
        if dim % t == 0:
            return t
        t -= mult
    return dim


def _alibi_slopes(n):
    def pow2(m):
        start = 2.0 ** (-(2.0 ** -(math.log2(m) - 3)))
        return [start ** (i + 1) for i in range(m)]
    if math.log2(n).is_integer():
        s = pow2(n)
    else:
        c = 2 ** math.floor(math.log2(n))
        s = pow2(c) + pow2(2 * c)[0::2][: n - c]
    s = np.asarray(s, dtype=np.float32)
    return -np.sort(-s)


def _my_place():
    return lax.axis_index("x"), lax.axis_index("y"), lax.axis_index("c")


def _all_gather_many(xs, name):
    n = len(xs)

    def body(*refs):
        x_refs, out_refs = refs[:n], refs[n:2 * n]
        send_sems, recv_sems, local_sems = refs[2 * n:]
        xi, yi, ci = _my_place()
        me, sibling = (xi, yi, ci), (xi, yi, 1 - ci)
        chips = [(1 - xi, yi), (xi, 1 - yi), (1 - xi, 1 - yi)]

        def slot(a, px, py, pc):
            return out_refs[a].at[4 * px + 2 * py + pc]

        def copy(a, k, block, to, src=None):
            return pltpu.make_async_remote_copy(
                src_ref=slot(a, *block) if src is None else src, dst_ref=slot(a, *block),
                send_sem=send_sems.at[7 * a + k], recv_sem=recv_sems.at[7 * a + k],
                device_id=to, device_id_type=MESH)

        mine = [pltpu.make_async_copy(x_refs[a], slot(a, *me), local_sems.at[a]) for a in range(n)]
        for cp in mine:
            cp.start()
        sent = []
        for a in range(n):
            first = [copy(a, 0, me, sibling, src=x_refs[a])]
            first += [copy(a, 1 + j, me, (*chip, ci), src=x_refs[a]) for j, chip in enumerate(chips)]
            for cp in first:
                cp.start()
            sent += first
        for a in range(n):
            for j, chip in enumerate(chips):
                copy(a, 1 + j, (*chip, ci), me).wait_recv()
                fwd = copy(a, 4 + j, (*chip, ci), sibling)
                fwd.start()
                sent.append(fwd)
        for a in range(n):
            copy(a, 0, sibling, me).wait_recv()
            for j, chip in enumerate(chips):
                copy(a, 4 + j, (*chip, 1 - ci), me).wait_recv()
        for cp in sent:
            cp.wait_send()
        for cp in mine:
            cp.wait()

    return pl.pallas_call(
        body, name=name,
        out_shape=[jax.ShapeDtypeStruct((N_DEV,) + x.shape, x.dtype) for x in xs],
        in_specs=[ANY] * n, out_specs=[ANY] * n,
        scratch_shapes=[pltpu.SemaphoreType.DMA((7 * n,)), pltpu.SemaphoreType.DMA((7 * n,)),
                        pltpu.SemaphoreType.DMA((n,))],
    )(*xs)


def _all_gather(x, name):
    return _all_gather_many([x], name)[0]


def _all_to_all_many(xs, name):
    n = len(xs)

    def body(*refs):
        x_refs, out_refs = refs[:n], refs[n:2 * n]
        send_sems, recv_sems, local_sems = refs[2 * n:]
        xi, yi, ci = _my_place()
        me = 4 * xi + 2 * yi + ci
        mine = [pltpu.make_async_copy(x_refs[a].at[me], out_refs[a].at[me], local_sems.at[a]) for a in range(n)]
        for cp in mine:
            cp.start()
        copies = []
        for a in range(n):
            for k in range(1, N_DEV):
                px = 1 - xi if k & 4 else xi
                py = 1 - yi if k & 2 else yi
                pc = 1 - ci if k & 1 else ci
                peer = 4 * px + 2 * py + pc
                cp = pltpu.make_async_remote_copy(
                    src_ref=x_refs[a].at[peer], dst_ref=out_refs[a].at[me],
                    send_sem=send_sems.at[7 * a + k - 1], recv_sem=recv_sems.at[7 * a + k - 1],
                    device_id=(px, py, pc), device_id_type=MESH)
                cp.start()
                copies.append(cp)
        for cp in copies:
            cp.wait()
        for cp in mine:
            cp.wait()

    return pl.pallas_call(
        body, name=name,
        out_shape=[jax.ShapeDtypeStruct(x.shape, x.dtype) for x in xs],
        in_specs=[ANY] * n, out_specs=[ANY] * n,
        scratch_shapes=[pltpu.SemaphoreType.DMA((7 * n,)), pltpu.SemaphoreType.DMA((7 * n,)),
                        pltpu.SemaphoreType.DMA((n,))],
    )(*xs)


def _all_to_all(x, name):
    return _all_to_all_many([x], name)[0]


HBM = pl.BlockSpec(memory_space=pltpu.HBM)
SEM = pl.BlockSpec(memory_space=pltpu.SEMAPHORE)
EFFECT = pltpu.SideEffectType.DATAFLOW_SIDE_EFFECTING


def _peer(k, xi, yi, ci):
    px = 1 - xi if k & 4 else xi
    py = 1 - yi if k & 2 else yi
    pc = 1 - ci if k & 1 else ci
    return (px, py, pc), 4 * px + 2 * py + pc


def _split_copies(x_refs, land_refs, send_sem, recv_sem, scatter):
    xi, yi, ci = _my_place()
    me = 4 * xi + 2 * yi + ci
    copies = []
    for p, (x_ref, land_ref) in enumerate(zip(x_refs, land_refs)):
        for k in range(1, N_DEV):
            place, peer = _peer(k, xi, yi, ci)
            copies.append(pltpu.make_async_remote_copy(
                src_ref=x_ref.at[peer] if scatter else x_ref, dst_ref=land_ref.at[me],
                send_sem=send_sem.at[7 * p + k - 1], recv_sem=recv_sem.at[7 * p + k - 1],
                device_id=place, device_id_type=MESH))
    return copies


def _comm_start(groups, name, scatter, after=None):
    sizes = [len(g) for g in groups]
    xs = [x for g in groups for x in g]
    n, ng = len(xs), len(groups)
    lands = [lax.empty(x.shape if scatter else (N_DEV,) + x.shape, x.dtype) for x in xs]
    starts = np.cumsum([0] + sizes)
    n_in = 2 * n + (after is not None)

    def body(*refs):
        x_refs, land_refs = refs[:n], refs[n:2 * n]
        send_sems, recv_sems = refs[n_in:n_in + ng], refs[n_in + ng:n_in + 2 * ng]
        token = refs[n_in + 2 * ng + 2 * n]
        for gi in range(ng):
            lo, hi = int(starts[gi]), int(starts[gi + 1])
            for cp in _split_copies(x_refs[lo:hi], land_refs[lo:hi], send_sems[gi], recv_sems[gi], scatter):
                cp.start()
        token[...] = jnp.zeros_like(token)

    sem_shapes = [pltpu.SemaphoreType.DMA((7 * m,)) for m in sizes]
    thru = [pltpu.HBM(a.shape, a.dtype) for a in xs + lands]
    res = pl.pallas_call(
        body, name=name,
        out_shape=sem_shapes + sem_shapes + thru + [jax.ShapeDtypeStruct((8, LANES), F32)],
        in_specs=[HBM] * n_in,
        out_specs=[SEM] * (2 * ng) + [HBM] * (2 * n) + [pl.BlockSpec(memory_space=pltpu.VMEM)],
        input_output_aliases={i: 2 * ng + i for i in range(2 * n)},
        compiler_params=pltpu.CompilerParams(has_side_effects=EFFECT),
    )(*[pltpu.with_memory_space_constraint(a, pltpu.HBM) for a in xs + lands + ([] if after is None else [after])])
    send_sems, recv_sems = res[:ng], res[ng:2 * ng]
    x_thru, land_thru = res[2 * ng:2 * ng + n], res[2 * ng + n:2 * ng + 2 * n]
    handles = []
    for gi in range(ng):
        lo, hi = int(starts[gi]), int(starts[gi + 1])
        handles.append((send_sems[gi], recv_sems[gi], list(x_thru[lo:hi]), list(land_thru[lo:hi]), scatter))
    return handles, res[-1][0, 0]


def _comm_wait(handle, after, name):
    send_sem, recv_sem, x_thru, land_thru, scatter = handle
    m = len(x_thru)

    blocks = [a.shape[1:] if scatter else a.shape for a in x_thru]

    def body(*refs):
        x_refs, land_refs = refs[:m], refs[m:2 * m]
        local_sems, stage = refs[4 * m + 3], refs[4 * m + 4:]
        xi, yi, ci = _my_place()
        me = 4 * xi + 2 * yi + ci
        load = [pltpu.make_async_copy(x_refs[p].at[me] if scatter else x_refs[p], stage[p], local_sems.at[2 * p])
                for p in range(m)]
        store = [pltpu.make_async_copy(stage[p], land_refs[p].at[me], local_sems.at[2 * p + 1]) for p in range(m)]
        for cp in load:
            cp.start()
        for p in range(m):
            load[p].wait()
            store[p].start()
        for cp in _split_copies(x_refs, land_refs, refs[2 * m], refs[2 * m + 1], scatter):
            cp.wait_send()
            cp.wait_recv()
        for cp in store:
            cp.wait()

    res = pl.pallas_call(
        body, name=name,
        out_shape=[pltpu.HBM(a.shape, a.dtype) for a in x_thru + land_thru],
        in_specs=[HBM] * (2 * m) + [SEM, SEM, ANY], out_specs=[HBM] * (2 * m),
        input_output_aliases={i: i for i in range(2 * m)},
        scratch_shapes=[pltpu.SemaphoreType.DMA((2 * m,))] + [pltpu.VMEM(b, a.dtype) for b, a in zip(blocks, x_thru)],
        compiler_params=pltpu.CompilerParams(has_side_effects=EFFECT),
    )(*x_thru, *land_thru, send_sem, recv_sem, after)
    return list(res[m:])


def _to_rows(vec):
    n = vec.shape[0]
    unit = 8 * LANES
    pad = (-n) % unit
    if pad:
        vec = jnp.concatenate([vec, jnp.zeros((pad,), vec.dtype)])
    return vec.reshape(-1, LANES)


def _mm(a, b, name, *, af="mk", bf="kn", of="mn", out_dtype=F32, tm=None, tn=None, tk=None,
        gate=None, resid=None, pre_dtype=None):
    if af == "mk":
        m, kk = a.shape
    elif af == "km":
        kk, m = a.shape
    elif af == "qmk":
        qa, m, tk = a.shape
        kk = qa * tk
    else:
        qa, kk, tm = a.shape
        m = qa * tm
    if bf == "kn":
        k2, n = b.shape
    elif bf == "nk":
        n, k2 = b.shape
    elif bf == "qkn":
        qb, k2, tn = b.shape
        n = qb * tn
    else:
        qb, n, tkb = b.shape
        k2 = qb * tkb
        assert af != "qmk" or tkb == tk
        tk = tkb
    assert kk == k2, (name, a.shape, b.shape, af, bf)
    tm = _pick(m, MM_TILE) if tm is None else tm
    tn = _pick(n, MM_TILE) if tn is None else tn
    tk = _pick(kk, MM_TILE) if tk is None else tk
    assert m % tm == 0 and n % tn == 0 and kk % tk == 0, (name, m, n, kk, tm, tn, tk)
    nk = kk // tk
    a_spec = {"mk": pl.BlockSpec((tm, tk), lambda i, j, k: (i, k)),
              "km": pl.BlockSpec((tk, tm), lambda i, j, k: (k, i)),
              "qmk": pl.BlockSpec((None, tm, tk), lambda i, j, k: (k, i, 0)),
              "qkm": pl.BlockSpec((None, tk, tm), lambda i, j, k: (i, k, 0))}[af]
    b_spec = {"kn": pl.BlockSpec((tk, tn), lambda i, j, k: (k, j)),
              "nk": pl.BlockSpec((tn, tk), lambda i, j, k: (j, k)),
              "qkn": pl.BlockSpec((None, tk, tn), lambda i, j, k: (j, k, 0)),
              "qnk": pl.BlockSpec((None, tn, tk), lambda i, j, k: (k, j, 0))}[bf]
    dims = (((1 if af in ("mk", "qmk") else 0,), (0 if bf in ("kn", "qkn") else 1,)), ((), ()))
    in_specs, args = [a_spec, b_spec], [a, b]
    if gate is not None:
        assert of == "mn"
        in_specs.append(pl.BlockSpec((1, tn), lambda i, j, k: (0, j)))
        args.append(gate)
    if resid is not None:
        assert of == "mn"
        in_specs.append(pl.BlockSpec((tm, tn), lambda i, j, k: (i, j)))
        args.append(resid)
    if of == "mn":
        o_spec, o_shape = pl.BlockSpec((tm, tn), lambda i, j, k: (i, j)), (m, n)
    else:
        o_spec, o_shape = pl.BlockSpec((None, tm, tn), lambda i, j, k: (j, i, 0)), (n // tn, m, tn)
    out_shape, out_specs = [jax.ShapeDtypeStruct(o_shape, out_dtype)], [o_spec]
    if pre_dtype is not None:
        out_shape.insert(0, jax.ShapeDtypeStruct(o_shape, pre_dtype))
        out_specs.insert(0, o_spec)
    n_in = len(args)
    n_out = len(out_shape)

    def body(*refs):
        a_ref, b_ref = refs[0], refs[1]
        extra = list(refs[2:n_in])
        outs = refs[n_in:n_in + n_out]
        gate_ref = extra.pop(0) if gate is not None else None
        resid_ref = extra.pop(0) if resid is not None else None

        def product():
            return lax.dot_general(a_ref[...].astype(BF), b_ref[...].astype(BF), dims, preferred_element_type=F32)

        def finish(r):
            if pre_dtype is not None:
                outs[0][...] = r.astype(pre_dtype)
            if gate_ref is not None:
                r = r * gate_ref[...]
            if resid_ref is not None:
                r = resid_ref[...] + r
            outs[-1][...] = r.astype(out_dtype)

        if nk == 1:
            finish(product())
        else:
            acc = refs[n_in + n_out]
            k = pl.program_id(2)

            @pl.when(k == 0)
            def _():
                acc[...] = product()

            @pl.when(k > 0)
            def _():
                acc[...] += product()

            @pl.when(k == nk - 1)
            def _():
                finish(acc[...])

    res = pl.pallas_call(
        body, name=name, grid=(m // tm, n // tn, nk),
        in_specs=in_specs, out_specs=out_specs, out_shape=out_shape,
        scratch_shapes=[pltpu.VMEM((tm, tn), F32)] if nk > 1 else [],
        compiler_params=_params(("parallel", "parallel", "arbitrary")),
    )(*args)
    return res if pre_dtype is not None else res[0]


def _ada_fwd(cond16, w, name):
    nl, d, n = w.shape

    def body(c_ref, w_ref, o_ref):
        o_ref[...] = jnp.dot(c_ref[...].astype(BF), w_ref[...].astype(BF), preferred_element_type=F32)

    return pl.pallas_call(
        body, name=name, grid=(nl,),
        in_specs=[pl.BlockSpec((16, d), lambda l: (0, 0)), pl.BlockSpec((None, d, n), lambda l: (l, 0, 0))],
        out_specs=pl.BlockSpec((None, 16, n), lambda l: (l, 0, 0)),
        out_shape=jax.ShapeDtypeStruct((nl, 16, n), F32),
        compiler_params=_params(("parallel",)),
    )(cond16, w)


def _ada_bwd(cond16, dmod, name):
    nl, _, n = dmod.shape
    d = cond16.shape[1]

    def body(c_ref, g_ref, o_ref):
        o_ref[...] = lax.dot_general(c_ref[...].astype(BF), g_ref[...].astype(BF), (((0,), (0,)), ((), ())),
                                     preferred_element_type=F32)

    return pl.pallas_call(
        body, name=name, grid=(nl,),
        in_specs=[pl.BlockSpec((16, d), lambda l: (0, 0)), pl.BlockSpec((None, 16, n), lambda l: (l, 0, 0))],
        out_specs=pl.BlockSpec((None, d, n), lambda l: (l, 0, 0)),
        out_shape=jax.ShapeDtypeStruct((nl, d, n), F32),
        compiler_params=_params(("parallel",)),
    )(cond16, dmod)


def _row_spec(tm, d):
    return pl.BlockSpec((tm, d), lambda i: (i, 0))


def _vec_spec(d):
    return pl.BlockSpec((1, d), lambda i: (0, 0))


def _norm_mod(x, g, sc, sh, name):
    s, d = x.shape
    tm = _pick(s, 512, 8)

    def body(x_ref, g_ref, sc_ref, sh_ref, o_ref):
        xv = x_ref[...]
        r = lax.rsqrt(jnp.mean(xv * xv, axis=-1, keepdims=True) + EPS)
        y = (xv * r) * g_ref[...]
        o_ref[...] = (y * (1.0 + sc_ref[...]) + sh_ref[...]).astype(BF)

    return pl.pallas_call(
        body, name=name, grid=(s // tm,),
        in_specs=[_row_spec(tm, d), _vec_spec(d), _vec_spec(d), _vec_spec(d)],
        out_specs=_row_spec(tm, d), out_shape=jax.ShapeDtypeStruct((s, d), BF),
        compiler_params=_params(("parallel",)),
    )(x, g, sc, sh)


def _norm_mod_bwd(x, dh, g, sc, dx_in, name):
    s, d = x.shape
    tm = _pick(s, 512, 8)

    def body(x_ref, dh_ref, g_ref, sc_ref, dxin_ref, dx_ref, dsh_ref, dw_ref):
        i = pl.program_id(0)
        xv = x_ref[...]
        dhv = dh_ref[...].astype(F32)
        r = lax.rsqrt(jnp.mean(xv * xv, axis=-1, keepdims=True) + EPS)
        xn = xv * r
        dxn = dhv * (g_ref[...] * (1.0 + sc_ref[...]))
        dx_ref[...] = dxin_ref[...] + r * (dxn - xn * jnp.mean(dxn * xn, axis=-1, keepdims=True))

        @pl.when(i == 0)
        def _():
            dsh_ref[...] = jnp.zeros_like(dsh_ref)
            dw_ref[...] = jnp.zeros_like(dw_ref)

        dsh_ref[...] += jnp.sum(dhv, axis=0, keepdims=True)
        dw_ref[...] += jnp.sum(dhv * xn, axis=0, keepdims=True)

    return pl.pallas_call(
        body, name=name, grid=(s // tm,),
        in_specs=[_row_spec(tm, d), _row_spec(tm, d), _vec_spec(d), _vec_spec(d), _row_spec(tm, d)],
        out_specs=[_row_spec(tm, d), _vec_spec(d), _vec_spec(d)],
        out_shape=[jax.ShapeDtypeStruct((s, d), F32), jax.ShapeDtypeStruct((1, d), F32),
                   jax.ShapeDtypeStruct((1, d), F32)],
        compiler_params=_params(("arbitrary",)),
    )(x, dh, g, sc, dx_in)


def _gate_bwd(dx, y, gate, name):
    s, d = dx.shape
    tm = _pick(s, 512, 8)

    def body(dx_ref, y_ref, g_ref, dy_ref, dg_ref):
        i = pl.program_id(0)
        dxv = dx_ref[...]
        dy_ref[...] = (dxv * g_ref[...]).astype(BF)

        @pl.when(i == 0)
        def _():
            dg_ref[...] = jnp.zeros_like(dg_ref)

        dg_ref[...] += jnp.sum(dxv * y_ref[...].astype(F32), axis=0, keepdims=True)

    return pl.pallas_call(
        body, name=name, grid=(s // tm,),
        in_specs=[_row_spec(tm, d), _row_spec(tm, d), _vec_spec(d)],
        out_specs=[_row_spec(tm, d), _vec_spec(d)],
        out_shape=[jax.ShapeDtypeStruct((s, d), BF), jax.ShapeDtypeStruct((1, d), F32)],
        compiler_params=_params(("arbitrary",)),
    )(dx, y, gate)


def _final_loss(x, tgt, g, name):
    s, d = x.shape
    tm = _pick(s, 512, 8)

    def body(x_ref, t_ref, g_ref, dx_ref, loss_ref, dg_ref):
        i = pl.program_id(0)
        xv = x_ref[...]
        gv = g_ref[...]
        r = lax.rsqrt(jnp.mean(xv * xv, axis=-1, keepdims=True) + EPS)
        xn = xv * r
        err = xn * gv - t_ref[...]
        dy = err * (1.0 / d)
        dxn = dy * gv
        dx_ref[...] = r * (dxn - xn * jnp.mean(dxn * xn, axis=-1, keepdims=True))

        @pl.when(i == 0)
        def _():
            loss_ref[...] = jnp.zeros_like(loss_ref)
            dg_ref[...] = jnp.zeros_like(dg_ref)

        part = 0.5 * jnp.sum(jnp.sum(err * err, axis=-1, keepdims=True) * (1.0 / d), axis=0, keepdims=True)
        loss_ref[...] += jnp.broadcast_to(part, loss_ref.shape)
        dg_ref[...] += jnp.sum(dy * xn, axis=0, keepdims=True)

    return pl.pallas_call(
        body, name=name, grid=(s // tm,),
        in_specs=[_row_spec(tm, d), _row_spec(tm, d), _vec_spec(d)],
        out_specs=[_row_spec(tm, d), pl.BlockSpec((8, LANES), lambda i: (0, 0)), _vec_spec(d)],
        out_shape=[jax.ShapeDtypeStruct((s, d), F32), jax.ShapeDtypeStruct((8, LANES), F32),
                   jax.ShapeDtypeStruct((1, d), F32)],
        compiler_params=_params(("arbitrary",)),
    )(x, tgt, g)


def _pool_counts(tm, gd, row0, w):
    t = lax.broadcasted_iota(jnp.int32, (tm, gd), 0) + row0
    return jnp.minimum(t + 1, w).astype(F32)


def _pool_fwd(u, name):
    s, d = u.shape
    tm = _pick(s, 256, POOL_HALO)
    gd = d // len(POOL_WINDOWS)
    per = tm // POOL_HALO

    def body(prev_ref, cur_ref, o_ref, ext):
        i = pl.program_id(0)
        ext[0:POOL_HALO, :] = jnp.where(i > 0, prev_ref[...], 0.0)
        ext[POOL_HALO:, :] = cur_ref[...]
        for g, w in enumerate(POOL_WINDOWS):
            cols = slice(g * gd, (g + 1) * gd)
            acc = ext[POOL_HALO:POOL_HALO + tm, cols]
            own = acc
            for k in range(1, w):
                acc = acc + ext[POOL_HALO - k:POOL_HALO - k + tm, cols]
            o_ref[:, cols] = (acc / _pool_counts(tm, gd, i * tm, w) - own).astype(BF)

    return pl.pallas_call(
        body, name=name, grid=(s // tm,),
        in_specs=[pl.BlockSpec((POOL_HALO, d), lambda i: (jnp.maximum(i * per - 1, 0), 0)), _row_spec(tm, d)],
        out_specs=_row_spec(tm, d), out_shape=jax.ShapeDtypeStruct((s, d), BF),
        scratch_shapes=[pltpu.VMEM((tm + POOL_HALO, d), F32)],
        compiler_params=_params(("parallel",)),
    )(u, u)


def _pool_bwd(dp, name):
    s, d = dp.shape
    tm = _pick(s, 256, POOL_HALO)
    gd = d // len(POOL_WINDOWS)
    per = tm // POOL_HALO
    nt = s // tm
    last_halo = s // POOL_HALO - 1

    def body(cur_ref, nxt_ref, o_ref, ext):
        i = pl.program_id(0)
        for g, w in enumerate(POOL_WINDOWS):
            cols = slice(g * gd, (g + 1) * gd)
            ext[0:tm, cols] = cur_ref[:, cols].astype(F32) / _pool_counts(tm, gd, i * tm, w)
            nxt = nxt_ref[:, cols].astype(F32) / _pool_counts(POOL_HALO, gd, (i + 1) * tm, w)
            ext[tm:, cols] = jnp.where(i < nt - 1, nxt, 0.0)
        for g, w in enumerate(POOL_WINDOWS):
            cols = slice(g * gd, (g + 1) * gd)
            acc = ext[0:tm, cols]
            for k in range(1, w):
                acc = acc + ext[k:k + tm, cols]
            o_ref[:, cols] = (acc - cur_ref[:, cols].astype(F32)).astype(BF)

    return pl.pallas_call(
        body, name=name, grid=(nt,),
        in_specs=[_row_spec(tm, d), pl.BlockSpec((POOL_HALO, d), lambda i: (jnp.minimum((i + 1) * per, last_halo), 0))],
        out_specs=_row_spec(tm, d), out_shape=jax.ShapeDtypeStruct((s, d), BF),
        scratch_shapes=[pltpu.VMEM((tm + POOL_HALO, d), F32)],
        compiler_params=_params(("parallel",)),
    )(dp, dp)


def _grp_fwd(p, w, scale, name):
    s, d = p.shape
    ng, gd, _ = w.shape
    tm = _pick(s, 1024, 8)

    def body(p_ref, w_ref, s_ref, z_ref, y_ref):
        z = jnp.dot(p_ref[...], w_ref[...].astype(BF), preferred_element_type=F32)
        z_ref[...] = z.astype(BF)
        y_ref[...] = (z * s_ref[...]).astype(BF)

    blk = pl.BlockSpec((tm, gd), lambda i, g: (i, g))
    return pl.pallas_call(
        body, name=name, grid=(s // tm, ng),
        in_specs=[blk, pl.BlockSpec((None, gd, gd), lambda i, g: (g, 0, 0)), pl.BlockSpec((1, gd), lambda i, g: (0, g))],
        out_specs=[blk, blk],
        out_shape=[jax.ShapeDtypeStruct((s, d), BF), jax.ShapeDtypeStruct((s, d), BF)],
        compiler_params=_params(("parallel", "parallel")),
    )(p, w, scale)


def _grp_bwd(dy, z, w, scale, name):
    s, d = dy.shape
    ng, gd, _ = w.shape
    tm = _pick(s, 1024, 8)

    def body(dy_ref, z_ref, w_ref, s_ref, dz_ref, dp_ref, ds_ref):
        i = pl.program_id(1)
        dyv = dy_ref[...].astype(F32)
        dz = (dyv * s_ref[...]).astype(BF)
        dz_ref[...] = dz
        dp_ref[...] = lax.dot_general(dz, w_ref[...].astype(BF), (((1,), (1,)), ((), ())),
                                      preferred_element_type=F32).astype(BF)

        @pl.when(i == 0)
        def _():
            ds_ref[...] = jnp.zeros_like(ds_ref)

        ds_ref[...] += jnp.sum(dyv * z_ref[...].astype(F32), axis=0, keepdims=True)

    blk = pl.BlockSpec((tm, gd), lambda g, i: (i, g))
    vec = pl.BlockSpec((1, gd), lambda g, i: (0, g))
    return pl.pallas_call(
        body, name=name, grid=(ng, s // tm),
        in_specs=[blk, blk, pl.BlockSpec((None, gd, gd), lambda g, i: (g, 0, 0)), vec],
        out_specs=[blk, blk, vec],
        out_shape=[jax.ShapeDtypeStruct((s, d), BF), jax.ShapeDtypeStruct((s, d), BF),
                   jax.ShapeDtypeStruct((1, d), F32)],
        compiler_params=_params(("parallel", "arbitrary")),
    )(dy, z, w, scale)


def _grp_dw(p, dz, ng, name):
    s, d = p.shape
    gd = d // ng
    tk = _pick(s, 1024, 8)

    def body(p_ref, dz_ref, o_ref):
        k = pl.program_id(1)

        @pl.when(k == 0)
        def _():
            o_ref[...] = jnp.zeros_like(o_ref)

        o_ref[...] += lax.dot_general(p_ref[...], dz_ref[...], (((0,), (0,)), ((), ())), preferred_element_type=F32)

    blk = pl.BlockSpec((tk, gd), lambda g, k: (k, g))
    return pl.pallas_call(
        body, name=name, grid=(ng, s // tk),
        in_specs=[blk, blk], out_specs=pl.BlockSpec((None, gd, gd), lambda g, k: (g, 0, 0)),
        out_shape=jax.ShapeDtypeStruct((ng, gd, gd), F32),
        compiler_params=_params(("parallel", "arbitrary")),
    )(p, dz)


def _sigmoid(a):
    return 0.5 * jnp.tanh(0.5 * a) + 0.5


def _ffn_act_down(up, cw, cb, w_down, gate, resid, name):
    _, nq, s, fq = up.shape
    d = w_down.shape[1]
    tm = _pick(s, 512, CONV_HALO)
    per = tm // CONV_HALO
    h = CONV_HALO
    rows = _pick(tm, FFN_ROWS, h)
    group = _pick(tm, FFN_GROUP, rows)

    def body(prev_ref, a_ref, v_ref, w_ref, b_ref, wd_ref, g_ref, r_ref, act_ref, ffo_ref, x_ref, ext, acc):
        i, q = pl.program_id(0), pl.program_id(1)
        ext[0:h, :] = jnp.where(i > 0, prev_ref[...].astype(F32), 0.0)
        ext[h:, :] = a_ref[...].astype(F32)

        @pl.when(q == 0)
        def _():
            acc[...] = jnp.zeros_like(acc)

        for g0 in range(0, tm, group):
            for r0 in range(g0, g0 + group, rows):
                e = ext[r0:r0 + rows + h, :]
                a2 = (b_ref[...] + e[h - 2:h - 2 + rows] * w_ref[0:1, :] + e[h - 1:h - 1 + rows] * w_ref[1:2, :]
                      + e[h:h + rows] * w_ref[2:3, :])
                vv = v_ref[r0:r0 + rows, :].astype(F32)
                act_ref[r0:r0 + rows, :] = (a2 * _sigmoid(a2) * vv).astype(BF)
            acc[g0:g0 + group, :] += jnp.dot(act_ref[g0:g0 + group, :], wd_ref[...], preferred_element_type=F32)

        @pl.when(q == nq - 1)
        def _():
            r = acc[...]
            ffo_ref[...] = r.astype(BF)
            x_ref[...] = r_ref[...] + g_ref[...] * r

    row = pl.BlockSpec((tm, d), lambda i, q: (i, 0))
    return pl.pallas_call(
        body, name=name, grid=(s // tm, nq),
        in_specs=[pl.BlockSpec((None, None, h, fq), lambda i, q: (0, q, jnp.maximum(i * per - 1, 0), 0)),
                  pl.BlockSpec((None, None, tm, fq), lambda i, q: (0, q, i, 0)),
                  pl.BlockSpec((None, None, tm, fq), lambda i, q: (1, q, i, 0)),
                  pl.BlockSpec((None, CONV_WIDTH, fq), lambda i, q: (q, 0, 0)),
                  pl.BlockSpec((None, 1, fq), lambda i, q: (q, 0, 0)),
                  pl.BlockSpec((fq, d), lambda i, q: (q, 0)),
                  pl.BlockSpec((1, d), lambda i, q: (0, 0)), row],
        out_specs=[pl.BlockSpec((None, tm, fq), lambda i, q: (q, i, 0)), row, row],
        out_shape=[jax.ShapeDtypeStruct((nq, s, fq), BF), jax.ShapeDtypeStruct((s, d), BF),
                   jax.ShapeDtypeStruct((s, d), F32)],
        scratch_shapes=[pltpu.VMEM((tm + h, fq), F32), pltpu.VMEM((tm, d), F32)],
        compiler_params=_params(("parallel", "arbitrary")),
    )(up, up, up, cw, cb, w_down, gate, resid)


def _ffn_act_bwd_up(up, dact, cw, cb, w_up, name):
    _, nq, s, fq = up.shape
    d = w_up.shape[1]
    tm = _pick(s, 512, CONV_HALO)
    per = tm // CONV_HALO
    nt = s // tm
    last_halo = s // CONV_HALO - 1
    h = CONV_HALO
    te = tm + h
    rows = _pick(tm, FFN_ROWS, h)
    group = _pick(tm, FFN_GROUP, rows)
    lanes = (((1,), (1,)), ((), ()))

    def body(ap_ref, a_ref, an_ref, v_ref, vn_ref, d_ref, dn_ref, w_ref, b_ref, wa_ref, wv_ref,
             dup_ref, dc_ref, dx_ref, ext_a, dap, sums, acc):
        i, q = pl.program_id(0), pl.program_id(1)
        ext_a[0:h, :] = jnp.where(i > 0, ap_ref[...].astype(F32), 0.0)
        ext_a[h:h + tm, :] = a_ref[...].astype(F32)
        ext_a[h + tm:, :] = an_ref[...].astype(F32)

        def pre_act(e, n):
            return (b_ref[...] + e[h - 2:h - 2 + n] * w_ref[0:1, :] + e[h - 1:h - 1 + n] * w_ref[1:2, :]
                    + e[h:h + n] * w_ref[2:3, :])

        def through_gate(a2, dd, vv):
            sig = _sigmoid(a2)
            return dd * vv * (sig * (1.0 + a2 * (1.0 - sig))), dd * (a2 * sig)

        def phase1(g0):
            for r0 in range(g0, g0 + group, rows):
                a2 = pre_act(ext_a[r0:r0 + rows + h, :], rows)
                g, dgate = through_gate(a2, d_ref[r0:r0 + rows, :].astype(F32), v_ref[r0:r0 + rows, :].astype(F32))
                dap[r0:r0 + rows, :] = g
                dup_ref[1, r0:r0 + rows, :] = dgate.astype(BF)

        def fold(t):
            part = t[0:8]
            for k in range(8, rows, 8):
                part = part + t[k:k + 8]
            return part

        def phase2(g0):
            for r0 in range(g0, g0 + group, rows):
                gch = dap[r0:r0 + rows + h, :]
                g0_ = gch[0:rows]
                dup_ref[0, r0:r0 + rows, :] = (gch[2:2 + rows] * w_ref[0:1, :] + gch[1:1 + rows] * w_ref[1:2, :]
                                               + g0_ * w_ref[2:3, :]).astype(BF)
                e = ext_a[r0:r0 + rows + h, :]
                sums[0] += fold(g0_ * e[h - 2:h - 2 + rows])
                sums[1] += fold(g0_ * e[h - 1:h - 1 + rows])
                sums[2] += fold(g0_ * e[h:h + rows])
                sums[3] += fold(g0_)

        @pl.when(q == 0)
        def _():
            acc[...] = jnp.zeros_like(acc)

        sums[...] = jnp.zeros_like(sums)
        phase1(0)
        for g0 in range(0, tm, group):
            if g0 + group < tm:
                phase1(g0 + group)
            else:
                d_nxt = jnp.where(i < nt - 1, dn_ref[...].astype(F32), 0.0)
                g, _ = through_gate(pre_act(ext_a[tm:tm + 2 * h, :], h), d_nxt, vn_ref[...].astype(F32))
                dap[tm:, :] = g
            phase2(g0)
            acc[g0:g0 + group, :] += (
                lax.dot_general(dup_ref[0, g0:g0 + group, :], wa_ref[...], lanes, preferred_element_type=F32)
                + lax.dot_general(dup_ref[1, g0:g0 + group, :], wv_ref[...], lanes, preferred_element_type=F32))

        @pl.when(i == 0)
        def _():
            dc_ref[q] = jnp.zeros((8, fq), F32)

        for k in range(4):
            dc_ref[q, k:k + 1, :] += jnp.sum(sums[k], axis=0, keepdims=True)

        @pl.when(q == nq - 1)
        def _():
            dx_ref[...] = acc[...]

    def cur(half):
        return pl.BlockSpec((None, None, tm, fq), lambda i, q: (half, q, i, 0))

    def nxt(half):
        return pl.BlockSpec((None, None, h, fq), lambda i, q: (half, q, jnp.minimum((i + 1) * per, last_halo), 0))

    return pl.pallas_call(
        body, name=name, grid=(nt, nq),
        in_specs=[pl.BlockSpec((None, None, h, fq), lambda i, q: (0, q, jnp.maximum(i * per - 1, 0), 0)),
                  cur(0), nxt(0), cur(1), nxt(1),
                  pl.BlockSpec((None, tm, fq), lambda i, q: (q, i, 0)),
                  pl.BlockSpec((None, h, fq), lambda i, q: (q, jnp.minimum((i + 1) * per, last_halo), 0)),
                  pl.BlockSpec((None, CONV_WIDTH, fq), lambda i, q: (q, 0, 0)),
                  pl.BlockSpec((None, 1, fq), lambda i, q: (q, 0, 0)),
                  pl.BlockSpec((None, d, fq), lambda i, q: (q, 0, 0)),
                  pl.BlockSpec((None, d, fq), lambda i, q: (q + nq, 0, 0))],
        out_specs=[pl.BlockSpec((2, None, tm, fq), lambda i, q: (0, q, i, 0)),
                   pl.BlockSpec((nq, 8, fq), lambda i, q: (0, 0, 0)),
                   pl.BlockSpec((tm, d), lambda i, q: (i, 0))],
        out_shape=[jax.ShapeDtypeStruct((2, nq, s, fq), BF), jax.ShapeDtypeStruct((nq, 8, fq), F32),
                   jax.ShapeDtypeStruct((s, d), F32)],
        scratch_shapes=[pltpu.VMEM((tm + 2 * h, fq), F32), pltpu.VMEM((te, fq), F32), pltpu.VMEM((4, 8, fq), F32),
                        pltpu.VMEM((tm, d), F32)],
        compiler_params=_params(("arbitrary", "arbitrary")),
    )(up, up, up, up, up, dact, dact, cw, cb, w_up, w_up)


def _band(blk, n_steps, dil, first):
    qi = lax.broadcasted_iota(jnp.int32, (blk, 2 * blk), 0) + blk
    ki = lax.broadcasted_iota(jnp.int32, (blk, 2 * blk), 1)
    delta = qi - ki
    valid = (delta >= 0) & (delta <= n_steps) & ((ki >= blk) | jnp.logical_not(first))
    return valid, (delta * dil).astype(F32)


def _branch_views(q_all, kv, g, d):
    _, dil = BRANCHES[g]
    nbr = len(BRANCHES)
    sub = q_all.shape[0] // dil
    if dil == 1:
        return (q_all, kv, kv), ((nbr, g), (2 * nbr, g), (2 * nbr, nbr + g))
    cols = [q_all[:, g * d:(g + 1) * d], kv[:, g * d:(g + 1) * d], kv[:, (nbr + g) * d:(nbr + g + 1) * d]]
    return tuple(a.reshape(sub, dil * d) for a in cols), ((1, 0), (1, 0), (1, 0))


def _attn_fwd(q_all, kv, g, slopes, d, name):
    window, dil = BRANCHES[g]
    n_steps = window // dil
    blk = max(ATTN_BLOCK, n_steps)
    s = q_all.shape[0]
    sub = s // dil
    nb = sub // blk
    assert nb * blk == sub
    nh = d // HEAD_DIM
    (qv, kview, vview), (qcol, kcol, vcol) = _branch_views(q_all, kv, g, d)
    scale = HEAD_DIM ** -0.5

    def body(q_ref, kp_ref, kc_ref, vp_ref, vc_ref, o_ref, l_ref):
        j = pl.program_id(1)
        valid, dist = _band(blk, n_steps, dil, j == 0)
        qb = q_ref[...]
        kb = jnp.concatenate([kp_ref[...], kc_ref[...]], axis=0)
        vb = jnp.concatenate([vp_ref[...], vc_ref[...]], axis=0)
        first = lax.broadcasted_iota(jnp.int32, (1, 2 * HEAD_DIM), 1) < HEAD_DIM
        head_of_lane = lax.broadcasted_iota(jnp.int32, (1, LANES), 1) // (LANES // nh)
        lse_all = jnp.zeros((blk, LANES), F32)
        for hp in range(nh // 2):
            sl = slice(2 * hp * HEAD_DIM, 2 * (hp + 1) * HEAD_DIM)
            qp, kp, vp = qb[:, sl], kb[:, sl], vb[:, sl]
            out = None
            for half in range(2):
                sel = first if half == 0 else jnp.logical_not(first)
                sc = lax.dot_general(jnp.where(sel, qp, 0), kp, (((1,), (1,)), ((), ())),
                                     preferred_element_type=F32) * scale
                sc = jnp.where(valid, sc - float(slopes[2 * hp + half]) * dist, NEG)
                m = jnp.max(sc, axis=-1, keepdims=True)
                p = jnp.exp(sc - m)
                den = jnp.sum(p, axis=-1, keepdims=True)
                o = jnp.dot(p.astype(BF), jnp.where(sel, vp, 0), preferred_element_type=F32) / den
                out = o if half == 0 else out + o
                lse_all = jnp.where(head_of_lane == 2 * hp + half, m + jnp.log(den), lse_all)
            o_ref[:, sl] = out.astype(BF)
        l_ref[...] = lse_all

    def spec(col, prev):
        if prev:
            return pl.BlockSpec((blk, d), lambda r, j: (jnp.maximum(j - 1, 0), r * col[0] + col[1]))
        return pl.BlockSpec((blk, d), lambda r, j: (j, r * col[0] + col[1]))

    ospec = pl.BlockSpec((blk, d), lambda r, j: (j, r))
    o, lse = pl.pallas_call(
        body, name=name, grid=(dil, nb),
        in_specs=[spec(qcol, False), spec(kcol, True), spec(kcol, False), spec(vcol, True), spec(vcol, False)],
        out_specs=[ospec, pl.BlockSpec((blk, LANES), lambda r, j: (j, r))],
        out_shape=[jax.ShapeDtypeStruct((sub, dil * d), BF), jax.ShapeDtypeStruct((sub, dil * LANES), F32)],
        compiler_params=_params(("parallel", "parallel")),
    )(qv, kview, kview, vview, vview)
    return o.reshape(s, d), lse.reshape(s, LANES)


def _attn_combine(os, lses, name):
    s, d = os[0].shape
    tm = _pick(s, 512, 8)
    nbr = len(os)
    nh = d // HEAD_DIM
    per_head = LANES // nh

    def body(*refs):
        o_refs, l_refs = refs[:nbr], refs[nbr:2 * nbr]
        o_ref, lt_ref = refs[2 * nbr], refs[2 * nbr + 1]
        ls = [r[...] for r in l_refs]
        m = ls[0]
        for v in ls[1:]:
            m = jnp.maximum(m, v)
        tot = jnp.exp(ls[0] - m)
        for v in ls[1:]:
            tot = tot + jnp.exp(v - m)
        lt = m + jnp.log(tot)
        lt_ref[...] = lt
        ws = [jnp.exp(v - lt) for v in ls]
        first = lax.broadcasted_iota(jnp.int32, (1, 2 * HEAD_DIM), 1) < HEAD_DIM
        for hp in range(nh // 2):
            sl = slice(2 * hp * HEAD_DIM, 2 * (hp + 1) * HEAD_DIM)
            la, lb = 2 * hp * per_head, (2 * hp + 1) * per_head
            acc = None
            for w, r in zip(ws, o_refs):
                term = jnp.where(first, w[:, la:la + 1], w[:, lb:lb + 1]) * r[:, sl].astype(F32)
                acc = term if acc is None else acc + term
            o_ref[:, sl] = acc.astype(BF)

    lspec = _row_spec(tm, LANES)
    return pl.pallas_call(
        body, name=name, grid=(s // tm,),
        in_specs=[_row_spec(tm, d)] * nbr + [lspec] * nbr, out_specs=[_row_spec(tm, d), lspec],
        out_shape=[jax.ShapeDtypeStruct((s, d), BF), jax.ShapeDtypeStruct((s, LANES), F32)],
        compiler_params=_params(("parallel",)),
    )(*os, *lses)


def _attn_bwd(q_all, kv, do, o, lt, g, slopes, d, name, dk_in=None, dv_in=None):
    window, dil = BRANCHES[g]
    n_steps = window // dil
    blk = max(ATTN_BLOCK, n_steps)
    s = q_all.shape[0]
    sub = s // dil
    nb = sub // blk
    nh = d // HEAD_DIM
    (qv, kview, vview), (qcol, kcol, vcol) = _branch_views(q_all, kv, g, d)
    scale = HEAD_DIM ** -0.5
    acc_in = dk_in is not None

    def body(*refs):
        q_ref, do_ref, o_ref, lt_ref, kp_ref, kc_ref, vp_ref, vc_ref = refs[:8]
        n_in = 10 if acc_in else 8
        dkin_ref, dvin_ref = (refs[8], refs[9]) if acc_in else (None, None)
        dq_ref, dk_ref, dv_ref, keep_k, keep_v, part_k, part_v = refs[n_in:n_in + 7]
        t = pl.program_id(1)

        def emit(prev_k, prev_v):
            if acc_in:
                prev_k = prev_k + dkin_ref[...].astype(F32)
                prev_v = prev_v + dvin_ref[...].astype(F32)
            dk_ref[...] = prev_k.astype(BF)
            dv_ref[...] = prev_v.astype(BF)

        @pl.when(t < nb)
        def _():
            valid, dist = _band(blk, n_steps, dil, t == 0)
            qb = q_ref[...]
            dob = do_ref[...]
            ltb = lt_ref[...]
            kb = jnp.concatenate([kp_ref[...], kc_ref[...]], axis=0)
            vb = jnp.concatenate([vp_ref[...], vc_ref[...]], axis=0)
            first = lax.broadcasted_iota(jnp.int32, (1, 2 * HEAD_DIM), 1) < HEAD_DIM
            for hp in range(nh // 2):
                sl = slice(2 * hp * HEAD_DIM, 2 * (hp + 1) * HEAD_DIM)
                qp, kp, vp, dop = qb[:, sl], kb[:, sl], vb[:, sl], dob[:, sl]
                op = o_ref[:, sl].astype(F32)
                dq, dk, dv = None, None, None
                for half in range(2):
                    sel = first if half == 0 else jnp.logical_not(first)
                    qh, doh = jnp.where(sel, qp, 0), jnp.where(sel, dop, 0)
                    sc = lax.dot_general(qh, kp, (((1,), (1,)), ((), ())), preferred_element_type=F32) * scale
                    sc = sc - float(slopes[2 * hp + half]) * dist
                    lane = (2 * hp + half) * (LANES // nh)
                    lt_h = ltb[:, lane:lane + 1]
                    p = jnp.where(valid, jnp.exp(jnp.minimum(sc - lt_h, 30.0)), 0.0)
                    dlt = jnp.sum(doh.astype(F32) * op, axis=-1, keepdims=True)
                    dp = lax.dot_general(doh, vp, (((1,), (1,)), ((), ())), preferred_element_type=F32)
                    ds = (p * (dp - dlt)).astype(BF)
                    dq_h = jnp.dot(ds, jnp.where(sel, kp, 0), preferred_element_type=F32)
                    dk_h = lax.dot_general(ds, qh, (((0,), (0,)), ((), ())), preferred_element_type=F32)
                    dv_h = lax.dot_general(p.astype(BF), doh, (((0,), (0,)), ((), ())), preferred_element_type=F32)
                    dq = dq_h if half == 0 else dq + dq_h
                    dk = dk_h if half == 0 else dk + dk_h
                    dv = dv_h if half == 0 else dv + dv_h
                dq_ref[:, sl] = (dq * scale).astype(BF)
                part_k[:, sl] = dk * scale
                part_v[:, sl] = dv

            @pl.when(t > 0)
            def _():
                emit(keep_k[...] + part_k[0:blk, :], keep_v[...] + part_v[0:blk, :])

            keep_k[...] = part_k[blk:, :]
            keep_v[...] = part_v[blk:, :]

        @pl.when(t == nb)
        def _():
            emit(keep_k[...], keep_v[...])

    def qspec(col):
        return pl.BlockSpec((blk, d), lambda r, t: (jnp.minimum(t, nb - 1), r * col[0] + col[1]))

    def kspec(col, prev):
        if prev:
            return pl.BlockSpec((blk, d), lambda r, t: (jnp.maximum(jnp.minimum(t, nb - 1) - 1, 0), r * col[0] + col[1]))
        return qspec(col)

    kout = pl.BlockSpec((blk, d), lambda r, t: (jnp.maximum(t - 1, 0), r))
    one = (1, 0)
    ltspec = pl.BlockSpec((blk, LANES), lambda r, t: (jnp.minimum(t, nb - 1), r))
    in_specs = [qspec(qcol), qspec(one), qspec(one), ltspec,
                kspec(kcol, True), kspec(kcol, False), kspec(vcol, True), kspec(vcol, False)]
    args = [qv, do.reshape(sub, dil * d), o.reshape(sub, dil * d), lt.reshape(sub, dil * LANES),
            kview, kview, vview, vview]
    if acc_in:
        in_specs += [kout, kout]
        args += [dk_in.reshape(sub, dil * d), dv_in.reshape(sub, dil * d)]
    shp = jax.ShapeDtypeStruct((sub, dil * d), BF)
    dq, dk, dv = pl.pallas_call(
        body, name=name, grid=(dil, nb + 1),
        in_specs=in_specs, out_specs=[qspec(one), kout, kout], out_shape=[shp, shp, shp],
        scratch_shapes=[pltpu.VMEM((blk, d), F32), pltpu.VMEM((blk, d), F32),
                        pltpu.VMEM((2 * blk, d), F32), pltpu.VMEM((2 * blk, d), F32)],
        compiler_params=_params(("parallel", "arbitrary")),
    )(*args)
    return dq.reshape(s, d), dk.reshape(s, d), dv.reshape(s, d)


def _adamw(parts_list, w, m, v, name):
    nl, r, c = w.shape
    assert len(parts_list) == nl
    npart = parts_list[0].shape[0]
    tr = _pick(r, 256, 16)
    c1 = 1.0 / (1.0 - ADAM_B1 ** ADAM_STEP)
    c2 = 1.0 / (1.0 - ADAM_B2 ** ADAM_STEP)

    def body(*refs):
        p_refs = refs[:nl]
        w_ref, m_ref, v_ref, g_ref, d_ref, nm_ref, nv_ref = refs[nl:]
        layer = pl.program_id(0)
        for idx in range(nl):
            @pl.when(layer == idx)
            def _(p_ref=p_refs[idx]):
                g = p_ref[0].astype(F32)
                for k in range(1, npart):
                    g = g + p_ref[k].astype(F32)
                nm = ADAM_B1 * m_ref[...] + (1.0 - ADAM_B1) * g
                nv = ADAM_B2 * v_ref[...] + (1.0 - ADAM_B2) * (g * g)
                g_ref[...] = g
                nm_ref[...] = nm
                nv_ref[...] = nv
                d_ref[...] = -ADAM_LR * ((nm * c1) / (jnp.sqrt(nv * c2) + ADAM_EPS) + ADAM_WD * w_ref[...])

    def part_spec(idx):
        return pl.BlockSpec((npart, tr, c), lambda l, i: (0, jnp.where(l == idx, i, 0), 0))

    blk = pl.BlockSpec((None, tr, c), lambda l, i: (l, i, 0))
    shp = jax.ShapeDtypeStruct((nl, r, c), F32)
    return pl.pallas_call(
        body, name=name, grid=(nl, r // tr),
        in_specs=[part_spec(idx) for idx in range(nl)] + [blk, blk, blk],
        out_specs=[blk, blk, blk, blk], out_shape=[shp, shp, shp, shp],
        compiler_params=_params(("parallel", "parallel")),
    )(*parts_list, w, m, v)


def _full_from_slots(slots, shard_shape, axis):
    a = slots.reshape((N_DEV,) + tuple(shard_shape))
    a = jnp.moveaxis(a, 0, axis)
    full = list(shard_shape)
    full[axis] *= N_DEV
    return a.reshape(full)


def kernel(x, c, ada_w, ada_b, norm1_g, norm2_g, pool_w_in, pool_w_grp, pool_scale, pool_w_out, kv_norm_g, kv_ada_w, kv_ada_b, w_kv, attn_w_q, attn_w_o, ffn_w_up, ffn_conv_w, ffn_conv_b, ffn_w_down, final_g, loss_target, m_ada_w, m_ada_b, m_norm1_g, m_norm2_g, m_pool_w_in, m_pool_w_grp, m_pool_scale, m_pool_w_out, m_kv_norm_g, m_kv_ada_w, m_kv_ada_b, m_w_kv, m_attn_w_q, m_attn_w_o, m_ffn_w_up, m_ffn_conv_w, m_ffn_conv_b, m_ffn_w_down, m_final_g, v_ada_w, v_ada_b, v_norm1_g, v_norm2_g, v_pool_w_in, v_pool_w_grp, v_pool_scale, v_pool_w_out, v_kv_norm_g, v_kv_ada_w, v_kv_ada_b, v_w_kv, v_attn_w_q, v_attn_w_o, v_ffn_w_up, v_ffn_conv_w, v_ffn_conv_b, v_ffn_w_down, v_final_g):
    weights = dict(ada_w=ada_w, ada_b=ada_b, norm1_g=norm1_g, norm2_g=norm2_g, pool_w_in=pool_w_in,
                   pool_w_grp=pool_w_grp, pool_scale=pool_scale, pool_w_out=pool_w_out, kv_norm_g=kv_norm_g,
                   kv_ada_w=kv_ada_w, kv_ada_b=kv_ada_b, w_kv=w_kv, attn_w_q=attn_w_q, attn_w_o=attn_w_o,
                   ffn_w_up=ffn_w_up, ffn_conv_w=ffn_conv_w, ffn_conv_b=ffn_conv_b, ffn_w_down=ffn_w_down,
                   final_g=final_g)
    mom1 = dict(ada_w=m_ada_w, ada_b=m_ada_b, norm1_g=m_norm1_g, norm2_g=m_norm2_g, pool_w_in=m_pool_w_in,
                pool_w_grp=m_pool_w_grp, pool_scale=m_pool_scale, pool_w_out=m_pool_w_out, kv_norm_g=m_kv_norm_g,
                kv_ada_w=m_kv_ada_w, kv_ada_b=m_kv_ada_b, w_kv=m_w_kv, attn_w_q=m_attn_w_q, attn_w_o=m_attn_w_o,
                ffn_w_up=m_ffn_w_up, ffn_conv_w=m_ffn_conv_w, ffn_conv_b=m_ffn_conv_b, ffn_w_down=m_ffn_w_down,
                final_g=m_final_g)
    mom2 = dict(ada_w=v_ada_w, ada_b=v_ada_b, norm1_g=v_norm1_g, norm2_g=v_norm2_g, pool_w_in=v_pool_w_in,
                pool_w_grp=v_pool_w_grp, pool_scale=v_pool_scale, pool_w_out=v_pool_w_out, kv_norm_g=v_kv_norm_g,
                kv_ada_w=v_kv_ada_w, kv_ada_b=v_kv_ada_b, w_kv=v_w_kv, attn_w_q=v_attn_w_q, attn_w_o=v_attn_w_o,
                ffn_w_up=v_ffn_w_up, ffn_conv_w=v_ffn_conv_w, ffn_conv_b=v_ffn_conv_b, ffn_w_down=v_ffn_w_down,
                final_g=v_final_g)
    order = list(weights)

    seq, d = x.shape[1], x.shape[2]
    depth = ada_w.shape[0]
    n_pool = pool_w_in.shape[0]
    f = ffn_conv_b.shape[1]
    nbr = len(BRANCHES)
    nh = d // HEAD_DIM
    slopes = _alibi_slopes(nbr * nh).reshape(nbr, nh)
    me = 4 * lax.axis_index("x") + 2 * lax.axis_index("y") + lax.axis_index("c")
    xs = x[0]
    tgt = loss_target[0]

    def start_gathers(after):
        keys, groups = [], []
        for l in range(depth):
            if l < n_pool:
                parts = [("mixer", [pool_w_in[l], pool_w_grp[l].reshape(-1, pool_w_grp.shape[-1])]),
                         ("out", [pool_w_out[l]])]
            else:
                j = l - n_pool
                parts = [("mixer", [attn_w_q[j]] + ([w_kv] if j == 0 else [])), ("out", [attn_w_o[j]])]
            for part, srcs in parts + [("ffn", [ffn_w_up[l]]), ("down", [ffn_w_down[l]])]:
                keys.append((l, part))
                groups.append([a.astype(BF) for a in srcs])
        handles, _ = _comm_start(groups, "gather_start", scatter=False, after=after)
        return dict(zip(keys, handles))

    cond = c * (1.0 / (1.0 + jnp.exp(-c)))
    small_in = jnp.concatenate([cond.reshape(-1), ffn_conv_w.reshape(-1), pool_scale.reshape(-1)])
    n_small_in = small_in.shape[0]
    gath = _all_gather(_to_rows(small_in), "gather_small").reshape(N_DEV, -1)[:, :n_small_in]
    cond_all = gath[:, :d]
    o1 = d + ffn_conv_w.size
    conv_w_full = _full_from_slots(gath[:, d:o1], ffn_conv_w.shape, 2)
    pool_scale_full = _full_from_slots(gath[:, o1:], pool_scale.shape, 1)
    cond16 = jnp.concatenate([cond_all, jnp.zeros_like(cond_all)], axis=0)

    mod_part = _ada_fwd(cond16, ada_w, "ada_fwd")[:, :N_DEV]
    kv_part = _ada_fwd(cond16, kv_ada_w[None], "kv_ada_fwd")[0, :N_DEV]
    n_mod = depth * mod_part.shape[2] + kv_part.shape[1]
    send = jnp.concatenate([jnp.moveaxis(mod_part, 1, 0).reshape(N_DEV, -1), kv_part], axis=1)
    send_rows = jax.vmap(_to_rows)(send)
    got = _all_to_all(send_rows, "exchange_mod").reshape(N_DEV, -1)[:, :n_mod]
    ncol = mod_part.shape[2]
    mods = []
    for l in range(depth):
        row = got[:, l * ncol:(l + 1) * ncol].reshape(1, -1) + ada_b[l][None]
        mods.append([row[:, k * d:(k + 1) * d] for k in range(6)])
    kv_row = got[:, depth * ncol:].reshape(1, -1) + kv_ada_b[None]
    kv_shift, kv_scale = kv_row[:, :d], kv_row[:, d:]
    gather_handles = start_gathers(got)

    def vec(a):
        return a.reshape(1, -1)

    tall = _pick(seq, MM_ROWS)
    nq = 4
    fq = f // nq
    ng = len(POOL_WINDOWS)
    cw_slots = jnp.moveaxis(conv_w_full.reshape(depth, CONV_WIDTH, nq, fq), 2, 1)
    cb_slots = ffn_conv_b.reshape(depth, nq, 1, fq)

    saved = []
    xcur = xs
    kvs = None
    hkv = None
    x_kv = None
    w_kv_slots = None
    for l in range(depth):
        sh1, sc1, g1, sh2, sc2, g2 = mods[l]
        st = dict(x=xcur)
        h = _norm_mod(xcur, vec(norm1_g[l]), sc1, sh1, f"norm1_{l}")
        st["h"] = h
        gw = _comm_wait(gather_handles[l, "mixer"], h, f"gather_wait_mixer_{l}")
        if l < n_pool:
            w_in = gw[0].reshape(d, d)
            w_grp = jnp.moveaxis(gw[1].reshape(N_DEV, ng, -1, d // ng), 0, 1).reshape(ng, d // ng, d // ng)
            u = _mm(h, w_in, f"pool_in_{l}")
            pooled = _pool_fwd(u, f"pool_fwd_{l}")
            z, y = _grp_fwd(pooled, w_grp, vec(pool_scale_full[l]), f"grp_fwd_{l}")
            w_out = _comm_wait(gather_handles[l, "out"], y, f"gather_wait_out_{l}")[0].reshape(d, d)
            mix, x1 = _mm(y, w_out, f"pool_out_{l}", gate=g1, resid=xcur, pre_dtype=BF)
            st.update(pooled=pooled, z=z, y=y, w_in=w_in, w_grp=w_grp, w_out=w_out)
        else:
            j = l - n_pool
            w_q_slots = gw[0]
            if j == 0:
                w_kv_slots = gw[1]
                x_kv = xcur
                hkv = _norm_mod(xcur, vec(kv_norm_g), kv_scale, kv_shift, "norm_kv")
                kvs = _mm(hkv, w_kv_slots, "kv_proj", bf="qkn", out_dtype=BF, tm=tall)
            q = _mm(h, w_q_slots, f"q_proj_{l}", bf="qkn", out_dtype=BF, tm=tall)
            outs, lses = [], []
            for g in range(nbr):
                og, lg = _attn_fwd(q, kvs, g, slopes[g], d, f"attn_fwd_{l}_{g}")
                outs.append(og)
                lses.append(lg)
            o, lt = _attn_combine(outs, lses, f"attn_mix_{l}")
            w_o = _comm_wait(gather_handles[l, "out"], o, f"gather_wait_out_{l}")[0].reshape(d, d)
            mix, x1 = _mm(o, w_o, f"attn_out_{l}", gate=g1, resid=xcur, pre_dtype=BF)
            st.update(q=q, o=o, lt=lt, w_q=w_q_slots, w_o=w_o)
        h2 = _norm_mod(x1, vec(norm2_g[l]), sc2, sh2, f"norm2_{l}")
        w_up_slots = _comm_wait(gather_handles[l, "ffn"], h2, f"gather_wait_ffn_{l}")[0]
        up = _mm(h2, w_up_slots, f"ffn_up_{l}", bf="qkn", of="qmn", out_dtype=BF, tm=tall).reshape(2, nq, seq, fq)
        w_down = _comm_wait(gather_handles[l, "down"], up, f"gather_wait_down_{l}")[0].reshape(f, d)
        st.update(w_up=w_up_slots, w_down=w_down)
        act, ffo, x2 = _ffn_act_down(up, cw_slots[l], cb_slots[l], w_down, g2, x1, f"ffn_act_down_{l}")
        st.update(mix=mix, x1=x1, h2=h2, up=up, act=act, ffo=ffo)
        saved.append(st)
        xcur = x2

    dx, loss_blk, d_final_g = _final_loss(xcur, tgt, vec(final_g), "final_loss")
    loss = lax.psum(loss_blk[0, 0], ("x", "y", "c"))

    d_mod = [None] * depth
    d_n1 = [None] * depth
    d_n2 = [None] * depth
    d_conv = [None] * depth
    d_pscale = [None] * n_pool
    dk_acc = [None] * nbr
    dv_acc = [None] * nbr
    ffn_handles = [None] * depth
    mixer_handles = [None] * depth
    tok = 0.0
    for l in reversed(range(depth)):
        sh1, sc1, g1, sh2, sc2, g2 = mods[l]
        st = saved[l]
        dffo, dg2 = _gate_bwd(dx, st["ffo"], g2 + tok, f"gate2_bwd_{l}")
        g_down = _mm(st["act"], dffo, f"ffn_down_dw_{l}", af="qkm", out_dtype=BF)
        dact = _mm(dffo, st["w_down"], f"ffn_down_dx_{l}", bf="nk", of="qmn", tn=fq, out_dtype=BF, tm=tall)
        dup, dc, dh2 = _ffn_act_bwd_up(st["up"], dact, cw_slots[l], cb_slots[l], st["w_up"], f"ffn_act_bwd_up_{l}")
        dup = dup.reshape(N_DEV, seq, -1)
        d_conv[l] = dc
        g_up = _mm(st["h2"], dup, f"ffn_up_dw_{l}", af="km", bf="qkn", of="qmn", out_dtype=BF)
        (ffn_handles[l],), tok = _comm_start([[g_up, g_down.reshape(N_DEV, -1, d)]], f"exchange_start_ffn_{l}",
                                             scatter=True)
        dx1, dsh2, dw2 = _norm_mod_bwd(st["x1"], dh2, vec(norm2_g[l]), sc2 + tok, dx, f"norm2_bwd_{l}")
        d_n2[l] = dw2 * (1.0 + sc2)
        dsc2 = dw2 * vec(norm2_g[l])

        dmix, dg1 = _gate_bwd(dx1, st["mix"], g1, f"gate1_bwd_{l}")
        if l < n_pool:
            g_out = _mm(st["y"], dmix, f"pool_out_dw_{l}", af="km", out_dtype=BF)
            dy = _mm(dmix, st["w_out"], f"pool_out_dx_{l}", bf="nk", out_dtype=BF)
            dz, dpool, dps = _grp_bwd(dy, st["z"], st["w_grp"], vec(pool_scale_full[l]), f"grp_bwd_{l}")
            d_pscale[l] = dps
            g_grp = _grp_dw(st["pooled"], dz, ng, f"grp_dw_{l}")
            du = _pool_bwd(dpool, f"pool_bwd_{l}")
            g_in = _mm(st["h"], du, f"pool_in_dw_{l}", af="km", out_dtype=BF)
            dh = _mm(du, st["w_in"], f"pool_in_dx_{l}", bf="nk")
            g_grp_slots = jnp.moveaxis(g_grp.reshape(ng, N_DEV, -1, d // ng), 1, 0).reshape(N_DEV, -1, d // ng)
            send = [g_in.reshape(N_DEV, -1, d), g_grp_slots.astype(BF), g_out.reshape(N_DEV, -1, d)]
        else:
            j = l - n_pool
            g_o = _mm(st["o"], dmix, f"attn_out_dw_{l}", af="km", out_dtype=BF)
            do = _mm(dmix, st["w_o"], f"attn_out_dx_{l}", bf="nk", out_dtype=BF)
            dqs = []
            for g in range(nbr):
                dq_g, dk_g, dv_g = _attn_bwd(st["q"], kvs, do, st["o"], st["lt"], g, slopes[g], d,
                                             f"attn_bwd_{l}_{g}", dk_in=dk_acc[g], dv_in=dv_acc[g])
                dqs.append(dq_g)
                dk_acc[g], dv_acc[g] = dk_g, dv_g
            dq = jnp.concatenate(dqs, axis=1)
            nqc = st["w_q"].shape[2]
            g_q = _mm(st["h"], dq, f"q_proj_dw_{l}", af="km", of="qmn", tn=nqc, out_dtype=BF)
            dh = _mm(dq, st["w_q"], f"q_proj_dx_{l}", bf="qnk")
            send = [g_q, g_o.reshape(N_DEV, -1, d)]
        dx0, dsh1, dw1 = _norm_mod_bwd(st["x"], dh, vec(norm1_g[l]), sc1, dx1, f"norm1_bwd_{l}")
        d_n1[l] = dw1 * (1.0 + sc1)
        dsc1 = dw1 * vec(norm1_g[l])
        d_mod[l] = jnp.concatenate([dsh1, dsc1, dg1, dsh2, dsc2, dg2], axis=1)
        dx = dx0
        if l == n_pool:
            dkv = jnp.concatenate(dk_acc + dv_acc, axis=1)
            nkc = w_kv_slots.shape[2]
            g_kv = _mm(hkv, dkv, "kv_proj_dw", af="km", of="qmn", tn=nkc, out_dtype=BF)
            dhkv = _mm(dkv, w_kv_slots, "kv_proj_dx", bf="qnk")
            dx, dsh_kv, dw_kv = _norm_mod_bwd(x_kv, dhkv, vec(kv_norm_g), kv_scale, dx, "norm_kv_bwd")
            d_kv_norm = dw_kv * (1.0 + kv_scale)
            d_kv_mod = jnp.concatenate([dsh_kv, dw_kv * vec(kv_norm_g)], axis=1)
            send.append(g_kv)
        (mixer_handles[l],), tok = _comm_start([send], f"exchange_start_mixer_{l}", scatter=True)
    grad_x = dx[None]

    small = [jnp.concatenate(d_mod, axis=1).reshape(-1), d_kv_mod.reshape(-1),
             jnp.concatenate(d_n1, axis=0).reshape(-1), jnp.concatenate(d_n2, axis=0).reshape(-1),
             d_kv_norm.reshape(-1), d_final_g.reshape(-1),
             jnp.stack([dcl[:, 3, :] for dcl in d_conv]).reshape(-1),
             jnp.stack([jnp.moveaxis(dcl[:, 0:CONV_WIDTH, :], 0, 1) for dcl in d_conv]).reshape(-1),
             jnp.concatenate(d_pscale, axis=0).reshape(-1)]
    sizes = [a.shape[0] for a in small]
    small_rows = _to_rows(jnp.concatenate(small))
    small_all = _all_gather(small_rows, "gather_small_grads")
    dmod_all = small_all.reshape(N_DEV, -1)[:, :sizes[0] + sizes[1]]

    dmod16 = jnp.concatenate([dmod_all, jnp.zeros_like(dmod_all)], axis=0)
    dm = dmod16[:, :sizes[0]].reshape(16, depth, N_DEV, ncol)
    dm_mine = lax.dynamic_index_in_dim(dm, me, axis=2, keepdims=False)
    g_ada_w = _ada_bwd(cond16, jnp.moveaxis(dm_mine, 0, 1), "ada_bwd")
    nkv = kv_part.shape[1]
    dkm = dmod16[:, sizes[0]:].reshape(16, N_DEV, nkv)
    dkm_mine = lax.dynamic_index_in_dim(dkm, me, axis=1, keepdims=False)
    g_kv_ada_w = _ada_bwd(cond16, dkm_mine[None], "kv_ada_bwd")

    res = {}

    def update(n, parts_list, shape3):
        w3, m3, v3 = (a[n].reshape(shape3) for a in (weights, mom1, mom2))
        outs = _adamw(parts_list, w3, m3, v3, f"adamw_{n}")
        res[n] = [a.reshape(weights[n].shape) for a in outs]

    update("ada_w", [g_ada_w[l][None] for l in range(depth)], ada_w.shape)
    update("kv_ada_w", [g_kv_ada_w], (1,) + kv_ada_w.shape)
    after = res["ada_w"][0]
    parts_ffn = [_comm_wait(ffn_handles[l], after, f"exchange_wait_ffn_{l}") for l in reversed(range(depth))][::-1]
    parts = [_comm_wait(mixer_handles[l], after, f"exchange_wait_mixer_{l}") for l in reversed(range(depth))][::-1]
    pool_layers, attn_layers = range(n_pool), range(n_pool, depth)
    update("ffn_w_up", [parts_ffn[l][0] for l in range(depth)], ffn_w_up.shape)
    update("ffn_w_down", [parts_ffn[l][1] for l in range(depth)], ffn_w_down.shape)
    update("attn_w_q", [parts[l][0] for l in attn_layers], attn_w_q.shape)
    update("attn_w_o", [parts[l][1] for l in attn_layers], (depth - n_pool, -1, d))
    update("w_kv", [parts[n_pool][2]], (1,) + w_kv.shape)
    update("pool_w_in", [parts[l][0] for l in pool_layers], (n_pool, -1, d))
    update("pool_w_grp", [parts[l][1] for l in pool_layers], (n_pool, -1, d // ng))
    update("pool_w_out", [parts[l][2] for l in pool_layers], (n_pool, -1, d))

    tot = small_all.reshape(N_DEV, -1)
    offs = np.cumsum([0] + sizes)
    seg = {k: (int(offs[i]), int(offs[i + 1])) for i, k in enumerate(
        ["mod", "kv_mod", "n1", "n2", "kv_norm", "final", "conv_b", "conv_w", "pscale"])}

    def rows_of(a, b):
        return tot[:, a:b]

    nf8 = f // N_DEV
    conv_w_parts = lax.dynamic_slice_in_dim(
        rows_of(*seg["conv_w"]).reshape(N_DEV, depth, CONV_WIDTH, N_DEV, nf8), me, 1, axis=3).reshape(N_DEV, -1)
    nd8 = d // N_DEV
    pscale_parts = lax.dynamic_slice_in_dim(
        rows_of(*seg["pscale"]).reshape(N_DEV, n_pool, N_DEV, nd8), me, 1, axis=2).reshape(N_DEV, -1)
    small_names = ["ada_b", "norm1_g", "norm2_g", "pool_scale", "kv_norm_g", "kv_ada_b", "ffn_conv_w",
                   "ffn_conv_b", "final_g"]
    small_parts = [rows_of(*seg["mod"]), rows_of(*seg["n1"]), rows_of(*seg["n2"]), pscale_parts,
                   rows_of(*seg["kv_norm"]), rows_of(*seg["kv_mod"]), conv_w_parts, rows_of(*seg["conv_b"]),
                   rows_of(*seg["final"])]
    sp = jnp.concatenate(small_parts, axis=1)
    n_sp = sp.shape[1]
    sp_rows = jax.vmap(_to_rows)(sp)

    def packed(src):
        return _to_rows(jnp.concatenate([src[n].reshape(-1) for n in small_names]))[None]

    outs = _adamw([sp_rows], packed(weights), packed(mom1), packed(mom2), "adamw_small")
    outs = [a.reshape(-1)[:n_sp] for a in outs]
    off = 0
    for n in small_names:
        size = weights[n].size
        res[n] = [a[off:off + size].reshape(weights[n].shape) for a in outs]
        off += size

    grads = [res[n][0] for n in order]
    deltas = [res[n][1] for n in order]
    new_m = [res[n][2] for n in order]
    new_v = [res[n][3] for n in order]
    return (loss, grad_x, *grads, *deltas, *new_m, *new_v)
```

```python
import math

import numpy as np
import jax
import jax.numpy as jnp
from jax import lax
from jax.experimental import pallas as pl
from jax.experimental.pallas import tpu as pltpu

F32 = jnp.float32
BF = jnp.bfloat16

POOL_WINDOWS = (2, 4, 8, 16)
BRANCHES = ((128, 1), (512, 4), (2048, 16))
HEAD_DIM = 64
ATTN_BLOCK = 128
CONV_WIDTH = 3
EPS = 1e-6
ADAM_LR = 0.001
ADAM_B1 = 0.9
ADAM_B2 = 0.999
ADAM_EPS = 1e-08
ADAM_WD = 0.01
ADAM_STEP = 10

N_DEV = 8
LANES = 128
POOL_HALO = 16
CONV_HALO = 8
FFN_ROWS = 16
FFN_GROUP = 128
VMEM_LIMIT = 48 * 1024 * 1024
MM_TILE = 1024
MM_ROWS = 2048
NEG = -1e30

MESH = pl.DeviceIdType.MESH
ANY = pl.BlockSpec(memory_space=pl.ANY)


def _params(sem=None):
    if sem is None:
        return pltpu.CompilerParams(vmem_limit_bytes=VMEM_LIMIT)
    return pltpu.CompilerParams(dimension_semantics=sem, vmem_limit_bytes=VMEM_LIMIT)


def _pick(dim, pref, mult=LANES):
    if dim <= pref:
        return dim
    t = (pref // mult) * mult
    while t >= mult:
        if dim % t == 0:
            return t
        t -= mult
    return dim


def _alibi_slopes(n):
    def pow2(m):
        start = 2.0 ** (-(2.0 ** -(math.log2(m) - 3)))
        return [start ** (i + 1) for i in range(m)]
    if math.log2(n).is_integer():
        s = pow2(n)
    else:
        c = 2 ** math.floor(math.log2(n))
        s = pow2(c) + pow2(2 * c)[0::2][: n - c]
    s = np.asarray(s, dtype=np.float32)
    return -np.sort(-s)


def _my_place():
    return lax.axis_index("x"), lax.axis_index("y"), lax.axis_index("c")


def _all_gather_many(xs, name):
    n = len(xs)

    def body(*refs):
        x_refs, out_refs = refs[:n], refs[n:2 * n]
        send_sems, recv_sems, local_sems = refs[2 * n:]
        xi, yi, ci = _my_place()
        me, sibling = (xi, yi, ci), (xi, yi, 1 - ci)
        chips = [(1 - xi, yi), (xi, 1 - yi), (1 - xi, 1 - yi)]

        def slot(a, px, py, pc):
            return out_refs[a].at[4 * px + 2 * py + pc]

        def copy(a, k, block, to, src=None):
            return pltpu.make_async_remote_copy(
                src_ref=slot(a, *block) if src is None else src, dst_ref=slot(a, *block),
                send_sem=send_sems.at[7 * a + k], recv_sem=recv_sems.at[7 * a + k],
                device_id=to, device_id_type=MESH)

        mine = [pltpu.make_async_copy(x_refs[a], slot(a, *me), local_sems.at[a]) for a in range(n)]
        for cp in mine:
            cp.start()
        sent = []
        for a in range(n):
            first = [copy(a, 0, me, sibling, src=x_refs[a])]
            first += [copy(a, 1 + j, me, (*chip, ci), src=x_refs[a]) for j, chip in enumerate(chips)]
            for cp in first:
                cp.start()
            sent += first
        for a in range(n):
            for j, chip in enumerate(chips):
                copy(a, 1 + j, (*chip, ci), me).wait_recv()
                fwd = copy(a, 4 + j, (*chip, ci), sibling)
                fwd.start()
                sent.append(fwd)
        for a in range(n):
            copy(a, 0, sibling, me).wait_recv()
            for j, chip in enumerate(chips):
                copy(a, 4 + j, (*chip, 1 - ci), me).wait_recv()
        for cp in sent:
            cp.wait_send()
        for cp in mine:
            cp.wait()

    return pl.pallas_call(
        body, name=name,
        out_shape=[jax.ShapeDtypeStruct((N_DEV,) + x.shape, x.dtype) for x in xs],
        in_specs=[ANY] * n, out_specs=[ANY] * n,
        scratch_shapes=[pltpu.SemaphoreType.DMA((7 * n,)), pltpu.SemaphoreType.DMA((7 * n,)),
                        pltpu.SemaphoreType.DMA((n,))],
    )(*xs)


def _all_gather(x, name):
    return _all_gather_many([x], name)[0]


def _all_to_all_many(xs, name):
    n = len(xs)

    def body(*refs):
        x_refs, out_refs = refs[:n], refs[n:2 * n]
        send_sems, recv_sems, local_sems = refs[2 * n:]
        xi, yi, ci = _my_place()
        me = 4 * xi + 2 * yi + ci
        mine = [pltpu.make_async_copy(x_refs[a].at[me], out_refs[a].at[me], local_sems.at[a]) for a in range(n)]
        for cp in mine:
            cp.start()
        copies = []
        for a in range(n):
            for k in range(1, N_DEV):
                px = 1 - xi if k & 4 else xi
                py = 1 - yi if k & 2 else yi
                pc = 1 - ci if k & 1 else ci
                peer = 4 * px + 2 * py + pc
                cp = pltpu.make_async_remote_copy(
                    src_ref=x_refs[a].at[peer], dst_ref=out_refs[a].at[me],
                    send_sem=send_sems.at[7 * a + k - 1], recv_sem=recv_sems.at[7 * a + k - 1],
                    device_id=(px, py, pc), device_id_type=MESH)
                cp.start()
                copies.append(cp)
        for cp in copies:
            cp.wait()
        for cp in mine:
            cp.wait()

    return pl.pallas_call(
        body, name=name,
        out_shape=[jax.ShapeDtypeStruct(x.shape, x.dtype) for x in xs],
        in_specs=[ANY] * n, out_specs=[ANY] * n,
        scratch_shapes=[pltpu.SemaphoreType.DMA((7 * n,)), pltpu.SemaphoreType.DMA((7 * n,)),
                        pltpu.SemaphoreType.DMA((n,))],
    )(*xs)


def _all_to_all(x, name):
    return _all_to_all_many([x], name)[0]


HBM = pl.BlockSpec(memory_space=pltpu.HBM)
SEM = pl.BlockSpec(memory_space=pltpu.SEMAPHORE)
EFFECT = pltpu.SideEffectType.DATAFLOW_SIDE_EFFECTING


def _peer(k, xi, yi, ci):
    px = 1 - xi if k & 4 else xi
    py = 1 - yi if k & 2 else yi
    pc = 1 - ci if k & 1 else ci
    return (px, py, pc), 4 * px + 2 * py + pc


def _split_copies(x_refs, land_refs, send_sem, recv_sem, scatter):
    xi, yi, ci = _my_place()
    me = 4 * xi + 2 * yi + ci
    copies = []
    for p, (x_ref, land_ref) in enumerate(zip(x_refs, land_refs)):
        for k in range(1, N_DEV):
            place, peer = _peer(k, xi, yi, ci)
            copies.append(pltpu.make_async_remote_copy(
                src_ref=x_ref.at[peer] if scatter else x_ref, dst_ref=land_ref.at[me],
                send_sem=send_sem.at[7 * p + k - 1], recv_sem=recv_sem.at[7 * p + k - 1],
                device_id=place, device_id_type=MESH))
    return copies


def _comm_start(groups, name, scatter, after=None):
    sizes = [len(g) for g in groups]
    xs = [x for g in groups for x in g]
    n, ng = len(xs), len(groups)
    lands = [lax.empty(x.shape if scatter else (N_DEV,) + x.shape, x.dtype) for x in xs]
    starts = np.cumsum([0] + sizes)
    n_in = 2 * n + (after is not None)

    def body(*refs):
        x_refs, land_refs = refs[:n], refs[n:2 * n]
        send_sems, recv_sems = refs[n_in:n_in + ng], refs[n_in + ng:n_in + 2 * ng]
        token = refs[n_in + 2 * ng + 2 * n]
        for gi in range(ng):
            lo, hi = int(starts[gi]), int(starts[gi + 1])
            for cp in _split_copies(x_refs[lo:hi], land_refs[lo:hi], send_sems[gi], recv_sems[gi], scatter):
                cp.start()
        token[...] = jnp.zeros_like(token)

    sem_shapes = [pltpu.SemaphoreType.DMA((7 * m,)) for m in sizes]
    thru = [pltpu.HBM(a.shape, a.dtype) for a in xs + lands]
    res = pl.pallas_call(
        body, name=name,
        out_shape=sem_shapes + sem_shapes + thru + [jax.ShapeDtypeStruct((8, LANES), F32)],
        in_specs=[HBM] * n_in,
        out_specs=[SEM] * (2 * ng) + [HBM] * (2 * n) + [pl.BlockSpec(memory_space=pltpu.VMEM)],
        input_output_aliases={i: 2 * ng + i for i in range(2 * n)},
        compiler_params=pltpu.CompilerParams(has_side_effects=EFFECT),
    )(*[pltpu.with_memory_space_constraint(a, pltpu.HBM) for a in xs + lands + ([] if after is None else [after])])
    send_sems, recv_sems = res[:ng], res[ng:2 * ng]
    x_thru, land_thru = res[2 * ng:2 * ng + n], res[2 * ng + n:2 * ng + 2 * n]
    handles = []
    for gi in range(ng):
        lo, hi = int(starts[gi]), int(starts[gi + 1])
        handles.append((send_sems[gi], recv_sems[gi], list(x_thru[lo:hi]), list(land_thru[lo:hi]), scatter))
    return handles, res[-1][0, 0]


def _comm_wait(handle, after, name):
    send_sem, recv_sem, x_thru, land_thru, scatter = handle
    m = len(x_thru)

    blocks = [a.shape[1:] if scatter else a.shape for a in x_thru]

    def body(*refs):
        x_refs, land_refs = refs[:m], refs[m:2 * m]
        local_sems, stage = refs[4 * m + 3], refs[4 * m + 4:]
        xi, yi, ci = _my_place()
        me = 4 * xi + 2 * yi + ci
        load = [pltpu.make_async_copy(x_refs[p].at[me] if scatter else x_refs[p], stage[p], local_sems.at[2 * p])
                for p in range(m)]
        store = [pltpu.make_async_copy(stage[p], land_refs[p].at[me], local_sems.at[2 * p + 1]) for p in range(m)]
        for cp in load:
            cp.start()
        for p in range(m):
            load[p].wait()
            store[p].start()
        for cp in _split_copies(x_refs, land_refs, refs[2 * m], refs[2 * m + 1], scatter):
            cp.wait_send()
            cp.wait_recv()
        for cp in store:
            cp.wait()

    res = pl.pallas_call(
        body, name=name,
        out_shape=[pltpu.HBM(a.shape, a.dtype) for a in x_thru + land_thru],
        in_specs=[HBM] * (2 * m) + [SEM, SEM, ANY], out_specs=[HBM] * (2 * m),
        input_output_aliases={i: i for i in range(2 * m)},
        scratch_shapes=[pltpu.SemaphoreType.DMA((2 * m,))] + [pltpu.VMEM(b, a.dtype) for b, a in zip(blocks, x_thru)],
        compiler_params=pltpu.CompilerParams(has_side_effects=EFFECT),
    )(*x_thru, *land_thru, send_sem, recv_sem, after)
    return list(res[m:])


def _to_rows(vec):
    n = vec.shape[0]
    unit = 8 * LANES
    pad = (-n) % unit
    if pad:
        vec = jnp.concatenate([vec, jnp.zeros((pad,), vec.dtype)])
    return vec.reshape(-1, LANES)


def _mm(a, b, name, *, af="mk", bf="kn", of="mn", out_dtype=F32, tm=None, tn=None, tk=None,
        gate=None, resid=None, pre_dtype=None):
    if af == "mk":
        m, kk = a.shape
    elif af == "km":
        kk, m = a.shape
    elif af == "qmk":
        qa, m, tk = a.shape
        kk = qa * tk
    else:
        qa, kk, tm = a.shape
        m = qa * tm
    if bf == "kn":
        k2, n = b.shape
    elif bf == "nk":
        n, k2 = b.shape
    elif bf == "qkn":
        qb, k2, tn = b.shape
        n = qb * tn
    else:
        qb, n, tkb = b.shape
        k2 = qb * tkb
        assert af != "qmk" or tkb == tk
        tk = tkb
    assert kk == k2, (name, a.shape, b.shape, af, bf)
    tm = _pick(m, MM_TILE) if tm is None else tm
    tn = _pick(n, MM_TILE) if tn is None else tn
    tk = _pick(kk, MM_TILE) if tk is None else tk
    assert m % tm == 0 and n % tn == 0 and kk % tk == 0, (name, m, n, kk, tm, tn, tk)
    nk = kk // tk
    a_spec = {"mk": pl.BlockSpec((tm, tk), lambda i, j, k: (i, k)),
              "km": pl.BlockSpec((tk, tm), lambda i, j, k: (k, i)),
              "qmk": pl.BlockSpec((None, tm, tk), lambda i, j, k: (k, i, 0)),
              "qkm": pl.BlockSpec((None, tk, tm), lambda i, j, k: (i, k, 0))}[af]
    b_spec = {"kn": pl.BlockSpec((tk, tn), lambda i, j, k: (k, j)),
              "nk": pl.BlockSpec((tn, tk), lambda i, j, k: (j, k)),
              "qkn": pl.BlockSpec((None, tk, tn), lambda i, j, k: (j, k, 0)),
              "qnk": pl.BlockSpec((None, tn, tk), lambda i, j, k: (k, j, 0))}[bf]
    dims = (((1 if af in ("mk", "qmk") else 0,), (0 if bf in ("kn", "qkn") else 1,)), ((), ()))
    in_specs, args = [a_spec, b_spec], [a, b]
    if gate is not None:
        assert of == "mn"
        in_specs.append(pl.BlockSpec((1, tn), lambda i, j, k: (0, j)))
        args.append(gate)
    if resid is not None:
        assert of == "mn"
        in_specs.append(pl.BlockSpec((tm, tn), lambda i, j, k: (i, j)))
        args.append(resid)
    if of == "mn":
        o_spec, o_shape = pl.BlockSpec((tm, tn), lambda i, j, k: (i, j)), (m, n)
    else:
        o_spec, o_shape = pl.BlockSpec((None, tm, tn), lambda i, j, k: (j, i, 0)), (n // tn, m, tn)
    out_shape, out_specs = [jax.ShapeDtypeStruct(o_shape, out_dtype)], [o_spec]
    if pre_dtype is not None:
        out_shape.insert(0, jax.ShapeDtypeStruct(o_shape, pre_dtype))
        out_specs.insert(0, o_spec)
    n_in = len(args)
    n_out = len(out_shape)

    def body(*refs):
        a_ref, b_ref = refs[0], refs[1]
        extra = list(refs[2:n_in])
        outs = refs[n_in:n_in + n_out]
        gate_ref = extra.pop(0) if gate is not None else None
        resid_ref = extra.pop(0) if resid is not None else None

        def product():
            return lax.dot_general(a_ref[...].astype(BF), b_ref[...].astype(BF), dims, preferred_element_type=F32)

        def finish(r):
            if pre_dtype is not None:
                outs[0][...] = r.astype(pre_dtype)
            if gate_ref is not None:
                r = r * gate_ref[...]
            if resid_ref is not None:
                r = resid_ref[...] + r
            outs[-1][...] = r.astype(out_dtype)

        if nk == 1:
            finish(product())
        else:
            acc = refs[n_in + n_out]
            k = pl.program_id(2)

            @pl.when(k == 0)
            def _():
                acc[...] = product()

            @pl.when(k > 0)
            def _():
                acc[...] += product()

            @pl.when(k == nk - 1)
            def _():
                finish(acc[...])

    res = pl.pallas_call(
        body, name=name, grid=(m // tm, n // tn, nk),
        in_specs=in_specs, out_specs=out_specs, out_shape=out_shape,
        scratch_shapes=[pltpu.VMEM((tm, tn), F32)] if nk > 1 else [],
        compiler_params=_params(("parallel", "parallel", "arbitrary")),
    )(*args)
    return res if pre_dtype is not None else res[0]


def _ada_fwd(cond16, w, name):
    nl, d, n = w.shape

    def body(c_ref, w_ref, o_ref):
        o_ref[...] = jnp.dot(c_ref[...].astype(BF), w_ref[...].astype(BF), preferred_element_type=F32)

    return pl.pallas_call(
        body, name=name, grid=(nl,),
        in_specs=[pl.BlockSpec((16, d), lambda l: (0, 0)), pl.BlockSpec((None, d, n), lambda l: (l, 0, 0))],
        out_specs=pl.BlockSpec((None, 16, n), lambda l: (l, 0, 0)),
        out_shape=jax.ShapeDtypeStruct((nl, 16, n), F32),
        compiler_params=_params(("parallel",)),
    )(cond16, w)


def _ada_bwd(cond16, dmod, name):
    nl, _, n = dmod.shape
    d = cond16.shape[1]

    def body(c_ref, g_ref, o_ref):
        o_ref[...] = lax.dot_general(c_ref[...].astype(BF), g_ref[...].astype(BF), (((0,), (0,)), ((), ())),
                                     preferred_element_type=F32)

    return pl.pallas_call(
        body, name=name, grid=(nl,),
        in_specs=[pl.BlockSpec((16, d), lambda l: (0, 0)), pl.BlockSpec((None, 16, n), lambda l: (l, 0, 0))],
        out_specs=pl.BlockSpec((None, d, n), lambda l: (l, 0, 0)),
        out_shape=jax.ShapeDtypeStruct((nl, d, n), F32),
        compiler_params=_params(("parallel",)),
    )(cond16, dmod)


def _row_spec(tm, d):
    return pl.BlockSpec((tm, d), lambda i: (i, 0))


def _vec_spec(d):
    return pl.BlockSpec((1, d), lambda i: (0, 0))


def _norm_mod(x, g, sc, sh, name):
    s, d = x.shape
    tm = _pick(s, 512, 8)

    def body(x_ref, g_ref, sc_ref, sh_ref, o_ref):
        xv = x_ref[...]
        r = lax.rsqrt(jnp.mean(xv * xv, axis=-1, keepdims=True) + EPS)
        y = (xv * r) * g_ref[...]
        o_ref[...] = (y * (1.0 + sc_ref[...]) + sh_ref[...]).astype(BF)

    return pl.pallas_call(
        body, name=name, grid=(s // tm,),
        in_specs=[_row_spec(tm, d), _vec_spec(d), _vec_spec(d), _vec_spec(d)],
        out_specs=_row_spec(tm, d), out_shape=jax.ShapeDtypeStruct((s, d), BF),
        compiler_params=_params(("parallel",)),
    )(x, g, sc, sh)


def _norm_mod_bwd(x, dh, g, sc, dx_in, name):
    s, d = x.shape
    tm = _pick(s, 512, 8)

    def body(x_ref, dh_ref, g_ref, sc_ref, dxin_ref, dx_ref, dsh_ref, dw_ref):
        i = pl.program_id(0)
        xv = x_ref[...]
        dhv = dh_ref[...].astype(F32)
        r = lax.rsqrt(jnp.mean(xv * xv, axis=-1, keepdims=True) + EPS)
        xn = xv * r
        dxn = dhv * (g_ref[...] * (1.0 + sc_ref[...]))
        dx_ref[...] = dxin_ref[...] + r * (dxn - xn * jnp.mean(dxn * xn, axis=-1, keepdims=True))

        @pl.when(i == 0)
        def _():
            dsh_ref[...] = jnp.zeros_like(dsh_ref)
            dw_ref[...] = jnp.zeros_like(dw_ref)

        dsh_ref[...] += jnp.sum(dhv, axis=0, keepdims=True)
        dw_ref[...] += jnp.sum(dhv * xn, axis=0, keepdims=True)

    return pl.pallas_call(
        body, name=name, grid=(s // tm,),
        in_specs=[_row_spec(tm, d), _row_spec(tm, d), _vec_spec(d), _vec_spec(d), _row_spec(tm, d)],
        out_specs=[_row_spec(tm, d), _vec_spec(d), _vec_spec(d)],
        out_shape=[jax.ShapeDtypeStruct((s, d), F32), jax.ShapeDtypeStruct((1, d), F32),
                   jax.ShapeDtypeStruct((1, d), F32)],
        compiler_params=_params(("arbitrary",)),
    )(x, dh, g, sc, dx_in)


def _gate_bwd(dx, y, gate, name):
    s, d = dx.shape
    tm = _pick(s, 512, 8)

    def body(dx_ref, y_ref, g_ref, dy_ref, dg_ref):
        i = pl.program_id(0)
        dxv = dx_ref[...]
        dy_ref[...] = (dxv * g_ref[...]).astype(BF)

        @pl.when(i == 0)
        def _():
            dg_ref[...] = jnp.zeros_like(dg_ref)

        dg_ref[...] += jnp.sum(dxv * y_ref[...].astype(F32), axis=0, keepdims=True)

    return pl.pallas_call(
        body, name=name, grid=(s // tm,),
        in_specs=[_row_spec(tm, d), _row_spec(tm, d), _vec_spec(d)],
        out_specs=[_row_spec(tm, d), _vec_spec(d)],
        out_shape=[jax.ShapeDtypeStruct((s, d), BF), jax.ShapeDtypeStruct((1, d), F32)],
        compiler_params=_params(("arbitrary",)),
    )(dx, y, gate)


def _final_loss(x, tgt, g, name):
    s, d = x.shape
    tm = _pick(s, 512, 8)

    def body(x_ref, t_ref, g_ref, dx_ref, loss_ref, dg_ref):
        i = pl.program_id(0)
        xv = x_ref[...]
        gv = g_ref[...]
        r = lax.rsqrt(jnp.mean(xv * xv, axis=-1, keepdims=True) + EPS)
        xn = xv * r
        err = xn * gv - t_ref[...]
        dy = err * (1.0 / d)
        dxn = dy * gv
        dx_ref[...] = r * (dxn - xn * jnp.mean(dxn * xn, axis=-1, keepdims=True))

        @pl.when(i == 0)
        def _():
            loss_ref[...] = jnp.zeros_like(loss_ref)
            dg_ref[...] = jnp.zeros_like(dg_ref)

        part = 0.5 * jnp.sum(jnp.sum(err * err, axis=-1, keepdims=True) * (1.0 / d), axis=0, keepdims=True)
        loss_ref[...] += jnp.broadcast_to(part, loss_ref.shape)
        dg_ref[...] += jnp.sum(dy * xn, axis=0, keepdims=True)

    return pl.pallas_call(
        body, name=name, grid=(s // tm,),
        in_specs=[_row_spec(tm, d), _row_spec(tm, d), _vec_spec(d)],
        out_specs=[_row_spec(tm, d), pl.BlockSpec((8, LANES), lambda i: (0, 0)), _vec_spec(d)],
        out_shape=[jax.ShapeDtypeStruct((s, d), F32), jax.ShapeDtypeStruct((8, LANES), F32),
                   jax.ShapeDtypeStruct((1, d), F32)],
        compiler_params=_params(("arbitrary",)),
    )(x, tgt, g)


def _pool_counts(tm, gd, row0, w):
    t = lax.broadcasted_iota(jnp.int32, (tm, gd), 0) + row0
    return jnp.minimum(t + 1, w).astype(F32)


def _pool_fwd(u, name):
    s, d = u.shape
    tm = _pick(s, 256, POOL_HALO)
    gd = d // len(POOL_WINDOWS)
    per = tm // POOL_HALO

    def body(prev_ref, cur_ref, o_ref, ext):
        i = pl.program_id(0)
        ext[0:POOL_HALO, :] = jnp.where(i > 0, prev_ref[...], 0.0)
        ext[POOL_HALO:, :] = cur_ref[...]
        for g, w in enumerate(POOL_WINDOWS):
            cols = slice(g * gd, (g + 1) * gd)
            acc = ext[POOL_HALO:POOL_HALO + tm, cols]
            own = acc
            for k in range(1, w):
                acc = acc + ext[POOL_HALO - k:POOL_HALO - k + tm, cols]
            o_ref[:, cols] = (acc / _pool_counts(tm, gd, i * tm, w) - own).astype(BF)

    return pl.pallas_call(
        body, name=name, grid=(s // tm,),
        in_specs=[pl.BlockSpec((POOL_HALO, d), lambda i: (jnp.maximum(i * per - 1, 0), 0)), _row_spec(tm, d)],
        out_specs=_row_spec(tm, d), out_shape=jax.ShapeDtypeStruct((s, d), BF),
        scratch_shapes=[pltpu.VMEM((tm + POOL_HALO, d), F32)],
        compiler_params=_params(("parallel",)),
    )(u, u)


def _pool_bwd(dp, name):
    s, d = dp.shape
    tm = _pick(s, 256, POOL_HALO)
    gd = d // len(POOL_WINDOWS)
    per = tm // POOL_HALO
    nt = s // tm
    last_halo = s // POOL_HALO - 1

    def body(cur_ref, nxt_ref, o_ref, ext):
        i = pl.program_id(0)
        for g, w in enumerate(POOL_WINDOWS):
            cols = slice(g * gd, (g + 1) * gd)
            ext[0:tm, cols] = cur_ref[:, cols].astype(F32) / _pool_counts(tm, gd, i * tm, w)
            nxt = nxt_ref[:, cols].astype(F32) / _pool_counts(POOL_HALO, gd, (i + 1) * tm, w)
            ext[tm:, cols] = jnp.where(i < nt - 1, nxt, 0.0)
        for g, w in enumerate(POOL_WINDOWS):
            cols = slice(g * gd, (g + 1) * gd)
            acc = ext[0:tm, cols]
            for k in range(1, w):
                acc = acc + ext[k:k + tm, cols]
            o_ref[:, cols] = (acc - cur_ref[:, cols].astype(F32)).astype(BF)

    return pl.pallas_call(
        body, name=name, grid=(nt,),
        in_specs=[_row_spec(tm, d), pl.BlockSpec((POOL_HALO, d), lambda i: (jnp.minimum((i + 1) * per, last_halo), 0))],
        out_specs=_row_spec(tm, d), out_shape=jax.ShapeDtypeStruct((s, d), BF),
        scratch_shapes=[pltpu.VMEM((tm + POOL_HALO, d), F32)],
        compiler_params=_params(("parallel",)),
    )(dp, dp)


def _grp_fwd(p, w, scale, name):
    s, d = p.shape
    ng, gd, _ = w.shape
    tm = _pick(s, 1024, 8)

    def body(p_ref, w_ref, s_ref, z_ref, y_ref):
        z = jnp.dot(p_ref[...], w_ref[...].astype(BF), preferred_element_type=F32)
        z_ref[...] = z.astype(BF)
        y_ref[...] = (z * s_ref[...]).astype(BF)

    blk = pl.BlockSpec((tm, gd), lambda i, g: (i, g))
    return pl.pallas_call(
        body, name=name, grid=(s // tm, ng),
        in_specs=[blk, pl.BlockSpec((None, gd, gd), lambda i, g: (g, 0, 0)), pl.BlockSpec((1, gd), lambda i, g: (0, g))],
        out_specs=[blk, blk],
        out_shape=[jax.ShapeDtypeStruct((s, d), BF), jax.ShapeDtypeStruct((s, d), BF)],
        compiler_params=_params(("parallel", "parallel")),
    )(p, w, scale)


def _grp_bwd(dy, z, w, scale, name):
    s, d = dy.shape
    ng, gd, _ = w.shape
    tm = _pick(s, 1024, 8)

    def body(dy_ref, z_ref, w_ref, s_ref, dz_ref, dp_ref, ds_ref):
        i = pl.program_id(1)
        dyv = dy_ref[...].astype(F32)
        dz = (dyv * s_ref[...]).astype(BF)
        dz_ref[...] = dz
        dp_ref[...] = lax.dot_general(dz, w_ref[...].astype(BF), (((1,), (1,)), ((), ())),
                                      preferred_element_type=F32).astype(BF)

        @pl.when(i == 0)
        def _():
            ds_ref[...] = jnp.zeros_like(ds_ref)

        ds_ref[...] += jnp.sum(dyv * z_ref[...].astype(F32), axis=0, keepdims=True)

    blk = pl.BlockSpec((tm, gd), lambda g, i: (i, g))
    vec = pl.BlockSpec((1, gd), lambda g, i: (0, g))
    return pl.pallas_call(
        body, name=name, grid=(ng, s // tm),
        in_specs=[blk, blk, pl.BlockSpec((None, gd, gd), lambda g, i: (g, 0, 0)), vec],
        out_specs=[blk, blk, vec],
        out_shape=[jax.ShapeDtypeStruct((s, d), BF), jax.ShapeDtypeStruct((s, d), BF),
                   jax.ShapeDtypeStruct((1, d), F32)],
        compiler_params=_params(("parallel", "arbitrary")),
    )(dy, z, w, scale)


def _grp_dw(p, dz, ng, name):
    s, d = p.shape
    gd = d // ng
    tk = _pick(s, 1024, 8)

    def body(p_ref, dz_ref, o_ref):
        k = pl.program_id(1)

        @pl.when(k == 0)
        def _():
            o_ref[...] = jnp.zeros_like(o_ref)

        o_ref[...] += lax.dot_general(p_ref[...], dz_ref[...], (((0,), (0,)), ((), ())), preferred_element_type=F32)

    blk = pl.BlockSpec((tk, gd), lambda g, k: (k, g))
    return pl.pallas_call(
        body, name=name, grid=(ng, s // tk),
        in_specs=[blk, blk], out_specs=pl.BlockSpec((None, gd, gd), lambda g, k: (g, 0, 0)),
        out_shape=jax.ShapeDtypeStruct((ng, gd, gd), F32),
        compiler_params=_params(("parallel", "arbitrary")),
    )(p, dz)


def _sigmoid(a):
    return 0.5 * jnp.tanh(0.5 * a) + 0.5


def _ffn_act_down(up, cw, cb, w_down, gate, resid, name):
    _, nq, s, fq = up.shape
    d = w_down.shape[1]
    tm = _pick(s, 512, CONV_HALO)
    per = tm // CONV_HALO
    h = CONV_HALO
    rows = _pick(tm, FFN_ROWS, h)
    group = _pick(tm, FFN_GROUP, rows)

    def body(prev_ref, a_ref, v_ref, w_ref, b_ref, wd_ref, g_ref, r_ref, act_ref, ffo_ref, x_ref, ext, acc):
        i, q = pl.program_id(0), pl.program_id(1)
        ext[0:h, :] = jnp.where(i > 0, prev_ref[...].astype(F32), 0.0)
        ext[h:, :] = a_ref[...].astype(F32)

        @pl.when(q == 0)
        def _():
            acc[...] = jnp.zeros_like(acc)

        for g0 in range(0, tm, group):
            for r0 in range(g0, g0 + group, rows):
                e = ext[r0:r0 + rows + h, :]
                a2 = (b_ref[...] + e[h - 2:h - 2 + rows] * w_ref[0:1, :] + e[h - 1:h - 1 + rows] * w_ref[1:2, :]
                      + e[h:h + rows] * w_ref[2:3, :])
                vv = v_ref[r0:r0 + rows, :].astype(F32)
                act_ref[r0:r0 + rows, :] = (a2 * _sigmoid(a2) * vv).astype(BF)
            acc[g0:g0 + group, :] += jnp.dot(act_ref[g0:g0 + group, :], wd_ref[...], preferred_element_type=F32)

        @pl.when(q == nq - 1)
        def _():
            r = acc[...]
            ffo_ref[...] = r.astype(BF)
            x_ref[...] = r_ref[...] + g_ref[...] * r

    row = pl.BlockSpec((tm, d), lambda i, q: (i, 0))
    return pl.pallas_call(
        body, name=name, grid=(s // tm, nq),
        in_specs=[pl.BlockSpec((None, None, h, fq), lambda i, q: (0, q, jnp.maximum(i * per - 1, 0), 0)),
                  pl.BlockSpec((None, None, tm, fq), lambda i, q: (0, q, i, 0)),
                  pl.BlockSpec((None, None, tm, fq), lambda i, q: (1, q, i, 0)),
                  pl.BlockSpec((None, CONV_WIDTH, fq), lambda i, q: (q, 0, 0)),
                  pl.BlockSpec((None, 1, fq), lambda i, q: (q, 0, 0)),
                  pl.BlockSpec((fq, d), lambda i, q: (q, 0)),
                  pl.BlockSpec((1, d), lambda i, q: (0, 0)), row],
        out_specs=[pl.BlockSpec((None, tm, fq), lambda i, q: (q, i, 0)), row, row],
        out_shape=[jax.ShapeDtypeStruct((nq, s, fq), BF), jax.ShapeDtypeStruct((s, d), BF),
                   jax.ShapeDtypeStruct((s, d), F32)],
        scratch_shapes=[pltpu.VMEM((tm + h, fq), F32), pltpu.VMEM((tm, d), F32)],
        compiler_params=_params(("parallel", "arbitrary")),
    )(up, up, up, cw, cb, w_down, gate, resid)


def _ffn_act_bwd_up(up, dact, cw, cb, w_up, name):
    _, nq, s, fq = up.shape
    d = w_up.shape[1]
    tm = _pick(s, 512, CONV_HALO)
    per = tm // CONV_HALO
    nt = s // tm
    last_halo = s // CONV_HALO - 1
    h = CONV_HALO
    te = tm + h
    rows = _pick(tm, FFN_ROWS, h)
    group = _pick(tm, FFN_GROUP, rows)
    lanes = (((1,), (1,)), ((), ()))

    def body(ap_ref, a_ref, an_ref, v_ref, vn_ref, d_ref, dn_ref, w_ref, b_ref, wa_ref, wv_ref,
             dup_ref, dc_ref, dx_ref, ext_a, dap, sums, acc):
        i, q = pl.program_id(0), pl.program_id(1)
        ext_a[0:h, :] = jnp.where(i > 0, ap_ref[...].astype(F32), 0.0)
        ext_a[h:h + tm, :] = a_ref[...].astype(F32)
        ext_a[h + tm:, :] = an_ref[...].astype(F32)

        def pre_act(e, n):
            return (b_ref[...] + e[h - 2:h - 2 + n] * w_ref[0:1, :] + e[h - 1:h - 1 + n] * w_ref[1:2, :]
                    + e[h:h + n] * w_ref[2:3, :])

        def through_gate(a2, dd, vv):
            sig = _sigmoid(a2)
            return dd * vv * (sig * (1.0 + a2 * (1.0 - sig))), dd * (a2 * sig)

        def phase1(g0):
            for r0 in range(g0, g0 + group, rows):
                a2 = pre_act(ext_a[r0:r0 + rows + h, :], rows)
                g, dgate = through_gate(a2, d_ref[r0:r0 + rows, :].astype(F32), v_ref[r0:r0 + rows, :].astype(F32))
                dap[r0:r0 + rows, :] = g
                dup_ref[1, r0:r0 + rows, :] = dgate.astype(BF)

        def fold(t):
            part = t[0:8]
            for k in range(8, rows, 8):
                part = part + t[k:k + 8]
            return part

        def phase2(g0):
            for r0 in range(g0, g0 + group, rows):
                gch = dap[r0:r0 + rows + h, :]
                g0_ = gch[0:rows]
                dup_ref[0, r0:r0 + rows, :] = (gch[2:2 + rows] * w_ref[0:1, :] + gch[1:1 + rows] * w_ref[1:2, :]
                                               + g0_ * w_ref[2:3, :]).astype(BF)
                e = ext_a[r0:r0 + rows + h, :]
                sums[0] += fold(g0_ * e[h - 2:h - 2 + rows])
                sums[1] += fold(g0_ * e[h - 1:h - 1 + rows])
                sums[2] += fold(g0_ * e[h:h + rows])
                sums[3] += fold(g0_)

        @pl.when(q == 0)
        def _():
            acc[...] = jnp.zeros_like(acc)

        sums[...] = jnp.zeros_like(sums)
        phase1(0)
        for g0 in range(0, tm, group):
            if g0 + group < tm:
                phase1(g0 + group)
            else:
                d_nxt = jnp.where(i < nt - 1, dn_ref[...].astype(F32), 0.0)
                g, _ = through_gate(pre_act(ext_a[tm:tm + 2 * h, :], h), d_nxt, vn_ref[...].astype(F32))
                dap[tm:, :] = g
            phase2(g0)
            acc[g0:g0 + group, :] += (
                lax.dot_general(dup_ref[0, g0:g0 + group, :], wa_ref[...], lanes, preferred_element_type=F32)
                + lax.dot_general(dup_ref[1, g0:g0 + group, :], wv_ref[...], lanes, preferred_element_type=F32))

        @pl.when(i == 0)
        def _():
            dc_ref[q] = jnp.zeros((8, fq), F32)

        for k in range(4):
            dc_ref[q, k:k + 1, :] += jnp.sum(sums[k], axis=0, keepdims=True)

        @pl.when(q == nq - 1)
        def _():
            dx_ref[...] = acc[...]

    def cur(half):
        return pl.BlockSpec((None, None, tm, fq), lambda i, q: (half, q, i, 0))

    def nxt(half):
        return pl.BlockSpec((None, None, h, fq), lambda i, q: (half, q, jnp.minimum((i + 1) * per, last_halo), 0))

    return pl.pallas_call(
        body, name=name, grid=(nt, nq),
        in_specs=[pl.BlockSpec((None, None, h, fq), lambda i, q: (0, q, jnp.maximum(i * per - 1, 0), 0)),
                  cur(0), nxt(0), cur(1), nxt(1),
                  pl.BlockSpec((None, tm, fq), lambda i, q: (q, i, 0)),
                  pl.BlockSpec((None, h, fq), lambda i, q: (q, jnp.minimum((i + 1) * per, last_halo), 0)),
                  pl.BlockSpec((None, CONV_WIDTH, fq), lambda i, q: (q, 0, 0)),
                  pl.BlockSpec((None, 1, fq), lambda i, q: (q, 0, 0)),
                  pl.BlockSpec((None, d, fq), lambda i, q: (q, 0, 0)),
                  pl.BlockSpec((None, d, fq), lambda i, q: (q + nq, 0, 0))],
        out_specs=[pl.BlockSpec((2, None, tm, fq), lambda i, q: (0, q, i, 0)),
                   pl.BlockSpec((nq, 8, fq), lambda i, q: (0, 0, 0)),
                   pl.BlockSpec((tm, d), lambda i, q: (i, 0))],
        out_shape=[jax.ShapeDtypeStruct((2, nq, s, fq), BF), jax.ShapeDtypeStruct((nq, 8, fq), F32),
                   jax.ShapeDtypeStruct((s, d), F32)],
        scratch_shapes=[pltpu.VMEM((tm + 2 * h, fq), F32), pltpu.VMEM((te, fq), F32), pltpu.VMEM((4, 8, fq), F32),
                        pltpu.VMEM((tm, d), F32)],
        compiler_params=_params(("arbitrary", "arbitrary")),
    )(up, up, up, up, up, dact, dact, cw, cb, w_up, w_up)


def _band(blk, n_steps, dil, first):
    qi = lax.broadcasted_iota(jnp.int32, (blk, 2 * blk), 0) + blk
    ki = lax.broadcasted_iota(jnp.int32, (blk, 2 * blk), 1)
    delta = qi - ki
    valid = (delta >= 0) & (delta <= n_steps) & ((ki >= blk) | jnp.logical_not(first))
    return valid, (delta * dil).astype(F32)


def _branch_views(q_all, kv, g, d):
    _, dil = BRANCHES[g]
    nbr = len(BRANCHES)
    sub = q_all.shape[0] // dil
    if dil == 1:
        return (q_all, kv, kv), ((nbr, g), (2 * nbr, g), (2 * nbr, nbr + g))
    cols = [q_all[:, g * d:(g + 1) * d], kv[:, g * d:(g + 1) * d], kv[:, (nbr + g) * d:(nbr + g + 1) * d]]
    return tuple(a.reshape(sub, dil * d) for a in cols), ((1, 0), (1, 0), (1, 0))


def _attn_fwd(q_all, kv, g, slopes, d, name):
    window, dil = BRANCHES[g]
    n_steps = window // dil
    blk = max(ATTN_BLOCK, n_steps)
    s = q_all.shape[0]
    sub = s // dil
    nb = sub // blk
    assert nb * blk == sub
    nh = d // HEAD_DIM
    (qv, kview, vview), (qcol, kcol, vcol) = _branch_views(q_all, kv, g, d)
    scale = HEAD_DIM ** -0.5

    hb = blk // 2
    assert n_steps == blk, "the half-block key window relies on a band of one block"

    def body(q_ref, kp_ref, kc_ref, vp_ref, vc_ref, o_ref, l_ref):
        j = pl.program_id(1)
        qb = q_ref[...]
        kb = jnp.concatenate([kp_ref[...], kc_ref[...]], axis=0)
        vb = jnp.concatenate([vp_ref[...], vc_ref[...]], axis=0)
        first = lax.broadcasted_iota(jnp.int32, (1, 2 * HEAD_DIM), 1) < HEAD_DIM
        head_of_lane = lax.broadcasted_iota(jnp.int32, (1, LANES), 1) // (LANES // nh)
        bands = []
        for rh in range(2):
            qi = lax.broadcasted_iota(jnp.int32, (hb, 3 * hb), 0) + blk
            ki = lax.broadcasted_iota(jnp.int32, (hb, 3 * hb), 1)
            delta = qi - ki
            ok = (delta >= 0) & (delta <= n_steps) & ((ki + rh * hb >= blk) | jnp.logical_not(j == 0))
            bands.append((ok, (delta * dil).astype(F32)))
        lse_rows = [jnp.zeros((hb, LANES), F32), jnp.zeros((hb, LANES), F32)]
        for hp in range(nh // 2):
            sl = slice(2 * hp * HEAD_DIM, 2 * (hp + 1) * HEAD_DIM)
            qp, kp, vp = qb[:, sl], kb[:, sl], vb[:, sl]
            for rh in range(2):
                ok, dist_h = bands[rh]
                qr = qp[rh * hb:(rh + 1) * hb]
                kw, vw = kp[rh * hb:(rh + 3) * hb], vp[rh * hb:(rh + 3) * hb]
                out = None
                for half in range(2):
                    sel = first if half == 0 else jnp.logical_not(first)
                    sc = lax.dot_general(jnp.where(sel, qr, 0), kw, (((1,), (1,)), ((), ())),
                                         preferred_element_type=F32) * scale
                    sc = jnp.where(ok, sc - float(slopes[2 * hp + half]) * dist_h, NEG)
                    m = jnp.max(sc, axis=-1, keepdims=True)
                    p = jnp.exp(sc - m)
                    den = jnp.sum(p, axis=-1, keepdims=True)
                    o = jnp.dot(p.astype(BF), jnp.where(sel, vw, 0), preferred_element_type=F32) / den
                    out = o if half == 0 else out + o
                    lse_rows[rh] = jnp.where(head_of_lane == 2 * hp + half, m + jnp.log(den), lse_rows[rh])
                o_ref[rh * hb:(rh + 1) * hb, sl] = out.astype(BF)
        l_ref[0:hb, :] = lse_rows[0]
        l_ref[hb:, :] = lse_rows[1]

    def spec(col, prev):
        if prev:
            return pl.BlockSpec((blk, d), lambda r, j: (jnp.maximum(j - 1, 0), r * col[0] + col[1]))
        return pl.BlockSpec((blk, d), lambda r, j: (j, r * col[0] + col[1]))

    ospec = pl.BlockSpec((blk, d), lambda r, j: (j, r))
    o, lse = pl.pallas_call(
        body, name=name, grid=(dil, nb),
        in_specs=[spec(qcol, False), spec(kcol, True), spec(kcol, False), spec(vcol, True), spec(vcol, False)],
        out_specs=[ospec, pl.BlockSpec((blk, LANES), lambda r, j: (j, r))],
        out_shape=[jax.ShapeDtypeStruct((sub, dil * d), BF), jax.ShapeDtypeStruct((sub, dil * LANES), F32)],
        compiler_params=_params(("parallel", "parallel")),
    )(qv, kview, kview, vview, vview)
    return o.reshape(s, d), lse.reshape(s, LANES)


def _attn_combine(os, lses, name):
    s, d = os[0].shape
    tm = _pick(s, 512, 8)
    nbr = len(os)
    nh = d // HEAD_DIM
    per_head = LANES // nh

    def body(*refs):
        o_refs, l_refs = refs[:nbr], refs[nbr:2 * nbr]
        o_ref, lt_ref = refs[2 * nbr], refs[2 * nbr + 1]
        ls = [r[...] for r in l_refs]
        m = ls[0]
        for v in ls[1:]:
            m = jnp.maximum(m, v)
        tot = jnp.exp(ls[0] - m)
        for v in ls[1:]:
            tot = tot + jnp.exp(v - m)
        lt = m + jnp.log(tot)
        lt_ref[...] = lt
        ws = [jnp.exp(v - lt) for v in ls]
        first = lax.broadcasted_iota(jnp.int32, (1, 2 * HEAD_DIM), 1) < HEAD_DIM
        for hp in range(nh // 2):
            sl = slice(2 * hp * HEAD_DIM, 2 * (hp + 1) * HEAD_DIM)
            la, lb = 2 * hp * per_head, (2 * hp + 1) * per_head
            acc = None
            for w, r in zip(ws, o_refs):
                term = jnp.where(first, w[:, la:la + 1], w[:, lb:lb + 1]) * r[:, sl].astype(F32)
                acc = term if acc is None else acc + term
            o_ref[:, sl] = acc.astype(BF)

    lspec = _row_spec(tm, LANES)
    return pl.pallas_call(
        body, name=name, grid=(s // tm,),
        in_specs=[_row_spec(tm, d)] * nbr + [lspec] * nbr, out_specs=[_row_spec(tm, d), lspec],
        out_shape=[jax.ShapeDtypeStruct((s, d), BF), jax.ShapeDtypeStruct((s, LANES), F32)],
        compiler_params=_params(("parallel",)),
    )(*os, *lses)


def _attn_bwd(q_all, kv, do, o, lt, g, slopes, d, name, dk_in=None, dv_in=None):
    window, dil = BRANCHES[g]
    n_steps = window // dil
    blk = max(ATTN_BLOCK, n_steps)
    s = q_all.shape[0]
    sub = s // dil
    nb = sub // blk
    nh = d // HEAD_DIM
    (qv, kview, vview), (qcol, kcol, vcol) = _branch_views(q_all, kv, g, d)
    scale = HEAD_DIM ** -0.5
    acc_in = dk_in is not None

    def body(*refs):
        q_ref, do_ref, o_ref, lt_ref, kp_ref, kc_ref, vp_ref, vc_ref = refs[:8]
        n_in = 10 if acc_in else 8
        dkin_ref, dvin_ref = (refs[8], refs[9]) if acc_in else (None, None)
        dq_ref, dk_ref, dv_ref, keep_k, keep_v, part_k, part_v = refs[n_in:n_in + 7]
        t = pl.program_id(1)

        def emit(prev_k, prev_v):
            if acc_in:
                prev_k = prev_k + dkin_ref[...].astype(F32)
                prev_v = prev_v + dvin_ref[...].astype(F32)
            dk_ref[...] = prev_k.astype(BF)
            dv_ref[...] = prev_v.astype(BF)

        @pl.when(t < nb)
        def _():
            valid, dist = _band(blk, n_steps, dil, t == 0)
            qb = q_ref[...]
            dob = do_ref[...]
            ltb = lt_ref[...]
            kb = jnp.concatenate([kp_ref[...], kc_ref[...]], axis=0)
            vb = jnp.concatenate([vp_ref[...], vc_ref[...]], axis=0)
            first = lax.broadcasted_iota(jnp.int32, (1, 2 * HEAD_DIM), 1) < HEAD_DIM
            for hp in range(nh // 2):
                sl = slice(2 * hp * HEAD_DIM, 2 * (hp + 1) * HEAD_DIM)
                qp, kp, vp, dop = qb[:, sl], kb[:, sl], vb[:, sl], dob[:, sl]
                op = o_ref[:, sl].astype(F32)
                dq, dk, dv = None, None, None
                for half in range(2):
                    sel = first if half == 0 else jnp.logical_not(first)
                    qh, doh = jnp.where(sel, qp, 0), jnp.where(sel, dop, 0)
                    sc = lax.dot_general(qh, kp, (((1,), (1,)), ((), ())), preferred_element_type=F32) * scale
                    sc = sc - float(slopes[2 * hp + half]) * dist
                    lane = (2 * hp + half) * (LANES // nh)
                    lt_h = ltb[:, lane:lane + 1]
                    p = jnp.where(valid, jnp.exp(jnp.minimum(sc - lt_h, 30.0)), 0.0)
                    dlt = jnp.sum(doh.astype(F32) * op, axis=-1, keepdims=True)
                    dp = lax.dot_general(doh, vp, (((1,), (1,)), ((), ())), preferred_element_type=F32)
                    ds = (p * (dp - dlt)).astype(BF)
                    dq_h = jnp.dot(ds, jnp.where(sel, kp, 0), preferred_element_type=F32)
                    dk_h = lax.dot_general(ds, qh, (((0,), (0,)), ((), ())), preferred_element_type=F32)
                    dv_h = lax.dot_general(p.astype(BF), doh, (((0,), (0,)), ((), ())), preferred_element_type=F32)
                    dq = dq_h if half == 0 else dq + dq_h
                    dk = dk_h if half == 0 else dk + dk_h
                    dv = dv_h if half == 0 else dv + dv_h
                dq_ref[:, sl] = (dq * scale).astype(BF)
                part_k[:, sl] = dk * scale
                part_v[:, sl] = dv

            @pl.when(t > 0)
            def _():
                emit(keep_k[...] + part_k[0:blk, :], keep_v[...] + part_v[0:blk, :])

            keep_k[...] = part_k[blk:, :]
            keep_v[...] = part_v[blk:, :]

        @pl.when(t == nb)
        def _():
            emit(keep_k[...], keep_v[...])

    def qspec(col):
        return pl.BlockSpec((blk, d), lambda r, t: (jnp.minimum(t, nb - 1), r * col[0] + col[1]))

    def kspec(col, prev):
        if prev:
            return pl.BlockSpec((blk, d), lambda r, t: (jnp.maximum(jnp.minimum(t, nb - 1) - 1, 0), r * col[0] + col[1]))
        return qspec(col)

    kout = pl.BlockSpec((blk, d), lambda r, t: (jnp.maximum(t - 1, 0), r))
    one = (1, 0)
    ltspec = pl.BlockSpec((blk, LANES), lambda r, t: (jnp.minimum(t, nb - 1), r))
    in_specs = [qspec(qcol), qspec(one), qspec(one), ltspec,
                kspec(kcol, True), kspec(kcol, False), kspec(vcol, True), kspec(vcol, False)]
    args = [qv, do.reshape(sub, dil * d), o.reshape(sub, dil * d), lt.reshape(sub, dil * LANES),
            kview, kview, vview, vview]
    if acc_in:
        in_specs += [kout, kout]
        args += [dk_in.reshape(sub, dil * d), dv_in.reshape(sub, dil * d)]
    shp = jax.ShapeDtypeStruct((sub, dil * d), BF)
    dq, dk, dv = pl.pallas_call(
        body, name=name, grid=(dil, nb + 1),
        in_specs=in_specs, out_specs=[qspec(one), kout, kout], out_shape=[shp, shp, shp],
        scratch_shapes=[pltpu.VMEM((blk, d), F32), pltpu.VMEM((blk, d), F32),
                        pltpu.VMEM((2 * blk, d), F32), pltpu.VMEM((2 * blk, d), F32)],
        compiler_params=_params(("parallel", "arbitrary")),
    )(*args)
    return dq.reshape(s, d), dk.reshape(s, d), dv.reshape(s, d)


def _adamw(parts_list, w, m, v, name):
    nl, r, c = w.shape
    assert len(parts_list) == nl
    npart = parts_list[0].shape[0]
    tr = _pick(r, 256, 16)
    c1 = 1.0 / (1.0 - ADAM_B1 ** ADAM_STEP)
    c2 = 1.0 / (1.0 - ADAM_B2 ** ADAM_STEP)

    def body(*refs):
        p_refs = refs[:nl]
        w_ref, m_ref, v_ref, g_ref, d_ref, nm_ref, nv_ref = refs[nl:]
        layer = pl.program_id(0)
        for idx in range(nl):
            @pl.when(layer == idx)
            def _(p_ref=p_refs[idx]):
                g = p_ref[0].astype(F32)
                for k in range(1, npart):
                    g = g + p_ref[k].astype(F32)
                nm = ADAM_B1 * m_ref[...] + (1.0 - ADAM_B1) * g
                nv = ADAM_B2 * v_ref[...] + (1.0 - ADAM_B2) * (g * g)
                g_ref[...] = g
                nm_ref[...] = nm
                nv_ref[...] = nv
                d_ref[...] = -ADAM_LR * ((nm * c1) / (jnp.sqrt(nv * c2) + ADAM_EPS) + ADAM_WD * w_ref[...])

    def part_spec(idx):
        return pl.BlockSpec((npart, tr, c), lambda l, i: (0, jnp.where(l == idx, i, 0), 0))

    blk = pl.BlockSpec((None, tr, c), lambda l, i: (l, i, 0))
    shp = jax.ShapeDtypeStruct((nl, r, c), F32)
    return pl.pallas_call(
        body, name=name, grid=(nl, r // tr),
        in_specs=[part_spec(idx) for idx in range(nl)] + [blk, blk, blk],
        out_specs=[blk, blk, blk, blk], out_shape=[shp, shp, shp, shp],
        compiler_params=_params(("parallel", "parallel")),
    )(*parts_list, w, m, v)


def _full_from_slots(slots, shard_shape, axis):
    a = slots.reshape((N_DEV,) + tuple(shard_shape))
    a = jnp.moveaxis(a, 0, axis)
    full = list(shard_shape)
    full[axis] *= N_DEV
    return a.reshape(full)


def kernel(x, c, ada_w, ada_b, norm1_g, norm2_g, pool_w_in, pool_w_grp, pool_scale, pool_w_out, kv_norm_g, kv_ada_w, kv_ada_b, w_kv, attn_w_q, attn_w_o, ffn_w_up, ffn_conv_w, ffn_conv_b, ffn_w_down, final_g, loss_target, m_ada_w, m_ada_b, m_norm1_g, m_norm2_g, m_pool_w_in, m_pool_w_grp, m_pool_scale, m_pool_w_out, m_kv_norm_g, m_kv_ada_w, m_kv_ada_b, m_w_kv, m_attn_w_q, m_attn_w_o, m_ffn_w_up, m_ffn_conv_w, m_ffn_conv_b, m_ffn_w_down, m_final_g, v_ada_w, v_ada_b, v_norm1_g, v_norm2_g, v_pool_w_in, v_pool_w_grp, v_pool_scale, v_pool_w_out, v_kv_norm_g, v_kv_ada_w, v_kv_ada_b, v_w_kv, v_attn_w_q, v_attn_w_o, v_ffn_w_up, v_ffn_conv_w, v_ffn_conv_b, v_ffn_w_down, v_final_g):
    weights = dict(ada_w=ada_w, ada_b=ada_b, norm1_g=norm1_g, norm2_g=norm2_g, pool_w_in=pool_w_in,
                   pool_w_grp=pool_w_grp, pool_scale=pool_scale, pool_w_out=pool_w_out, kv_norm_g=kv_norm_g,
                   kv_ada_w=kv_ada_w, kv_ada_b=kv_ada_b, w_kv=w_kv, attn_w_q=attn_w_q, attn_w_o=attn_w_o,
                   ffn_w_up=ffn_w_up, ffn_conv_w=ffn_conv_w, ffn_conv_b=ffn_conv_b, ffn_w_down=ffn_w_down,
                   final_g=final_g)
    mom1 = dict(ada_w=m_ada_w, ada_b=m_ada_b, norm1_g=m_norm1_g, norm2_g=m_norm2_g, pool_w_in=m_pool_w_in,
                pool_w_grp=m_pool_w_grp, pool_scale=m_pool_scale, pool_w_out=m_pool_w_out, kv_norm_g=m_kv_norm_g,
                kv_ada_w=m_kv_ada_w, kv_ada_b=m_kv_ada_b, w_kv=m_w_kv, attn_w_q=m_attn_w_q, attn_w_o=m_attn_w_o,
                ffn_w_up=m_ffn_w_up, ffn_conv_w=m_ffn_conv_w, ffn_conv_b=m_ffn_conv_b, ffn_w_down=m_ffn_w_down,
                final_g=m_final_g)
    mom2 = dict(ada_w=v_ada_w, ada_b=v_ada_b, norm1_g=v_norm1_g, norm2_g=v_norm2_g, pool_w_in=v_pool_w_in,
                pool_w_grp=v_pool_w_grp, pool_scale=v_pool_scale, pool_w_out=v_pool_w_out, kv_norm_g=v_kv_norm_g,
                kv_ada_w=v_kv_ada_w, kv_ada_b=v_kv_ada_b, w_kv=v_w_kv, attn_w_q=v_attn_w_q, attn_w_o=v_attn_w_o,
                ffn_w_up=v_ffn_w_up, ffn_conv_w=v_ffn_conv_w, ffn_conv_b=v_ffn_conv_b, ffn_w_down=v_ffn_w_down,
                final_g=v_final_g)
    order = list(weights)

    seq, d = x.shape[1], x.shape[2]
    depth = ada_w.shape[0]
    n_pool = pool_w_in.shape[0]
    f = ffn_conv_b.shape[1]
    nbr = len(BRANCHES)
    nh = d // HEAD_DIM
    slopes = _alibi_slopes(nbr * nh).reshape(nbr, nh)
    me = 4 * lax.axis_index("x") + 2 * lax.axis_index("y") + lax.axis_index("c")
    xs = x[0]
    tgt = loss_target[0]

    def start_gathers(after):
        keys, groups = [], []
        for l in range(depth):
            if l < n_pool:
                parts = [("mixer", [pool_w_in[l], pool_w_grp[l].reshape(-1, pool_w_grp.shape[-1]), pool_w_out[l]])]
            else:
                j = l - n_pool
                parts = [("mixer", [attn_w_q[j]] + ([w_kv] if j == 0 else [])), ("out", [attn_w_o[j]])]
            for part, srcs in parts + [("ffn", [ffn_w_up[l], ffn_w_down[l]])]:
                keys.append((l, part))
                groups.append([a.astype(BF) for a in srcs])
        handles, _ = _comm_start(groups, "gather_start", scatter=False, after=after)
        return dict(zip(keys, handles))

    cond = c * (1.0 / (1.0 + jnp.exp(-c)))
    small_in = jnp.concatenate([cond.reshape(-1), ffn_conv_w.reshape(-1), pool_scale.reshape(-1)])
    n_small_in = small_in.shape[0]
    gath = _all_gather(_to_rows(small_in), "gather_small").reshape(N_DEV, -1)[:, :n_small_in]
    cond_all = gath[:, :d]
    o1 = d + ffn_conv_w.size
    conv_w_full = _full_from_slots(gath[:, d:o1], ffn_conv_w.shape, 2)
    pool_scale_full = _full_from_slots(gath[:, o1:], pool_scale.shape, 1)
    cond16 = jnp.concatenate([cond_all, jnp.zeros_like(cond_all)], axis=0)

    mod_part = _ada_fwd(cond16, ada_w, "ada_fwd")[:, :N_DEV]
    kv_part = _ada_fwd(cond16, kv_ada_w[None], "kv_ada_fwd")[0, :N_DEV]
    n_mod = depth * mod_part.shape[2] + kv_part.shape[1]
    send = jnp.concatenate([jnp.moveaxis(mod_part, 1, 0).reshape(N_DEV, -1), kv_part], axis=1)
    send_rows = jax.vmap(_to_rows)(send)
    got = _all_to_all(send_rows, "exchange_mod").reshape(N_DEV, -1)[:, :n_mod]
    ncol = mod_part.shape[2]
    mods = []
    for l in range(depth):
        row = got[:, l * ncol:(l + 1) * ncol].reshape(1, -1) + ada_b[l][None]
        mods.append([row[:, k * d:(k + 1) * d] for k in range(6)])
    kv_row = got[:, depth * ncol:].reshape(1, -1) + kv_ada_b[None]
    kv_shift, kv_scale = kv_row[:, :d], kv_row[:, d:]
    gather_handles = start_gathers(got)

    def vec(a):
        return a.reshape(1, -1)

    tall = _pick(seq, MM_ROWS)
    nq = 4
    fq = f // nq
    ng = len(POOL_WINDOWS)
    cw_slots = jnp.moveaxis(conv_w_full.reshape(depth, CONV_WIDTH, nq, fq), 2, 1)
    cb_slots = ffn_conv_b.reshape(depth, nq, 1, fq)

    saved = []
    xcur = xs
    kvs = None
    hkv = None
    x_kv = None
    w_kv_slots = None
    for l in range(depth):
        sh1, sc1, g1, sh2, sc2, g2 = mods[l]
        st = dict(x=xcur)
        h = _norm_mod(xcur, vec(norm1_g[l]), sc1, sh1, f"norm1_{l}")
        st["h"] = h
        gw = _comm_wait(gather_handles[l, "mixer"], h, f"gather_wait_mixer_{l}")
        if l < n_pool:
            w_in = gw[0].reshape(d, d)
            w_grp = jnp.moveaxis(gw[1].reshape(N_DEV, ng, -1, d // ng), 0, 1).reshape(ng, d // ng, d // ng)
            w_out = gw[2].reshape(d, d)
            u = _mm(h, w_in, f"pool_in_{l}")
            pooled = _pool_fwd(u, f"pool_fwd_{l}")
            z, y = _grp_fwd(pooled, w_grp, vec(pool_scale_full[l]), f"grp_fwd_{l}")
            mix, x1 = _mm(y, w_out, f"pool_out_{l}", gate=g1, resid=xcur, pre_dtype=BF)
            st.update(pooled=pooled, z=z, y=y, w_in=w_in, w_grp=w_grp, w_out=w_out)
        else:
            j = l - n_pool
            w_q_slots = gw[0]
            if j == 0:
                w_kv_slots = gw[1]
                x_kv = xcur
                hkv = _norm_mod(xcur, vec(kv_norm_g), kv_scale, kv_shift, "norm_kv")
                kvs = _mm(hkv, w_kv_slots, "kv_proj", bf="qkn", out_dtype=BF, tm=tall)
            q = _mm(h, w_q_slots, f"q_proj_{l}", bf="qkn", out_dtype=BF, tm=tall)
            outs, lses = [], []
            for g in range(nbr):
                og, lg = _attn_fwd(q, kvs, g, slopes[g], d, f"attn_fwd_{l}_{g}")
                outs.append(og)
                lses.append(lg)
            o, lt = _attn_combine(outs, lses, f"attn_mix_{l}")
            w_o = _comm_wait(gather_handles[l, "out"], o, f"gather_wait_out_{l}")[0].reshape(d, d)
            mix, x1 = _mm(o, w_o, f"attn_out_{l}", gate=g1, resid=xcur, pre_dtype=BF)
            st.update(q=q, o=o, lt=lt, w_q=w_q_slots, w_o=w_o)
        h2 = _norm_mod(x1, vec(norm2_g[l]), sc2, sh2, f"norm2_{l}")
        w_up_slots, w_down = _comm_wait(gather_handles[l, "ffn"], h2, f"gather_wait_ffn_{l}")
        w_down = w_down.reshape(f, d)
        st.update(w_up=w_up_slots, w_down=w_down)
        up = _mm(h2, w_up_slots, f"ffn_up_{l}", bf="qkn", of="qmn", out_dtype=BF, tm=tall).reshape(2, nq, seq, fq)
        act, ffo, x2 = _ffn_act_down(up, cw_slots[l], cb_slots[l], w_down, g2, x1, f"ffn_act_down_{l}")
        st.update(mix=mix, x1=x1, h2=h2, up=up, act=act, ffo=ffo)
        saved.append(st)
        xcur = x2

    dx, loss_blk, d_final_g = _final_loss(xcur, tgt, vec(final_g), "final_loss")
    loss = lax.psum(loss_blk[0, 0], ("x", "y", "c"))

    d_mod = [None] * depth
    d_n1 = [None] * depth
    d_n2 = [None] * depth
    d_conv = [None] * depth
    d_pscale = [None] * n_pool
    dk_acc = [None] * nbr
    dv_acc = [None] * nbr
    ffn_handles = [None] * depth
    mixer_handles = [None] * depth
    tok = 0.0
    for l in reversed(range(depth)):
        sh1, sc1, g1, sh2, sc2, g2 = mods[l]
        st = saved[l]
        dffo, dg2 = _gate_bwd(dx, st["ffo"], g2 + tok, f"gate2_bwd_{l}")
        g_down = _mm(st["act"], dffo, f"ffn_down_dw_{l}", af="qkm", out_dtype=BF)
        dact = _mm(dffo, st["w_down"], f"ffn_down_dx_{l}", bf="nk", of="qmn", tn=fq, out_dtype=BF, tm=tall)
        dup, dc, dh2 = _ffn_act_bwd_up(st["up"], dact, cw_slots[l], cb_slots[l], st["w_up"], f"ffn_act_bwd_up_{l}")
        dup = dup.reshape(N_DEV, seq, -1)
        d_conv[l] = dc
        g_up = _mm(st["h2"], dup, f"ffn_up_dw_{l}", af="km", bf="qkn", of="qmn", out_dtype=BF)
        (ffn_handles[l],), tok = _comm_start([[g_up, g_down.reshape(N_DEV, -1, d)]], f"exchange_start_ffn_{l}",
                                             scatter=True)
        dx1, dsh2, dw2 = _norm_mod_bwd(st["x1"], dh2, vec(norm2_g[l]), sc2 + tok, dx, f"norm2_bwd_{l}")
        d_n2[l] = dw2 * (1.0 + sc2)
        dsc2 = dw2 * vec(norm2_g[l])

        dmix, dg1 = _gate_bwd(dx1, st["mix"], g1, f"gate1_bwd_{l}")
        if l < n_pool:
            g_out = _mm(st["y"], dmix, f"pool_out_dw_{l}", af="km", out_dtype=BF)
            dy = _mm(dmix, st["w_out"], f"pool_out_dx_{l}", bf="nk", out_dtype=BF)
            dz, dpool, dps = _grp_bwd(dy, st["z"], st["w_grp"], vec(pool_scale_full[l]), f"grp_bwd_{l}")
            d_pscale[l] = dps
            g_grp = _grp_dw(st["pooled"], dz, ng, f"grp_dw_{l}")
            du = _pool_bwd(dpool, f"pool_bwd_{l}")
            g_in = _mm(st["h"], du, f"pool_in_dw_{l}", af="km", out_dtype=BF)
            dh = _mm(du, st["w_in"], f"pool_in_dx_{l}", bf="nk")
            g_grp_slots = jnp.moveaxis(g_grp.reshape(ng, N_DEV, -1, d // ng), 1, 0).reshape(N_DEV, -1, d // ng)
            send = [g_in.reshape(N_DEV, -1, d), g_grp_slots.astype(BF), g_out.reshape(N_DEV, -1, d)]
        else:
            j = l - n_pool
            g_o = _mm(st["o"], dmix, f"attn_out_dw_{l}", af="km", out_dtype=BF)
            do = _mm(dmix, st["w_o"], f"attn_out_dx_{l}", bf="nk", out_dtype=BF)
            dqs = []
            for g in range(nbr):
                dq_g, dk_g, dv_g = _attn_bwd(st["q"], kvs, do, st["o"], st["lt"], g, slopes[g], d,
                                             f"attn_bwd_{l}_{g}", dk_in=dk_acc[g], dv_in=dv_acc[g])
                dqs.append(dq_g)
                dk_acc[g], dv_acc[g] = dk_g, dv_g
            dq = jnp.concatenate(dqs, axis=1)
            nqc = st["w_q"].shape[2]
            g_q = _mm(st["h"], dq, f"q_proj_dw_{l}", af="km", of="qmn", tn=nqc, out_dtype=BF)
            dh = _mm(dq, st["w_q"], f"q_proj_dx_{l}", bf="qnk")
            send = [g_q, g_o.reshape(N_DEV, -1, d)]
        dx0, dsh1, dw1 = _norm_mod_bwd(st["x"], dh, vec(norm1_g[l]), sc1, dx1, f"norm1_bwd_{l}")
        d_n1[l] = dw1 * (1.0 + sc1)
        dsc1 = dw1 * vec(norm1_g[l])
        d_mod[l] = jnp.concatenate([dsh1, dsc1, dg1, dsh2, dsc2, dg2], axis=1)
        dx = dx0
        if l == n_pool:
            dkv = jnp.concatenate(dk_acc + dv_acc, axis=1)
            nkc = w_kv_slots.shape[2]
            g_kv = _mm(hkv, dkv, "kv_proj_dw", af="km", of="qmn", tn=nkc, out_dtype=BF)
            dhkv = _mm(dkv, w_kv_slots, "kv_proj_dx", bf="qnk")
            dx, dsh_kv, dw_kv = _norm_mod_bwd(x_kv, dhkv, vec(kv_norm_g), kv_scale, dx, "norm_kv_bwd")
            d_kv_norm = dw_kv * (1.0 + kv_scale)
            d_kv_mod = jnp.concatenate([dsh_kv, dw_kv * vec(kv_norm_g)], axis=1)
            send.append(g_kv)
        if l > 0:
            (mixer_handles[l],), tok = _comm_start([send], f"exchange_start_mixer_{l}", scatter=True)
    grad_x = dx[None]

    small = [jnp.concatenate(d_mod, axis=1).reshape(-1), d_kv_mod.reshape(-1),
             jnp.concatenate(d_n1, axis=0).reshape(-1), jnp.concatenate(d_n2, axis=0).reshape(-1),
             d_kv_norm.reshape(-1), d_final_g.reshape(-1),
             jnp.stack([dcl[:, 3, :] for dcl in d_conv]).reshape(-1),
             jnp.stack([jnp.moveaxis(dcl[:, 0:CONV_WIDTH, :], 0, 1) for dcl in d_conv]).reshape(-1),
             jnp.concatenate(d_pscale, axis=0).reshape(-1)]
    sizes = [a.shape[0] for a in small]
    small_rows = _to_rows(jnp.concatenate(small))
    small_all = _all_gather(small_rows, "gather_small_grads")
    (mixer_handles[0],), _ = _comm_start([send], "exchange_start_mixer_0", scatter=True, after=small_all)
    dmod_all = small_all.reshape(N_DEV, -1)[:, :sizes[0] + sizes[1]]

    dmod16 = jnp.concatenate([dmod_all, jnp.zeros_like(dmod_all)], axis=0)
    dm = dmod16[:, :sizes[0]].reshape(16, depth, N_DEV, ncol)
    dm_mine = lax.dynamic_index_in_dim(dm, me, axis=2, keepdims=False)
    g_ada_w = _ada_bwd(cond16, jnp.moveaxis(dm_mine, 0, 1), "ada_bwd")
    nkv = kv_part.shape[1]
    dkm = dmod16[:, sizes[0]:].reshape(16, N_DEV, nkv)
    dkm_mine = lax.dynamic_index_in_dim(dkm, me, axis=1, keepdims=False)
    g_kv_ada_w = _ada_bwd(cond16, dkm_mine[None], "kv_ada_bwd")

    res = {}

    def update(n, parts_list, shape3):
        w3, m3, v3 = (a[n].reshape(shape3) for a in (weights, mom1, mom2))
        outs = _adamw(parts_list, w3, m3, v3, f"adamw_{n}")
        res[n] = [a.reshape(weights[n].shape) for a in outs]

    update("ada_w", [g_ada_w[l][None] for l in range(depth)], ada_w.shape)
    update("kv_ada_w", [g_kv_ada_w], (1,) + kv_ada_w.shape)
    after = res["ada_w"][0]
    parts_ffn = [_comm_wait(ffn_handles[l], after, f"exchange_wait_ffn_{l}") for l in reversed(range(depth))][::-1]
    parts = [_comm_wait(mixer_handles[l], after, f"exchange_wait_mixer_{l}") for l in reversed(range(depth))][::-1]
    pool_layers, attn_layers = range(n_pool), range(n_pool, depth)
    update("ffn_w_up", [parts_ffn[l][0] for l in range(depth)], ffn_w_up.shape)
    update("ffn_w_down", [parts_ffn[l][1] for l in range(depth)], ffn_w_down.shape)
    update("attn_w_q", [parts[l][0] for l in attn_layers], attn_w_q.shape)
    update("attn_w_o", [parts[l][1] for l in attn_layers], (depth - n_pool, -1, d))
    update("w_kv", [parts[n_pool][2]], (1,) + w_kv.shape)
    update("pool_w_in", [parts[l][0] for l in pool_layers], (n_pool, -1, d))
    update("pool_w_grp", [parts[l][1] for l in pool_layers], (n_pool, -1, d // ng))
    update("pool_w_out", [parts[l][2] for l in pool_layers], (n_pool, -1, d))

    tot = small_all.reshape(N_DEV, -1)
    offs = np.cumsum([0] + sizes)
    seg = {k: (int(offs[i]), int(offs[i + 1])) for i, k in enumerate(
        ["mod", "kv_mod", "n1", "n2", "kv_norm", "final", "conv_b", "conv_w", "pscale"])}

    def rows_of(a, b):
        return tot[:, a:b]

    nf8 = f // N_DEV
    conv_w_parts = lax.dynamic_slice_in_dim(
        rows_of(*seg["conv_w"]).reshape(N_DEV, depth, CONV_WIDTH, N_DEV, nf8), me, 1, axis=3).reshape(N_DEV, -1)
    nd8 = d // N_DEV
    pscale_parts = lax.dynamic_slice_in_dim(
        rows_of(*seg["pscale"]).reshape(N_DEV, n_pool, N_DEV, nd8), me, 1, axis=2).reshape(N_DEV, -1)
    small_names = ["ada_b", "norm1_g", "norm2_g", "pool_scale", "kv_norm_g", "kv_ada_b", "ffn_conv_w",
                   "ffn_conv_b", "final_g"]
    small_parts = [rows_of(*seg["mod"]), rows_of(*seg["n1"]), rows_of(*seg["n2"]), pscale_parts,
                   rows_of(*seg["kv_norm"]), rows_of(*seg["kv_mod"]), conv_w_parts, rows_of(*seg["conv_b"]),
                   rows_of(*seg["final"])]
    sp = jnp.concatenate(small_parts, axis=1)
    n_sp = sp.shape[1]
    sp_rows = jax.vmap(_to_rows)(sp)

    def packed(src):
        return _to_rows(jnp.concatenate([src[n].reshape(-1) for n in small_names]))[None]

    outs = _adamw([sp_rows], packed(weights), packed(mom1), packed(mom2), "adamw_small")
    outs = [a.reshape(-1)[:n_sp] for a in outs]
    off = 0
    for n in small_names:
        size = weights[n].size
        res[n] = [a[off:off + size].reshape(weights[n].shape) for a in outs]
        off += size

    grads = [res[n][0] for n in order]
    deltas = [res[n][1] for n in order]
    new_m = [res[n][2] for n in order]
    new_v = [res[n][3] for n in order]
    return (loss, grad_x, *grads, *deltas, *new_m, *new_v)
```
